```python
import math
import jax, jax.numpy as jnp
from jax import lax
import numpy as np

D_MODEL = 1024
BATCH = 16
SEQ = 4096
DEPTH = 2

N_META = 16
BLOCK = 128
PAD_LEN = BLOCK - N_META
EPS = 1e-6
NEG_INF = -1e30

N_BRANCH = 4
D_BRANCH = 256

FOX_HEADS = 4
FOX_DH = 64

MLA_HEADS = 4
MLA_NOPE = 64
MLA_ROPE = 32
MLA_DV = 64
MLA_Q_RANK = 192
MLA_KV_RANK = 128
ROPE_BASE = 10000.0

GDN_HEADS = 4
GDN_DK = 64
GDN_DV = 64
GDN_CONV = 4
GDN_CHUNK = 64

LRU_WIDTH = 256
LRU_BLOCKS = 4
LRU_CONV = 4
LRU_C = 8.0

D_FF = 2816

OFF_FOX_QKV = 0
OFF_FOX_F = OFF_FOX_QKV + 3 * FOX_HEADS * FOX_DH
OFF_MLA_CQ = OFF_FOX_F + FOX_HEADS
OFF_MLA_CKV = OFF_MLA_CQ + MLA_Q_RANK
OFF_MLA_KR = OFF_MLA_CKV + MLA_KV_RANK
OFF_GDN_QKV = OFF_MLA_KR + MLA_ROPE
OFF_GDN_A = OFF_GDN_QKV + GDN_HEADS * (2 * GDN_DK + GDN_DV)
OFF_GDN_B = OFF_GDN_A + GDN_HEADS
OFF_GDN_G = OFF_GDN_B + GDN_HEADS
OFF_LRU = OFF_GDN_G + GDN_HEADS * GDN_DV
N_IN = OFF_LRU + LRU_WIDTH

kernel_name = "hybrid_fox_mla_gdn_rglru_macaron"


def rmsnorm(x, g):
    xf = x.astype(jnp.float32)
    y = xf * lax.rsqrt(jnp.mean(xf * xf, axis=-1, keepdims=True) + EPS)
    return y.astype(x.dtype) * g


def l2norm(x):
    return x * lax.rsqrt(jnp.sum(x * x, axis=-1, keepdims=True) + EPS)


def swiglu(h, wi, wo):
    gu = h @ wi
    g, u = jnp.split(gu, 2, axis=-1)
    return (jax.nn.silu(g) * u) @ wo


def causal_dwconv(x, w):
    K, C = w.shape
    return lax.conv_general_dilated(
        x, w[:, None, :].astype(x.dtype), window_strides=(1,), padding=[(K - 1, 0)],
        dimension_numbers=('NWC', 'WIO', 'NWC'), feature_group_count=C)


def rope(x, cos, sin):
    half = x.shape[-1] // 2
    x1, x2 = x[..., :half], x[..., half:]
    return jnp.concatenate([x1 * cos - x2 * sin, x2 * cos + x1 * sin], axis=-1)


def blocked_causal_attention(q, k, v, scale, cum=None):
    B, H, T, dk = q.shape
    nb = T // BLOCK
    kpos = jnp.arange(T)
    key_ok = kpos >= PAD_LEN
    q_blocks = jnp.moveaxis(q.reshape(B, H, nb, BLOCK, dk), 2, 0)
    xs = (jnp.arange(nb), q_blocks)
    if cum is not None:
        xs = xs + (jnp.moveaxis(cum.reshape(B, H, nb, BLOCK), 2, 0),)

    def one_block(blk):
        i, q_i = blk[0], blk[1]
        s = jnp.einsum('bhqd,bhkd->bhqk', q_i, k, preferred_element_type=jnp.float32) * scale
        if cum is not None:
            s = s + blk[2][..., :, None] - cum[:, :, None, :]
        qpos = i * BLOCK + jnp.arange(BLOCK)
        mask = (kpos[None, :] <= qpos[:, None]) & key_ok[None, :]
        s = jnp.where(mask, s, NEG_INF)
        prob = jax.nn.softmax(s, axis=-1)
        return jnp.einsum('bhqk,bhkd->bhqd', prob.astype(v.dtype), v)

    out = lax.map(one_block, xs)
    return jnp.moveaxis(out, 0, 2).reshape(B, H, T, v.shape[-1])


def fox_branch(p, b_f):
    B, T, _ = p.shape
    qkv = p[..., OFF_FOX_QKV:OFF_FOX_F].reshape(B, T, 3, FOX_HEADS, FOX_DH)
    q = qkv[:, :, 0].transpose(0, 2, 1, 3)
    k = qkv[:, :, 1].transpose(0, 2, 1, 3)
    v = qkv[:, :, 2].transpose(0, 2, 1, 3)
    log_f = jax.nn.log_sigmoid((p[..., OFF_FOX_F:OFF_MLA_CQ] + b_f).astype(jnp.float32))
    cum = jnp.cumsum(log_f, axis=1).transpose(0, 2, 1)
    o = blocked_causal_attention(q, k, v, FOX_DH ** -0.5, cum)
    return o.transpose(0, 2, 1, 3).reshape(B, T, FOX_HEADS * FOX_DH)


def mla_branch(p, g_qn, w_q_up, g_kvn, w_kv_up, cos, sin):
    B, T, _ = p.shape
    cq = rmsnorm(p[..., OFF_MLA_CQ:OFF_MLA_CKV], g_qn)
    q = (cq @ w_q_up).reshape(B, T, MLA_HEADS, MLA_NOPE + MLA_ROPE)
    ckv = rmsnorm(p[..., OFF_MLA_CKV:OFF_MLA_KR], g_kvn)
    kv = (ckv @ w_kv_up).reshape(B, T, MLA_HEADS, MLA_NOPE + MLA_DV)
    k_rope = rope(p[..., OFF_MLA_KR:OFF_GDN_QKV], cos, sin)
    q_rope = rope(q[..., MLA_NOPE:], cos[:, None], sin[:, None])
    q = jnp.concatenate([q[..., :MLA_NOPE], q_rope], axis=-1)
    k = jnp.concatenate([kv[..., :MLA_NOPE],
                         jnp.broadcast_to(k_rope[:, :, None], (B, T, MLA_HEADS, MLA_ROPE))], axis=-1)
    v = kv[..., MLA_NOPE:]
    o = blocked_causal_attention(q.transpose(0, 2, 1, 3), k.transpose(0, 2, 1, 3),
                                 v.transpose(0, 2, 1, 3), (MLA_NOPE + MLA_ROPE) ** -0.5)
    return o.transpose(0, 2, 1, 3).reshape(B, T, MLA_HEADS * MLA_DV)


def gdn_branch(p, conv_w, a_log, dt_bias, g_on):
    B, T, _ = p.shape
    H, DK, DV, C = GDN_HEADS, GDN_DK, GDN_DV, GDN_CHUNK
    f32 = jnp.float32
    qkv = jax.nn.silu(causal_dwconv(p[..., OFF_GDN_QKV:OFF_GDN_A], conv_w)).astype(f32)
    q = l2norm(qkv[..., :H * DK].reshape(B, T, H, DK)) * DK ** -0.5
    k = l2norm(qkv[..., H * DK:2 * H * DK].reshape(B, T, H, DK))
    v = qkv[..., 2 * H * DK:].reshape(B, T, H, DV)
    beta = jax.nn.sigmoid(p[..., OFF_GDN_B:OFF_GDN_G].astype(f32))
    g = -jnp.exp(a_log.astype(f32)) * jax.nn.softplus(
        p[..., OFF_GDN_A:OFF_GDN_B].astype(f32) + dt_bias.astype(f32))
    nc = T // C

    def chunks(t):
        return jnp.moveaxis(t, 2, 1).reshape((B, H, nc, C) + t.shape[3:])

    q, k, v, beta, g = chunks(q), chunks(k), chunks(v), chunks(beta), chunks(g)
    G = jnp.cumsum(g, axis=-1)
    idx = jnp.arange(C)
    strict = idx[:, None] > idx[None, :]
    incl = idx[:, None] >= idx[None, :]
    decay = jnp.exp(jnp.where(incl, G[..., :, None] - G[..., None, :], NEG_INF))
    kb = k * beta[..., None]
    vb = v * beta[..., None]
    m = jnp.eye(C, dtype=f32) + jnp.where(
        strict, jnp.einsum('bhnik,bhnjk->bhnij', kb, k) * decay, 0.0)
    rhs = jnp.concatenate([kb * jnp.exp(G)[..., None], vb], axis=-1)
    sol = lax.linalg.triangular_solve(m, rhs, left_side=True, lower=True, unit_diagonal=True)
    w, u = sol[..., :DK], sol[..., DK:]
    qk = jnp.where(incl, jnp.einsum('bhnik,bhnjk->bhnij', q, k) * decay, 0.0)
    q_dec = q * jnp.exp(G)[..., None]
    k_dec = k * jnp.exp(G[..., -1:] - G)[..., None]
    g_last = jnp.exp(G[..., -1])
    xs = (jnp.moveaxis(q_dec, 2, 0), jnp.moveaxis(k_dec, 2, 0), jnp.moveaxis(w, 2, 0),
          jnp.moveaxis(u, 2, 0), jnp.moveaxis(qk, 2, 0), jnp.moveaxis(g_last, 2, 0))

    def step(S, inp):
        q_c, k_c, w_c, u_c, qk_c, gl_c = inp
        v_new = u_c - jnp.einsum('bhck,bhkv->bhcv', w_c, S)
        o_c = jnp.einsum('bhck,bhkv->bhcv', q_c, S) + jnp.einsum('bhij,bhjv->bhiv', qk_c, v_new)
        S = S * gl_c[..., None, None] + jnp.einsum('bhck,bhcv->bhkv', k_c, v_new)
        return S, o_c

    S0 = jnp.zeros((B, H, DK, DV), f32)
    _, o = lax.scan(step, S0, xs)
    o = jnp.moveaxis(o, 0, 2).reshape(B, H, T, DV).transpose(0, 2, 1, 3)
    gate = jax.nn.silu(p[..., OFF_GDN_G:OFF_LRU].astype(f32)).reshape(B, T, H, DV)
    o = rmsnorm(o, g_on) * gate
    return o.reshape(B, T, H * DV).astype(p.dtype)


def rglru_branch(p, valid, conv_w, conv_b, w_a, b_a, w_x, b_x, lam):
    B, T, _ = p.shape
    f32 = jnp.float32
    xr = causal_dwconv(p[..., OFF_LRU:N_IN], conv_w) + conv_b
    xr = jnp.where(valid[None, :, None], xr, 0)
    xb = xr.reshape(B, T, LRU_BLOCKS, LRU_WIDTH // LRU_BLOCKS)
    r = jax.nn.sigmoid(jnp.einsum('btni,nij->btnj', xb, w_a).reshape(B, T, LRU_WIDTH) + b_a).astype(f32)
    ig = jax.nn.sigmoid(jnp.einsum('btni,nij->btnj', xb, w_x).reshape(B, T, LRU_WIDTH) + b_x).astype(f32)
    log_a = -LRU_C * r * jax.nn.softplus(-lam.astype(f32))
    a = jnp.exp(log_a)
    b = jnp.sqrt(-jnp.expm1(2.0 * log_a)) * ig * xr.astype(f32)

    def combine(e1, e2):
        return (e1[0] * e2[0], e2[0] * e1[1] + e2[1])

    _, h = lax.associative_scan(combine, (a, b), axis=1)
    return h.astype(p.dtype)


def hybrid_mixer(u, valid, cos, sin, w_in, fox_bf, mla_gq, mla_wq, mla_gkv, mla_wkv,
                 gdn_conv, gdn_alog, gdn_dtb, gdn_gon, lru_conv, lru_conv_b, lru_wa, lru_ba,
                 lru_wx, lru_bx, lru_lam, w_gate, b_gate, w_branch, w_out):
    p = u @ w_in
    ys = (fox_branch(p, fox_bf),
          mla_branch(p, mla_gq, mla_wq, mla_gkv, mla_wkv, cos, sin),
          gdn_branch(p, gdn_conv, gdn_alog, gdn_dtb, gdn_gon),
          rglru_branch(p, valid, lru_conv, lru_conv_b, lru_wa, lru_ba, lru_wx, lru_bx, lru_lam))
    merged = jax.nn.sigmoid(u @ w_gate[0] + b_gate[0]) * (ys[0] @ w_branch[0])
    for n in range(1, N_BRANCH):
        merged = merged + jax.nn.sigmoid(u @ w_gate[n] + b_gate[n]) * (ys[n] @ w_branch[n])
    return merged @ w_out


def _fwd_setup_inputs(seed: int = 0) -> dict:
    key = jax.random.key(seed)
    k = jax.random.split(key, 32)
    f32 = jnp.float32
    D, L, F = D_MODEL, DEPTH, D_FF

    def nrm(i, shape, scale):
        return scale * jax.random.normal(k[i], shape, f32)

    def gain(i, shape):
        return 1.0 + 0.02 * jax.random.normal(k[i], shape, f32)

    u_a = jax.random.uniform(k[22], (L, LRU_WIDTH), f32, 0.9, 0.999)
    a_base = u_a ** (1.0 / LRU_C)
    lru_lam = jnp.log(a_base) - jnp.log1p(-a_base)
    dt = jnp.exp(jax.random.uniform(k[14], (L, GDN_HEADS), f32, math.log(1e-3), math.log(1e-1)))
    gdn_dtb = dt + jnp.log(-jnp.expm1(-dt))
    gdn_alog = jnp.log(jax.random.uniform(k[13], (L, GDN_HEADS), f32, 1.0, 16.0))
    return {
        "x": nrm(0, (BATCH, SEQ, D), 1.0),
        "meta": nrm(1, (N_META, D), 1.0),
        "ln_ffn1": gain(2, (L, D)),
        "ffn1_wi": nrm(3, (L, D, 2 * F), D ** -0.5),
        "ffn1_wo": nrm(4, (L, F, D), F ** -0.5),
        "ln_mix": gain(5, (L, D)),
        "w_in": nrm(6, (L, D, N_IN), D ** -0.5),
        "fox_bf": 3.0 + nrm(7, (L, FOX_HEADS), 0.1),
        "mla_gq": gain(8, (L, MLA_Q_RANK)),
        "mla_wq": nrm(9, (L, MLA_Q_RANK, MLA_HEADS * (MLA_NOPE + MLA_ROPE)), MLA_Q_RANK ** -0.5),
        "mla_gkv": gain(10, (L, MLA_KV_RANK)),
        "mla_wkv": nrm(11, (L, MLA_KV_RANK, MLA_HEADS * (MLA_NOPE + MLA_DV)), MLA_KV_RANK ** -0.5),
        "gdn_conv": nrm(12, (L, GDN_CONV, GDN_HEADS * (2 * GDN_DK + GDN_DV)), GDN_CONV ** -0.5),
        "gdn_alog": gdn_alog,
        "gdn_dtb": gdn_dtb,
        "gdn_gon": gain(15, (L, GDN_DV)),
        "lru_conv": nrm(16, (L, LRU_CONV, LRU_WIDTH), LRU_CONV ** -0.5),
        "lru_conv_b": nrm(17, (L, LRU_WIDTH), 0.01),
        "lru_wa": nrm(18, (L, LRU_BLOCKS, LRU_WIDTH // LRU_BLOCKS, LRU_WIDTH // LRU_BLOCKS), (LRU_WIDTH // LRU_BLOCKS) ** -0.5),
        "lru_ba": nrm(19, (L, LRU_WIDTH), 0.01),
        "lru_wx": nrm(20, (L, LRU_BLOCKS, LRU_WIDTH // LRU_BLOCKS, LRU_WIDTH // LRU_BLOCKS), (LRU_WIDTH // LRU_BLOCKS) ** -0.5),
        "lru_bx": nrm(21, (L, LRU_WIDTH), 0.01),
        "lru_lam": lru_lam,
        "w_gate": nrm(23, (L, N_BRANCH, D, D), D ** -0.5),
        "b_gate": nrm(24, (L, N_BRANCH, D), 0.01),
        "w_branch": nrm(25, (L, N_BRANCH, D_BRANCH, D), D_BRANCH ** -0.5),
        "w_out": nrm(26, (L, D, D), D ** -0.5),
        "ln_ffn2": gain(27, (L, D)),
        "ffn2_wi": nrm(28, (L, D, 2 * F), D ** -0.5),
        "ffn2_wo": nrm(29, (L, F, D), F ** -0.5),
        "ln_final": gain(30, (D,)),
    }


def _fwd_reference(x, meta, ln_ffn1, ffn1_wi, ffn1_wo, ln_mix, w_in, fox_bf, mla_gq, mla_wq,
              mla_gkv, mla_wkv, gdn_conv, gdn_alog, gdn_dtb, gdn_gon, lru_conv, lru_conv_b,
              lru_wa, lru_ba, lru_wx, lru_bx, lru_lam, w_gate, b_gate, w_branch, w_out,
              ln_ffn2, ffn2_wi, ffn2_wo, ln_final):
    B, S, D = x.shape
    T = BLOCK + S
    h = jnp.concatenate([jnp.zeros((B, PAD_LEN, D), x.dtype),
                         jnp.broadcast_to(meta.astype(x.dtype)[None], (B, N_META, D)), x], axis=1)
    pos = jnp.arange(T)
    valid = pos >= PAD_LEN
    rel = (pos - PAD_LEN).astype(jnp.float32)
    inv_freq = ROPE_BASE ** (-(jnp.arange(0, MLA_ROPE, 2, dtype=jnp.float32) / MLA_ROPE))
    ang = rel[:, None] * inv_freq[None, :]
    cos = jnp.cos(ang).astype(x.dtype)
    sin = jnp.sin(ang).astype(x.dtype)
    for l in range(DEPTH):
        h = h + 0.5 * swiglu(rmsnorm(h, ln_ffn1[l]), ffn1_wi[l], ffn1_wo[l])
        u = jnp.where(valid[None, :, None], rmsnorm(h, ln_mix[l]), 0)
        h = h + hybrid_mixer(u, valid, cos, sin, w_in[l], fox_bf[l], mla_gq[l], mla_wq[l],
                             mla_gkv[l], mla_wkv[l], gdn_conv[l], gdn_alog[l], gdn_dtb[l],
                             gdn_gon[l], lru_conv[l], lru_conv_b[l], lru_wa[l], lru_ba[l],
                             lru_wx[l], lru_bx[l], lru_lam[l], w_gate[l], b_gate[l],
                             w_branch[l], w_out[l])
        h = h + 0.5 * swiglu(rmsnorm(h, ln_ffn2[l]), ffn2_wi[l], ffn2_wo[l])
    y = rmsnorm(h, ln_final)
    return y[:, BLOCK:]


import jax as _jax
import jax.numpy as _jnp

TWIN_FORMAT = 'train_step'
FWD_PARAMS = ['x', 'meta', 'ln_ffn1', 'ffn1_wi', 'ffn1_wo', 'ln_mix', 'w_in', 'fox_bf', 'mla_gq', 'mla_wq', 'mla_gkv', 'mla_wkv', 'gdn_conv', 'gdn_alog', 'gdn_dtb', 'gdn_gon', 'lru_conv', 'lru_conv_b', 'lru_wa', 'lru_ba', 'lru_wx', 'lru_bx', 'lru_lam', 'w_gate', 'b_gate', 'w_branch', 'w_out', 'ln_ffn2', 'ffn2_wi', 'ffn2_wo', 'ln_final']
TWIN_WEIGHTS = ['meta', 'ln_ffn1', 'ffn1_wi', 'ffn1_wo', 'ln_mix', 'w_in', 'fox_bf', 'mla_gq', 'mla_wq', 'mla_gkv', 'mla_wkv', 'gdn_conv', 'gdn_alog', 'gdn_dtb', 'gdn_gon', 'lru_conv', 'lru_conv_b', 'lru_wa', 'lru_ba', 'lru_wx', 'lru_bx', 'lru_lam', 'w_gate', 'b_gate', 'w_branch', 'w_out', 'ln_ffn2', 'ffn2_wi', 'ffn2_wo', 'ln_final']
TWIN_DIFF_INPUT = 'x'
TWIN_INPUTS = ['x', 'meta', 'ln_ffn1', 'ffn1_wi', 'ffn1_wo', 'ln_mix', 'w_in', 'fox_bf', 'mla_gq', 'mla_wq', 'mla_gkv', 'mla_wkv', 'gdn_conv', 'gdn_alog', 'gdn_dtb', 'gdn_gon', 'lru_conv', 'lru_conv_b', 'lru_wa', 'lru_ba', 'lru_wx', 'lru_bx', 'lru_lam', 'w_gate', 'b_gate', 'w_branch', 'w_out', 'ln_ffn2', 'ffn2_wi', 'ffn2_wo', 'ln_final', 'loss_target', 'm_meta', 'm_ln_ffn1', 'm_ffn1_wi', 'm_ffn1_wo', 'm_ln_mix', 'm_w_in', 'm_fox_bf', 'm_mla_gq', 'm_mla_wq', 'm_mla_gkv', 'm_mla_wkv', 'm_gdn_conv', 'm_gdn_alog', 'm_gdn_dtb', 'm_gdn_gon', 'm_lru_conv', 'm_lru_conv_b', 'm_lru_wa', 'm_lru_ba', 'm_lru_wx', 'm_lru_bx', 'm_lru_lam', 'm_w_gate', 'm_b_gate', 'm_w_branch', 'm_w_out', 'm_ln_ffn2', 'm_ffn2_wi', 'm_ffn2_wo', 'm_ln_final', 'v_meta', 'v_ln_ffn1', 'v_ffn1_wi', 'v_ffn1_wo', 'v_ln_mix', 'v_w_in', 'v_fox_bf', 'v_mla_gq', 'v_mla_wq', 'v_mla_gkv', 'v_mla_wkv', 'v_gdn_conv', 'v_gdn_alog', 'v_gdn_dtb', 'v_gdn_gon', 'v_lru_conv', 'v_lru_conv_b', 'v_lru_wa', 'v_lru_ba', 'v_lru_wx', 'v_lru_bx', 'v_lru_lam', 'v_w_gate', 'v_b_gate', 'v_w_branch', 'v_w_out', 'v_ln_ffn2', 'v_ffn2_wi', 'v_ffn2_wo', 'v_ln_final']
TWIN_OUTPUTS = ['loss', 'grad_x', 'grad_meta', 'grad_ln_ffn1', 'grad_ffn1_wi', 'grad_ffn1_wo', 'grad_ln_mix', 'grad_w_in', 'grad_fox_bf', 'grad_mla_gq', 'grad_mla_wq', 'grad_mla_gkv', 'grad_mla_wkv', 'grad_gdn_conv', 'grad_gdn_alog', 'grad_gdn_dtb', 'grad_gdn_gon', 'grad_lru_conv', 'grad_lru_conv_b', 'grad_lru_wa', 'grad_lru_ba', 'grad_lru_wx', 'grad_lru_bx', 'grad_lru_lam', 'grad_w_gate', 'grad_b_gate', 'grad_w_branch', 'grad_w_out', 'grad_ln_ffn2', 'grad_ffn2_wi', 'grad_ffn2_wo', 'grad_ln_final', 'delta_meta', 'delta_ln_ffn1', 'delta_ffn1_wi', 'delta_ffn1_wo', 'delta_ln_mix', 'delta_w_in', 'delta_fox_bf', 'delta_mla_gq', 'delta_mla_wq', 'delta_mla_gkv', 'delta_mla_wkv', 'delta_gdn_conv', 'delta_gdn_alog', 'delta_gdn_dtb', 'delta_gdn_gon', 'delta_lru_conv', 'delta_lru_conv_b', 'delta_lru_wa', 'delta_lru_ba', 'delta_lru_wx', 'delta_lru_bx', 'delta_lru_lam', 'delta_w_gate', 'delta_b_gate', 'delta_w_branch', 'delta_w_out', 'delta_ln_ffn2', 'delta_ffn2_wi', 'delta_ffn2_wo', 'delta_ln_final', 'new_m_meta', 'new_m_ln_ffn1', 'new_m_ffn1_wi', 'new_m_ffn1_wo', 'new_m_ln_mix', 'new_m_w_in', 'new_m_fox_bf', 'new_m_mla_gq', 'new_m_mla_wq', 'new_m_mla_gkv', 'new_m_mla_wkv', 'new_m_gdn_conv', 'new_m_gdn_alog', 'new_m_gdn_dtb', 'new_m_gdn_gon', 'new_m_lru_conv', 'new_m_lru_conv_b', 'new_m_lru_wa', 'new_m_lru_ba', 'new_m_lru_wx', 'new_m_lru_bx', 'new_m_lru_lam', 'new_m_w_gate', 'new_m_b_gate', 'new_m_w_branch', 'new_m_w_out', 'new_m_ln_ffn2', 'new_m_ffn2_wi', 'new_m_ffn2_wo', 'new_m_ln_final', 'new_v_meta', 'new_v_ln_ffn1', 'new_v_ffn1_wi', 'new_v_ffn1_wo', 'new_v_ln_mix', 'new_v_w_in', 'new_v_fox_bf', 'new_v_mla_gq', 'new_v_mla_wq', 'new_v_mla_gkv', 'new_v_mla_wkv', 'new_v_gdn_conv', 'new_v_gdn_alog', 'new_v_gdn_dtb', 'new_v_gdn_gon', 'new_v_lru_conv', 'new_v_lru_conv_b', 'new_v_lru_wa', 'new_v_lru_ba', 'new_v_lru_wx', 'new_v_lru_bx', 'new_v_lru_lam', 'new_v_w_gate', 'new_v_b_gate', 'new_v_w_branch', 'new_v_w_out', 'new_v_ln_ffn2', 'new_v_ffn2_wi', 'new_v_ffn2_wo', 'new_v_ln_final']
TWIN_LEAF_KINDS = {'loss': 'loss', 'grad_x': 'grad_x', 'grad_meta': 'grad_w', 'grad_ln_ffn1': 'grad_w', 'grad_ffn1_wi': 'grad_w', 'grad_ffn1_wo': 'grad_w', 'grad_ln_mix': 'grad_w', 'grad_w_in': 'grad_w', 'grad_fox_bf': 'grad_w', 'grad_mla_gq': 'grad_w', 'grad_mla_wq': 'grad_w', 'grad_mla_gkv': 'grad_w', 'grad_mla_wkv': 'grad_w', 'grad_gdn_conv': 'grad_w', 'grad_gdn_alog': 'grad_w', 'grad_gdn_dtb': 'grad_w', 'grad_gdn_gon': 'grad_w', 'grad_lru_conv': 'grad_w', 'grad_lru_conv_b': 'grad_w', 'grad_lru_wa': 'grad_w', 'grad_lru_ba': 'grad_w', 'grad_lru_wx': 'grad_w', 'grad_lru_bx': 'grad_w', 'grad_lru_lam': 'grad_w', 'grad_w_gate': 'grad_w', 'grad_b_gate': 'grad_w', 'grad_w_branch': 'grad_w', 'grad_w_out': 'grad_w', 'grad_ln_ffn2': 'grad_w', 'grad_ffn2_wi': 'grad_w', 'grad_ffn2_wo': 'grad_w', 'grad_ln_final': 'grad_w', 'delta_meta': 'delta_w', 'delta_ln_ffn1': 'delta_w', 'delta_ffn1_wi': 'delta_w', 'delta_ffn1_wo': 'delta_w', 'delta_ln_mix': 'delta_w', 'delta_w_in': 'delta_w', 'delta_fox_bf': 'delta_w', 'delta_mla_gq': 'delta_w', 'delta_mla_wq': 'delta_w', 'delta_mla_gkv': 'delta_w', 'delta_mla_wkv': 'delta_w', 'delta_gdn_conv': 'delta_w', 'delta_gdn_alog': 'delta_w', 'delta_gdn_dtb': 'delta_w', 'delta_gdn_gon': 'delta_w', 'delta_lru_conv': 'delta_w', 'delta_lru_conv_b': 'delta_w', 'delta_lru_wa': 'delta_w', 'delta_lru_ba': 'delta_w', 'delta_lru_wx': 'delta_w', 'delta_lru_bx': 'delta_w', 'delta_lru_lam': 'delta_w', 'delta_w_gate': 'delta_w', 'delta_b_gate': 'delta_w', 'delta_w_branch': 'delta_w', 'delta_w_out': 'delta_w', 'delta_ln_ffn2': 'delta_w', 'delta_ffn2_wi': 'delta_w', 'delta_ffn2_wo': 'delta_w', 'delta_ln_final': 'delta_w', 'new_m_meta': 'new_m', 'new_m_ln_ffn1': 'new_m', 'new_m_ffn1_wi': 'new_m', 'new_m_ffn1_wo': 'new_m', 'new_m_ln_mix': 'new_m', 'new_m_w_in': 'new_m', 'new_m_fox_bf': 'new_m', 'new_m_mla_gq': 'new_m', 'new_m_mla_wq': 'new_m', 'new_m_mla_gkv': 'new_m', 'new_m_mla_wkv': 'new_m', 'new_m_gdn_conv': 'new_m', 'new_m_gdn_alog': 'new_m', 'new_m_gdn_dtb': 'new_m', 'new_m_gdn_gon': 'new_m', 'new_m_lru_conv': 'new_m', 'new_m_lru_conv_b': 'new_m', 'new_m_lru_wa': 'new_m', 'new_m_lru_ba': 'new_m', 'new_m_lru_wx': 'new_m', 'new_m_lru_bx': 'new_m', 'new_m_lru_lam': 'new_m', 'new_m_w_gate': 'new_m', 'new_m_b_gate': 'new_m', 'new_m_w_branch': 'new_m', 'new_m_w_out': 'new_m', 'new_m_ln_ffn2': 'new_m', 'new_m_ffn2_wi': 'new_m', 'new_m_ffn2_wo': 'new_m', 'new_m_ln_final': 'new_m', 'new_v_meta': 'new_v', 'new_v_ln_ffn1': 'new_v', 'new_v_ffn1_wi': 'new_v', 'new_v_ffn1_wo': 'new_v', 'new_v_ln_mix': 'new_v', 'new_v_w_in': 'new_v', 'new_v_fox_bf': 'new_v', 'new_v_mla_gq': 'new_v', 'new_v_mla_wq': 'new_v', 'new_v_mla_gkv': 'new_v', 'new_v_mla_wkv': 'new_v', 'new_v_gdn_conv': 'new_v', 'new_v_gdn_alog': 'new_v', 'new_v_gdn_dtb': 'new_v', 'new_v_gdn_gon': 'new_v', 'new_v_lru_conv': 'new_v', 'new_v_lru_conv_b': 'new_v', 'new_v_lru_wa': 'new_v', 'new_v_lru_ba': 'new_v', 'new_v_lru_wx': 'new_v', 'new_v_lru_bx': 'new_v', 'new_v_lru_lam': 'new_v', 'new_v_w_gate': 'new_v', 'new_v_b_gate': 'new_v', 'new_v_w_branch': 'new_v', 'new_v_w_out': 'new_v', 'new_v_ln_ffn2': 'new_v', 'new_v_ffn2_wi': 'new_v', 'new_v_ffn2_wo': 'new_v', 'new_v_ln_final': 'new_v'}


def _forward(args):
    return _fwd_reference(*[args[k] for k in FWD_PARAMS])


def _output_shape():
    out = _jax.eval_shape(lambda: _forward(_fwd_setup_inputs(0)))
    return out.shape, out.dtype

N_MICROBATCH = 1
ADAM_LR = 0.001
ADAM_B1 = 0.9
ADAM_B2 = 0.999
ADAM_EPS = 1e-08
ADAM_WD = 0.01
ADAM_STEP = 10
PER_EXAMPLE_BATCH_AXIS = {'x': 0, 'loss_target': 0}
SHARED_INPUTS = []
_WEIGHT_DTYPES = {'meta': _jnp.float32, 'ln_ffn1': _jnp.float32, 'ffn1_wi': _jnp.float32, 'ffn1_wo': _jnp.float32, 'ln_mix': _jnp.float32, 'w_in': _jnp.float32, 'fox_bf': _jnp.float32, 'mla_gq': _jnp.float32, 'mla_wq': _jnp.float32, 'mla_gkv': _jnp.float32, 'mla_wkv': _jnp.float32, 'gdn_conv': _jnp.float32, 'gdn_alog': _jnp.float32, 'gdn_dtb': _jnp.float32, 'gdn_gon': _jnp.float32, 'lru_conv': _jnp.float32, 'lru_conv_b': _jnp.float32, 'lru_wa': _jnp.float32, 'lru_ba': _jnp.float32, 'lru_wx': _jnp.float32, 'lru_bx': _jnp.float32, 'lru_lam': _jnp.float32, 'w_gate': _jnp.float32, 'b_gate': _jnp.float32, 'w_branch': _jnp.float32, 'w_out': _jnp.float32, 'ln_ffn2': _jnp.float32, 'ffn2_wi': _jnp.float32, 'ffn2_wo': _jnp.float32, 'ln_final': _jnp.float32}
MOMENT_SCALE = {'meta': 1.393655e-02, 'ln_ffn1': 1.211507e-01, 'ffn1_wi': 5.023399e-02, 'ffn1_wo': 8.205331e-02, 'ln_mix': 2.071816e-01, 'w_in': 1.244824e-01, 'fox_bf': 9.896984e-01, 'mla_gq': 4.718110e-02, 'mla_wq': 3.269072e-02, 'mla_gkv': 9.321462e-02, 'mla_wkv': 4.761682e-02, 'gdn_conv': 1.314870e-01, 'gdn_alog': 1.388426e+00, 'gdn_dtb': 1.384385e+00, 'gdn_gon': 2.884105e-01, 'lru_conv': 3.231834e-01, 'lru_conv_b': 4.101828e+00, 'lru_wa': 1.490407e-01, 'lru_ba': 1.514804e-01, 'lru_wx': 2.834596e-01, 'lru_bx': 1.041710e-01, 'lru_lam': 2.409014e-01, 'w_gate': 2.635706e-02, 'b_gate': 4.046934e-02, 'w_branch': 9.919014e-02, 'w_out': 1.998339e-01, 'ln_ffn2': 9.089852e-02, 'ffn2_wi': 3.818963e-02, 'ffn2_wo': 6.251800e-02, 'ln_final': 6.449249e+01}


def _to_microbatches(a, axis):
    t = _jnp.moveaxis(a, axis, 0)
    t = t.reshape((N_MICROBATCH, t.shape[0] // N_MICROBATCH) + t.shape[1:])
    return _jnp.moveaxis(t, 1, axis + 1)


def setup_inputs(seed: int = 0) -> dict:
    inp = _fwd_setup_inputs(seed)
    key = _jax.random.fold_in(_jax.random.key(seed), 7919)
    shape, _ = _output_shape()
    out = dict(inp)
    out["loss_target"] = _jax.random.normal(_jax.random.fold_in(key, 0), shape, _jnp.float32)
    for i, name in enumerate(TWIN_WEIGHTS):
        w = inp[name].astype(_jnp.float32)
        if MOMENT_SCALE is None:
            s = _jnp.sqrt(_jnp.mean(_jnp.square(w)) + 1e-30)
        else:
            s = MOMENT_SCALE[name]
        km, kv = _jax.random.split(_jax.random.fold_in(key, i + 1))
        out[name] = w
        out["m_" + name] = s * _jax.random.normal(km, w.shape, _jnp.float32)
        out["v_" + name] = (s * s) * _jax.random.uniform(kv, w.shape, _jnp.float32, 0.5, 1.5)
    if N_MICROBATCH > 1:
        for name, axis in PER_EXAMPLE_BATCH_AXIS.items():
            out[name] = _to_microbatches(out[name], axis)
    return {'x': out['x'], 'meta': out['meta'], 'ln_ffn1': out['ln_ffn1'], 'ffn1_wi': out['ffn1_wi'], 'ffn1_wo': out['ffn1_wo'], 'ln_mix': out['ln_mix'], 'w_in': out['w_in'], 'fox_bf': out['fox_bf'], 'mla_gq': out['mla_gq'], 'mla_wq': out['mla_wq'], 'mla_gkv': out['mla_gkv'], 'mla_wkv': out['mla_wkv'], 'gdn_conv': out['gdn_conv'], 'gdn_alog': out['gdn_alog'], 'gdn_dtb': out['gdn_dtb'], 'gdn_gon': out['gdn_gon'], 'lru_conv': out['lru_conv'], 'lru_conv_b': out['lru_conv_b'], 'lru_wa': out['lru_wa'], 'lru_ba': out['lru_ba'], 'lru_wx': out['lru_wx'], 'lru_bx': out['lru_bx'], 'lru_lam': out['lru_lam'], 'w_gate': out['w_gate'], 'b_gate': out['b_gate'], 'w_branch': out['w_branch'], 'w_out': out['w_out'], 'ln_ffn2': out['ln_ffn2'], 'ffn2_wi': out['ffn2_wi'], 'ffn2_wo': out['ffn2_wo'], 'ln_final': out['ln_final'], 'loss_target': out['loss_target'], 'm_meta': out['m_meta'], 'm_ln_ffn1': out['m_ln_ffn1'], 'm_ffn1_wi': out['m_ffn1_wi'], 'm_ffn1_wo': out['m_ffn1_wo'], 'm_ln_mix': out['m_ln_mix'], 'm_w_in': out['m_w_in'], 'm_fox_bf': out['m_fox_bf'], 'm_mla_gq': out['m_mla_gq'], 'm_mla_wq': out['m_mla_wq'], 'm_mla_gkv': out['m_mla_gkv'], 'm_mla_wkv': out['m_mla_wkv'], 'm_gdn_conv': out['m_gdn_conv'], 'm_gdn_alog': out['m_gdn_alog'], 'm_gdn_dtb': out['m_gdn_dtb'], 'm_gdn_gon': out['m_gdn_gon'], 'm_lru_conv': out['m_lru_conv'], 'm_lru_conv_b': out['m_lru_conv_b'], 'm_lru_wa': out['m_lru_wa'], 'm_lru_ba': out['m_lru_ba'], 'm_lru_wx': out['m_lru_wx'], 'm_lru_bx': out['m_lru_bx'], 'm_lru_lam': out['m_lru_lam'], 'm_w_gate': out['m_w_gate'], 'm_b_gate': out['m_b_gate'], 'm_w_branch': out['m_w_branch'], 'm_w_out': out['m_w_out'], 'm_ln_ffn2': out['m_ln_ffn2'], 'm_ffn2_wi': out['m_ffn2_wi'], 'm_ffn2_wo': out['m_ffn2_wo'], 'm_ln_final': out['m_ln_final'], 'v_meta': out['v_meta'], 'v_ln_ffn1': out['v_ln_ffn1'], 'v_ffn1_wi': out['v_ffn1_wi'], 'v_ffn1_wo': out['v_ffn1_wo'], 'v_ln_mix': out['v_ln_mix'], 'v_w_in': out['v_w_in'], 'v_fox_bf': out['v_fox_bf'], 'v_mla_gq': out['v_mla_gq'], 'v_mla_wq': out['v_mla_wq'], 'v_mla_gkv': out['v_mla_gkv'], 'v_mla_wkv': out['v_mla_wkv'], 'v_gdn_conv': out['v_gdn_conv'], 'v_gdn_alog': out['v_gdn_alog'], 'v_gdn_dtb': out['v_gdn_dtb'], 'v_gdn_gon': out['v_gdn_gon'], 'v_lru_conv': out['v_lru_conv'], 'v_lru_conv_b': out['v_lru_conv_b'], 'v_lru_wa': out['v_lru_wa'], 'v_lru_ba': out['v_lru_ba'], 'v_lru_wx': out['v_lru_wx'], 'v_lru_bx': out['v_lru_bx'], 'v_lru_lam': out['v_lru_lam'], 'v_w_gate': out['v_w_gate'], 'v_b_gate': out['v_b_gate'], 'v_w_branch': out['v_w_branch'], 'v_w_out': out['v_w_out'], 'v_ln_ffn2': out['v_ln_ffn2'], 'v_ffn2_wi': out['v_ffn2_wi'], 'v_ffn2_wo': out['v_ffn2_wo'], 'v_ln_final': out['v_ln_final']}


def _loss(weights, diff, rest, loss_target):
    with _jax.named_scope("forward"):
        args = {**rest, TWIN_DIFF_INPUT: diff, **{k: w.astype(_WEIGHT_DTYPES[k]) for k, w in weights.items()}}
        y = _forward(args)
    with _jax.named_scope("loss_head"):
        err = _jnp.square(y.astype(_jnp.float32) - loss_target)
        return 0.5 * _jnp.sum(_jnp.mean(err, axis=-1)) if err.ndim else 0.5 * err


def _adamw(w, g, m, v):
    m = ADAM_B1 * m + (1.0 - ADAM_B1) * g
    v = ADAM_B2 * v + (1.0 - ADAM_B2) * _jnp.square(g)
    m_hat = m / (1.0 - ADAM_B1 ** ADAM_STEP)
    v_hat = v / (1.0 - ADAM_B2 ** ADAM_STEP)
    delta = -ADAM_LR * (m_hat / (_jnp.sqrt(v_hat) + ADAM_EPS) + ADAM_WD * w)
    return delta, m, v


def reference(x, meta, ln_ffn1, ffn1_wi, ffn1_wo, ln_mix, w_in, fox_bf, mla_gq, mla_wq, mla_gkv, mla_wkv, gdn_conv, gdn_alog, gdn_dtb, gdn_gon, lru_conv, lru_conv_b, lru_wa, lru_ba, lru_wx, lru_bx, lru_lam, w_gate, b_gate, w_branch, w_out, ln_ffn2, ffn2_wi, ffn2_wo, ln_final, loss_target, m_meta, m_ln_ffn1, m_ffn1_wi, m_ffn1_wo, m_ln_mix, m_w_in, m_fox_bf, m_mla_gq, m_mla_wq, m_mla_gkv, m_mla_wkv, m_gdn_conv, m_gdn_alog, m_gdn_dtb, m_gdn_gon, m_lru_conv, m_lru_conv_b, m_lru_wa, m_lru_ba, m_lru_wx, m_lru_bx, m_lru_lam, m_w_gate, m_b_gate, m_w_branch, m_w_out, m_ln_ffn2, m_ffn2_wi, m_ffn2_wo, m_ln_final, v_meta, v_ln_ffn1, v_ffn1_wi, v_ffn1_wo, v_ln_mix, v_w_in, v_fox_bf, v_mla_gq, v_mla_wq, v_mla_gkv, v_mla_wkv, v_gdn_conv, v_gdn_alog, v_gdn_dtb, v_gdn_gon, v_lru_conv, v_lru_conv_b, v_lru_wa, v_lru_ba, v_lru_wx, v_lru_bx, v_lru_lam, v_w_gate, v_b_gate, v_w_branch, v_w_out, v_ln_ffn2, v_ffn2_wi, v_ffn2_wo, v_ln_final):
    given = dict(x=x, meta=meta, ln_ffn1=ln_ffn1, ffn1_wi=ffn1_wi, ffn1_wo=ffn1_wo, ln_mix=ln_mix, w_in=w_in, fox_bf=fox_bf, mla_gq=mla_gq, mla_wq=mla_wq, mla_gkv=mla_gkv, mla_wkv=mla_wkv, gdn_conv=gdn_conv, gdn_alog=gdn_alog, gdn_dtb=gdn_dtb, gdn_gon=gdn_gon, lru_conv=lru_conv, lru_conv_b=lru_conv_b, lru_wa=lru_wa, lru_ba=lru_ba, lru_wx=lru_wx, lru_bx=lru_bx, lru_lam=lru_lam, w_gate=w_gate, b_gate=b_gate, w_branch=w_branch, w_out=w_out, ln_ffn2=ln_ffn2, ffn2_wi=ffn2_wi, ffn2_wo=ffn2_wo, ln_final=ln_final, loss_target=loss_target, m_meta=m_meta, m_ln_ffn1=m_ln_ffn1, m_ffn1_wi=m_ffn1_wi, m_ffn1_wo=m_ffn1_wo, m_ln_mix=m_ln_mix, m_w_in=m_w_in, m_fox_bf=m_fox_bf, m_mla_gq=m_mla_gq, m_mla_wq=m_mla_wq, m_mla_gkv=m_mla_gkv, m_mla_wkv=m_mla_wkv, m_gdn_conv=m_gdn_conv, m_gdn_alog=m_gdn_alog, m_gdn_dtb=m_gdn_dtb, m_gdn_gon=m_gdn_gon, m_lru_conv=m_lru_conv, m_lru_conv_b=m_lru_conv_b, m_lru_wa=m_lru_wa, m_lru_ba=m_lru_ba, m_lru_wx=m_lru_wx, m_lru_bx=m_lru_bx, m_lru_lam=m_lru_lam, m_w_gate=m_w_gate, m_b_gate=m_b_gate, m_w_branch=m_w_branch, m_w_out=m_w_out, m_ln_ffn2=m_ln_ffn2, m_ffn2_wi=m_ffn2_wi, m_ffn2_wo=m_ffn2_wo, m_ln_final=m_ln_final, v_meta=v_meta, v_ln_ffn1=v_ln_ffn1, v_ffn1_wi=v_ffn1_wi, v_ffn1_wo=v_ffn1_wo, v_ln_mix=v_ln_mix, v_w_in=v_w_in, v_fox_bf=v_fox_bf, v_mla_gq=v_mla_gq, v_mla_wq=v_mla_wq, v_mla_gkv=v_mla_gkv, v_mla_wkv=v_mla_wkv, v_gdn_conv=v_gdn_conv, v_gdn_alog=v_gdn_alog, v_gdn_dtb=v_gdn_dtb, v_gdn_gon=v_gdn_gon, v_lru_conv=v_lru_conv, v_lru_conv_b=v_lru_conv_b, v_lru_wa=v_lru_wa, v_lru_ba=v_lru_ba, v_lru_wx=v_lru_wx, v_lru_bx=v_lru_bx, v_lru_lam=v_lru_lam, v_w_gate=v_w_gate, v_b_gate=v_b_gate, v_w_branch=v_w_branch, v_w_out=v_w_out, v_ln_ffn2=v_ln_ffn2, v_ffn2_wi=v_ffn2_wi, v_ffn2_wo=v_ffn2_wo, v_ln_final=v_ln_final)
    weights = {n: given[n] for n in TWIN_WEIGHTS}
    shared = {n: given[n] for n in SHARED_INPUTS}
    per_example = {n: given[n] for n in ['x']}
    grad_fn = _jax.value_and_grad(_loss, argnums=(0, 1))

    def one_microbatch(ex, loss_target):
        ex = dict(ex)
        diff = ex.pop(TWIN_DIFF_INPUT)
        return grad_fn(weights, diff, {**shared, **ex}, loss_target)

    if N_MICROBATCH == 1:
        loss, (grad_w, grad_x) = one_microbatch(per_example, given["loss_target"])
    else:
        def body(carry, xs):
            loss_sum, grad_sum = carry
            l_k, (gw_k, gx_k) = one_microbatch(xs[0], xs[1])
            with _jax.named_scope("update"):
                return (loss_sum + l_k, _jax.tree.map(_jnp.add, grad_sum, gw_k)), gx_k

        init = (_jnp.zeros((), _jnp.float32), _jax.tree.map(_jnp.zeros_like, weights))
        (loss, grad_w), grad_x = _jax.lax.scan(body, init, (per_example, given["loss_target"]))
    with _jax.named_scope("update"):
        delta_w, new_m, new_v = {}, {}, {}
        for n in TWIN_WEIGHTS:
            delta_w[n], new_m[n], new_v[n] = _adamw(weights[n], grad_w[n], given["m_" + n], given["v_" + n])
    return (loss, grad_x, *[grad_w[n] for n in TWIN_WEIGHTS], *[delta_w[n] for n in TWIN_WEIGHTS],
            *[new_m[n] for n in TWIN_WEIGHTS], *[new_v[n] for n in TWIN_WEIGHTS])
```

```python
import functools
import math

import jax
import jax.numpy as jnp
from jax import lax
from jax.experimental import pallas as pl
from jax.experimental.pallas import tpu as pltpu

F32 = jnp.float32
BF16 = jnp.bfloat16

N_META = 16
BLOCK = 128
PAD_LEN = BLOCK - N_META
EPS = 1e-6
NEG_INF = -1e30
N_BRANCH = 4
FOX_HEADS, FOX_DH = 4, 64
MLA_HEADS, MLA_NOPE, MLA_ROPE, MLA_DV = 4, 64, 32, 64
MLA_Q_RANK, MLA_KV_RANK = 192, 128
ROPE_BASE = 10000.0
GDN_HEADS, GDN_DK, GDN_DV, GDN_CHUNK = 4, 64, 64, 64
LRU_WIDTH, LRU_BLOCKS, LRU_C = 256, 4, 8.0
DEPTH = 2

OFF_FOX_QKV = 0
OFF_FOX_F = OFF_FOX_QKV + 3 * FOX_HEADS * FOX_DH
OFF_MLA_CQ = OFF_FOX_F + FOX_HEADS
OFF_MLA_CKV = OFF_MLA_CQ + MLA_Q_RANK
OFF_MLA_KR = OFF_MLA_CKV + MLA_KV_RANK
OFF_GDN_QKV = OFF_MLA_KR + MLA_ROPE
OFF_GDN_A = OFF_GDN_QKV + GDN_HEADS * (2 * GDN_DK + GDN_DV)
OFF_GDN_B = OFF_GDN_A + GDN_HEADS
OFF_GDN_G = OFF_GDN_B + GDN_HEADS
OFF_LRU = OFF_GDN_G + GDN_HEADS * GDN_DV
N_IN = OFF_LRU + LRU_WIDTH

ADAM_LR, ADAM_B1, ADAM_B2, ADAM_EPS, ADAM_WD, ADAM_STEP = 0.001, 0.9, 0.999, 1e-08, 0.01, 10

WEIGHT_SPECS = (
    ("meta", 1), ("ln_ffn1", None), ("ffn1_wi", 2), ("ffn1_wo", 1), ("ln_mix", None), ("w_in", 2),
    ("fox_bf", None), ("mla_gq", None), ("mla_wq", 2), ("mla_gkv", None), ("mla_wkv", 2),
    ("gdn_conv", 2), ("gdn_alog", None), ("gdn_dtb", None), ("gdn_gon", None), ("lru_conv", 2),
    ("lru_conv_b", None), ("lru_wa", None), ("lru_ba", None), ("lru_wx", None), ("lru_bx", None),
    ("lru_lam", None), ("w_gate", 2), ("b_gate", 2), ("w_branch", 3), ("w_out", 1),
    ("ln_ffn2", None), ("ffn2_wi", 2), ("ffn2_wo", 1), ("ln_final", None),
)
N_CHIPS = 4
LARGE = ("ffn1_wi", "ffn1_wo", "w_in", "w_gate", "w_branch", "w_out", "ffn2_wi", "ffn2_wo")

LANES = 128
VMEM_LIMIT_BYTES = 48 * 1024 * 1024
FLAT_COLS = 512
FLAT_ROW_ALIGN = 64


def _pcall(body, **kw):
    return pl.pallas_call(body, **kw)


def _pick(n, cands):
    for c in cands:
        if n % c == 0:
            return c
    return n


_DN = {"nn": (((1,), (0,)), ((), ())), "nt": (((1,), (1,)), ((), ())), "tn": (((0,), (0,)), ((), ()))}


def _matmul(a, b, mode, name):
    if mode == "nn":
        (M, K), (_, N) = a.shape, b.shape
    elif mode == "nt":
        (M, K), (N, _) = a.shape, b.shape
    else:
        (K, M), (_, N) = a.shape, b.shape
    tm = _pick(M, (768, 512, 384, 256, 128, 64, 32, 16, 8))
    tn = _pick(N, (512, 256, 128))
    tk = K if K <= 1024 else _pick(K, (1024, 768, 512, 256, 128))
    nk = K // tk
    a_spec = {"nn": pl.BlockSpec((tm, tk), lambda i, j, k: (i, k)),
              "nt": pl.BlockSpec((tm, tk), lambda i, j, k: (i, k)),
              "tn": pl.BlockSpec((tk, tm), lambda i, j, k: (k, i))}[mode]
    b_spec = {"nn": pl.BlockSpec((tk, tn), lambda i, j, k: (k, j)),
              "nt": pl.BlockSpec((tn, tk), lambda i, j, k: (j, k)),
              "tn": pl.BlockSpec((tk, tn), lambda i, j, k: (k, j))}[mode]
    dn = _DN[mode]

    def body(a_ref, b_ref, o_ref, acc_ref):
        k = pl.program_id(2)
        part = lax.dot_general(a_ref[...].astype(BF16), b_ref[...].astype(BF16), dn,
                               preferred_element_type=F32)

        @pl.when(k == 0)
        def _():
            acc_ref[...] = part

        @pl.when(k > 0)
        def _():
            acc_ref[...] += part

        @pl.when(k == nk - 1)
        def _():
            o_ref[...] = acc_ref[...]

    return _pcall(
        body, name=name, grid=(M // tm, N // tn, nk),
        in_specs=[a_spec, b_spec], out_specs=pl.BlockSpec((tm, tn), lambda i, j, k: (i, j)),
        out_shape=jax.ShapeDtypeStruct((M, N), F32),
        scratch_shapes=[pltpu.VMEM((tm, tn), F32)],
        compiler_params=pltpu.CompilerParams(
            dimension_semantics=("parallel", "parallel", "arbitrary"),
            vmem_limit_bytes=VMEM_LIMIT_BYTES),
    )(a, b)


@jax.custom_vjp
def dense(a, w):
    return _matmul(a, w, "nn", "dense_fwd")


def _dense_fwd(a, w):
    return _matmul(a, w, "nn", "dense_fwd"), (a, w)


def _dense_bwd(res, g):
    a, w = res
    return _matmul(g, w, "nt", "dense_dgrad"), _matmul(a, g, "tn", "dense_wgrad")


dense.defvjp(_dense_fwd, _dense_bwd)


def dense_nd(a, w):
    K, N = w.shape
    pad = (-N) % (4 * LANES if N > 4 * LANES else LANES)
    if pad:
        w = jnp.pad(w, ((0, 0), (0, pad)))
    out = dense(a.reshape(-1, K), w)
    if pad:
        out = out[:, :N]
    return out.reshape(a.shape[:-1] + (N,))


_NT = (((1,), (1,)), ((), ()))


def _att_blk(T):
    return _pick(T, (384, 256, 128))


def _attn_fwd_call(q, k, v, cum_col, cum_row, scale):
    BH, T, dk = q.shape
    dv = v.shape[2]
    blk = _att_blk(T)
    n = T // blk
    has_cum = cum_col is not None

    def body(*refs):
        if has_cum:
            q_ref, k_ref, v_ref, cq_ref, ck_ref, o_ref, lse_ref, m_s, l_s, acc_s = refs
        else:
            q_ref, k_ref, v_ref, o_ref, lse_ref, m_s, l_s, acc_s = refs
        i = pl.program_id(1)
        j = pl.program_id(2)

        @pl.when(j == 0)
        def _():
            m_s[...] = jnp.full_like(m_s, NEG_INF)
            l_s[...] = jnp.zeros_like(l_s)
            acc_s[...] = jnp.zeros_like(acc_s)

        @pl.when(j <= i)
        def _():
            s = lax.dot_general(q_ref[0].astype(BF16), k_ref[0].astype(BF16), _NT,
                                preferred_element_type=F32) * scale
            if has_cum:
                s = s + cq_ref[0] - ck_ref[0]
            qpos = i * blk + lax.broadcasted_iota(jnp.int32, (blk, blk), 0)
            kpos = j * blk + lax.broadcasted_iota(jnp.int32, (blk, blk), 1)
            mask = (kpos <= qpos) & (kpos >= PAD_LEN)
            s = jnp.where(mask, s, NEG_INF)
            m_prev = m_s[...]
            m_new = jnp.maximum(m_prev, jnp.max(s, axis=1, keepdims=True))
            p = jnp.exp(s - m_new)
            alpha = jnp.exp(m_prev - m_new)
            l_s[...] = alpha * l_s[...] + jnp.sum(p, axis=1, keepdims=True)
            acc_s[...] = alpha * acc_s[...] + jnp.dot(p.astype(BF16), v_ref[0].astype(BF16),
                                                      preferred_element_type=F32)
            m_s[...] = m_new

        @pl.when(j == n - 1)
        def _():
            o_ref[0] = acc_s[...] / l_s[...]
            lse_ref[0] = m_s[...] + jnp.log(l_s[...])

    kv_idx = lambda b, i, j: (b, jnp.minimum(i, j), 0)
    in_specs = [pl.BlockSpec((1, blk, dk), lambda b, i, j: (b, i, 0)),
                pl.BlockSpec((1, blk, dk), kv_idx),
                pl.BlockSpec((1, blk, dv), kv_idx)]
    args = [q, k, v]
    if has_cum:
        in_specs += [pl.BlockSpec((1, blk, 1), lambda b, i, j: (b, i, 0)),
                     pl.BlockSpec((1, 1, blk), lambda b, i, j: (b, 0, jnp.minimum(i, j)))]
        args += [cum_col, cum_row]
    return _pcall(
        body, name="attn_fwd_cum" if has_cum else "attn_fwd", grid=(BH, n, n),
        in_specs=in_specs,
        out_specs=[pl.BlockSpec((1, blk, dv), lambda b, i, j: (b, i, 0)),
                   pl.BlockSpec((1, blk, 1), lambda b, i, j: (b, i, 0))],
        out_shape=[jax.ShapeDtypeStruct((BH, T, dv), F32), jax.ShapeDtypeStruct((BH, T, 1), F32)],
        scratch_shapes=[pltpu.VMEM((blk, 1), F32), pltpu.VMEM((blk, 1), F32), pltpu.VMEM((blk, dv), F32)],
        compiler_params=pltpu.CompilerParams(
            dimension_semantics=("parallel", "parallel", "arbitrary"),
            vmem_limit_bytes=VMEM_LIMIT_BYTES),
    )(*args)


def _attn_dkv_call(q, k, v, do, lse_row, delta_row, cum_row, cum_col, scale):
    BH, T, dk = q.shape
    dv = v.shape[2]
    blk = _att_blk(T)
    n = T // blk
    has_cum = cum_row is not None

    def body(*refs):
        if has_cum:
            (k_ref, v_ref, q_ref, do_ref, lse_ref, dl_ref, cq_ref, ck_ref,
             dk_ref, dv_ref, dc_ref, dk_s, dv_s, dc_s) = refs
        else:
            k_ref, v_ref, q_ref, do_ref, lse_ref, dl_ref, dk_ref, dv_ref, dk_s, dv_s = refs
        j = pl.program_id(1)
        i = pl.program_id(2)

        @pl.when(i == 0)
        def _():
            dk_s[...] = jnp.zeros_like(dk_s)
            dv_s[...] = jnp.zeros_like(dv_s)
            if has_cum:
                dc_s[...] = jnp.zeros_like(dc_s)

        @pl.when(i >= j)
        def _():
            kb = k_ref[0].astype(BF16)
            qb = q_ref[0].astype(BF16)
            dob = do_ref[0].astype(BF16)
            st = lax.dot_general(kb, qb, _NT, preferred_element_type=F32) * scale
            if has_cum:
                st = st + cq_ref[0] - ck_ref[0]
            kpos = j * blk + lax.broadcasted_iota(jnp.int32, (blk, blk), 0)
            qpos = i * blk + lax.broadcasted_iota(jnp.int32, (blk, blk), 1)
            mask = (kpos <= qpos) & (kpos >= PAD_LEN)
            pt = jnp.where(mask, jnp.exp(jnp.minimum(st - lse_ref[0], 0.0)), 0.0)
            dv_s[...] += jnp.dot(pt.astype(BF16), dob, preferred_element_type=F32)
            dpt = lax.dot_general(v_ref[0].astype(BF16), dob, _NT, preferred_element_type=F32)
            dst = pt * (dpt - dl_ref[0])
            dk_s[...] += jnp.dot(dst.astype(BF16), qb, preferred_element_type=F32) * scale
            if has_cum:
                dc_s[...] -= jnp.sum(dst, axis=1, keepdims=True)

        @pl.when(i == n - 1)
        def _():
            dk_ref[0] = dk_s[...]
            dv_ref[0] = dv_s[...]
            if has_cum:
                dc_ref[0] = dc_s[...]

    q_idx = lambda b, j, i: (b, jnp.maximum(i, j), 0)
    row_idx = lambda b, j, i: (b, 0, jnp.maximum(i, j))
    in_specs = [pl.BlockSpec((1, blk, dk), lambda b, j, i: (b, j, 0)),
                pl.BlockSpec((1, blk, dv), lambda b, j, i: (b, j, 0)),
                pl.BlockSpec((1, blk, dk), q_idx),
                pl.BlockSpec((1, blk, dv), q_idx),
                pl.BlockSpec((1, 1, blk), row_idx),
                pl.BlockSpec((1, 1, blk), row_idx)]
    args = [k, v, q, do, lse_row, delta_row]
    out_specs = [pl.BlockSpec((1, blk, dk), lambda b, j, i: (b, j, 0)),
                 pl.BlockSpec((1, blk, dv), lambda b, j, i: (b, j, 0))]
    out_shape = [jax.ShapeDtypeStruct((BH, T, dk), F32), jax.ShapeDtypeStruct((BH, T, dv), F32)]
    scratch = [pltpu.VMEM((blk, dk), F32), pltpu.VMEM((blk, dv), F32)]
    if has_cum:
        in_specs += [pl.BlockSpec((1, 1, blk), row_idx),
                     pl.BlockSpec((1, blk, 1), lambda b, j, i: (b, j, 0))]
        args += [cum_row, cum_col]
        out_specs.append(pl.BlockSpec((1, blk, 1), lambda b, j, i: (b, j, 0)))
        out_shape.append(jax.ShapeDtypeStruct((BH, T, 1), F32))
        scratch.append(pltpu.VMEM((blk, 1), F32))
    return _pcall(
        body, name="attn_dkv_cum" if has_cum else "attn_dkv", grid=(BH, n, n),
        in_specs=in_specs, out_specs=out_specs, out_shape=out_shape, scratch_shapes=scratch,
        compiler_params=pltpu.CompilerParams(
            dimension_semantics=("parallel", "parallel", "arbitrary"),
            vmem_limit_bytes=VMEM_LIMIT_BYTES),
    )(*args)


def _attn_dq_call(q, k, v, do, lse_col, delta_col, cum_col, cum_row, scale):
    BH, T, dk = q.shape
    dv = v.shape[2]
    blk = _att_blk(T)
    n = T // blk
    has_cum = cum_col is not None

    def body(*refs):
        if has_cum:
            q_ref, k_ref, v_ref, do_ref, lse_ref, dl_ref, cq_ref, ck_ref, dq_ref, dc_ref, dq_s, dc_s = refs
        else:
            q_ref, k_ref, v_ref, do_ref, lse_ref, dl_ref, dq_ref, dq_s = refs
        i = pl.program_id(1)
        j = pl.program_id(2)

        @pl.when(j == 0)
        def _():
            dq_s[...] = jnp.zeros_like(dq_s)
            if has_cum:
                dc_s[...] = jnp.zeros_like(dc_s)

        @pl.when(j <= i)
        def _():
            kb = k_ref[0].astype(BF16)
            s = lax.dot_general(q_ref[0].astype(BF16), kb, _NT, preferred_element_type=F32) * scale
            if has_cum:
                s = s + cq_ref[0] - ck_ref[0]
            qpos = i * blk + lax.broadcasted_iota(jnp.int32, (blk, blk), 0)
            kpos = j * blk + lax.broadcasted_iota(jnp.int32, (blk, blk), 1)
            mask = (kpos <= qpos) & (kpos >= PAD_LEN)
            p = jnp.where(mask, jnp.exp(jnp.minimum(s - lse_ref[0], 0.0)), 0.0)
            dp = lax.dot_general(do_ref[0].astype(BF16), v_ref[0].astype(BF16), _NT,
                                 preferred_element_type=F32)
            ds = p * (dp - dl_ref[0])
            dq_s[...] += jnp.dot(ds.astype(BF16), kb, preferred_element_type=F32) * scale
            if has_cum:
                dc_s[...] += jnp.sum(ds, axis=1, keepdims=True)

        @pl.when(j == n - 1)
        def _():
            dq_ref[0] = dq_s[...]
            if has_cum:
                dc_ref[0] = dc_s[...]

    kv_idx = lambda b, i, j: (b, jnp.minimum(i, j), 0)
    q_idx = lambda b, i, j: (b, i, 0)
    in_specs = [pl.BlockSpec((1, blk, dk), q_idx), pl.BlockSpec((1, blk, dk), kv_idx),
                pl.BlockSpec((1, blk, dv), kv_idx), pl.BlockSpec((1, blk, dv), q_idx),
                pl.BlockSpec((1, blk, 1), q_idx), pl.BlockSpec((1, blk, 1), q_idx)]
    args = [q, k, v, do, lse_col, delta_col]
    if has_cum:
        in_specs += [pl.BlockSpec((1, blk, 1), q_idx),
                     pl.BlockSpec((1, 1, blk), lambda b, i, j: (b, 0, jnp.minimum(i, j)))]
        args += [cum_col, cum_row]
    out_specs = [pl.BlockSpec((1, blk, dk), q_idx)]
    out_shape = [jax.ShapeDtypeStruct((BH, T, dk), F32)]
    scratch = [pltpu.VMEM((blk, dk), F32)]
    if has_cum:
        out_specs.append(pl.BlockSpec((1, blk, 1), q_idx))
        out_shape.append(jax.ShapeDtypeStruct((BH, T, 1), F32))
        scratch.append(pltpu.VMEM((blk, 1), F32))
    return _pcall(
        body, name="attn_dq_cum" if has_cum else "attn_dq", grid=(BH, n, n),
        in_specs=in_specs, out_specs=out_specs, out_shape=out_shape, scratch_shapes=scratch,
        compiler_params=pltpu.CompilerParams(
            dimension_semantics=("parallel", "parallel", "arbitrary"),
            vmem_limit_bytes=VMEM_LIMIT_BYTES),
    )(*args)


def _make_attention(scale, has_cum):
    def fold(t):
        return t.reshape((-1,) + t.shape[2:])

    def run_fwd(q, k, v, cum):
        B, H, T, _ = q.shape
        col = cum.reshape(B * H, T, 1) if has_cum else None
        row = cum.reshape(B * H, 1, T) if has_cum else None
        o, lse = _attn_fwd_call(fold(q), fold(k), fold(v), col, row, scale)
        return o.reshape(B, H, T, -1), lse

    @jax.custom_vjp
    def attn(q, k, v, cum):
        return run_fwd(q, k, v, cum)[0]

    def attn_fwd(q, k, v, cum):
        o, lse = run_fwd(q, k, v, cum)
        return o, (q, k, v, cum, o, lse)

    def attn_bwd(res, do):
        q, k, v, cum, o, lse = res
        B, H, T, _ = q.shape
        delta = jnp.sum(do * o, axis=-1).reshape(B * H, T, 1)
        col = cum.reshape(B * H, T, 1) if has_cum else None
        row = cum.reshape(B * H, 1, T) if has_cum else None
        qf, kf, vf, dof = fold(q), fold(k), fold(v), fold(do)
        outs = _attn_dkv_call(qf, kf, vf, dof, lse.reshape(B * H, 1, T), delta.reshape(B * H, 1, T),
                              row, col, scale)
        dqs = _attn_dq_call(qf, kf, vf, dof, lse, delta, col, row, scale)
        dcum = (outs[2] + dqs[1]).reshape(B, H, T) if has_cum else jnp.zeros_like(cum)
        return (dqs[0].reshape(q.shape), outs[0].reshape(k.shape), outs[1].reshape(v.shape), dcum)

    attn.defvjp(attn_fwd, attn_bwd)
    return attn


def _rmsnorm(x, g):
    return x * lax.rsqrt(jnp.mean(x * x, axis=-1, keepdims=True) + EPS) * g


def _l2norm(x):
    return x * lax.rsqrt(jnp.sum(x * x, axis=-1, keepdims=True) + EPS)


def _row_tile(rows, row_bytes, budget):
    for t in (2048, 1024, 512, 256, 128, 64, 32, 16, 8):
        if rows % t == 0 and t * row_bytes <= budget:
            return t
    return rows


ROW_BLOCK_BYTES = 2 * 1024 * 1024


def _rms_fwd_call(x, g):
    M, D = x.shape
    tr = _row_tile(M, 4 * D, ROW_BLOCK_BYTES)

    def body(x_ref, g_ref, y_ref, r_ref):
        xv = x_ref[...]
        r = lax.rsqrt(jnp.mean(xv * xv, axis=-1, keepdims=True) + EPS)
        y_ref[...] = xv * r * g_ref[...]
        r_ref[...] = r

    return _pcall(
        body, name="rmsnorm_fwd", grid=(M // tr,),
        in_specs=[pl.BlockSpec((tr, D), lambda i: (i, 0)), pl.BlockSpec((1, D), lambda i: (0, 0))],
        out_specs=[pl.BlockSpec((tr, D), lambda i: (i, 0)), pl.BlockSpec((tr, 1), lambda i: (i, 0))],
        out_shape=[jax.ShapeDtypeStruct((M, D), F32), jax.ShapeDtypeStruct((M, 1), F32)],
        compiler_params=pltpu.CompilerParams(dimension_semantics=("parallel",)),
    )(x, g)


def _rms_bwd_call(x, g, r, dy):
    M, D = x.shape
    tr = _row_tile(M, 4 * D, ROW_BLOCK_BYTES)

    def body(x_ref, g_ref, r_ref, dy_ref, dx_ref, dg_ref):
        i = pl.program_id(0)
        rv = r_ref[...]
        xh = x_ref[...] * rv
        dyv = dy_ref[...]
        dyg = dyv * g_ref[...]
        dx_ref[...] = rv * (dyg - xh * jnp.mean(dyg * xh, axis=-1, keepdims=True))
        part = jnp.sum(dyv * xh, axis=0, keepdims=True)

        @pl.when(i == 0)
        def _():
            dg_ref[...] = part

        @pl.when(i > 0)
        def _():
            dg_ref[...] += part

    row = pl.BlockSpec((tr, D), lambda i: (i, 0))
    return _pcall(
        body, name="rmsnorm_bwd", grid=(M // tr,),
        in_specs=[row, pl.BlockSpec((1, D), lambda i: (0, 0)), pl.BlockSpec((tr, 1), lambda i: (i, 0)), row],
        out_specs=[row, pl.BlockSpec((1, D), lambda i: (0, 0))],
        out_shape=[jax.ShapeDtypeStruct((M, D), F32), jax.ShapeDtypeStruct((1, D), F32)],
        compiler_params=pltpu.CompilerParams(dimension_semantics=("arbitrary",)),
    )(x, g, r, dy)


@jax.custom_vjp
def _rmsnorm2d(x, g):
    return _rms_fwd_call(x, g)[0]


def _rmsnorm2d_fwd(x, g):
    y, r = _rms_fwd_call(x, g)
    return y, (x, g, r)


def _rmsnorm2d_bwd(res, dy):
    x, g, r = res
    return _rms_bwd_call(x, g, r, dy)


_rmsnorm2d.defvjp(_rmsnorm2d_fwd, _rmsnorm2d_bwd)


def _rmsnorm_rows(x, g):
    D = x.shape[-1]
    return _rmsnorm2d(x.reshape(-1, D), g.reshape(1, D)).reshape(x.shape)


def _glu_fwd_call(gu):
    M, F2 = gu.shape
    F = F2 // 2
    tr = _row_tile(M, 4 * F2, 2 * ROW_BLOCK_BYTES)

    def body(g_ref, u_ref, o_ref):
        gv = g_ref[...]
        o_ref[...] = gv * jax.nn.sigmoid(gv) * u_ref[...]

    return _pcall(
        body, name="swiglu_fwd", grid=(M // tr,),
        in_specs=[pl.BlockSpec((tr, F), lambda i: (i, 0)), pl.BlockSpec((tr, F), lambda i: (i, 1))],
        out_specs=pl.BlockSpec((tr, F), lambda i: (i, 0)),
        out_shape=jax.ShapeDtypeStruct((M, F), F32),
        compiler_params=pltpu.CompilerParams(dimension_semantics=("parallel",),
                                             vmem_limit_bytes=VMEM_LIMIT_BYTES),
    )(gu, gu)


def _glu_bwd_call(gu, da):
    M, F2 = gu.shape
    F = F2 // 2
    tr = _row_tile(M, 4 * F2, 2 * ROW_BLOCK_BYTES)

    def body(g_ref, u_ref, da_ref, o_ref):
        gv = g_ref[...]
        s = jax.nn.sigmoid(gv)
        dav = da_ref[...]
        o_ref[:, :F] = dav * u_ref[...] * (s * (1.0 + gv * (1.0 - s)))
        o_ref[:, F:] = dav * (gv * s)

    return _pcall(
        body, name="swiglu_bwd", grid=(M // tr,),
        in_specs=[pl.BlockSpec((tr, F), lambda i: (i, 0)), pl.BlockSpec((tr, F), lambda i: (i, 1)),
                  pl.BlockSpec((tr, F), lambda i: (i, 0))],
        out_specs=pl.BlockSpec((tr, F2), lambda i: (i, 0)),
        out_shape=jax.ShapeDtypeStruct((M, F2), F32),
        compiler_params=pltpu.CompilerParams(dimension_semantics=("parallel",),
                                             vmem_limit_bytes=VMEM_LIMIT_BYTES),
    )(gu, gu, da)


@jax.custom_vjp
def _glu(gu):
    return _glu_fwd_call(gu)


def _glu_fwd(gu):
    return _glu_fwd_call(gu), gu


def _glu_bwd(gu, da):
    return (_glu_bwd_call(gu, da),)


_glu.defvjp(_glu_fwd, _glu_bwd)


def _swiglu(h, wi, wo):
    lead, D = h.shape[:-1], h.shape[-1]
    gu = dense(h.reshape(-1, D), wi)
    return dense(_glu(gu), wo).reshape(lead + (wo.shape[1],))


def _causal_dwconv(x, w):
    K, C = w.shape
    return lax.conv_general_dilated(
        x, w[:, None, :], window_strides=(1,), padding=[(K - 1, 0)],
        dimension_numbers=("NWC", "WIO", "NWC"), feature_group_count=C)


def _rope(x, cos, sin):
    half = x.shape[-1] // 2
    x1, x2 = x[..., :half], x[..., half:]
    return jnp.concatenate([x1 * cos - x2 * sin, x2 * cos + x1 * sin], axis=-1)


def _fox_branch(p, b_f):
    B, T, _ = p.shape
    qkv = p[..., OFF_FOX_QKV:OFF_FOX_F].reshape(B, T, 3, FOX_HEADS, FOX_DH)
    q = qkv[:, :, 0].transpose(0, 2, 1, 3)
    k = qkv[:, :, 1].transpose(0, 2, 1, 3)
    v = qkv[:, :, 2].transpose(0, 2, 1, 3)
    log_f = jax.nn.log_sigmoid(p[..., OFF_FOX_F:OFF_MLA_CQ] + b_f)
    cum = jnp.cumsum(log_f, axis=1).transpose(0, 2, 1)
    o = _make_attention(FOX_DH ** -0.5, True)(q, k, v, cum)
    return o.transpose(0, 2, 1, 3).reshape(B, T, FOX_HEADS * FOX_DH)


def _mla_branch(p, g_qn, w_q_up, g_kvn, w_kv_up, cos, sin):
    B, T, _ = p.shape
    cq = _rmsnorm(p[..., OFF_MLA_CQ:OFF_MLA_CKV], g_qn)
    q = (cq @ w_q_up).reshape(B, T, MLA_HEADS, MLA_NOPE + MLA_ROPE)
    ckv = _rmsnorm(p[..., OFF_MLA_CKV:OFF_MLA_KR], g_kvn)
    kv = (ckv @ w_kv_up).reshape(B, T, MLA_HEADS, MLA_NOPE + MLA_DV)
    k_rope = _rope(p[..., OFF_MLA_KR:OFF_GDN_QKV], cos, sin)
    q_rope = _rope(q[..., MLA_NOPE:], cos[:, None], sin[:, None])
    q = jnp.concatenate([q[..., :MLA_NOPE], q_rope], axis=-1)
    k = jnp.concatenate([kv[..., :MLA_NOPE],
                         jnp.broadcast_to(k_rope[:, :, None], (B, T, MLA_HEADS, MLA_ROPE))], axis=-1)
    v = kv[..., MLA_NOPE:]
    o = _make_attention((MLA_NOPE + MLA_ROPE) ** -0.5, False)(
        q.transpose(0, 2, 1, 3), k.transpose(0, 2, 1, 3), v.transpose(0, 2, 1, 3),
        jnp.zeros((B, MLA_HEADS, T), F32))
    return o.transpose(0, 2, 1, 3).reshape(B, T, MLA_HEADS * MLA_DV)


def _gdn_branch(p, conv_w, a_log, dt_bias, g_on):
    B, T, _ = p.shape
    H, DK, DV, C = GDN_HEADS, GDN_DK, GDN_DV, GDN_CHUNK
    qkv = jax.nn.silu(_causal_dwconv(p[..., OFF_GDN_QKV:OFF_GDN_A], conv_w))
    q = _l2norm(qkv[..., :H * DK].reshape(B, T, H, DK)) * DK ** -0.5
    k = _l2norm(qkv[..., H * DK:2 * H * DK].reshape(B, T, H, DK))
    v = qkv[..., 2 * H * DK:].reshape(B, T, H, DV)
    beta = jax.nn.sigmoid(p[..., OFF_GDN_B:OFF_GDN_G])
    g = -jnp.exp(a_log) * jax.nn.softplus(p[..., OFF_GDN_A:OFF_GDN_B] + dt_bias)
    nc = T // C

    def chunks(t):
        return jnp.moveaxis(t, 2, 1).reshape((B, H, nc, C) + t.shape[3:])

    q, k, v, beta, g = chunks(q), chunks(k), chunks(v), chunks(beta), chunks(g)
    G = jnp.cumsum(g, axis=-1)
    idx = jnp.arange(C)
    strict = idx[:, None] > idx[None, :]
    incl = idx[:, None] >= idx[None, :]
    decay = jnp.exp(jnp.where(incl, G[..., :, None] - G[..., None, :], NEG_INF))
    kb = k * beta[..., None]
    vb = v * beta[..., None]
    m = jnp.eye(C, dtype=F32) + jnp.where(
        strict, jnp.einsum("bhnik,bhnjk->bhnij", kb, k) * decay, 0.0)
    rhs = jnp.concatenate([kb * jnp.exp(G)[..., None], vb], axis=-1)
    sol = lax.linalg.triangular_solve(m, rhs, left_side=True, lower=True, unit_diagonal=True)
    w, u = sol[..., :DK], sol[..., DK:]
    qk = jnp.where(incl, jnp.einsum("bhnik,bhnjk->bhnij", q, k) * decay, 0.0)
    q_dec = q * jnp.exp(G)[..., None]
    k_dec = k * jnp.exp(G[..., -1:] - G)[..., None]
    g_last = jnp.exp(G[..., -1])
    xs = (jnp.moveaxis(q_dec, 2, 0), jnp.moveaxis(k_dec, 2, 0), jnp.moveaxis(w, 2, 0),
          jnp.moveaxis(u, 2, 0), jnp.moveaxis(qk, 2, 0), jnp.moveaxis(g_last, 2, 0))

    def step(S, inp):
        q_c, k_c, w_c, u_c, qk_c, gl_c = inp
        v_new = u_c - jnp.einsum("bhck,bhkv->bhcv", w_c, S)
        o_c = jnp.einsum("bhck,bhkv->bhcv", q_c, S) + jnp.einsum("bhij,bhjv->bhiv", qk_c, v_new)
        S = S * gl_c[..., None, None] + jnp.einsum("bhck,bhcv->bhkv", k_c, v_new)
        return S, o_c

    S0 = jnp.zeros((B, H, DK, DV), F32)
    _, o = lax.scan(step, S0, xs)
    o = jnp.moveaxis(o, 0, 2).reshape(B, H, T, DV).transpose(0, 2, 1, 3)
    gate = jax.nn.silu(p[..., OFF_GDN_G:OFF_LRU]).reshape(B, T, H, DV)
    o = _rmsnorm(o, g_on) * gate
    return o.reshape(B, T, H * DV)


def _rglru_branch(p, valid, conv_w, conv_b, w_a, b_a, w_x, b_x, lam):
    B, T, _ = p.shape
    xr = _causal_dwconv(p[..., OFF_LRU:N_IN], conv_w) + conv_b
    xr = jnp.where(valid[None, :, None], xr, 0)
    xb = xr.reshape(B, T, LRU_BLOCKS, LRU_WIDTH // LRU_BLOCKS)
    r = jax.nn.sigmoid(jnp.einsum("btni,nij->btnj", xb, w_a).reshape(B, T, LRU_WIDTH) + b_a)
    ig = jax.nn.sigmoid(jnp.einsum("btni,nij->btnj", xb, w_x).reshape(B, T, LRU_WIDTH) + b_x)
    log_a = -LRU_C * r * jax.nn.softplus(-lam)
    a = jnp.exp(log_a)
    b = jnp.sqrt(-jnp.expm1(2.0 * log_a)) * ig * xr

    def combine(e1, e2):
        return (e1[0] * e2[0], e2[0] * e1[1] + e2[1])

    _, h = lax.associative_scan(combine, (a, b), axis=1)
    return h


def _mixer(u, valid, cos, sin, w, l):
    p = dense_nd(u, w["w_in"][l])
    ys = (_fox_branch(p, w["fox_bf"][l]),
          _mla_branch(p, w["mla_gq"][l], w["mla_wq"][l], w["mla_gkv"][l], w["mla_wkv"][l], cos, sin),
          _gdn_branch(p, w["gdn_conv"][l], w["gdn_alog"][l], w["gdn_dtb"][l], w["gdn_gon"][l]),
          _rglru_branch(p, valid, w["lru_conv"][l], w["lru_conv_b"][l], w["lru_wa"][l], w["lru_ba"][l],
                        w["lru_wx"][l], w["lru_bx"][l], w["lru_lam"][l]))
    D = u.shape[-1]
    w_gate = jnp.moveaxis(w["w_gate"][l], 0, 1).reshape(D, N_BRANCH * D)
    gates = jax.nn.sigmoid(dense_nd(u, w_gate) + w["b_gate"][l].reshape(-1))
    merged = gates[..., :D] * dense_nd(ys[0], w["w_branch"][l, 0])
    for n in range(1, N_BRANCH):
        merged = merged + gates[..., n * D:(n + 1) * D] * dense_nd(ys[n], w["w_branch"][l, n])
    return dense_nd(merged, w["w_out"][l])


def _local_loss(w, x, loss_target):
    B, S, D = x.shape
    T = BLOCK + S
    h = jnp.concatenate([jnp.zeros((B, PAD_LEN, D), F32),
                         jnp.broadcast_to(w["meta"][None], (B, N_META, D)), x], axis=1)
    pos = jnp.arange(T)
    valid = pos >= PAD_LEN
    rel = (pos - PAD_LEN).astype(F32)
    inv_freq = ROPE_BASE ** (-(jnp.arange(0, MLA_ROPE, 2, dtype=F32) / MLA_ROPE))
    ang = rel[:, None] * inv_freq[None, :]
    cos, sin = jnp.cos(ang), jnp.sin(ang)
    for l in range(DEPTH):
        h = h + 0.5 * _swiglu(_rmsnorm_rows(h, w["ln_ffn1"][l]), w["ffn1_wi"][l], w["ffn1_wo"][l])
        u = jnp.where(valid[None, :, None], _rmsnorm_rows(h, w["ln_mix"][l]), 0)
        h = h + _mixer(u, valid, cos, sin, w, l)
        h = h + 0.5 * _swiglu(_rmsnorm_rows(h, w["ln_ffn2"][l]), w["ffn2_wi"][l], w["ffn2_wo"][l])
    y = _rmsnorm_rows(h, w["ln_final"])[:, BLOCK:]
    err = jnp.square(y - loss_target)
    return 0.5 * jnp.sum(jnp.mean(err, axis=-1))


MESH = pl.DeviceIdType.MESH
HBM = pl.BlockSpec(memory_space=pl.ANY)


def _place():
    x, y, c = lax.axis_index("x"), lax.axis_index("y"), lax.axis_index("c")
    return x, y, c


def _other_chip(x, y, r):
    return (1 - x if r & 2 else x), (1 - y if r & 1 else y)


def _gather_chips(shard):
    R, C = shard.shape
    H = R // 2

    def body(x_ref, out_ref, send_sems, recv_sems, local_sem):
        x, y, c = _place()
        me = 2 * x + y

        def rows(chip, half):
            return out_ref.at[chip, pl.ds(half * H, H), :]

        def copy(sem, src, dst, to):
            return pltpu.make_async_remote_copy(src_ref=src, dst_ref=dst, send_sem=send_sems.at[sem],
                                                recv_sem=recv_sems.at[sem], device_id=to, device_id_type=MESH)

        mine = pltpu.make_async_copy(x_ref, out_ref.at[me], local_sem)
        mine.start()
        started = []
        for r in (1, 2, 3):
            ox, oy = _other_chip(x, y, r)
            cp = copy(r - 1, x_ref.at[pl.ds(c * H, H), :], rows(me, c), (ox, oy, c))
            cp.start()
            started.append(cp)
        for r in (1, 2, 3):
            ox, oy = _other_chip(x, y, r)
            src = 2 * ox + oy
            copy(r - 1, rows(src, c), rows(src, c), (x, y, c)).wait_recv()
            fw = copy(2 + r, rows(src, c), rows(src, c), (x, y, 1 - c))
            fw.start()
            started.append(fw)
        for r in (1, 2, 3):
            ox, oy = _other_chip(x, y, r)
            src = 2 * ox + oy
            copy(2 + r, rows(src, 1 - c), rows(src, 1 - c), (x, y, c)).wait_recv()
        for cp in started:
            cp.wait_send()
        mine.wait()

    return _pcall(
        body, name="gather_chips", in_specs=[HBM], out_specs=HBM,
        out_shape=jax.ShapeDtypeStruct((N_CHIPS, R, C), shard.dtype),
        scratch_shapes=[pltpu.SemaphoreType.DMA((6,)), pltpu.SemaphoreType.DMA((6,)),
                        pltpu.SemaphoreType.DMA],
    )(shard)


def _window(ref, lead, axis, chip, width):
    idx = [slice(None)] * len(ref.shape)
    if lead is not None:
        idx[0] = lead
    idx[axis] = pl.ds(chip * width, width)
    return ref.at[tuple(idx)]


def _gather_layers(shards, axes):
    n_t = len(shards)
    widths = [s.shape[a] for s, a in zip(shards, axes)]
    fulls = [s.shape[:a] + (N_CHIPS * s.shape[a],) + s.shape[a + 1:] for s, a in zip(shards, axes)]

    def body(*refs):
        x_refs, out_refs = refs[:n_t], refs[n_t:2 * n_t]
        send_sems, recv_sems, local_sems = refs[2 * n_t:]
        x, y, c = _place()
        me = 2 * x + y

        def win(t, lead, chip):
            return _window(out_refs[t], lead, axes[t], chip, widths[t])

        def copy(sem, src, dst, to):
            return pltpu.make_async_remote_copy(src_ref=src, dst_ref=dst, send_sem=send_sems.at[sem],
                                                recv_sem=recv_sems.at[sem], device_id=to, device_id_type=MESH)

        local, started = [], []
        for t in range(n_t):
            cp = pltpu.make_async_copy(x_refs[t], win(t, None, me), local_sems.at[t])
            cp.start()
            local.append(cp)
        for r in (1, 2, 3):
            ox, oy = _other_chip(x, y, r)
            for t in range(n_t):
                cp = copy(6 * t + r - 1, x_refs[t].at[c], win(t, c, me), (ox, oy, c))
                cp.start()
                started.append(cp)
        for r in (1, 2, 3):
            ox, oy = _other_chip(x, y, r)
            src = 2 * ox + oy
            for t in range(n_t):
                copy(6 * t + r - 1, win(t, c, src), win(t, c, src), (x, y, c)).wait_recv()
                fw = copy(6 * t + 2 + r, win(t, c, src), win(t, c, src), (x, y, 1 - c))
                fw.start()
                started.append(fw)
        for r in (1, 2, 3):
            ox, oy = _other_chip(x, y, r)
            src = 2 * ox + oy
            for t in range(n_t):
                copy(6 * t + 2 + r, win(t, 1 - c, src), win(t, 1 - c, src), (x, y, c)).wait_recv()
        for cp in started:
            cp.wait_send()
        for cp in local:
            cp.wait()

    return _pcall(
        body, name="gather_layers", in_specs=[HBM] * n_t, out_specs=[HBM] * n_t,
        out_shape=[jax.ShapeDtypeStruct(f, s.dtype) for f, s in zip(fulls, shards)],
        scratch_shapes=[pltpu.SemaphoreType.DMA((6 * n_t,)), pltpu.SemaphoreType.DMA((6 * n_t,)),
                        pltpu.SemaphoreType.DMA((n_t,))],
    )(*shards)


def _swap_layers(gs):
    n_t = len(gs)

    def body(*refs):
        g_refs, out_refs = refs[:n_t], refs[n_t:2 * n_t]
        send_sems, recv_sems = refs[2 * n_t:]
        x, y, c = _place()
        cps = []
        for t in range(n_t):
            cp = pltpu.make_async_remote_copy(
                src_ref=g_refs[t].at[1 - c], dst_ref=out_refs[t], send_sem=send_sems.at[t],
                recv_sem=recv_sems.at[t], device_id=(x, y, 1 - c), device_id_type=MESH)
            cp.start()
            cps.append(cp)
        for cp in cps:
            cp.wait()

    return _pcall(
        body, name="swap_layers", in_specs=[HBM] * n_t, out_specs=[HBM] * n_t,
        out_shape=[jax.ShapeDtypeStruct(g.shape[1:], g.dtype) for g in gs],
        scratch_shapes=[pltpu.SemaphoreType.DMA((n_t,)), pltpu.SemaphoreType.DMA((n_t,))],
    )(*gs)


def _add_layer(g, other, c):
    shape = other.shape
    last = shape[-1]
    rows = math.prod(shape[:-1])
    tr = _row_tile(rows, 4 * last, ROW_BLOCK_BYTES)

    def body(c_ref, a_ref, b_ref, o_ref):
        o_ref[...] = a_ref[0] + b_ref[...]

    out = _pcall(
        body, name="add_layer",
        grid_spec=pltpu.PrefetchScalarGridSpec(
            num_scalar_prefetch=1, grid=(rows // tr,),
            in_specs=[pl.BlockSpec((1, tr, last), lambda i, c_ref: (c_ref[0], i, 0)),
                      pl.BlockSpec((tr, last), lambda i, c_ref: (i, 0))],
            out_specs=pl.BlockSpec((tr, last), lambda i, c_ref: (i, 0))),
        out_shape=jax.ShapeDtypeStruct((rows, last), F32),
        compiler_params=pltpu.CompilerParams(dimension_semantics=("parallel",)),
    )(c.reshape(1).astype(jnp.int32), g.reshape(2, rows, last), other.reshape(rows, last))
    return out.reshape(shape)


def _scatter_layers(ps, axes):
    n_t = len(ps)
    widths = [p.shape[a] // N_CHIPS for p, a in zip(ps, axes)]
    wins = [p.shape[:a] + (w,) + p.shape[a + 1:] for p, a, w in zip(ps, axes, widths)]

    def body(*refs):
        p_refs, out_refs = refs[:n_t], refs[n_t:2 * n_t]
        send_sems, recv_sems, local_sems = refs[2 * n_t:]
        x, y, c = _place()
        me = 2 * x + y
        cps = []
        for t in range(n_t):
            cp = pltpu.make_async_copy(_window(p_refs[t], None, axes[t], me, widths[t]), out_refs[t].at[0],
                                       local_sems.at[t])
            cp.start()
            cps.append(cp)
        for r in (1, 2, 3):
            ox, oy = _other_chip(x, y, r)
            for t in range(n_t):
                cp = pltpu.make_async_remote_copy(
                    src_ref=_window(p_refs[t], None, axes[t], 2 * ox + oy, widths[t]), dst_ref=out_refs[t].at[r],
                    send_sem=send_sems.at[3 * t + r - 1], recv_sem=recv_sems.at[3 * t + r - 1],
                    device_id=(ox, oy, c), device_id_type=MESH)
                cp.start()
                cps.append(cp)
        for cp in cps:
            cp.wait()

    return _pcall(
        body, name="scatter_layers", in_specs=[HBM] * n_t, out_specs=[HBM] * n_t,
        out_shape=[jax.ShapeDtypeStruct((N_CHIPS,) + w, p.dtype) for w, p in zip(wins, ps)],
        scratch_shapes=[pltpu.SemaphoreType.DMA((3 * n_t,)), pltpu.SemaphoreType.DMA((3 * n_t,)),
                        pltpu.SemaphoreType.DMA((n_t,))],
    )(*ps)


def _sum_chips(q, me):
    shape = q.shape[1:]
    C = shape[-1]
    H = math.prod(shape[:-1])
    tr = _row_tile(H, 4 * C, ROW_BLOCK_BYTES)

    def body(me_ref, q0, q1, q2, q3, o_ref):
        o_ref[...] = ((q0[0] + q1[0]) + q2[0]) + q3[0]

    def spec(chip):
        return pl.BlockSpec((1, tr, C), lambda i, me_ref: (jnp.bitwise_xor(me_ref[0], chip), i, 0))

    q = q.reshape(N_CHIPS, H, C)
    return _pcall(
        body, name="sum_chips",
        grid_spec=pltpu.PrefetchScalarGridSpec(
            num_scalar_prefetch=1, grid=(H // tr,),
            in_specs=[spec(0), spec(1), spec(2), spec(3)],
            out_specs=pl.BlockSpec((tr, C), lambda i, me_ref: (i, 0))),
        out_shape=jax.ShapeDtypeStruct((H, C), F32),
        compiler_params=pltpu.CompilerParams(dimension_semantics=("parallel",)),
    )(me.reshape(1).astype(jnp.int32), q, q, q, q).reshape(shape)


def _join_layers(rs):
    n_t = len(rs)

    def body(*refs):
        r_refs, out_refs = refs[:n_t], refs[n_t:2 * n_t]
        send_sems, recv_sems, local_sems = refs[2 * n_t:]
        x, y, c = _place()
        cps = []
        for t in range(n_t):
            mine = pltpu.make_async_copy(r_refs[t], out_refs[t].at[c], local_sems.at[t])
            mine.start()
            cps.append(mine)
            cp = pltpu.make_async_remote_copy(
                src_ref=r_refs[t], dst_ref=out_refs[t].at[c], send_sem=send_sems.at[t],
                recv_sem=recv_sems.at[t], device_id=(x, y, 1 - c), device_id_type=MESH)
            cp.start()
            cps.append(cp)
        for cp in cps:
            cp.wait()

    return _pcall(
        body, name="join_layers", in_specs=[HBM] * n_t, out_specs=[HBM] * n_t,
        out_shape=[jax.ShapeDtypeStruct((2,) + r.shape, r.dtype) for r in rs],
        scratch_shapes=[pltpu.SemaphoreType.DMA((n_t,)), pltpu.SemaphoreType.DMA((n_t,)),
                        pltpu.SemaphoreType.DMA((n_t,))],
    )(*rs)


def _reduce_scatter_layers(gs, axes):
    x, y, c = _place()
    others = _swap_layers(gs)
    ps = [_add_layer(g, o, c) for g, o in zip(gs, others)]
    qs = _scatter_layers(ps, [a - 1 for a in axes])
    rs = [_sum_chips(q, 2 * x + y) for q in qs]
    return _join_layers(rs)


def _allgather_devices(flat):
    R, C = flat.shape

    def body(x_ref, out_ref, send_sems, recv_sems, local_sem):
        x, y, c = _place()
        me = 4 * x + 2 * y + c
        mine = pltpu.make_async_copy(x_ref, out_ref.at[me], local_sem)
        mine.start()
        cps = []
        for m in range(1, 8):
            ox, oy = _other_chip(x, y, m >> 1)
            oc = 1 - c if m & 1 else c
            cp = pltpu.make_async_remote_copy(
                src_ref=x_ref, dst_ref=out_ref.at[me], send_sem=send_sems.at[m - 1],
                recv_sem=recv_sems.at[m - 1], device_id=(ox, oy, oc), device_id_type=MESH)
            cp.start()
            cps.append(cp)
        for cp in cps:
            cp.wait()
        mine.wait()

    return _pcall(
        body, name="allgather_devices", in_specs=[HBM], out_specs=HBM,
        out_shape=jax.ShapeDtypeStruct((8, R, C), flat.dtype),
        scratch_shapes=[pltpu.SemaphoreType.DMA((7,)), pltpu.SemaphoreType.DMA((7,)),
                        pltpu.SemaphoreType.DMA],
    )(flat)


def _sum_devices(slots):
    _, R, C = slots.shape
    tr = _row_tile(R, 4 * C, ROW_BLOCK_BYTES // 4)

    def body(*refs):
        o_ref = refs[8]
        acc = refs[0][0]
        for d in range(1, 8):
            acc = acc + refs[d][0]
        o_ref[...] = acc

    def spec(d):
        return pl.BlockSpec((1, tr, C), lambda i: (d, i, 0))

    return _pcall(
        body, name="sum_devices", grid=(R // tr,), in_specs=[spec(d) for d in range(8)],
        out_specs=pl.BlockSpec((tr, C), lambda i: (i, 0)),
        out_shape=jax.ShapeDtypeStruct((R, C), F32),
        compiler_params=pltpu.CompilerParams(dimension_semantics=("parallel",)),
    )(*([slots] * 8))


def _adamw(w, g, m, v):
    shape = w.shape
    C = shape[-1]
    R = math.prod(shape[:-1])
    w, g, m, v = (t.reshape(R, C) for t in (w, g, m, v))
    tr = _row_tile(R, 4 * C, ROW_BLOCK_BYTES // 2)
    c1 = 1.0 - ADAM_B1 ** ADAM_STEP
    c2 = 1.0 - ADAM_B2 ** ADAM_STEP

    def body(w_ref, g_ref, m_ref, v_ref, d_ref, nm_ref, nv_ref):
        gg = g_ref[...]
        nm = ADAM_B1 * m_ref[...] + (1.0 - ADAM_B1) * gg
        nv = ADAM_B2 * v_ref[...] + (1.0 - ADAM_B2) * jnp.square(gg)
        d_ref[...] = -ADAM_LR * ((nm / c1) / (jnp.sqrt(nv / c2) + ADAM_EPS) + ADAM_WD * w_ref[...])
        nm_ref[...] = nm
        nv_ref[...] = nv

    spec = pl.BlockSpec((tr, C), lambda i: (i, 0))
    outs = _pcall(
        body, name="adamw", grid=(R // tr,), in_specs=[spec] * 4, out_specs=[spec] * 3,
        out_shape=[jax.ShapeDtypeStruct((R, C), F32)] * 3,
        compiler_params=pltpu.CompilerParams(dimension_semantics=("parallel",)),
    )(w, g, m, v)
    return [o.reshape(shape) for o in outs]


def _to_flat(parts):
    flat = jnp.concatenate([p.reshape(-1) for p in parts])
    unit = FLAT_COLS * FLAT_ROW_ALIGN
    pad = (-flat.shape[0]) % unit
    if pad:
        flat = jnp.concatenate([flat, jnp.zeros((pad,), flat.dtype)])
    return flat.reshape(-1, FLAT_COLS)


def _from_flat(flat, shapes):
    flat = flat.reshape(-1)
    out, off = [], 0
    for s in shapes:
        n = math.prod(s)
        out.append(flat[off:off + n].reshape(s))
        off += n
    return out


def kernel(x, meta, ln_ffn1, ffn1_wi, ffn1_wo, ln_mix, w_in, fox_bf, mla_gq, mla_wq, mla_gkv, mla_wkv, gdn_conv, gdn_alog, gdn_dtb, gdn_gon, lru_conv, lru_conv_b, lru_wa, lru_ba, lru_wx, lru_bx, lru_lam, w_gate, b_gate, w_branch, w_out, ln_ffn2, ffn2_wi, ffn2_wo, ln_final, loss_target, m_meta, m_ln_ffn1, m_ffn1_wi, m_ffn1_wo, m_ln_mix, m_w_in, m_fox_bf, m_mla_gq, m_mla_wq, m_mla_gkv, m_mla_wkv, m_gdn_conv, m_gdn_alog, m_gdn_dtb, m_gdn_gon, m_lru_conv, m_lru_conv_b, m_lru_wa, m_lru_ba, m_lru_wx, m_lru_bx, m_lru_lam, m_w_gate, m_b_gate, m_w_branch, m_w_out, m_ln_ffn2, m_ffn2_wi, m_ffn2_wo, m_ln_final, v_meta, v_ln_ffn1, v_ffn1_wi, v_ffn1_wo, v_ln_mix, v_w_in, v_fox_bf, v_mla_gq, v_mla_wq, v_mla_gkv, v_mla_wkv, v_gdn_conv, v_gdn_alog, v_gdn_dtb, v_gdn_gon, v_lru_conv, v_lru_conv_b, v_lru_wa, v_lru_ba, v_lru_wx, v_lru_bx, v_lru_lam, v_w_gate, v_b_gate, v_w_branch, v_w_out, v_ln_ffn2, v_ffn2_wi, v_ffn2_wo, v_ln_final):
    ws = (meta, ln_ffn1, ffn1_wi, ffn1_wo, ln_mix, w_in, fox_bf, mla_gq, mla_wq, mla_gkv, mla_wkv, gdn_conv, gdn_alog, gdn_dtb, gdn_gon, lru_conv, lru_conv_b, lru_wa, lru_ba, lru_wx, lru_bx, lru_lam, w_gate, b_gate, w_branch, w_out, ln_ffn2, ffn2_wi, ffn2_wo, ln_final)
    ms = (m_meta, m_ln_ffn1, m_ffn1_wi, m_ffn1_wo, m_ln_mix, m_w_in, m_fox_bf, m_mla_gq, m_mla_wq, m_mla_gkv, m_mla_wkv, m_gdn_conv, m_gdn_alog, m_gdn_dtb, m_gdn_gon, m_lru_conv, m_lru_conv_b, m_lru_wa, m_lru_ba, m_lru_wx, m_lru_bx, m_lru_lam, m_w_gate, m_b_gate, m_w_branch, m_w_out, m_ln_ffn2, m_ffn2_wi, m_ffn2_wo, m_ln_final)
    vs = (v_meta, v_ln_ffn1, v_ffn1_wi, v_ffn1_wo, v_ln_mix, v_w_in, v_fox_bf, v_mla_gq, v_mla_wq, v_mla_gkv, v_mla_wkv, v_gdn_conv, v_gdn_alog, v_gdn_dtb, v_gdn_gon, v_lru_conv, v_lru_conv_b, v_lru_wa, v_lru_ba, v_lru_wx, v_lru_bx, v_lru_lam, v_w_gate, v_b_gate, v_w_branch, v_w_out, v_ln_ffn2, v_ffn2_wi, v_ffn2_wo, v_ln_final)
    names = [n for n, _ in WEIGHT_SPECS]
    axis = dict(WEIGHT_SPECS)
    wd, md, vd = dict(zip(names, ws)), dict(zip(names, ms)), dict(zip(names, vs))
    shapes = {n: wd[n].shape for n in names}
    big = [n for n in names if n in LARGE]
    few = [n for n in names if axis[n] is not None and n not in LARGE]
    whole = [n for n in names if axis[n] is None]
    x_, y_, _ = _place()
    chip = 2 * x_ + y_

    def to_dma(n, a):
        ax, w = axis[n], shapes[n][axis[n]]
        nd = len(shapes[n])
        if (ax == nd - 1 and w % LANES) or (ax == nd - 2 and w % 8):
            parts = a.shape[ax] // w
            a = jnp.moveaxis(a.reshape(a.shape[:ax] + (parts, w) + a.shape[ax + 1:]), ax, 1)
            return a, 1
        return a, ax

    def from_dma(n, a):
        ax = axis[n]
        if a.ndim == len(shapes[n]):
            return a
        a = jnp.moveaxis(a, 1, ax)
        return a.reshape(a.shape[:ax] + (-1,) + a.shape[ax + 2:])

    dma = [to_dma(n, wd[n]) for n in big]
    fulls = _gather_layers([a for a, _ in dma], [ax for _, ax in dma])
    full = {n: from_dma(n, f) for n, f in zip(big, fulls)}
    gathered = _gather_chips(_to_flat([wd[n] for n in few]))
    per_chip = [_from_flat(gathered[k], [shapes[n] for n in few]) for k in range(N_CHIPS)]
    for i, n in enumerate(few):
        full[n] = jnp.concatenate([per_chip[k][i] for k in range(N_CHIPS)], axis=axis[n])
    full.update({n: wd[n] for n in whole})

    loss, (gw, gx) = jax.value_and_grad(_local_loss, argnums=(0, 1))(full, x, loss_target)
    loss = lax.psum(loss, ("x", "y", "c"))

    dma = [to_dma(n, gw[n]) for n in big]
    reduced = _reduce_scatter_layers([a for a, _ in dma], [ax for _, ax in dma])
    grads = {n: r.reshape(shapes[n]) for n, r in zip(big, reduced)}
    rest = few + whole
    summed = _from_flat(_sum_devices(_allgather_devices(_to_flat([gw[n] for n in rest]))),
                        [gw[n].shape for n in rest])
    for n, g in zip(rest, summed):
        if axis[n] is not None:
            g = lax.dynamic_slice_in_dim(g, chip * shapes[n][axis[n]], shapes[n][axis[n]], axis=axis[n])
        grads[n] = g

    delta, new_m, new_v = {}, {}, {}
    for n in big:
        delta[n], new_m[n], new_v[n] = _adamw(wd[n], grads[n], md[n], vd[n])
    outs = _adamw(*[_to_flat([d[n] for n in rest]) for d in (wd, grads, md, vd)])
    for res, flat in zip((delta, new_m, new_v), outs):
        res.update(zip(rest, _from_flat(flat, [shapes[n] for n in rest])))

    return (loss, gx, *[grads[n] for n in names], *[delta[n] for n in names],
            *[new_m[n] for n in names], *[new_v[n] for n in names])
```

```python
import functools
import math

import jax
import jax.numpy as jnp
from jax import lax
from jax.experimental import pallas as pl
from jax.experimental.pallas import tpu as pltpu

F32 = jnp.float32
BF16 = jnp.bfloat16

N_META = 16
BLOCK = 128
PAD_LEN = BLOCK - N_META
EPS = 1e-6
NEG_INF = -1e30
N_BRANCH = 4
FOX_HEADS, FOX_DH = 4, 64
MLA_HEADS, MLA_NOPE, MLA_ROPE, MLA_DV = 4, 64, 32, 64
MLA_Q_RANK, MLA_KV_RANK = 192, 128
ROPE_BASE = 10000.0
GDN_HEADS, GDN_DK, GDN_DV, GDN_CHUNK = 4, 64, 64, 64
LRU_WIDTH, LRU_BLOCKS, LRU_C = 256, 4, 8.0
DEPTH = 2

OFF_FOX_QKV = 0
OFF_FOX_F = OFF_FOX_QKV + 3 * FOX_HEADS * FOX_DH
OFF_MLA_CQ = OFF_FOX_F + FOX_HEADS
OFF_MLA_CKV = OFF_MLA_CQ + MLA_Q_RANK
OFF_MLA_KR = OFF_MLA_CKV + MLA_KV_RANK
OFF_GDN_QKV = OFF_MLA_KR + MLA_ROPE
OFF_GDN_A = OFF_GDN_QKV + GDN_HEADS * (2 * GDN_DK + GDN_DV)
OFF_GDN_B = OFF_GDN_A + GDN_HEADS
OFF_GDN_G = OFF_GDN_B + GDN_HEADS
OFF_LRU = OFF_GDN_G + GDN_HEADS * GDN_DV
N_IN = OFF_LRU + LRU_WIDTH

ADAM_LR, ADAM_B1, ADAM_B2, ADAM_EPS, ADAM_WD, ADAM_STEP = 0.001, 0.9, 0.999, 1e-08, 0.01, 10

WEIGHT_SPECS = (
    ("meta", 1), ("ln_ffn1", None), ("ffn1_wi", 2), ("ffn1_wo", 1), ("ln_mix", None), ("w_in", 2),
    ("fox_bf", None), ("mla_gq", None), ("mla_wq", 2), ("mla_gkv", None), ("mla_wkv", 2),
    ("gdn_conv", 2), ("gdn_alog", None), ("gdn_dtb", None), ("gdn_gon", None), ("lru_conv", 2),
    ("lru_conv_b", None), ("lru_wa", None), ("lru_ba", None), ("lru_wx", None), ("lru_bx", None),
    ("lru_lam", None), ("w_gate", 2), ("b_gate", 2), ("w_branch", 3), ("w_out", 1),
    ("ln_ffn2", None), ("ffn2_wi", 2), ("ffn2_wo", 1), ("ln_final", None),
)
N_CHIPS = 4
LARGE = ("ffn1_wi", "ffn1_wo", "w_in", "w_gate", "w_branch", "w_out", "ffn2_wi", "ffn2_wo")

LANES = 128
VMEM_LIMIT_BYTES = 48 * 1024 * 1024
FLAT_COLS = 512
FLAT_ROW_ALIGN = 64


def _pcall(body, **kw):
    return pl.pallas_call(body, **kw)


def _pick(n, cands):
    for c in cands:
        if n % c == 0:
            return c
    return n


_DN = {"nn": (((1,), (0,)), ((), ())), "nt": (((1,), (1,)), ((), ())), "tn": (((0,), (0,)), ((), ()))}


MATMUL_OPERAND_TILE_BYTES = 8 * 1024 * 1024


def _k_tile(K, row_bytes, lane_axis):
    for tk in (K, 4224, 2816, 2112, 1408, 1056, 1024, 768, 704, 512, 384, 256, 128):
        aligned = tk == K or tk % LANES == 0 or (not lane_axis and tk % 16 == 0)
        if tk <= K and K % tk == 0 and aligned and tk * row_bytes <= MATMUL_OPERAND_TILE_BYTES:
            return tk
    return K


def _matmul(a, b, mode, name, out_dtype=F32):
    if mode == "nn":
        (M, K), (_, N) = a.shape, b.shape
    elif mode == "nt":
        (M, K), (N, _) = a.shape, b.shape
    else:
        (K, M), (_, N) = a.shape, b.shape
    tm = _pick(M, (768, 512, 1408, 384, 256, 128, 64, 32, 16, 8))
    tn = _pick(N, (512, 1408, 256, 128))
    tk = _k_tile(K, tm * a.dtype.itemsize + tn * b.dtype.itemsize, mode != "tn")
    nk = K // tk
    a_spec = {"nn": pl.BlockSpec((tm, tk), lambda i, j, k: (i, k)),
              "nt": pl.BlockSpec((tm, tk), lambda i, j, k: (i, k)),
              "tn": pl.BlockSpec((tk, tm), lambda i, j, k: (k, i))}[mode]
    b_spec = {"nn": pl.BlockSpec((tk, tn), lambda i, j, k: (k, j)),
              "nt": pl.BlockSpec((tn, tk), lambda i, j, k: (j, k)),
              "tn": pl.BlockSpec((tk, tn), lambda i, j, k: (k, j))}[mode]
    dn = _DN[mode]

    def body(a_ref, b_ref, o_ref, acc_ref):
        k = pl.program_id(2)
        part = lax.dot_general(a_ref[...].astype(BF16), b_ref[...].astype(BF16), dn,
                               preferred_element_type=F32)

        if nk == 1:
            o_ref[...] = part.astype(o_ref.dtype)
        else:
            @pl.when(k == 0)
            def _():
                acc_ref[...] = part

            @pl.when((k > 0) & (k < nk - 1))
            def _():
                acc_ref[...] += part

            @pl.when(k == nk - 1)
            def _():
                o_ref[...] = (acc_ref[...] + part).astype(o_ref.dtype)

    return _pcall(
        body, name=name, grid=(M // tm, N // tn, nk),
        in_specs=[a_spec, b_spec], out_specs=pl.BlockSpec((tm, tn), lambda i, j, k: (i, j)),
        out_shape=jax.ShapeDtypeStruct((M, N), out_dtype),
        scratch_shapes=[pltpu.VMEM((tm, tn) if nk > 1 else (8, LANES), F32)],
        compiler_params=pltpu.CompilerParams(
            dimension_semantics=("parallel", "parallel", "arbitrary"),
            vmem_limit_bytes=VMEM_LIMIT_BYTES),
    )(a, b)


@functools.partial(jax.custom_vjp, nondiff_argnums=(3,))
def dense(a, wb, w, out_dtype):
    return _matmul(a, wb, "nn", "dense_fwd", out_dtype)


def _dense_fwd(a, wb, w, out_dtype):
    return _matmul(a, wb, "nn", "dense_fwd", out_dtype), (a, wb)


def _dense_bwd(out_dtype, res, g):
    a, wb = res
    return (_matmul(g, wb, "nt", "dense_dgrad", a.dtype), jnp.zeros_like(wb),
            _matmul(a, g, "tn", "dense_wgrad", F32))


dense.defvjp(_dense_fwd, _dense_bwd)


def dense_nd(a, wb, w, out_dtype=F32):
    K, N = wb.shape
    pad = (-N) % (4 * LANES if N > 4 * LANES else LANES)
    if pad:
        wb = jnp.pad(wb, ((0, 0), (0, pad)))
        w = jnp.pad(w, ((0, 0), (0, pad)))
    out = dense(a.reshape(-1, K), wb, w, out_dtype)
    if pad:
        out = out[:, :N]
    return out.reshape(a.shape[:-1] + (N,))


_NT = (((1,), (1,)), ((), ()))


def _att_blk(T):
    return _pick(T, (384, 256, 128))


def _attn_fwd_call(q, k, v, cum_col, cum_row, scale):
    BH, T, dk = q.shape
    dv = v.shape[2]
    blk = _att_blk(T)
    n = T // blk
    has_cum = cum_col is not None

    def body(*refs):
        if has_cum:
            q_ref, k_ref, v_ref, cq_ref, ck_ref, o_ref, lse_ref, m_s, l_s, acc_s = refs
        else:
            q_ref, k_ref, v_ref, o_ref, lse_ref, m_s, l_s, acc_s = refs
        i = pl.program_id(1)
        j = pl.program_id(2)

        @pl.when(j == 0)
        def _():
            m_s[...] = jnp.full_like(m_s, NEG_INF)
            l_s[...] = jnp.zeros_like(l_s)
            acc_s[...] = jnp.zeros_like(acc_s)

        @pl.when(j <= i)
        def _():
            s = lax.dot_general(q_ref[0].astype(BF16), k_ref[0].astype(BF16), _NT,
                                preferred_element_type=F32) * scale
            if has_cum:
                s = s + cq_ref[0] - ck_ref[0]
            qpos = i * blk + lax.broadcasted_iota(jnp.int32, (blk, blk), 0)
            kpos = j * blk + lax.broadcasted_iota(jnp.int32, (blk, blk), 1)
            mask = (kpos <= qpos) & (kpos >= PAD_LEN)
            s = jnp.where(mask, s, NEG_INF)
            m_prev = m_s[...]
            m_new = jnp.maximum(m_prev, jnp.max(s, axis=1, keepdims=True))
            p = jnp.exp(s - m_new)
            alpha = jnp.exp(m_prev - m_new)
            l_s[...] = alpha * l_s[...] + jnp.sum(p, axis=1, keepdims=True)
            acc_s[...] = alpha * acc_s[...] + jnp.dot(p.astype(BF16), v_ref[0].astype(BF16),
                                                      preferred_element_type=F32)
            m_s[...] = m_new

        @pl.when(j == n - 1)
        def _():
            o_ref[0] = acc_s[...] / l_s[...]
            lse_ref[0] = m_s[...] + jnp.log(l_s[...])

    kv_idx = lambda b, i, j: (b, jnp.minimum(i, j), 0)
    in_specs = [pl.BlockSpec((1, blk, dk), lambda b, i, j: (b, i, 0)),
                pl.BlockSpec((1, blk, dk), kv_idx),
                pl.BlockSpec((1, blk, dv), kv_idx)]
    args = [q, k, v]
    if has_cum:
        in_specs += [pl.BlockSpec((1, blk, 1), lambda b, i, j: (b, i, 0)),
                     pl.BlockSpec((1, 1, blk), lambda b, i, j: (b, 0, jnp.minimum(i, j)))]
        args += [cum_col, cum_row]
    return _pcall(
        body, name="attn_fwd_cum" if has_cum else "attn_fwd", grid=(BH, n, n),
        in_specs=in_specs,
        out_specs=[pl.BlockSpec((1, blk, dv), lambda b, i, j: (b, i, 0)),
                   pl.BlockSpec((1, blk, 1), lambda b, i, j: (b, i, 0))],
        out_shape=[jax.ShapeDtypeStruct((BH, T, dv), F32), jax.ShapeDtypeStruct((BH, T, 1), F32)],
        scratch_shapes=[pltpu.VMEM((blk, 1), F32), pltpu.VMEM((blk, 1), F32), pltpu.VMEM((blk, dv), F32)],
        compiler_params=pltpu.CompilerParams(
            dimension_semantics=("parallel", "parallel", "arbitrary"),
            vmem_limit_bytes=VMEM_LIMIT_BYTES),
    )(*args)


def _attn_dkv_call(q, k, v, do, lse_row, delta_row, cum_row, cum_col, scale):
    BH, T, dk = q.shape
    dv = v.shape[2]
    blk = _att_blk(T)
    n = T // blk
    has_cum = cum_row is not None

    def body(*refs):
        if has_cum:
            (k_ref, v_ref, q_ref, do_ref, lse_ref, dl_ref, cq_ref, ck_ref,
             dk_ref, dv_ref, dc_ref, dk_s, dv_s, dc_s) = refs
        else:
            k_ref, v_ref, q_ref, do_ref, lse_ref, dl_ref, dk_ref, dv_ref, dk_s, dv_s = refs
        j = pl.program_id(1)
        i = pl.program_id(2)

        @pl.when(i == 0)
        def _():
            dk_s[...] = jnp.zeros_like(dk_s)
            dv_s[...] = jnp.zeros_like(dv_s)
            if has_cum:
                dc_s[...] = jnp.zeros_like(dc_s)

        @pl.when(i >= j)
        def _():
            kb = k_ref[0].astype(BF16)
            qb = q_ref[0].astype(BF16)
            dob = do_ref[0].astype(BF16)
            st = lax.dot_general(kb, qb, _NT, preferred_element_type=F32) * scale
            if has_cum:
                st = st + cq_ref[0] - ck_ref[0]
            kpos = j * blk + lax.broadcasted_iota(jnp.int32, (blk, blk), 0)
            qpos = i * blk + lax.broadcasted_iota(jnp.int32, (blk, blk), 1)
            mask = (kpos <= qpos) & (kpos >= PAD_LEN)
            pt = jnp.where(mask, jnp.exp(jnp.minimum(st - lse_ref[0], 0.0)), 0.0)
            dv_s[...] += jnp.dot(pt.astype(BF16), dob, preferred_element_type=F32)
            dpt = lax.dot_general(v_ref[0].astype(BF16), dob, _NT, preferred_element_type=F32)
            dst = pt * (dpt - dl_ref[0])
            dk_s[...] += jnp.dot(dst.astype(BF16), qb, preferred_element_type=F32) * scale
            if has_cum:
                dc_s[...] -= jnp.sum(dst, axis=1, keepdims=True)

        @pl.when(i == n - 1)
        def _():
            dk_ref[0] = dk_s[...]
            dv_ref[0] = dv_s[...]
            if has_cum:
                dc_ref[0] = dc_s[...]

    q_idx = lambda b, j, i: (b, jnp.maximum(i, j), 0)
    row_idx = lambda b, j, i: (b, 0, jnp.maximum(i, j))
    in_specs = [pl.BlockSpec((1, blk, dk), lambda b, j, i: (b, j, 0)),
                pl.BlockSpec((1, blk, dv), lambda b, j, i: (b, j, 0)),
                pl.BlockSpec((1, blk, dk), q_idx),
                pl.BlockSpec((1, blk, dv), q_idx),
                pl.BlockSpec((1, 1, blk), row_idx),
                pl.BlockSpec((1, 1, blk), row_idx)]
    args = [k, v, q, do, lse_row, delta_row]
    out_specs = [pl.BlockSpec((1, blk, dk), lambda b, j, i: (b, j, 0)),
                 pl.BlockSpec((1, blk, dv), lambda b, j, i: (b, j, 0))]
    out_shape = [jax.ShapeDtypeStruct((BH, T, dk), F32), jax.ShapeDtypeStruct((BH, T, dv), F32)]
    scratch = [pltpu.VMEM((blk, dk), F32), pltpu.VMEM((blk, dv), F32)]
    if has_cum:
        in_specs += [pl.BlockSpec((1, 1, blk), row_idx),
                     pl.BlockSpec((1, blk, 1), lambda b, j, i: (b, j, 0))]
        args += [cum_row, cum_col]
        out_specs.append(pl.BlockSpec((1, blk, 1), lambda b, j, i: (b, j, 0)))
        out_shape.append(jax.ShapeDtypeStruct((BH, T, 1), F32))
        scratch.append(pltpu.VMEM((blk, 1), F32))
    return _pcall(
        body, name="attn_dkv_cum" if has_cum else "attn_dkv", grid=(BH, n, n),
        in_specs=in_specs, out_specs=out_specs, out_shape=out_shape, scratch_shapes=scratch,
        compiler_params=pltpu.CompilerParams(
            dimension_semantics=("parallel", "parallel", "arbitrary"),
            vmem_limit_bytes=VMEM_LIMIT_BYTES),
    )(*args)


def _attn_dq_call(q, k, v, do, lse_col, delta_col, cum_col, cum_row, scale):
    BH, T, dk = q.shape
    dv = v.shape[2]
    blk = _att_blk(T)
    n = T // blk
    has_cum = cum_col is not None

    def body(*refs):
        if has_cum:
            q_ref, k_ref, v_ref, do_ref, lse_ref, dl_ref, cq_ref, ck_ref, dq_ref, dc_ref, dq_s, dc_s = refs
        else:
            q_ref, k_ref, v_ref, do_ref, lse_ref, dl_ref, dq_ref, dq_s = refs
        i = pl.program_id(1)
        j = pl.program_id(2)

        @pl.when(j == 0)
        def _():
            dq_s[...] = jnp.zeros_like(dq_s)
            if has_cum:
                dc_s[...] = jnp.zeros_like(dc_s)

        @pl.when(j <= i)
        def _():
            kb = k_ref[0].astype(BF16)
            s = lax.dot_general(q_ref[0].astype(BF16), kb, _NT, preferred_element_type=F32) * scale
            if has_cum:
                s = s + cq_ref[0] - ck_ref[0]
            qpos = i * blk + lax.broadcasted_iota(jnp.int32, (blk, blk), 0)
            kpos = j * blk + lax.broadcasted_iota(jnp.int32, (blk, blk), 1)
            mask = (kpos <= qpos) & (kpos >= PAD_LEN)
            p = jnp.where(mask, jnp.exp(jnp.minimum(s - lse_ref[0], 0.0)), 0.0)
            dp = lax.dot_general(do_ref[0].astype(BF16), v_ref[0].astype(BF16), _NT,
                                 preferred_element_type=F32)
            ds = p * (dp - dl_ref[0])
            dq_s[...] += jnp.dot(ds.astype(BF16), kb, preferred_element_type=F32) * scale
            if has_cum:
                dc_s[...] += jnp.sum(ds, axis=1, keepdims=True)

        @pl.when(j == n - 1)
        def _():
            dq_ref[0] = dq_s[...]
            if has_cum:
                dc_ref[0] = dc_s[...]

    kv_idx = lambda b, i, j: (b, jnp.minimum(i, j), 0)
    q_idx = lambda b, i, j: (b, i, 0)
    in_specs = [pl.BlockSpec((1, blk, dk), q_idx), pl.BlockSpec((1, blk, dk), kv_idx),
                pl.BlockSpec((1, blk, dv), kv_idx), pl.BlockSpec((1, blk, dv), q_idx),
                pl.BlockSpec((1, blk, 1), q_idx), pl.BlockSpec((1, blk, 1), q_idx)]
    args = [q, k, v, do, lse_col, delta_col]
    if has_cum:
        in_specs += [pl.BlockSpec((1, blk, 1), q_idx),
                     pl.BlockSpec((1, 1, blk), lambda b, i, j: (b, 0, jnp.minimum(i, j)))]
        args += [cum_col, cum_row]
    out_specs = [pl.BlockSpec((1, blk, dk), q_idx)]
    out_shape = [jax.ShapeDtypeStruct((BH, T, dk), F32)]
    scratch = [pltpu.VMEM((blk, dk), F32)]
    if has_cum:
        out_specs.append(pl.BlockSpec((1, blk, 1), q_idx))
        out_shape.append(jax.ShapeDtypeStruct((BH, T, 1), F32))
        scratch.append(pltpu.VMEM((blk, 1), F32))
    return _pcall(
        body, name="attn_dq_cum" if has_cum else "attn_dq", grid=(BH, n, n),
        in_specs=in_specs, out_specs=out_specs, out_shape=out_shape, scratch_shapes=scratch,
        compiler_params=pltpu.CompilerParams(
            dimension_semantics=("parallel", "parallel", "arbitrary"),
            vmem_limit_bytes=VMEM_LIMIT_BYTES),
    )(*args)


def _make_attention(scale, has_cum):
    def fold(t):
        return t.reshape((-1,) + t.shape[2:])

    def run_fwd(q, k, v, cum):
        B, H, T, _ = q.shape
        col = cum.reshape(B * H, T, 1) if has_cum else None
        row = cum.reshape(B * H, 1, T) if has_cum else None
        o, lse = _attn_fwd_call(fold(q), fold(k), fold(v), col, row, scale)
        return o.reshape(B, H, T, -1), lse

    @jax.custom_vjp
    def attn(q, k, v, cum):
        return run_fwd(q, k, v, cum)[0]

    def attn_fwd(q, k, v, cum):
        o, lse = run_fwd(q, k, v, cum)
        return o, (q, k, v, cum, o, lse)

    def attn_bwd(res, do):
        q, k, v, cum, o, lse = res
        B, H, T, _ = q.shape
        delta = jnp.sum(do * o, axis=-1).reshape(B * H, T, 1)
        col = cum.reshape(B * H, T, 1) if has_cum else None
        row = cum.reshape(B * H, 1, T) if has_cum else None
        qf, kf, vf, dof = fold(q), fold(k), fold(v), fold(do)
        outs = _attn_dkv_call(qf, kf, vf, dof, lse.reshape(B * H, 1, T), delta.reshape(B * H, 1, T),
                              row, col, scale)
        dqs = _attn_dq_call(qf, kf, vf, dof, lse, delta, col, row, scale)
        dcum = (outs[2] + dqs[1]).reshape(B, H, T) if has_cum else jnp.zeros_like(cum)
        return (dqs[0].reshape(q.shape), outs[0].reshape(k.shape), outs[1].reshape(v.shape), dcum)

    attn.defvjp(attn_fwd, attn_bwd)
    return attn


def _rmsnorm(x, g):
    return x * lax.rsqrt(jnp.mean(x * x, axis=-1, keepdims=True) + EPS) * g


def _l2norm(x):
    return x * lax.rsqrt(jnp.sum(x * x, axis=-1, keepdims=True) + EPS)


def _row_tile(rows, row_bytes, budget):
    for t in (2048, 1024, 512, 256, 128, 64, 32, 16, 8):
        if rows % t == 0 and t * row_bytes <= budget:
            return t
    return rows


ROW_BLOCK_BYTES = 2 * 1024 * 1024


def _rms_fwd_call(x, g):
    M, D = x.shape
    tr = _row_tile(M, 4 * D, ROW_BLOCK_BYTES)

    def body(x_ref, g_ref, y_ref, r_ref):
        xv = x_ref[...]
        r = lax.rsqrt(jnp.mean(xv * xv, axis=-1, keepdims=True) + EPS)
        y_ref[...] = xv * r * g_ref[...]
        r_ref[...] = r

    return _pcall(
        body, name="rmsnorm_fwd", grid=(M // tr,),
        in_specs=[pl.BlockSpec((tr, D), lambda i: (i, 0)), pl.BlockSpec((1, D), lambda i: (0, 0))],
        out_specs=[pl.BlockSpec((tr, D), lambda i: (i, 0)), pl.BlockSpec((tr, 1), lambda i: (i, 0))],
        out_shape=[jax.ShapeDtypeStruct((M, D), F32), jax.ShapeDtypeStruct((M, 1), F32)],
        compiler_params=pltpu.CompilerParams(dimension_semantics=("parallel",)),
    )(x, g)


def _rms_bwd_call(x, g, r, dy):
    M, D = x.shape
    tr = _row_tile(M, 4 * D, ROW_BLOCK_BYTES)

    def body(x_ref, g_ref, r_ref, dy_ref, dx_ref, dg_ref):
        i = pl.program_id(0)
        rv = r_ref[...]
        xh = x_ref[...] * rv
        dyv = dy_ref[...]
        dyg = dyv * g_ref[...]
        dx_ref[...] = rv * (dyg - xh * jnp.mean(dyg * xh, axis=-1, keepdims=True))
        part = jnp.sum(dyv * xh, axis=0, keepdims=True)

        @pl.when(i == 0)
        def _():
            dg_ref[...] = part

        @pl.when(i > 0)
        def _():
            dg_ref[...] += part

    row = pl.BlockSpec((tr, D), lambda i: (i, 0))
    return _pcall(
        body, name="rmsnorm_bwd", grid=(M // tr,),
        in_specs=[row, pl.BlockSpec((1, D), lambda i: (0, 0)), pl.BlockSpec((tr, 1), lambda i: (i, 0)), row],
        out_specs=[row, pl.BlockSpec((1, D), lambda i: (0, 0))],
        out_shape=[jax.ShapeDtypeStruct((M, D), F32), jax.ShapeDtypeStruct((1, D), F32)],
        compiler_params=pltpu.CompilerParams(dimension_semantics=("arbitrary",)),
    )(x, g, r, dy)


@jax.custom_vjp
def _rmsnorm2d(x, g):
    return _rms_fwd_call(x, g)[0]


def _rmsnorm2d_fwd(x, g):
    y, r = _rms_fwd_call(x, g)
    return y, (x, g, r)


def _rmsnorm2d_bwd(res, dy):
    x, g, r = res
    return _rms_bwd_call(x, g, r, dy)


_rmsnorm2d.defvjp(_rmsnorm2d_fwd, _rmsnorm2d_bwd)


def _rmsnorm_rows(x, g):
    D = x.shape[-1]
    return _rmsnorm2d(x.reshape(-1, D), g.reshape(1, D)).reshape(x.shape)


def _glu_fwd_call(gu):
    M, F2 = gu.shape
    F = F2 // 2
    tr = _row_tile(M, 4 * F2, 2 * ROW_BLOCK_BYTES)

    def body(g_ref, u_ref, o_ref):
        gv = g_ref[...].astype(F32)
        o_ref[...] = (gv * jax.nn.sigmoid(gv) * u_ref[...].astype(F32)).astype(o_ref.dtype)

    return _pcall(
        body, name="swiglu_fwd", grid=(M // tr,),
        in_specs=[pl.BlockSpec((tr, F), lambda i: (i, 0)), pl.BlockSpec((tr, F), lambda i: (i, 1))],
        out_specs=pl.BlockSpec((tr, F), lambda i: (i, 0)),
        out_shape=jax.ShapeDtypeStruct((M, F), gu.dtype),
        compiler_params=pltpu.CompilerParams(dimension_semantics=("parallel",),
                                             vmem_limit_bytes=VMEM_LIMIT_BYTES),
    )(gu, gu)


def _glu_bwd_call(gu, da):
    M, F2 = gu.shape
    F = F2 // 2
    tr = _row_tile(M, 4 * F2, 2 * ROW_BLOCK_BYTES)

    def body(g_ref, u_ref, da_ref, o_ref):
        gv = g_ref[...].astype(F32)
        s = jax.nn.sigmoid(gv)
        dav = da_ref[...].astype(F32)
        o_ref[:, :F] = (dav * u_ref[...].astype(F32) * (s * (1.0 + gv * (1.0 - s)))).astype(o_ref.dtype)
        o_ref[:, F:] = (dav * (gv * s)).astype(o_ref.dtype)

    return _pcall(
        body, name="swiglu_bwd", grid=(M // tr,),
        in_specs=[pl.BlockSpec((tr, F), lambda i: (i, 0)), pl.BlockSpec((tr, F), lambda i: (i, 1)),
                  pl.BlockSpec((tr, F), lambda i: (i, 0))],
        out_specs=pl.BlockSpec((tr, F2), lambda i: (i, 0)),
        out_shape=jax.ShapeDtypeStruct((M, F2), gu.dtype),
        compiler_params=pltpu.CompilerParams(dimension_semantics=("parallel",),
                                             vmem_limit_bytes=VMEM_LIMIT_BYTES),
    )(gu, gu, da)


@jax.custom_vjp
def _glu(gu):
    return _glu_fwd_call(gu)


def _glu_fwd(gu):
    return _glu_fwd_call(gu), gu


def _glu_bwd(gu, da):
    return (_glu_bwd_call(gu, da),)


_glu.defvjp(_glu_fwd, _glu_bwd)


def _swiglu(h, wi, wo):
    lead, D = h.shape[:-1], h.shape[-1]
    gu = dense(h.reshape(-1, D), wi[0], wi[1], BF16)
    return dense(_glu(gu), wo[0], wo[1], F32).reshape(lead + (wo[0].shape[1],))


def _causal_dwconv(x, w):
    K, C = w.shape
    return lax.conv_general_dilated(
        x, w[:, None, :], window_strides=(1,), padding=[(K - 1, 0)],
        dimension_numbers=("NWC", "WIO", "NWC"), feature_group_count=C)


def _rope(x, cos, sin):
    half = x.shape[-1] // 2
    x1, x2 = x[..., :half], x[..., half:]
    return jnp.concatenate([x1 * cos - x2 * sin, x2 * cos + x1 * sin], axis=-1)


def _fox_branch(p, b_f):
    B, T, _ = p.shape
    qkv = p[..., OFF_FOX_QKV:OFF_FOX_F].reshape(B, T, 3, FOX_HEADS, FOX_DH)
    q = qkv[:, :, 0].transpose(0, 2, 1, 3)
    k = qkv[:, :, 1].transpose(0, 2, 1, 3)
    v = qkv[:, :, 2].transpose(0, 2, 1, 3)
    log_f = jax.nn.log_sigmoid(p[..., OFF_FOX_F:OFF_MLA_CQ] + b_f)
    cum = jnp.cumsum(log_f, axis=1).transpose(0, 2, 1)
    o = _make_attention(FOX_DH ** -0.5, True)(q, k, v, cum)
    return o.transpose(0, 2, 1, 3).reshape(B, T, FOX_HEADS * FOX_DH)


def _mla_branch(p, g_qn, w_q_up, g_kvn, w_kv_up, cos, sin):
    B, T, _ = p.shape
    cq = _rmsnorm(p[..., OFF_MLA_CQ:OFF_MLA_CKV], g_qn)
    q = (cq @ w_q_up).reshape(B, T, MLA_HEADS, MLA_NOPE + MLA_ROPE)
    ckv = _rmsnorm(p[..., OFF_MLA_CKV:OFF_MLA_KR], g_kvn)
    kv = (ckv @ w_kv_up).reshape(B, T, MLA_HEADS, MLA_NOPE + MLA_DV)
    k_rope = _rope(p[..., OFF_MLA_KR:OFF_GDN_QKV], cos, sin)
    q_rope = _rope(q[..., MLA_NOPE:], cos[:, None], sin[:, None])
    q = jnp.concatenate([q[..., :MLA_NOPE], q_rope], axis=-1)
    k = jnp.concatenate([kv[..., :MLA_NOPE],
                         jnp.broadcast_to(k_rope[:, :, None], (B, T, MLA_HEADS, MLA_ROPE))], axis=-1)
    v = kv[..., MLA_NOPE:]
    o = _make_attention((MLA_NOPE + MLA_ROPE) ** -0.5, False)(
        q.transpose(0, 2, 1, 3), k.transpose(0, 2, 1, 3), v.transpose(0, 2, 1, 3),
        jnp.zeros((B, MLA_HEADS, T), F32))
    return o.transpose(0, 2, 1, 3).reshape(B, T, MLA_HEADS * MLA_DV)


def _gdn_branch(p, conv_w, a_log, dt_bias, g_on):
    B, T, _ = p.shape
    H, DK, DV, C = GDN_HEADS, GDN_DK, GDN_DV, GDN_CHUNK
    qkv = jax.nn.silu(_causal_dwconv(p[..., OFF_GDN_QKV:OFF_GDN_A], conv_w))
    q = _l2norm(qkv[..., :H * DK].reshape(B, T, H, DK)) * DK ** -0.5
    k = _l2norm(qkv[..., H * DK:2 * H * DK].reshape(B, T, H, DK))
    v = qkv[..., 2 * H * DK:].reshape(B, T, H, DV)
    beta = jax.nn.sigmoid(p[..., OFF_GDN_B:OFF_GDN_G])
    g = -jnp.exp(a_log) * jax.nn.softplus(p[..., OFF_GDN_A:OFF_GDN_B] + dt_bias)
    nc = T // C

    def chunks(t):
        return jnp.moveaxis(t, 2, 1).reshape((B, H, nc, C) + t.shape[3:])

    q, k, v, beta, g = chunks(q), chunks(k), chunks(v), chunks(beta), chunks(g)
    G = jnp.cumsum(g, axis=-1)
    idx = jnp.arange(C)
    strict = idx[:, None] > idx[None, :]
    incl = idx[:, None] >= idx[None, :]
    decay = jnp.exp(jnp.where(incl, G[..., :, None] - G[..., None, :], NEG_INF))
    kb = k * beta[..., None]
    vb = v * beta[..., None]
    m = jnp.eye(C, dtype=F32) + jnp.where(
        strict, jnp.einsum("bhnik,bhnjk->bhnij", kb, k) * decay, 0.0)
    rhs = jnp.concatenate([kb * jnp.exp(G)[..., None], vb], axis=-1)
    sol = lax.linalg.triangular_solve(m, rhs, left_side=True, lower=True, unit_diagonal=True)
    w, u = sol[..., :DK], sol[..., DK:]
    qk = jnp.where(incl, jnp.einsum("bhnik,bhnjk->bhnij", q, k) * decay, 0.0)
    q_dec = q * jnp.exp(G)[..., None]
    k_dec = k * jnp.exp(G[..., -1:] - G)[..., None]
    g_last = jnp.exp(G[..., -1])
    xs = (jnp.moveaxis(q_dec, 2, 0), jnp.moveaxis(k_dec, 2, 0), jnp.moveaxis(w, 2, 0),
          jnp.moveaxis(u, 2, 0), jnp.moveaxis(qk, 2, 0), jnp.moveaxis(g_last, 2, 0))

    def step(S, inp):
        q_c, k_c, w_c, u_c, qk_c, gl_c = inp
        v_new = u_c - jnp.einsum("bhck,bhkv->bhcv", w_c, S)
        o_c = jnp.einsum("bhck,bhkv->bhcv", q_c, S) + jnp.einsum("bhij,bhjv->bhiv", qk_c, v_new)
        S = S * gl_c[..., None, None] + jnp.einsum("bhck,bhcv->bhkv", k_c, v_new)
        return S, o_c

    S0 = jnp.zeros((B, H, DK, DV), F32)
    _, o = lax.scan(step, S0, xs)
    o = jnp.moveaxis(o, 0, 2).reshape(B, H, T, DV).transpose(0, 2, 1, 3)
    gate = jax.nn.silu(p[..., OFF_GDN_G:OFF_LRU]).reshape(B, T, H, DV)
    o = _rmsnorm(o, g_on) * gate
    return o.reshape(B, T, H * DV)


def _rglru_branch(p, valid, conv_w, conv_b, w_a, b_a, w_x, b_x, lam):
    B, T, _ = p.shape
    xr = _causal_dwconv(p[..., OFF_LRU:N_IN], conv_w) + conv_b
    xr = jnp.where(valid[None, :, None], xr, 0)
    xb = xr.reshape(B, T, LRU_BLOCKS, LRU_WIDTH // LRU_BLOCKS)
    r = jax.nn.sigmoid(jnp.einsum("btni,nij->btnj", xb, w_a).reshape(B, T, LRU_WIDTH) + b_a)
    ig = jax.nn.sigmoid(jnp.einsum("btni,nij->btnj", xb, w_x).reshape(B, T, LRU_WIDTH) + b_x)
    log_a = -LRU_C * r * jax.nn.softplus(-lam)
    a = jnp.exp(log_a)
    b = jnp.sqrt(-jnp.expm1(2.0 * log_a)) * ig * xr

    def combine(e1, e2):
        return (e1[0] * e2[0], e2[0] * e1[1] + e2[1])

    _, h = lax.associative_scan(combine, (a, b), axis=1)
    return h


def _mixer(u, valid, cos, sin, w, wb, l):
    def pair(name, *idx):
        return wb[name][(l,) + idx], w[name][(l,) + idx]

    p = dense_nd(u, *pair("w_in"))
    ys = (_fox_branch(p, w["fox_bf"][l]),
          _mla_branch(p, w["mla_gq"][l], w["mla_wq"][l], w["mla_gkv"][l], w["mla_wkv"][l], cos, sin),
          _gdn_branch(p, w["gdn_conv"][l], w["gdn_alog"][l], w["gdn_dtb"][l], w["gdn_gon"][l]),
          _rglru_branch(p, valid, w["lru_conv"][l], w["lru_conv_b"][l], w["lru_wa"][l], w["lru_ba"][l],
                        w["lru_wx"][l], w["lru_bx"][l], w["lru_lam"][l]))
    D = u.shape[-1]
    w_gate = [jnp.moveaxis(t, 0, 1).reshape(D, N_BRANCH * D) for t in pair("w_gate")]
    gates = jax.nn.sigmoid(dense_nd(u, *w_gate) + w["b_gate"][l].reshape(-1))
    merged = gates[..., :D] * dense_nd(ys[0], *pair("w_branch", 0))
    for n in range(1, N_BRANCH):
        merged = merged + gates[..., n * D:(n + 1) * D] * dense_nd(ys[n], *pair("w_branch", n))
    return dense_nd(merged, *pair("w_out"))


def _local_loss(w, x, loss_target, wb):
    B, S, D = x.shape
    T = BLOCK + S
    h = jnp.concatenate([jnp.zeros((B, PAD_LEN, D), F32),
                         jnp.broadcast_to(w["meta"][None], (B, N_META, D)), x], axis=1)
    pos = jnp.arange(T)
    valid = pos >= PAD_LEN
    rel = (pos - PAD_LEN).astype(F32)
    inv_freq = ROPE_BASE ** (-(jnp.arange(0, MLA_ROPE, 2, dtype=F32) / MLA_ROPE))
    ang = rel[:, None] * inv_freq[None, :]
    cos, sin = jnp.cos(ang), jnp.sin(ang)
    def pair(name, l):
        return wb[name][l], w[name][l]

    for l in range(DEPTH):
        h = h + 0.5 * _swiglu(_rmsnorm_rows(h, w["ln_ffn1"][l]), pair("ffn1_wi", l), pair("ffn1_wo", l))
        u = jnp.where(valid[None, :, None], _rmsnorm_rows(h, w["ln_mix"][l]), 0)
        h = h + _mixer(u, valid, cos, sin, w, wb, l)
        h = h + 0.5 * _swiglu(_rmsnorm_rows(h, w["ln_ffn2"][l]), pair("ffn2_wi", l), pair("ffn2_wo", l))
    y = _rmsnorm_rows(h, w["ln_final"])[:, BLOCK:]
    err = jnp.square(y - loss_target)
    return 0.5 * jnp.sum(jnp.mean(err, axis=-1))


MESH = pl.DeviceIdType.MESH
HBM = pl.BlockSpec(memory_space=pl.ANY)


def _place():
    x, y, c = lax.axis_index("x"), lax.axis_index("y"), lax.axis_index("c")
    return x, y, c


def _other_chip(x, y, r):
    return (1 - x if r & 2 else x), (1 - y if r & 1 else y)


def _gather_chips(shard):
    R, C = shard.shape
    H = R // 2

    def body(x_ref, out_ref, send_sems, recv_sems, local_sem):
        x, y, c = _place()
        me = 2 * x + y

        def rows(chip, half):
            return out_ref.at[chip, pl.ds(half * H, H), :]

        def copy(sem, src, dst, to):
            return pltpu.make_async_remote_copy(src_ref=src, dst_ref=dst, send_sem=send_sems.at[sem],
                                                recv_sem=recv_sems.at[sem], device_id=to, device_id_type=MESH)

        mine = pltpu.make_async_copy(x_ref, out_ref.at[me], local_sem)
        mine.start()
        started = []
        for r in (1, 2, 3):
            ox, oy = _other_chip(x, y, r)
            cp = copy(r - 1, x_ref.at[pl.ds(c * H, H), :], rows(me, c), (ox, oy, c))
            cp.start()
            started.append(cp)
        for r in (1, 2, 3):
            ox, oy = _other_chip(x, y, r)
            src = 2 * ox + oy
            copy(r - 1, rows(src, c), rows(src, c), (x, y, c)).wait_recv()
            fw = copy(2 + r, rows(src, c), rows(src, c), (x, y, 1 - c))
            fw.start()
            started.append(fw)
        for r in (1, 2, 3):
            ox, oy = _other_chip(x, y, r)
            src = 2 * ox + oy
            copy(2 + r, rows(src, 1 - c), rows(src, 1 - c), (x, y, c)).wait_recv()
        for cp in started:
            cp.wait_send()
        mine.wait()

    return _pcall(
        body, name="gather_chips", in_specs=[HBM], out_specs=HBM,
        out_shape=jax.ShapeDtypeStruct((N_CHIPS, R, C), shard.dtype),
        scratch_shapes=[pltpu.SemaphoreType.DMA((6,)), pltpu.SemaphoreType.DMA((6,)),
                        pltpu.SemaphoreType.DMA],
    )(shard)


def _window(ref, lead, axis, chip, width):
    idx = [slice(None)] * len(ref.shape)
    if lead is not None:
        idx[0] = lead
    idx[axis] = pl.ds(chip * width, width)
    return ref.at[tuple(idx)]


def _gather_layers(shards, axes):
    n_t = len(shards)
    widths = [s.shape[a] for s, a in zip(shards, axes)]
    fulls = [s.shape[:a] + (N_CHIPS * s.shape[a],) + s.shape[a + 1:] for s, a in zip(shards, axes)]

    def body(*refs):
        x_refs, out_refs = refs[:n_t], refs[n_t:2 * n_t]
        send_sems, recv_sems, local_sems = refs[2 * n_t:]
        x, y, c = _place()
        me = 2 * x + y

        def win(t, lead, chip):
            return _window(out_refs[t], lead, axes[t], chip, widths[t])

        def copy(sem, src, dst, to):
            return pltpu.make_async_remote_copy(src_ref=src, dst_ref=dst, send_sem=send_sems.at[sem],
                                                recv_sem=recv_sems.at[sem], device_id=to, device_id_type=MESH)

        local, started = [], []
        for t in range(n_t):
            cp = pltpu.make_async_copy(x_refs[t], win(t, None, me), local_sems.at[t])
            cp.start()
            local.append(cp)
        for r in (1, 2, 3):
            ox, oy = _other_chip(x, y, r)
            for t in range(n_t):
                cp = copy(6 * t + r - 1, x_refs[t].at[c], win(t, c, me), (ox, oy, c))
                cp.start()
                started.append(cp)
        for r in (1, 2, 3):
            ox, oy = _other_chip(x, y, r)
            src = 2 * ox + oy
            for t in range(n_t):
                copy(6 * t + r - 1, win(t, c, src), win(t, c, src), (x, y, c)).wait_recv()
                fw = copy(6 * t + 2 + r, win(t, c, src), win(t, c, src), (x, y, 1 - c))
                fw.start()
                started.append(fw)
        for r in (1, 2, 3):
            ox, oy = _other_chip(x, y, r)
            src = 2 * ox + oy
            for t in range(n_t):
                copy(6 * t + 2 + r, win(t, 1 - c, src), win(t, 1 - c, src), (x, y, c)).wait_recv()
        for cp in started:
            cp.wait_send()
        for cp in local:
            cp.wait()

    return _pcall(
        body, name="gather_layers", in_specs=[HBM] * n_t, out_specs=[HBM] * n_t,
        out_shape=[jax.ShapeDtypeStruct(f, s.dtype) for f, s in zip(fulls, shards)],
        scratch_shapes=[pltpu.SemaphoreType.DMA((6 * n_t,)), pltpu.SemaphoreType.DMA((6 * n_t,)),
                        pltpu.SemaphoreType.DMA((n_t,))],
    )(*shards)


def _swap_layers(gs):
    n_t = len(gs)

    def body(*refs):
        g_refs, out_refs = refs[:n_t], refs[n_t:2 * n_t]
        send_sems, recv_sems = refs[2 * n_t:]
        x, y, c = _place()
        cps = []
        for t in range(n_t):
            cp = pltpu.make_async_remote_copy(
                src_ref=g_refs[t].at[1 - c], dst_ref=out_refs[t], send_sem=send_sems.at[t],
                recv_sem=recv_sems.at[t], device_id=(x, y, 1 - c), device_id_type=MESH)
            cp.start()
            cps.append(cp)
        for cp in cps:
            cp.wait()

    return _pcall(
        body, name="swap_layers", in_specs=[HBM] * n_t, out_specs=[HBM] * n_t,
        out_shape=[jax.ShapeDtypeStruct(g.shape[1:], g.dtype) for g in gs],
        scratch_shapes=[pltpu.SemaphoreType.DMA((n_t,)), pltpu.SemaphoreType.DMA((n_t,))],
    )(*gs)


def _add_layer(g, other, c):
    shape = other.shape
    last = shape[-1]
    rows = math.prod(shape[:-1])
    tr = _row_tile(rows, 4 * last, ROW_BLOCK_BYTES)

    def body(c_ref, a_ref, b_ref, o_ref):
        o_ref[...] = a_ref[0] + b_ref[...]

    out = _pcall(
        body, name="add_layer",
        grid_spec=pltpu.PrefetchScalarGridSpec(
            num_scalar_prefetch=1, grid=(rows // tr,),
            in_specs=[pl.BlockSpec((1, tr, last), lambda i, c_ref: (c_ref[0], i, 0)),
                      pl.BlockSpec((tr, last), lambda i, c_ref: (i, 0))],
            out_specs=pl.BlockSpec((tr, last), lambda i, c_ref: (i, 0))),
        out_shape=jax.ShapeDtypeStruct((rows, last), F32),
        compiler_params=pltpu.CompilerParams(dimension_semantics=("parallel",)),
    )(c.reshape(1).astype(jnp.int32), g.reshape(2, rows, last), other.reshape(rows, last))
    return out.reshape(shape)


def _scatter_layers(ps, axes):
    n_t = len(ps)
    widths = [p.shape[a] // N_CHIPS for p, a in zip(ps, axes)]
    wins = [p.shape[:a] + (w,) + p.shape[a + 1:] for p, a, w in zip(ps, axes, widths)]

    def body(*refs):
        p_refs, out_refs = refs[:n_t], refs[n_t:2 * n_t]
        send_sems, recv_sems, local_sems = refs[2 * n_t:]
        x, y, c = _place()
        me = 2 * x + y
        cps = []
        for t in range(n_t):
            cp = pltpu.make_async_copy(_window(p_refs[t], None, axes[t], me, widths[t]), out_refs[t].at[0],
                                       local_sems.at[t])
            cp.start()
            cps.append(cp)
        for r in (1, 2, 3):
            ox, oy = _other_chip(x, y, r)
            for t in range(n_t):
                cp = pltpu.make_async_remote_copy(
                    src_ref=_window(p_refs[t], None, axes[t], 2 * ox + oy, widths[t]), dst_ref=out_refs[t].at[r],
                    send_sem=send_sems.at[3 * t + r - 1], recv_sem=recv_sems.at[3 * t + r - 1],
                    device_id=(ox, oy, c), device_id_type=MESH)
                cp.start()
                cps.append(cp)
        for cp in cps:
            cp.wait()

    return _pcall(
        body, name="scatter_layers", in_specs=[HBM] * n_t, out_specs=[HBM] * n_t,
        out_shape=[jax.ShapeDtypeStruct((N_CHIPS,) + w, p.dtype) for w, p in zip(wins, ps)],
        scratch_shapes=[pltpu.SemaphoreType.DMA((3 * n_t,)), pltpu.SemaphoreType.DMA((3 * n_t,)),
                        pltpu.SemaphoreType.DMA((n_t,))],
    )(*ps)


def _sum_chips(q, me):
    shape = q.shape[1:]
    C = shape[-1]
    H = math.prod(shape[:-1])
    tr = _row_tile(H, 4 * C, ROW_BLOCK_BYTES)

    def body(me_ref, q0, q1, q2, q3, o_ref):
        o_ref[...] = ((q0[0] + q1[0]) + q2[0]) + q3[0]

    def spec(chip):
        return pl.BlockSpec((1, tr, C), lambda i, me_ref: (jnp.bitwise_xor(me_ref[0], chip), i, 0))

    q = q.reshape(N_CHIPS, H, C)
    return _pcall(
        body, name="sum_chips",
        grid_spec=pltpu.PrefetchScalarGridSpec(
            num_scalar_prefetch=1, grid=(H // tr,),
            in_specs=[spec(0), spec(1), spec(2), spec(3)],
            out_specs=pl.BlockSpec((tr, C), lambda i, me_ref: (i, 0))),
        out_shape=jax.ShapeDtypeStruct((H, C), F32),
        compiler_params=pltpu.CompilerParams(dimension_semantics=("parallel",)),
    )(me.reshape(1).astype(jnp.int32), q, q, q, q).reshape(shape)


def _join_layers(rs):
    n_t = len(rs)

    def body(*refs):
        r_refs, out_refs = refs[:n_t], refs[n_t:2 * n_t]
        send_sems, recv_sems, local_sems = refs[2 * n_t:]
        x, y, c = _place()
        cps = []
        for t in range(n_t):
            mine = pltpu.make_async_copy(r_refs[t], out_refs[t].at[c], local_sems.at[t])
            mine.start()
            cps.append(mine)
            cp = pltpu.make_async_remote_copy(
                src_ref=r_refs[t], dst_ref=out_refs[t].at[c], send_sem=send_sems.at[t],
                recv_sem=recv_sems.at[t], device_id=(x, y, 1 - c), device_id_type=MESH)
            cp.start()
            cps.append(cp)
        for cp in cps:
            cp.wait()

    return _pcall(
        body, name="join_layers", in_specs=[HBM] * n_t, out_specs=[HBM] * n_t,
        out_shape=[jax.ShapeDtypeStruct((2,) + r.shape, r.dtype) for r in rs],
        scratch_shapes=[pltpu.SemaphoreType.DMA((n_t,)), pltpu.SemaphoreType.DMA((n_t,)),
                        pltpu.SemaphoreType.DMA((n_t,))],
    )(*rs)


def _reduce_scatter_layers(gs, axes):
    x, y, c = _place()
    others = _swap_layers(gs)
    ps = [_add_layer(g, o, c) for g, o in zip(gs, others)]
    qs = _scatter_layers(ps, [a - 1 for a in axes])
    rs = [_sum_chips(q, 2 * x + y) for q in qs]
    return _join_layers(rs)


def _allgather_devices(flat):
    R, C = flat.shape

    def body(x_ref, out_ref, send_sems, recv_sems, local_sem):
        x, y, c = _place()
        me = 4 * x + 2 * y + c
        mine = pltpu.make_async_copy(x_ref, out_ref.at[me], local_sem)
        mine.start()
        cps = []
        for m in range(1, 8):
            ox, oy = _other_chip(x, y, m >> 1)
            oc = 1 - c if m & 1 else c
            cp = pltpu.make_async_remote_copy(
                src_ref=x_ref, dst_ref=out_ref.at[me], send_sem=send_sems.at[m - 1],
                recv_sem=recv_sems.at[m - 1], device_id=(ox, oy, oc), device_id_type=MESH)
            cp.start()
            cps.append(cp)
        for cp in cps:
            cp.wait()
        mine.wait()

    return _pcall(
        body, name="allgather_devices", in_specs=[HBM], out_specs=HBM,
        out_shape=jax.ShapeDtypeStruct((8, R, C), flat.dtype),
        scratch_shapes=[pltpu.SemaphoreType.DMA((7,)), pltpu.SemaphoreType.DMA((7,)),
                        pltpu.SemaphoreType.DMA],
    )(flat)


def _sum_devices(slots):
    _, R, C = slots.shape
    tr = _row_tile(R, 4 * C, ROW_BLOCK_BYTES // 4)

    def body(*refs):
        o_ref = refs[8]
        acc = refs[0][0]
        for d in range(1, 8):
            acc = acc + refs[d][0]
        o_ref[...] = acc

    def spec(d):
        return pl.BlockSpec((1, tr, C), lambda i: (d, i, 0))

    return _pcall(
        body, name="sum_devices", grid=(R // tr,), in_specs=[spec(d) for d in range(8)],
        out_specs=pl.BlockSpec((tr, C), lambda i: (i, 0)),
        out_shape=jax.ShapeDtypeStruct((R, C), F32),
        compiler_params=pltpu.CompilerParams(dimension_semantics=("parallel",)),
    )(*([slots] * 8))


def _adamw(w, g, m, v):
    shape = w.shape
    C = shape[-1]
    R = math.prod(shape[:-1])
    w, g, m, v = (t.reshape(R, C) for t in (w, g, m, v))
    tr = _row_tile(R, 4 * C, ROW_BLOCK_BYTES // 2)
    c1 = 1.0 - ADAM_B1 ** ADAM_STEP
    c2 = 1.0 - ADAM_B2 ** ADAM_STEP

    def body(w_ref, g_ref, m_ref, v_ref, d_ref, nm_ref, nv_ref):
        gg = g_ref[...]
        nm = ADAM_B1 * m_ref[...] + (1.0 - ADAM_B1) * gg
        nv = ADAM_B2 * v_ref[...] + (1.0 - ADAM_B2) * jnp.square(gg)
        d_ref[...] = -ADAM_LR * ((nm / c1) / (jnp.sqrt(nv / c2) + ADAM_EPS) + ADAM_WD * w_ref[...])
        nm_ref[...] = nm
        nv_ref[...] = nv

    spec = pl.BlockSpec((tr, C), lambda i: (i, 0))
    outs = _pcall(
        body, name="adamw", grid=(R // tr,), in_specs=[spec] * 4, out_specs=[spec] * 3,
        out_shape=[jax.ShapeDtypeStruct((R, C), F32)] * 3,
        compiler_params=pltpu.CompilerParams(dimension_semantics=("parallel",)),
    )(w, g, m, v)
    return [o.reshape(shape) for o in outs]


def _to_flat(parts):
    flat = jnp.concatenate([p.reshape(-1) for p in parts])
    unit = FLAT_COLS * FLAT_ROW_ALIGN
    pad = (-flat.shape[0]) % unit
    if pad:
        flat = jnp.concatenate([flat, jnp.zeros((pad,), flat.dtype)])
    return flat.reshape(-1, FLAT_COLS)


def _from_flat(flat, shapes):
    flat = flat.reshape(-1)
    out, off = [], 0
    for s in shapes:
        n = math.prod(s)
        out.append(flat[off:off + n].reshape(s))
        off += n
    return out


def kernel(x, meta, ln_ffn1, ffn1_wi, ffn1_wo, ln_mix, w_in, fox_bf, mla_gq, mla_wq, mla_gkv, mla_wkv, gdn_conv, gdn_alog, gdn_dtb, gdn_gon, lru_conv, lru_conv_b, lru_wa, lru_ba, lru_wx, lru_bx, lru_lam, w_gate, b_gate, w_branch, w_out, ln_ffn2, ffn2_wi, ffn2_wo, ln_final, loss_target, m_meta, m_ln_ffn1, m_ffn1_wi, m_ffn1_wo, m_ln_mix, m_w_in, m_fox_bf, m_mla_gq, m_mla_wq, m_mla_gkv, m_mla_wkv, m_gdn_conv, m_gdn_alog, m_gdn_dtb, m_gdn_gon, m_lru_conv, m_lru_conv_b, m_lru_wa, m_lru_ba, m_lru_wx, m_lru_bx, m_lru_lam, m_w_gate, m_b_gate, m_w_branch, m_w_out, m_ln_ffn2, m_ffn2_wi, m_ffn2_wo, m_ln_final, v_meta, v_ln_ffn1, v_ffn1_wi, v_ffn1_wo, v_ln_mix, v_w_in, v_fox_bf, v_mla_gq, v_mla_wq, v_mla_gkv, v_mla_wkv, v_gdn_conv, v_gdn_alog, v_gdn_dtb, v_gdn_gon, v_lru_conv, v_lru_conv_b, v_lru_wa, v_lru_ba, v_lru_wx, v_lru_bx, v_lru_lam, v_w_gate, v_b_gate, v_w_branch, v_w_out, v_ln_ffn2, v_ffn2_wi, v_ffn2_wo, v_ln_final):
    ws = (meta, ln_ffn1, ffn1_wi, ffn1_wo, ln_mix, w_in, fox_bf, mla_gq, mla_wq, mla_gkv, mla_wkv, gdn_conv, gdn_alog, gdn_dtb, gdn_gon, lru_conv, lru_conv_b, lru_wa, lru_ba, lru_wx, lru_bx, lru_lam, w_gate, b_gate, w_branch, w_out, ln_ffn2, ffn2_wi, ffn2_wo, ln_final)
    ms = (m_meta, m_ln_ffn1, m_ffn1_wi, m_ffn1_wo, m_ln_mix, m_w_in, m_fox_bf, m_mla_gq, m_mla_wq, m_mla_gkv, m_mla_wkv, m_gdn_conv, m_gdn_alog, m_gdn_dtb, m_gdn_gon, m_lru_conv, m_lru_conv_b, m_lru_wa, m_lru_ba, m_lru_wx, m_lru_bx, m_lru_lam, m_w_gate, m_b_gate, m_w_branch, m_w_out, m_ln_ffn2, m_ffn2_wi, m_ffn2_wo, m_ln_final)
    vs = (v_meta, v_ln_ffn1, v_ffn1_wi, v_ffn1_wo, v_ln_mix, v_w_in, v_fox_bf, v_mla_gq, v_mla_wq, v_mla_gkv, v_mla_wkv, v_gdn_conv, v_gdn_alog, v_gdn_dtb, v_gdn_gon, v_lru_conv, v_lru_conv_b, v_lru_wa, v_lru_ba, v_lru_wx, v_lru_bx, v_lru_lam, v_w_gate, v_b_gate, v_w_branch, v_w_out, v_ln_ffn2, v_ffn2_wi, v_ffn2_wo, v_ln_final)
    names = [n for n, _ in WEIGHT_SPECS]
    axis = dict(WEIGHT_SPECS)
    wd, md, vd = dict(zip(names, ws)), dict(zip(names, ms)), dict(zip(names, vs))
    shapes = {n: wd[n].shape for n in names}
    big = [n for n in names if n in LARGE]
    few = [n for n in names if axis[n] is not None and n not in LARGE]
    whole = [n for n in names if axis[n] is None]
    x_, y_, _ = _place()
    chip = 2 * x_ + y_

    def to_dma(n, a):
        ax, w = axis[n], shapes[n][axis[n]]
        nd = len(shapes[n])
        if (ax == nd - 1 and w % LANES) or (ax == nd - 2 and w % 16):
            parts = a.shape[ax] // w
            a = jnp.moveaxis(a.reshape(a.shape[:ax] + (parts, w) + a.shape[ax + 1:]), ax, 1)
            return a, 1
        return a, ax

    def from_dma(n, a):
        ax = axis[n]
        if a.ndim == len(shapes[n]):
            return a
        a = jnp.moveaxis(a, 1, ax)
        return a.reshape(a.shape[:ax] + (-1,) + a.shape[ax + 2:])

    dma = [to_dma(n, wd[n].astype(BF16)) for n in big]
    fulls = _gather_layers([a for a, _ in dma], [ax for _, ax in dma])
    full_bf16 = {n: from_dma(n, f) for n, f in zip(big, fulls)}
    full = {n: jnp.zeros(f.shape, F32) for n, f in full_bf16.items()}
    gathered = _gather_chips(_to_flat([wd[n] for n in few]))
    per_chip = [_from_flat(gathered[k], [shapes[n] for n in few]) for k in range(N_CHIPS)]
    for i, n in enumerate(few):
        full[n] = jnp.concatenate([per_chip[k][i] for k in range(N_CHIPS)], axis=axis[n])
    full.update({n: wd[n] for n in whole})

    loss, (gw, gx) = jax.value_and_grad(_local_loss, argnums=(0, 1))(full, x, loss_target, full_bf16)
    loss = lax.psum(loss, ("x", "y", "c"))

    dma = [to_dma(n, gw[n]) for n in big]
    reduced = _reduce_scatter_layers([a for a, _ in dma], [ax for _, ax in dma])
    grads = {n: r.reshape(shapes[n]) for n, r in zip(big, reduced)}
    rest = few + whole
    summed = _from_flat(_sum_devices(_allgather_devices(_to_flat([gw[n] for n in rest]))),
                        [gw[n].shape for n in rest])
    for n, g in zip(rest, summed):
        if axis[n] is not None:
            g = lax.dynamic_slice_in_dim(g, chip * shapes[n][axis[n]], shapes[n][axis[n]], axis=axis[n])
        grads[n] = g

    delta, new_m, new_v = {}, {}, {}
    for n in big:
        delta[n], new_m[n], new_v[n] = _adamw(wd[n], grads[n], md[n], vd[n])
    outs = _adamw(*[_to_flat([d[n] for n in rest]) for d in (wd, grads, md, vd)])
    for res, flat in zip((delta, new_m, new_v), outs):
        res.update(zip(rest, _from_flat(flat, [shapes[n] for n in rest])))

    return (loss, gx, *[grads[n] for n in names], *[delta[n] for n in names],
            *[new_m[n] for n in names], *[new_v[n] for n in names])
```

```python
import functools
import math

import jax
import jax.numpy as jnp
from jax import lax
from jax.experimental import pallas as pl
from jax.experimental.pallas import tpu as pltpu

F32 = jnp.float32
BF16 = jnp.bfloat16

N_META = 16
BLOCK = 128
PAD_LEN = BLOCK - N_META
EPS = 1e-6
NEG_INF = -1e30
N_BRANCH = 4
FOX_HEADS, FOX_DH = 4, 64
MLA_HEADS, MLA_NOPE, MLA_ROPE, MLA_DV = 4, 64, 32, 64
MLA_Q_RANK, MLA_KV_RANK = 192, 128
ROPE_BASE = 10000.0
GDN_HEADS, GDN_DK, GDN_DV, GDN_CHUNK = 4, 64, 64, 64
LRU_WIDTH, LRU_BLOCKS, LRU_C = 256, 4, 8.0
DEPTH = 2

OFF_FOX_QKV = 0
OFF_FOX_F = OFF_FOX_QKV + 3 * FOX_HEADS * FOX_DH
OFF_MLA_CQ = OFF_FOX_F + FOX_HEADS
OFF_MLA_CKV = OFF_MLA_CQ + MLA_Q_RANK
OFF_MLA_KR = OFF_MLA_CKV + MLA_KV_RANK
OFF_GDN_QKV = OFF_MLA_KR + MLA_ROPE
OFF_GDN_A = OFF_GDN_QKV + GDN_HEADS * (2 * GDN_DK + GDN_DV)
OFF_GDN_B = OFF_GDN_A + GDN_HEADS
OFF_GDN_G = OFF_GDN_B + GDN_HEADS
OFF_LRU = OFF_GDN_G + GDN_HEADS * GDN_DV
N_IN = OFF_LRU + LRU_WIDTH

ADAM_LR, ADAM_B1, ADAM_B2, ADAM_EPS, ADAM_WD, ADAM_STEP = 0.001, 0.9, 0.999, 1e-08, 0.01, 10

WEIGHT_SPECS = (
    ("meta", 1), ("ln_ffn1", None), ("ffn1_wi", 2), ("ffn1_wo", 1), ("ln_mix", None), ("w_in", 2),
    ("fox_bf", None), ("mla_gq", None), ("mla_wq", 2), ("mla_gkv", None), ("mla_wkv", 2),
    ("gdn_conv", 2), ("gdn_alog", None), ("gdn_dtb", None), ("gdn_gon", None), ("lru_conv", 2),
    ("lru_conv_b", None), ("lru_wa", None), ("lru_ba", None), ("lru_wx", None), ("lru_bx", None),
    ("lru_lam", None), ("w_gate", 2), ("b_gate", 2), ("w_branch", 3), ("w_out", 1),
    ("ln_ffn2", None), ("ffn2_wi", 2), ("ffn2_wo", 1), ("ln_final", None),
)
N_CHIPS = 4
LARGE = ("ffn1_wi", "ffn1_wo", "w_in", "w_gate", "w_branch", "w_out", "ffn2_wi", "ffn2_wo")

LANES = 128
VMEM_LIMIT_BYTES = 48 * 1024 * 1024
FLAT_COLS = 512
FLAT_ROW_ALIGN = 64


def _pcall(body, **kw):
    return pl.pallas_call(body, **kw)


def _pick(n, cands):
    for c in cands:
        if n % c == 0:
            return c
    return n


_DN = {"nn": (((1,), (0,)), ((), ())), "nt": (((1,), (1,)), ((), ())), "tn": (((0,), (0,)), ((), ()))}


MATMUL_OPERAND_TILE_BYTES = 8 * 1024 * 1024


def _k_tile(K, row_bytes, lane_axis):
    for tk in (K, 4224, 2816, 2112, 1408, 1056, 1024, 768, 704, 512, 384, 256, 128):
        aligned = tk == K or tk % LANES == 0 or (not lane_axis and tk % 16 == 0)
        if tk <= K and K % tk == 0 and aligned and tk * row_bytes <= MATMUL_OPERAND_TILE_BYTES:
            return tk
    return K


def _matmul(a, b, mode, name, out_dtype=F32):
    if mode == "nn":
        (M, K), (_, N) = a.shape, b.shape
    elif mode == "nt":
        (M, K), (N, _) = a.shape, b.shape
    else:
        (K, M), (_, N) = a.shape, b.shape
    tm = _pick(M, (768, 512, 1408, 384, 256, 128, 64, 32, 16, 8))
    tn = _pick(N, (512, 1408, 256, 128))
    tk = _k_tile(K, tm * a.dtype.itemsize + tn * b.dtype.itemsize, mode != "tn")
    nk = K // tk
    a_spec = {"nn": pl.BlockSpec((tm, tk), lambda i, j, k: (i, k)),
              "nt": pl.BlockSpec((tm, tk), lambda i, j, k: (i, k)),
              "tn": pl.BlockSpec((tk, tm), lambda i, j, k: (k, i))}[mode]
    b_spec = {"nn": pl.BlockSpec((tk, tn), lambda i, j, k: (k, j)),
              "nt": pl.BlockSpec((tn, tk), lambda i, j, k: (j, k)),
              "tn": pl.BlockSpec((tk, tn), lambda i, j, k: (k, j))}[mode]
    dn = _DN[mode]

    def body(a_ref, b_ref, o_ref, acc_ref):
        k = pl.program_id(2)
        part = lax.dot_general(a_ref[...].astype(BF16), b_ref[...].astype(BF16), dn,
                               preferred_element_type=F32)

        if nk == 1:
            o_ref[...] = part.astype(o_ref.dtype)
        else:
            @pl.when(k == 0)
            def _():
                acc_ref[...] = part

            @pl.when((k > 0) & (k < nk - 1))
            def _():
                acc_ref[...] += part

            @pl.when(k == nk - 1)
            def _():
                o_ref[...] = (acc_ref[...] + part).astype(o_ref.dtype)

    return _pcall(
        body, name=name, grid=(M // tm, N // tn, nk),
        in_specs=[a_spec, b_spec], out_specs=pl.BlockSpec((tm, tn), lambda i, j, k: (i, j)),
        out_shape=jax.ShapeDtypeStruct((M, N), out_dtype),
        scratch_shapes=[pltpu.VMEM((tm, tn) if nk > 1 else (8, LANES), F32)],
        compiler_params=pltpu.CompilerParams(
            dimension_semantics=("parallel", "parallel", "arbitrary"),
            vmem_limit_bytes=VMEM_LIMIT_BYTES),
    )(a, b)


@functools.partial(jax.custom_vjp, nondiff_argnums=(3,))
def dense(a, wb, w, out_dtype):
    return _matmul(a, wb, "nn", "dense_fwd", out_dtype)


def _dense_fwd(a, wb, w, out_dtype):
    return _matmul(a, wb, "nn", "dense_fwd", out_dtype), (a, wb)


def _dense_bwd(out_dtype, res, g):
    a, wb = res
    return (_matmul(g, wb, "nt", "dense_dgrad", a.dtype), jnp.zeros_like(wb),
            _matmul(a, g, "tn", "dense_wgrad", F32))


dense.defvjp(_dense_fwd, _dense_bwd)


def dense_nd(a, wb, w, out_dtype=F32):
    K, N = wb.shape
    pad = (-N) % (4 * LANES if N > 4 * LANES else LANES)
    if pad:
        wb = jnp.pad(wb, ((0, 0), (0, pad)))
        w = jnp.pad(w, ((0, 0), (0, pad)))
    out = dense(a.reshape(-1, K), wb, w, out_dtype)
    if pad:
        out = out[:, :N]
    return out.reshape(a.shape[:-1] + (N,))


_NT = (((1,), (1,)), ((), ()))


def _att_blk(T):
    return _pick(T, (384, 256, 128))


def _attn_fwd_call(q, k, v, cum_col, cum_row, scale):
    BH, T, dk = q.shape
    dv = v.shape[2]
    blk = _att_blk(T)
    n = T // blk
    has_cum = cum_col is not None

    def body(*refs):
        if has_cum:
            q_ref, k_ref, v_ref, cq_ref, ck_ref, o_ref, lse_ref, m_s, l_s, acc_s = refs
        else:
            q_ref, k_ref, v_ref, o_ref, lse_ref, m_s, l_s, acc_s = refs
        i = pl.program_id(1)
        j = pl.program_id(2)

        @pl.when(j == 0)
        def _():
            m_s[...] = jnp.full_like(m_s, NEG_INF)
            l_s[...] = jnp.zeros_like(l_s)
            acc_s[...] = jnp.zeros_like(acc_s)

        def step(masked):
            s = lax.dot_general(q_ref[0].astype(BF16), k_ref[0].astype(BF16), _NT,
                                preferred_element_type=F32) * scale
            if has_cum:
                s = s + cq_ref[0] - ck_ref[0]
            if masked:
                qpos = i * blk + lax.broadcasted_iota(jnp.int32, (blk, blk), 0)
                kpos = j * blk + lax.broadcasted_iota(jnp.int32, (blk, blk), 1)
                mask = (kpos <= qpos) & (kpos >= PAD_LEN)
                s = jnp.where(mask, s, NEG_INF)
            m_prev = m_s[...]
            m_new = jnp.maximum(m_prev, jnp.max(s, axis=1, keepdims=True))
            p = jnp.exp(s - m_new)
            alpha = jnp.exp(m_prev - m_new)
            l_s[...] = alpha * l_s[...] + jnp.sum(p, axis=1, keepdims=True)
            acc_s[...] = alpha * acc_s[...] + jnp.dot(p.astype(BF16), v_ref[0].astype(BF16),
                                                      preferred_element_type=F32)
            m_s[...] = m_new

        @pl.when((j <= i) & ((j == i) | (j == 0)))
        def _():
            step(True)

        @pl.when((j < i) & (j > 0))
        def _():
            step(False)

        @pl.when(j == n - 1)
        def _():
            o_ref[0] = acc_s[...] / l_s[...]
            lse_ref[0] = m_s[...] + jnp.log(l_s[...])

    kv_idx = lambda b, i, j: (b, jnp.minimum(i, j), 0)
    in_specs = [pl.BlockSpec((1, blk, dk), lambda b, i, j: (b, i, 0)),
                pl.BlockSpec((1, blk, dk), kv_idx),
                pl.BlockSpec((1, blk, dv), kv_idx)]
    args = [q, k, v]
    if has_cum:
        in_specs += [pl.BlockSpec((1, blk, 1), lambda b, i, j: (b, i, 0)),
                     pl.BlockSpec((1, 1, blk), lambda b, i, j: (b, 0, jnp.minimum(i, j)))]
        args += [cum_col, cum_row]
    return _pcall(
        body, name="attn_fwd_cum" if has_cum else "attn_fwd", grid=(BH, n, n),
        in_specs=in_specs,
        out_specs=[pl.BlockSpec((1, blk, dv), lambda b, i, j: (b, i, 0)),
                   pl.BlockSpec((1, blk, 1), lambda b, i, j: (b, i, 0))],
        out_shape=[jax.ShapeDtypeStruct((BH, T, dv), F32), jax.ShapeDtypeStruct((BH, T, 1), F32)],
        scratch_shapes=[pltpu.VMEM((blk, 1), F32), pltpu.VMEM((blk, 1), F32), pltpu.VMEM((blk, dv), F32)],
        compiler_params=pltpu.CompilerParams(
            dimension_semantics=("parallel", "parallel", "arbitrary"),
            vmem_limit_bytes=VMEM_LIMIT_BYTES),
    )(*args)


def _attn_dkv_call(q, k, v, do, lse_row, delta_row, cum_row, cum_col, scale):
    BH, T, dk = q.shape
    dv = v.shape[2]
    blk = _att_blk(T)
    n = T // blk
    has_cum = cum_row is not None

    def body(*refs):
        if has_cum:
            (k_ref, v_ref, q_ref, do_ref, lse_ref, dl_ref, cq_ref, ck_ref,
             dk_ref, dv_ref, dc_ref, dk_s, dv_s, dc_s) = refs
        else:
            k_ref, v_ref, q_ref, do_ref, lse_ref, dl_ref, dk_ref, dv_ref, dk_s, dv_s = refs
        j = pl.program_id(1)
        i = pl.program_id(2)

        @pl.when(i == 0)
        def _():
            dk_s[...] = jnp.zeros_like(dk_s)
            dv_s[...] = jnp.zeros_like(dv_s)
            if has_cum:
                dc_s[...] = jnp.zeros_like(dc_s)

        def step(masked):
            kb = k_ref[0].astype(BF16)
            qb = q_ref[0].astype(BF16)
            dob = do_ref[0].astype(BF16)
            st = lax.dot_general(kb, qb, _NT, preferred_element_type=F32) * scale
            if has_cum:
                st = st + cq_ref[0] - ck_ref[0]
            pt = jnp.exp(jnp.minimum(st - lse_ref[0], 0.0))
            if masked:
                kpos = j * blk + lax.broadcasted_iota(jnp.int32, (blk, blk), 0)
                qpos = i * blk + lax.broadcasted_iota(jnp.int32, (blk, blk), 1)
                pt = jnp.where((kpos <= qpos) & (kpos >= PAD_LEN), pt, 0.0)
            dv_s[...] += jnp.dot(pt.astype(BF16), dob, preferred_element_type=F32)
            dpt = lax.dot_general(v_ref[0].astype(BF16), dob, _NT, preferred_element_type=F32)
            dst = pt * (dpt - dl_ref[0])
            dk_s[...] += jnp.dot(dst.astype(BF16), qb, preferred_element_type=F32) * scale
            if has_cum:
                dc_s[...] -= jnp.sum(dst, axis=1, keepdims=True)

        @pl.when((i >= j) & ((i == j) | (j == 0)))
        def _():
            step(True)

        @pl.when((i > j) & (j > 0))
        def _():
            step(False)

        @pl.when(i == n - 1)
        def _():
            dk_ref[0] = dk_s[...]
            dv_ref[0] = dv_s[...]
            if has_cum:
                dc_ref[0] = dc_s[...]

    q_idx = lambda b, j, i: (b, jnp.maximum(i, j), 0)
    row_idx = lambda b, j, i: (b, 0, jnp.maximum(i, j))
    in_specs = [pl.BlockSpec((1, blk, dk), lambda b, j, i: (b, j, 0)),
                pl.BlockSpec((1, blk, dv), lambda b, j, i: (b, j, 0)),
                pl.BlockSpec((1, blk, dk), q_idx),
                pl.BlockSpec((1, blk, dv), q_idx),
                pl.BlockSpec((1, 1, blk), row_idx),
                pl.BlockSpec((1, 1, blk), row_idx)]
    args = [k, v, q, do, lse_row, delta_row]
    out_specs = [pl.BlockSpec((1, blk, dk), lambda b, j, i: (b, j, 0)),
                 pl.BlockSpec((1, blk, dv), lambda b, j, i: (b, j, 0))]
    out_shape = [jax.ShapeDtypeStruct((BH, T, dk), F32), jax.ShapeDtypeStruct((BH, T, dv), F32)]
    scratch = [pltpu.VMEM((blk, dk), F32), pltpu.VMEM((blk, dv), F32)]
    if has_cum:
        in_specs += [pl.BlockSpec((1, 1, blk), row_idx),
                     pl.BlockSpec((1, blk, 1), lambda b, j, i: (b, j, 0))]
        args += [cum_row, cum_col]
        out_specs.append(pl.BlockSpec((1, blk, 1), lambda b, j, i: (b, j, 0)))
        out_shape.append(jax.ShapeDtypeStruct((BH, T, 1), F32))
        scratch.append(pltpu.VMEM((blk, 1), F32))
    return _pcall(
        body, name="attn_dkv_cum" if has_cum else "attn_dkv", grid=(BH, n, n),
        in_specs=in_specs, out_specs=out_specs, out_shape=out_shape, scratch_shapes=scratch,
        compiler_params=pltpu.CompilerParams(
            dimension_semantics=("parallel", "parallel", "arbitrary"),
            vmem_limit_bytes=VMEM_LIMIT_BYTES),
    )(*args)


def _attn_dq_call(q, k, v, do, lse_col, delta_col, cum_col, cum_row, scale):
    BH, T, dk = q.shape
    dv = v.shape[2]
    blk = _att_blk(T)
    n = T // blk
    has_cum = cum_col is not None

    def body(*refs):
        if has_cum:
            q_ref, k_ref, v_ref, do_ref, lse_ref, dl_ref, cq_ref, ck_ref, dq_ref, dc_ref, dq_s, dc_s = refs
        else:
            q_ref, k_ref, v_ref, do_ref, lse_ref, dl_ref, dq_ref, dq_s = refs
        i = pl.program_id(1)
        j = pl.program_id(2)

        @pl.when(j == 0)
        def _():
            dq_s[...] = jnp.zeros_like(dq_s)
            if has_cum:
                dc_s[...] = jnp.zeros_like(dc_s)

        def step(masked):
            kb = k_ref[0].astype(BF16)
            s = lax.dot_general(q_ref[0].astype(BF16), kb, _NT, preferred_element_type=F32) * scale
            if has_cum:
                s = s + cq_ref[0] - ck_ref[0]
            p = jnp.exp(jnp.minimum(s - lse_ref[0], 0.0))
            if masked:
                qpos = i * blk + lax.broadcasted_iota(jnp.int32, (blk, blk), 0)
                kpos = j * blk + lax.broadcasted_iota(jnp.int32, (blk, blk), 1)
                p = jnp.where((kpos <= qpos) & (kpos >= PAD_LEN), p, 0.0)
            dp = lax.dot_general(do_ref[0].astype(BF16), v_ref[0].astype(BF16), _NT,
                                 preferred_element_type=F32)
            ds = p * (dp - dl_ref[0])
            dq_s[...] += jnp.dot(ds.astype(BF16), kb, preferred_element_type=F32) * scale
            if has_cum:
                dc_s[...] += jnp.sum(ds, axis=1, keepdims=True)

        @pl.when((j <= i) & ((j == i) | (j == 0)))
        def _():
            step(True)

        @pl.when((j < i) & (j > 0))
        def _():
            step(False)

        @pl.when(j == n - 1)
        def _():
            dq_ref[0] = dq_s[...]
            if has_cum:
                dc_ref[0] = dc_s[...]

    kv_idx = lambda b, i, j: (b, jnp.minimum(i, j), 0)
    q_idx = lambda b, i, j: (b, i, 0)
    in_specs = [pl.BlockSpec((1, blk, dk), q_idx), pl.BlockSpec((1, blk, dk), kv_idx),
                pl.BlockSpec((1, blk, dv), kv_idx), pl.BlockSpec((1, blk, dv), q_idx),
                pl.BlockSpec((1, blk, 1), q_idx), pl.BlockSpec((1, blk, 1), q_idx)]
    args = [q, k, v, do, lse_col, delta_col]
    if has_cum:
        in_specs += [pl.BlockSpec((1, blk, 1), q_idx),
                     pl.BlockSpec((1, 1, blk), lambda b, i, j: (b, 0, jnp.minimum(i, j)))]
        args += [cum_col, cum_row]
    out_specs = [pl.BlockSpec((1, blk, dk), q_idx)]
    out_shape = [jax.ShapeDtypeStruct((BH, T, dk), F32)]
    scratch = [pltpu.VMEM((blk, dk), F32)]
    if has_cum:
        out_specs.append(pl.BlockSpec((1, blk, 1), q_idx))
        out_shape.append(jax.ShapeDtypeStruct((BH, T, 1), F32))
        scratch.append(pltpu.VMEM((blk, 1), F32))
    return _pcall(
        body, name="attn_dq_cum" if has_cum else "attn_dq", grid=(BH, n, n),
        in_specs=in_specs, out_specs=out_specs, out_shape=out_shape, scratch_shapes=scratch,
        compiler_params=pltpu.CompilerParams(
            dimension_semantics=("parallel", "parallel", "arbitrary"),
            vmem_limit_bytes=VMEM_LIMIT_BYTES),
    )(*args)


def _make_attention(scale, has_cum):
    def fold(t):
        return t.reshape((-1,) + t.shape[2:])

    def run_fwd(q, k, v, cum):
        B, H, T, _ = q.shape
        col = cum.reshape(B * H, T, 1) if has_cum else None
        row = cum.reshape(B * H, 1, T) if has_cum else None
        o, lse = _attn_fwd_call(fold(q), fold(k), fold(v), col, row, scale)
        return o.reshape(B, H, T, -1), lse

    @jax.custom_vjp
    def attn(q, k, v, cum):
        return run_fwd(q, k, v, cum)[0]

    def attn_fwd(q, k, v, cum):
        o, lse = run_fwd(q, k, v, cum)
        return o, (q, k, v, cum, o, lse)

    def attn_bwd(res, do):
        q, k, v, cum, o, lse = res
        B, H, T, _ = q.shape
        delta = jnp.sum(do * o, axis=-1).reshape(B * H, T, 1)
        col = cum.reshape(B * H, T, 1) if has_cum else None
        row = cum.reshape(B * H, 1, T) if has_cum else None
        qf, kf, vf, dof = fold(q), fold(k), fold(v), fold(do)
        outs = _attn_dkv_call(qf, kf, vf, dof, lse.reshape(B * H, 1, T), delta.reshape(B * H, 1, T),
                              row, col, scale)
        dqs = _attn_dq_call(qf, kf, vf, dof, lse, delta, col, row, scale)
        dcum = (outs[2] + dqs[1]).reshape(B, H, T) if has_cum else jnp.zeros_like(cum)
        return (dqs[0].reshape(q.shape), outs[0].reshape(k.shape), outs[1].reshape(v.shape), dcum)

    attn.defvjp(attn_fwd, attn_bwd)
    return attn


def _rmsnorm(x, g):
    return x * lax.rsqrt(jnp.mean(x * x, axis=-1, keepdims=True) + EPS) * g


def _l2norm(x):
    return x * lax.rsqrt(jnp.sum(x * x, axis=-1, keepdims=True) + EPS)


def _row_tile(rows, row_bytes, budget):
    for t in (2048, 1024, 512, 256, 128, 64, 32, 16, 8):
        if rows % t == 0 and t * row_bytes <= budget:
            return t
    return rows


ROW_BLOCK_BYTES = 2 * 1024 * 1024


def _rms_fwd_call(x, g):
    M, D = x.shape
    tr = _row_tile(M, 4 * D, ROW_BLOCK_BYTES)

    def body(x_ref, g_ref, y_ref, r_ref):
        xv = x_ref[...]
        r = lax.rsqrt(jnp.mean(xv * xv, axis=-1, keepdims=True) + EPS)
        y_ref[...] = xv * r * g_ref[...]
        r_ref[...] = r

    return _pcall(
        body, name="rmsnorm_fwd", grid=(M // tr,),
        in_specs=[pl.BlockSpec((tr, D), lambda i: (i, 0)), pl.BlockSpec((1, D), lambda i: (0, 0))],
        out_specs=[pl.BlockSpec((tr, D), lambda i: (i, 0)), pl.BlockSpec((tr, 1), lambda i: (i, 0))],
        out_shape=[jax.ShapeDtypeStruct((M, D), F32), jax.ShapeDtypeStruct((M, 1), F32)],
        compiler_params=pltpu.CompilerParams(dimension_semantics=("parallel",)),
    )(x, g)


def _rms_bwd_call(x, g, r, dy):
    M, D = x.shape
    tr = _row_tile(M, 4 * D, ROW_BLOCK_BYTES)

    def body(x_ref, g_ref, r_ref, dy_ref, dx_ref, dg_ref):
        i = pl.program_id(0)
        rv = r_ref[...]
        xh = x_ref[...] * rv
        dyv = dy_ref[...]
        dyg = dyv * g_ref[...]
        dx_ref[...] = rv * (dyg - xh * jnp.mean(dyg * xh, axis=-1, keepdims=True))
        part = jnp.sum(dyv * xh, axis=0, keepdims=True)

        @pl.when(i == 0)
        def _():
            dg_ref[...] = part

        @pl.when(i > 0)
        def _():
            dg_ref[...] += part

    row = pl.BlockSpec((tr, D), lambda i: (i, 0))
    return _pcall(
        body, name="rmsnorm_bwd", grid=(M // tr,),
        in_specs=[row, pl.BlockSpec((1, D), lambda i: (0, 0)), pl.BlockSpec((tr, 1), lambda i: (i, 0)), row],
        out_specs=[row, pl.BlockSpec((1, D), lambda i: (0, 0))],
        out_shape=[jax.ShapeDtypeStruct((M, D), F32), jax.ShapeDtypeStruct((1, D), F32)],
        compiler_params=pltpu.CompilerParams(dimension_semantics=("arbitrary",)),
    )(x, g, r, dy)


@jax.custom_vjp
def _rmsnorm2d(x, g):
    return _rms_fwd_call(x, g)[0]


def _rmsnorm2d_fwd(x, g):
    y, r = _rms_fwd_call(x, g)
    return y, (x, g, r)


def _rmsnorm2d_bwd(res, dy):
    x, g, r = res
    return _rms_bwd_call(x, g, r, dy)


_rmsnorm2d.defvjp(_rmsnorm2d_fwd, _rmsnorm2d_bwd)


def _rmsnorm_rows(x, g):
    D = x.shape[-1]
    return _rmsnorm2d(x.reshape(-1, D), g.reshape(1, D)).reshape(x.shape)


def _glu_fwd_call(gu):
    M, F2 = gu.shape
    F = F2 // 2
    tr = _row_tile(M, 4 * F2, 2 * ROW_BLOCK_BYTES)

    def body(g_ref, u_ref, o_ref):
        gv = g_ref[...].astype(F32)
        o_ref[...] = (gv * jax.nn.sigmoid(gv) * u_ref[...].astype(F32)).astype(o_ref.dtype)

    return _pcall(
        body, name="swiglu_fwd", grid=(M // tr,),
        in_specs=[pl.BlockSpec((tr, F), lambda i: (i, 0)), pl.BlockSpec((tr, F), lambda i: (i, 1))],
        out_specs=pl.BlockSpec((tr, F), lambda i: (i, 0)),
        out_shape=jax.ShapeDtypeStruct((M, F), gu.dtype),
        compiler_params=pltpu.CompilerParams(dimension_semantics=("parallel",),
                                             vmem_limit_bytes=VMEM_LIMIT_BYTES),
    )(gu, gu)


def _glu_bwd_call(gu, da):
    M, F2 = gu.shape
    F = F2 // 2
    tr = _row_tile(M, 4 * F2, 2 * ROW_BLOCK_BYTES)

    def body(g_ref, u_ref, da_ref, o_ref):
        gv = g_ref[...].astype(F32)
        s = jax.nn.sigmoid(gv)
        dav = da_ref[...].astype(F32)
        o_ref[:, :F] = (dav * u_ref[...].astype(F32) * (s * (1.0 + gv * (1.0 - s)))).astype(o_ref.dtype)
        o_ref[:, F:] = (dav * (gv * s)).astype(o_ref.dtype)

    return _pcall(
        body, name="swiglu_bwd", grid=(M // tr,),
        in_specs=[pl.BlockSpec((tr, F), lambda i: (i, 0)), pl.BlockSpec((tr, F), lambda i: (i, 1)),
                  pl.BlockSpec((tr, F), lambda i: (i, 0))],
        out_specs=pl.BlockSpec((tr, F2), lambda i: (i, 0)),
        out_shape=jax.ShapeDtypeStruct((M, F2), gu.dtype),
        compiler_params=pltpu.CompilerParams(dimension_semantics=("parallel",),
                                             vmem_limit_bytes=VMEM_LIMIT_BYTES),
    )(gu, gu, da)


@jax.custom_vjp
def _glu(gu):
    return _glu_fwd_call(gu)


def _glu_fwd(gu):
    return _glu_fwd_call(gu), gu


def _glu_bwd(gu, da):
    return (_glu_bwd_call(gu, da),)


_glu.defvjp(_glu_fwd, _glu_bwd)


def _swiglu(h, wi, wo):
    lead, D = h.shape[:-1], h.shape[-1]
    gu = dense(h.reshape(-1, D), wi[0], wi[1], BF16)
    return dense(_glu(gu), wo[0], wo[1], F32).reshape(lead + (wo[0].shape[1],))


def _causal_dwconv(x, w):
    K, C = w.shape
    return lax.conv_general_dilated(
        x, w[:, None, :], window_strides=(1,), padding=[(K - 1, 0)],
        dimension_numbers=("NWC", "WIO", "NWC"), feature_group_count=C)


def _rope(x, cos, sin):
    half = x.shape[-1] // 2
    x1, x2 = x[..., :half], x[..., half:]
    return jnp.concatenate([x1 * cos - x2 * sin, x2 * cos + x1 * sin], axis=-1)


def _fox_branch(p, b_f):
    B, T, _ = p.shape
    qkv = p[..., OFF_FOX_QKV:OFF_FOX_F].reshape(B, T, 3, FOX_HEADS, FOX_DH)
    q = qkv[:, :, 0].transpose(0, 2, 1, 3)
    k = qkv[:, :, 1].transpose(0, 2, 1, 3)
    v = qkv[:, :, 2].transpose(0, 2, 1, 3)
    log_f = jax.nn.log_sigmoid(p[..., OFF_FOX_F:OFF_MLA_CQ] + b_f)
    cum = jnp.cumsum(log_f, axis=1).transpose(0, 2, 1)
    o = _make_attention(FOX_DH ** -0.5, True)(q, k, v, cum)
    return o.transpose(0, 2, 1, 3).reshape(B, T, FOX_HEADS * FOX_DH)


def _mla_branch(p, g_qn, w_q_up, g_kvn, w_kv_up, cos, sin):
    B, T, _ = p.shape
    cq = _rmsnorm(p[..., OFF_MLA_CQ:OFF_MLA_CKV], g_qn)
    q = (cq @ w_q_up).reshape(B, T, MLA_HEADS, MLA_NOPE + MLA_ROPE)
    ckv = _rmsnorm(p[..., OFF_MLA_CKV:OFF_MLA_KR], g_kvn)
    kv = (ckv @ w_kv_up).reshape(B, T, MLA_HEADS, MLA_NOPE + MLA_DV)
    k_rope = _rope(p[..., OFF_MLA_KR:OFF_GDN_QKV], cos, sin)
    q_rope = _rope(q[..., MLA_NOPE:], cos[:, None], sin[:, None])
    q = jnp.concatenate([q[..., :MLA_NOPE], q_rope], axis=-1)
    k = jnp.concatenate([kv[..., :MLA_NOPE],
                         jnp.broadcast_to(k_rope[:, :, None], (B, T, MLA_HEADS, MLA_ROPE))], axis=-1)
    v = kv[..., MLA_NOPE:]
    o = _make_attention((MLA_NOPE + MLA_ROPE) ** -0.5, False)(
        q.transpose(0, 2, 1, 3), k.transpose(0, 2, 1, 3), v.transpose(0, 2, 1, 3),
        jnp.zeros((B, MLA_HEADS, T), F32))
    return o.transpose(0, 2, 1, 3).reshape(B, T, MLA_HEADS * MLA_DV)


def _bmm(a, b):
    return lax.dot_general(a.astype(BF16), b.astype(BF16), (((2,), (1,)), ((0,), (0,))),
                           preferred_element_type=F32)


def _bmm_nt(a, b):
    return lax.dot_general(a.astype(BF16), b.astype(BF16), (((2,), (2,)), ((0,), (0,))),
                           preferred_element_type=F32)


def _chunk_spec(shape, index):
    return pl.BlockSpec((1,) + tuple(shape[1:]), lambda n: (index(n), 0, 0, 0))


def _gdn_scan_fwd_call(qd, kdt, w, u, qk, gl):
    nc, BH, C, DK = qd.shape
    DV = u.shape[3]

    def body(qd_ref, kdt_ref, w_ref, u_ref, qk_ref, gl_ref, o_ref, s_ref, vn_ref, state):
        @pl.when(pl.program_id(0) == 0)
        def _():
            state[...] = jnp.zeros_like(state)

        s = state[...]
        s_ref[0] = s
        vn = u_ref[0] - _bmm(w_ref[0], s)
        vn_ref[0] = vn
        o_ref[0] = _bmm(qd_ref[0], s) + _bmm(qk_ref[0], vn)
        state[...] = s * gl_ref[0] + _bmm(kdt_ref[0], vn)

    fwd = lambda n: n
    outs = [(nc, BH, C, DV), (nc, BH, DK, DV), (nc, BH, C, DV)]
    return _pcall(
        body, name="gdn_scan_fwd", grid=(nc,),
        in_specs=[_chunk_spec(t.shape, fwd) for t in (qd, kdt, w, u, qk, gl)],
        out_specs=[_chunk_spec(s, fwd) for s in outs],
        out_shape=[jax.ShapeDtypeStruct(s, F32) for s in outs],
        scratch_shapes=[pltpu.VMEM((BH, DK, DV), F32)],
        compiler_params=pltpu.CompilerParams(dimension_semantics=("arbitrary",)),
    )(qd, kdt, w, u, qk, gl)


def _gdn_scan_bwd_call(do, qdt, kd, wt, qkt, gl, s_all, vn):
    nc, BH, C, DV = do.shape
    DK = kd.shape[3]

    def body(do_ref, qdt_ref, kd_ref, wt_ref, qkt_ref, gl_ref, s_ref, vn_ref,
             dqd_ref, dkd_ref, dw_ref, du_ref, dqk_ref, dgl_ref, dstate):
        @pl.when(pl.program_id(0) == 0)
        def _():
            dstate[...] = jnp.zeros_like(dstate)

        ds = dstate[...]
        s, v, dov = s_ref[0], vn_ref[0], do_ref[0]
        dkd_ref[0] = _bmm_nt(v, ds)
        dgl_ref[0] = s * ds
        dqd_ref[0] = _bmm_nt(dov, s)
        dqk_ref[0] = _bmm_nt(dov, v)
        dv = _bmm(kd_ref[0], ds) + _bmm(qkt_ref[0], dov)
        du_ref[0] = dv
        dw_ref[0] = -_bmm_nt(dv, s)
        dstate[...] = ds * gl_ref[0] + _bmm(qdt_ref[0], dov) - _bmm(wt_ref[0], dv)

    rev = lambda n: nc - 1 - n
    outs = [(nc, BH, C, DK)] * 3 + [(nc, BH, C, DV), (nc, BH, C, C), (nc, BH, DK, DV)]
    return _pcall(
        body, name="gdn_scan_bwd", grid=(nc,),
        in_specs=[_chunk_spec(t.shape, rev) for t in (do, qdt, kd, wt, qkt, gl, s_all, vn)],
        out_specs=[_chunk_spec(s, rev) for s in outs],
        out_shape=[jax.ShapeDtypeStruct(s, F32) for s in outs],
        scratch_shapes=[pltpu.VMEM((BH, DK, DV), F32)],
        compiler_params=pltpu.CompilerParams(dimension_semantics=("arbitrary",)),
    )(do, qdt, kd, wt, qkt, gl, s_all, vn)


def _gl_rows(gl, dv):
    return jnp.broadcast_to(gl[:, :, None, None], gl.shape + (1, dv))


@jax.custom_vjp
def _gdn_scan(qd, kd, w, u, qk, gl):
    return _gdn_scan_fwd_call(qd, jnp.swapaxes(kd, 2, 3), w, u, qk, _gl_rows(gl, u.shape[3]))[0]


def _gdn_scan_fwd(qd, kd, w, u, qk, gl):
    o, s_all, vn = _gdn_scan_fwd_call(qd, jnp.swapaxes(kd, 2, 3), w, u, qk, _gl_rows(gl, u.shape[3]))
    return o, (qd, kd, w, qk, gl, s_all, vn)


def _gdn_scan_bwd(res, do):
    qd, kd, w, qk, gl, s_all, vn = res
    dqd, dkd, dw, du, dqk, dgl = _gdn_scan_bwd_call(
        do, jnp.swapaxes(qd, 2, 3), kd, jnp.swapaxes(w, 2, 3), jnp.swapaxes(qk, 2, 3),
        _gl_rows(gl, do.shape[3]), s_all, vn)
    return dqd, dkd, dw, du, dqk, jnp.sum(dgl, axis=(2, 3))


_gdn_scan.defvjp(_gdn_scan_fwd, _gdn_scan_bwd)


def _gdn_branch(p, conv_w, a_log, dt_bias, g_on):
    B, T, _ = p.shape
    H, DK, DV, C = GDN_HEADS, GDN_DK, GDN_DV, GDN_CHUNK
    qkv = jax.nn.silu(_causal_dwconv(p[..., OFF_GDN_QKV:OFF_GDN_A], conv_w))
    q = _l2norm(qkv[..., :H * DK].reshape(B, T, H, DK)) * DK ** -0.5
    k = _l2norm(qkv[..., H * DK:2 * H * DK].reshape(B, T, H, DK))
    v = qkv[..., 2 * H * DK:].reshape(B, T, H, DV)
    beta = jax.nn.sigmoid(p[..., OFF_GDN_B:OFF_GDN_G])
    g = -jnp.exp(a_log) * jax.nn.softplus(p[..., OFF_GDN_A:OFF_GDN_B] + dt_bias)
    nc = T // C

    def chunks(t):
        return jnp.moveaxis(t, 2, 1).reshape((B, H, nc, C) + t.shape[3:])

    q, k, v, beta, g = chunks(q), chunks(k), chunks(v), chunks(beta), chunks(g)
    G = jnp.cumsum(g, axis=-1)
    idx = jnp.arange(C)
    strict = idx[:, None] > idx[None, :]
    incl = idx[:, None] >= idx[None, :]
    decay = jnp.exp(jnp.where(incl, G[..., :, None] - G[..., None, :], NEG_INF))
    kb = k * beta[..., None]
    vb = v * beta[..., None]
    m = jnp.eye(C, dtype=F32) + jnp.where(
        strict, jnp.einsum("bhnik,bhnjk->bhnij", kb, k) * decay, 0.0)
    rhs = jnp.concatenate([kb * jnp.exp(G)[..., None], vb], axis=-1)
    sol = lax.linalg.triangular_solve(m, rhs, left_side=True, lower=True, unit_diagonal=True)
    w, u = sol[..., :DK], sol[..., DK:]
    qk = jnp.where(incl, jnp.einsum("bhnik,bhnjk->bhnij", q, k) * decay, 0.0)
    q_dec = q * jnp.exp(G)[..., None]
    k_dec = k * jnp.exp(G[..., -1:] - G)[..., None]
    g_last = jnp.exp(G[..., -1])
    def chunk_major(t):
        return jnp.moveaxis(t, 2, 0).reshape((nc, B * H) + t.shape[3:])

    o = _gdn_scan(chunk_major(q_dec), chunk_major(k_dec), chunk_major(w), chunk_major(u),
                  chunk_major(qk), chunk_major(g_last))
    o = jnp.moveaxis(o.reshape(nc, B, H, C, DV), 0, 2).reshape(B, H, T, DV).transpose(0, 2, 1, 3)
    gate = jax.nn.silu(p[..., OFF_GDN_G:OFF_LRU]).reshape(B, T, H, DV)
    o = _rmsnorm(o, g_on) * gate
    return o.reshape(B, T, H * DV)


def _rglru_branch(p, valid, conv_w, conv_b, w_a, b_a, w_x, b_x, lam):
    B, T, _ = p.shape
    xr = _causal_dwconv(p[..., OFF_LRU:N_IN], conv_w) + conv_b
    xr = jnp.where(valid[None, :, None], xr, 0)
    xb = xr.reshape(B, T, LRU_BLOCKS, LRU_WIDTH // LRU_BLOCKS)
    r = jax.nn.sigmoid(jnp.einsum("btni,nij->btnj", xb, w_a).reshape(B, T, LRU_WIDTH) + b_a)
    ig = jax.nn.sigmoid(jnp.einsum("btni,nij->btnj", xb, w_x).reshape(B, T, LRU_WIDTH) + b_x)
    log_a = -LRU_C * r * jax.nn.softplus(-lam)
    a = jnp.exp(log_a)
    b = jnp.sqrt(-jnp.expm1(2.0 * log_a)) * ig * xr

    def combine(e1, e2):
        return (e1[0] * e2[0], e2[0] * e1[1] + e2[1])

    _, h = lax.associative_scan(combine, (a, b), axis=1)
    return h


def _mixer(u, valid, cos, sin, w, wb, l):
    def pair(name, *idx):
        return wb[name][(l,) + idx], w[name][(l,) + idx]

    p = dense_nd(u, *pair("w_in"))
    ys = (_fox_branch(p, w["fox_bf"][l]),
          _mla_branch(p, w["mla_gq"][l], w["mla_wq"][l], w["mla_gkv"][l], w["mla_wkv"][l], cos, sin),
          _gdn_branch(p, w["gdn_conv"][l], w["gdn_alog"][l], w["gdn_dtb"][l], w["gdn_gon"][l]),
          _rglru_branch(p, valid, w["lru_conv"][l], w["lru_conv_b"][l], w["lru_wa"][l], w["lru_ba"][l],
                        w["lru_wx"][l], w["lru_bx"][l], w["lru_lam"][l]))
    D = u.shape[-1]
    w_gate = [jnp.moveaxis(t, 0, 1).reshape(D, N_BRANCH * D) for t in pair("w_gate")]
    gates = jax.nn.sigmoid(dense_nd(u, *w_gate) + w["b_gate"][l].reshape(-1))
    merged = gates[..., :D] * dense_nd(ys[0], *pair("w_branch", 0))
    for n in range(1, N_BRANCH):
        merged = merged + gates[..., n * D:(n + 1) * D] * dense_nd(ys[n], *pair("w_branch", n))
    return dense_nd(merged, *pair("w_out"))


def _local_loss(w, x, loss_target, wb):
    B, S, D = x.shape
    T = BLOCK + S
    h = jnp.concatenate([jnp.zeros((B, PAD_LEN, D), F32),
                         jnp.broadcast_to(w["meta"][None], (B, N_META, D)), x], axis=1)
    pos = jnp.arange(T)
    valid = pos >= PAD_LEN
    rel = (pos - PAD_LEN).astype(F32)
    inv_freq = ROPE_BASE ** (-(jnp.arange(0, MLA_ROPE, 2, dtype=F32) / MLA_ROPE))
    ang = rel[:, None] * inv_freq[None, :]
    cos, sin = jnp.cos(ang), jnp.sin(ang)
    def pair(name, l):
        return wb[name][l], w[name][l]

    for l in range(DEPTH):
        h = h + 0.5 * _swiglu(_rmsnorm_rows(h, w["ln_ffn1"][l]), pair("ffn1_wi", l), pair("ffn1_wo", l))
        u = jnp.where(valid[None, :, None], _rmsnorm_rows(h, w["ln_mix"][l]), 0)
        h = h + _mixer(u, valid, cos, sin, w, wb, l)
        h = h + 0.5 * _swiglu(_rmsnorm_rows(h, w["ln_ffn2"][l]), pair("ffn2_wi", l), pair("ffn2_wo", l))
    y = _rmsnorm_rows(h, w["ln_final"])[:, BLOCK:]
    err = jnp.square(y - loss_target)
    return 0.5 * jnp.sum(jnp.mean(err, axis=-1))


MESH = pl.DeviceIdType.MESH
HBM = pl.BlockSpec(memory_space=pl.ANY)


def _place():
    x, y, c = lax.axis_index("x"), lax.axis_index("y"), lax.axis_index("c")
    return x, y, c


def _other_chip(x, y, r):
    return (1 - x if r & 2 else x), (1 - y if r & 1 else y)


def _gather_chips(shard):
    R, C = shard.shape
    H = R // 2

    def body(x_ref, out_ref, send_sems, recv_sems, local_sem):
        x, y, c = _place()
        me = 2 * x + y

        def rows(chip, half):
            return out_ref.at[chip, pl.ds(half * H, H), :]

        def copy(sem, src, dst, to):
            return pltpu.make_async_remote_copy(src_ref=src, dst_ref=dst, send_sem=send_sems.at[sem],
                                                recv_sem=recv_sems.at[sem], device_id=to, device_id_type=MESH)

        mine = pltpu.make_async_copy(x_ref, out_ref.at[me], local_sem)
        mine.start()
        started = []
        for r in (1, 2, 3):
            ox, oy = _other_chip(x, y, r)
            cp = copy(r - 1, x_ref.at[pl.ds(c * H, H), :], rows(me, c), (ox, oy, c))
            cp.start()
            started.append(cp)
        for r in (1, 2, 3):
            ox, oy = _other_chip(x, y, r)
            src = 2 * ox + oy
            copy(r - 1, rows(src, c), rows(src, c), (x, y, c)).wait_recv()
            fw = copy(2 + r, rows(src, c), rows(src, c), (x, y, 1 - c))
            fw.start()
            started.append(fw)
        for r in (1, 2, 3):
            ox, oy = _other_chip(x, y, r)
            src = 2 * ox + oy
            copy(2 + r, rows(src, 1 - c), rows(src, 1 - c), (x, y, c)).wait_recv()
        for cp in started:
            cp.wait_send()
        mine.wait()

    return _pcall(
        body, name="gather_chips", in_specs=[HBM], out_specs=HBM,
        out_shape=jax.ShapeDtypeStruct((N_CHIPS, R, C), shard.dtype),
        scratch_shapes=[pltpu.SemaphoreType.DMA((6,)), pltpu.SemaphoreType.DMA((6,)),
                        pltpu.SemaphoreType.DMA],
    )(shard)


def _window(ref, lead, axis, chip, width):
    idx = [slice(None)] * len(ref.shape)
    if lead is not None:
        idx[0] = lead
    idx[axis] = pl.ds(chip * width, width)
    return ref.at[tuple(idx)]


def _gather_layers(shards, axes):
    n_t = len(shards)
    widths = [s.shape[a] for s, a in zip(shards, axes)]
    fulls = [s.shape[:a] + (N_CHIPS * s.shape[a],) + s.shape[a + 1:] for s, a in zip(shards, axes)]

    def body(*refs):
        x_refs, out_refs = refs[:n_t], refs[n_t:2 * n_t]
        send_sems, recv_sems, local_sems = refs[2 * n_t:]
        x, y, c = _place()
        me = 2 * x + y

        def win(t, lead, chip):
            return _window(out_refs[t], lead, axes[t], chip, widths[t])

        def copy(sem, src, dst, to):
            return pltpu.make_async_remote_copy(src_ref=src, dst_ref=dst, send_sem=send_sems.at[sem],
                                                recv_sem=recv_sems.at[sem], device_id=to, device_id_type=MESH)

        local, started = [], []
        for t in range(n_t):
            cp = pltpu.make_async_copy(x_refs[t], win(t, None, me), local_sems.at[t])
            cp.start()
            local.append(cp)
        for r in (1, 2, 3):
            ox, oy = _other_chip(x, y, r)
            for t in range(n_t):
                cp = copy(6 * t + r - 1, x_refs[t].at[c], win(t, c, me), (ox, oy, c))
                cp.start()
                started.append(cp)
        for r in (1, 2, 3):
            ox, oy = _other_chip(x, y, r)
            src = 2 * ox + oy
            for t in range(n_t):
                copy(6 * t + r - 1, win(t, c, src), win(t, c, src), (x, y, c)).wait_recv()
                fw = copy(6 * t + 2 + r, win(t, c, src), win(t, c, src), (x, y, 1 - c))
                fw.start()
                started.append(fw)
        for r in (1, 2, 3):
            ox, oy = _other_chip(x, y, r)
            src = 2 * ox + oy
            for t in range(n_t):
                copy(6 * t + 2 + r, win(t, 1 - c, src), win(t, 1 - c, src), (x, y, c)).wait_recv()
        for cp in started:
            cp.wait_send()
        for cp in local:
            cp.wait()

    return _pcall(
        body, name="gather_layers", in_specs=[HBM] * n_t, out_specs=[HBM] * n_t,
        out_shape=[jax.ShapeDtypeStruct(f, s.dtype) for f, s in zip(fulls, shards)],
        scratch_shapes=[pltpu.SemaphoreType.DMA((6 * n_t,)), pltpu.SemaphoreType.DMA((6 * n_t,)),
                        pltpu.SemaphoreType.DMA((n_t,))],
    )(*shards)


def _swap_layers(gs):
    n_t = len(gs)

    def body(*refs):
        g_refs, out_refs = refs[:n_t], refs[n_t:2 * n_t]
        send_sems, recv_sems = refs[2 * n_t:]
        x, y, c = _place()
        cps = []
        for t in range(n_t):
            cp = pltpu.make_async_remote_copy(
                src_ref=g_refs[t].at[1 - c], dst_ref=out_refs[t], send_sem=send_sems.at[t],
                recv_sem=recv_sems.at[t], device_id=(x, y, 1 - c), device_id_type=MESH)
            cp.start()
            cps.append(cp)
        for cp in cps:
            cp.wait()

    return _pcall(
        body, name="swap_layers", in_specs=[HBM] * n_t, out_specs=[HBM] * n_t,
        out_shape=[jax.ShapeDtypeStruct(g.shape[1:], g.dtype) for g in gs],
        scratch_shapes=[pltpu.SemaphoreType.DMA((n_t,)), pltpu.SemaphoreType.DMA((n_t,))],
    )(*gs)


def _add_layer(g, other, c):
    shape = other.shape
    last = shape[-1]
    rows = math.prod(shape[:-1])
    tr = _row_tile(rows, 4 * last, ROW_BLOCK_BYTES)

    def body(c_ref, a_ref, b_ref, o_ref):
        o_ref[...] = a_ref[0] + b_ref[...]

    out = _pcall(
        body, name="add_layer",
        grid_spec=pltpu.PrefetchScalarGridSpec(
            num_scalar_prefetch=1, grid=(rows // tr,),
            in_specs=[pl.BlockSpec((1, tr, last), lambda i, c_ref: (c_ref[0], i, 0)),
                      pl.BlockSpec((tr, last), lambda i, c_ref: (i, 0))],
            out_specs=pl.BlockSpec((tr, last), lambda i, c_ref: (i, 0))),
        out_shape=jax.ShapeDtypeStruct((rows, last), F32),
        compiler_params=pltpu.CompilerParams(dimension_semantics=("parallel",)),
    )(c.reshape(1).astype(jnp.int32), g.reshape(2, rows, last), other.reshape(rows, last))
    return out.reshape(shape)


def _scatter_layers(ps, axes):
    n_t = len(ps)
    widths = [p.shape[a] // N_CHIPS for p, a in zip(ps, axes)]
    wins = [p.shape[:a] + (w,) + p.shape[a + 1:] for p, a, w in zip(ps, axes, widths)]

    def body(*refs):
        p_refs, out_refs = refs[:n_t], refs[n_t:2 * n_t]
        send_sems, recv_sems, local_sems = refs[2 * n_t:]
        x, y, c = _place()
        me = 2 * x + y
        cps = []
        for t in range(n_t):
            cp = pltpu.make_async_copy(_window(p_refs[t], None, axes[t], me, widths[t]), out_refs[t].at[0],
                                       local_sems.at[t])
            cp.start()
            cps.append(cp)
        for r in (1, 2, 3):
            ox, oy = _other_chip(x, y, r)
            for t in range(n_t):
                cp = pltpu.make_async_remote_copy(
                    src_ref=_window(p_refs[t], None, axes[t], 2 * ox + oy, widths[t]), dst_ref=out_refs[t].at[r],
                    send_sem=send_sems.at[3 * t + r - 1], recv_sem=recv_sems.at[3 * t + r - 1],
                    device_id=(ox, oy, c), device_id_type=MESH)
                cp.start()
                cps.append(cp)
        for cp in cps:
            cp.wait()

    return _pcall(
        body, name="scatter_layers", in_specs=[HBM] * n_t, out_specs=[HBM] * n_t,
        out_shape=[jax.ShapeDtypeStruct((N_CHIPS,) + w, p.dtype) for w, p in zip(wins, ps)],
        scratch_shapes=[pltpu.SemaphoreType.DMA((3 * n_t,)), pltpu.SemaphoreType.DMA((3 * n_t,)),
                        pltpu.SemaphoreType.DMA((n_t,))],
    )(*ps)


def _sum_chips(q, me):
    shape = q.shape[1:]
    C = shape[-1]
    H = math.prod(shape[:-1])
    tr = _row_tile(H, 4 * C, ROW_BLOCK_BYTES)

    def body(me_ref, q0, q1, q2, q3, o_ref):
        o_ref[...] = ((q0[0] + q1[0]) + q2[0]) + q3[0]

    def spec(chip):
        return pl.BlockSpec((1, tr, C), lambda i, me_ref: (jnp.bitwise_xor(me_ref[0], chip), i, 0))

    q = q.reshape(N_CHIPS, H, C)
    return _pcall(
        body, name="sum_chips",
        grid_spec=pltpu.PrefetchScalarGridSpec(
            num_scalar_prefetch=1, grid=(H // tr,),
            in_specs=[spec(0), spec(1), spec(2), spec(3)],
            out_specs=pl.BlockSpec((tr, C), lambda i, me_ref: (i, 0))),
        out_shape=jax.ShapeDtypeStruct((H, C), F32),
        compiler_params=pltpu.CompilerParams(dimension_semantics=("parallel",)),
    )(me.reshape(1).astype(jnp.int32), q, q, q, q).reshape(shape)


def _join_layers(rs):
    n_t = len(rs)

    def body(*refs):
        r_refs, out_refs = refs[:n_t], refs[n_t:2 * n_t]
        send_sems, recv_sems, local_sems = refs[2 * n_t:]
        x, y, c = _place()
        cps = []
        for t in range(n_t):
            mine = pltpu.make_async_copy(r_refs[t], out_refs[t].at[c], local_sems.at[t])
            mine.start()
            cps.append(mine)
            cp = pltpu.make_async_remote_copy(
                src_ref=r_refs[t], dst_ref=out_refs[t].at[c], send_sem=send_sems.at[t],
                recv_sem=recv_sems.at[t], device_id=(x, y, 1 - c), device_id_type=MESH)
            cp.start()
            cps.append(cp)
        for cp in cps:
            cp.wait()

    return _pcall(
        body, name="join_layers", in_specs=[HBM] * n_t, out_specs=[HBM] * n_t,
        out_shape=[jax.ShapeDtypeStruct((2,) + r.shape, r.dtype) for r in rs],
        scratch_shapes=[pltpu.SemaphoreType.DMA((n_t,)), pltpu.SemaphoreType.DMA((n_t,)),
                        pltpu.SemaphoreType.DMA((n_t,))],
    )(*rs)


def _reduce_scatter_layers(gs, axes):
    x, y, c = _place()
    others = _swap_layers(gs)
    ps = [_add_layer(g, o, c) for g, o in zip(gs, others)]
    qs = _scatter_layers(ps, [a - 1 for a in axes])
    rs = [_sum_chips(q, 2 * x + y) for q in qs]
    return _join_layers(rs)


def _allgather_devices(flat):
    R, C = flat.shape

    def body(x_ref, out_ref, send_sems, recv_sems, local_sem):
        x, y, c = _place()
        me = 4 * x + 2 * y + c
        mine = pltpu.make_async_copy(x_ref, out_ref.at[me], local_sem)
        mine.start()
        cps = []
        for m in range(1, 8):
            ox, oy = _other_chip(x, y, m >> 1)
            oc = 1 - c if m & 1 else c
            cp = pltpu.make_async_remote_copy(
                src_ref=x_ref, dst_ref=out_ref.at[me], send_sem=send_sems.at[m - 1],
                recv_sem=recv_sems.at[m - 1], device_id=(ox, oy, oc), device_id_type=MESH)
            cp.start()
            cps.append(cp)
        for cp in cps:
            cp.wait()
        mine.wait()

    return _pcall(
        body, name="allgather_devices", in_specs=[HBM], out_specs=HBM,
        out_shape=jax.ShapeDtypeStruct((8, R, C), flat.dtype),
        scratch_shapes=[pltpu.SemaphoreType.DMA((7,)), pltpu.SemaphoreType.DMA((7,)),
                        pltpu.SemaphoreType.DMA],
    )(flat)


def _sum_devices(slots):
    _, R, C = slots.shape
    tr = _row_tile(R, 4 * C, ROW_BLOCK_BYTES // 4)

    def body(*refs):
        o_ref = refs[8]
        acc = refs[0][0]
        for d in range(1, 8):
            acc = acc + refs[d][0]
        o_ref[...] = acc

    def spec(d):
        return pl.BlockSpec((1, tr, C), lambda i: (d, i, 0))

    return _pcall(
        body, name="sum_devices", grid=(R // tr,), in_specs=[spec(d) for d in range(8)],
        out_specs=pl.BlockSpec((tr, C), lambda i: (i, 0)),
        out_shape=jax.ShapeDtypeStruct((R, C), F32),
        compiler_params=pltpu.CompilerParams(dimension_semantics=("parallel",)),
    )(*([slots] * 8))


def _adamw(w, g, m, v):
    shape = w.shape
    C = shape[-1]
    R = math.prod(shape[:-1])
    w, g, m, v = (t.reshape(R, C) for t in (w, g, m, v))
    tr = _row_tile(R, 4 * C, ROW_BLOCK_BYTES // 2)
    c1 = 1.0 - ADAM_B1 ** ADAM_STEP
    c2 = 1.0 - ADAM_B2 ** ADAM_STEP

    def body(w_ref, g_ref, m_ref, v_ref, d_ref, nm_ref, nv_ref):
        gg = g_ref[...]
        nm = ADAM_B1 * m_ref[...] + (1.0 - ADAM_B1) * gg
        nv = ADAM_B2 * v_ref[...] + (1.0 - ADAM_B2) * jnp.square(gg)
        d_ref[...] = -ADAM_LR * ((nm / c1) / (jnp.sqrt(nv / c2) + ADAM_EPS) + ADAM_WD * w_ref[...])
        nm_ref[...] = nm
        nv_ref[...] = nv

    spec = pl.BlockSpec((tr, C), lambda i: (i, 0))
    outs = _pcall(
        body, name="adamw", grid=(R // tr,), in_specs=[spec] * 4, out_specs=[spec] * 3,
        out_shape=[jax.ShapeDtypeStruct((R, C), F32)] * 3,
        compiler_params=pltpu.CompilerParams(dimension_semantics=("parallel",)),
    )(w, g, m, v)
    return [o.reshape(shape) for o in outs]


def _to_flat(parts):
    flat = jnp.concatenate([p.reshape(-1) for p in parts])
    unit = FLAT_COLS * FLAT_ROW_ALIGN
    pad = (-flat.shape[0]) % unit
    if pad:
        flat = jnp.concatenate([flat, jnp.zeros((pad,), flat.dtype)])
    return flat.reshape(-1, FLAT_COLS)


def _from_flat(flat, shapes):
    flat = flat.reshape(-1)
    out, off = [], 0
    for s in shapes:
        n = math.prod(s)
        out.append(flat[off:off + n].reshape(s))
        off += n
    return out


def kernel(x, meta, ln_ffn1, ffn1_wi, ffn1_wo, ln_mix, w_in, fox_bf, mla_gq, mla_wq, mla_gkv, mla_wkv, gdn_conv, gdn_alog, gdn_dtb, gdn_gon, lru_conv, lru_conv_b, lru_wa, lru_ba, lru_wx, lru_bx, lru_lam, w_gate, b_gate, w_branch, w_out, ln_ffn2, ffn2_wi, ffn2_wo, ln_final, loss_target, m_meta, m_ln_ffn1, m_ffn1_wi, m_ffn1_wo, m_ln_mix, m_w_in, m_fox_bf, m_mla_gq, m_mla_wq, m_mla_gkv, m_mla_wkv, m_gdn_conv, m_gdn_alog, m_gdn_dtb, m_gdn_gon, m_lru_conv, m_lru_conv_b, m_lru_wa, m_lru_ba, m_lru_wx, m_lru_bx, m_lru_lam, m_w_gate, m_b_gate, m_w_branch, m_w_out, m_ln_ffn2, m_ffn2_wi, m_ffn2_wo, m_ln_final, v_meta, v_ln_ffn1, v_ffn1_wi, v_ffn1_wo, v_ln_mix, v_w_in, v_fox_bf, v_mla_gq, v_mla_wq, v_mla_gkv, v_mla_wkv, v_gdn_conv, v_gdn_alog, v_gdn_dtb, v_gdn_gon, v_lru_conv, v_lru_conv_b, v_lru_wa, v_lru_ba, v_lru_wx, v_lru_bx, v_lru_lam, v_w_gate, v_b_gate, v_w_branch, v_w_out, v_ln_ffn2, v_ffn2_wi, v_ffn2_wo, v_ln_final):
    ws = (meta, ln_ffn1, ffn1_wi, ffn1_wo, ln_mix, w_in, fox_bf, mla_gq, mla_wq, mla_gkv, mla_wkv, gdn_conv, gdn_alog, gdn_dtb, gdn_gon, lru_conv, lru_conv_b, lru_wa, lru_ba, lru_wx, lru_bx, lru_lam, w_gate, b_gate, w_branch, w_out, ln_ffn2, ffn2_wi, ffn2_wo, ln_final)
    ms = (m_meta, m_ln_ffn1, m_ffn1_wi, m_ffn1_wo, m_ln_mix, m_w_in, m_fox_bf, m_mla_gq, m_mla_wq, m_mla_gkv, m_mla_wkv, m_gdn_conv, m_gdn_alog, m_gdn_dtb, m_gdn_gon, m_lru_conv, m_lru_conv_b, m_lru_wa, m_lru_ba, m_lru_wx, m_lru_bx, m_lru_lam, m_w_gate, m_b_gate, m_w_branch, m_w_out, m_ln_ffn2, m_ffn2_wi, m_ffn2_wo, m_ln_final)
    vs = (v_meta, v_ln_ffn1, v_ffn1_wi, v_ffn1_wo, v_ln_mix, v_w_in, v_fox_bf, v_mla_gq, v_mla_wq, v_mla_gkv, v_mla_wkv, v_gdn_conv, v_gdn_alog, v_gdn_dtb, v_gdn_gon, v_lru_conv, v_lru_conv_b, v_lru_wa, v_lru_ba, v_lru_wx, v_lru_bx, v_lru_lam, v_w_gate, v_b_gate, v_w_branch, v_w_out, v_ln_ffn2, v_ffn2_wi, v_ffn2_wo, v_ln_final)
    names = [n for n, _ in WEIGHT_SPECS]
    axis = dict(WEIGHT_SPECS)
    wd, md, vd = dict(zip(names, ws)), dict(zip(names, ms)), dict(zip(names, vs))
    shapes = {n: wd[n].shape for n in names}
    big = [n for n in names if n in LARGE]
    few = [n for n in names if axis[n] is not None and n not in LARGE]
    whole = [n for n in names if axis[n] is None]
    x_, y_, _ = _place()
    chip = 2 * x_ + y_

    def to_dma(n, a):
        ax, w = axis[n], shapes[n][axis[n]]
        nd = len(shapes[n])
        if (ax == nd - 1 and w % LANES) or (ax == nd - 2 and w % 16):
            parts = a.shape[ax] // w
            a = jnp.moveaxis(a.reshape(a.shape[:ax] + (parts, w) + a.shape[ax + 1:]), ax, 1)
            return a, 1
        return a, ax

    def from_dma(n, a):
        ax = axis[n]
        if a.ndim == len(shapes[n]):
            return a
        a = jnp.moveaxis(a, 1, ax)
        return a.reshape(a.shape[:ax] + (-1,) + a.shape[ax + 2:])

    dma = [to_dma(n, wd[n].astype(BF16)) for n in big]
    fulls = _gather_layers([a for a, _ in dma], [ax for _, ax in dma])
    full_bf16 = {n: from_dma(n, f) for n, f in zip(big, fulls)}
    full = {n: jnp.zeros(f.shape, F32) for n, f in full_bf16.items()}
    gathered = _gather_chips(_to_flat([wd[n] for n in few]))
    per_chip = [_from_flat(gathered[k], [shapes[n] for n in few]) for k in range(N_CHIPS)]
    for i, n in enumerate(few):
        full[n] = jnp.concatenate([per_chip[k][i] for k in range(N_CHIPS)], axis=axis[n])
    full.update({n: wd[n] for n in whole})

    loss, (gw, gx) = jax.value_and_grad(_local_loss, argnums=(0, 1))(full, x, loss_target, full_bf16)
    loss = lax.psum(loss, ("x", "y", "c"))

    dma = [to_dma(n, gw[n]) for n in big]
    reduced = _reduce_scatter_layers([a for a, _ in dma], [ax for _, ax in dma])
    grads = {n: r.reshape(shapes[n]) for n, r in zip(big, reduced)}
    rest = few + whole
    summed = _from_flat(_sum_devices(_allgather_devices(_to_flat([gw[n] for n in rest]))),
                        [gw[n].shape for n in rest])
    for n, g in zip(rest, summed):
        if axis[n] is not None:
            g = lax.dynamic_slice_in_dim(g, chip * shapes[n][axis[n]], shapes[n][axis[n]], axis=axis[n])
        grads[n] = g

    delta, new_m, new_v = {}, {}, {}
    for n in big:
        delta[n], new_m[n], new_v[n] = _adamw(wd[n], grads[n], md[n], vd[n])
    outs = _adamw(*[_to_flat([d[n] for n in rest]) for d in (wd, grads, md, vd)])
    for res, flat in zip((delta, new_m, new_v), outs):
        res.update(zip(rest, _from_flat(flat, [shapes[n] for n in rest])))

    return (loss, gx, *[grads[n] for n in names], *[delta[n] for n in names],
            *[new_m[n] for n in names], *[new_v[n] for n in names])
```

```python
import functools
import math

import jax
import jax.numpy as jnp
from jax import lax
from jax.experimental import pallas as pl
from jax.experimental.pallas import tpu as pltpu

F32 = jnp.float32
BF16 = jnp.bfloat16

N_META = 16
BLOCK = 128
PAD_LEN = BLOCK - N_META
EPS = 1e-6
NEG_INF = -1e30
N_BRANCH = 4
FOX_HEADS, FOX_DH = 4, 64
MLA_HEADS, MLA_NOPE, MLA_ROPE, MLA_DV = 4, 64, 32, 64
MLA_Q_RANK, MLA_KV_RANK = 192, 128
ROPE_BASE = 10000.0
GDN_HEADS, GDN_DK, GDN_DV, GDN_CHUNK = 4, 64, 64, 64
LRU_WIDTH, LRU_BLOCKS, LRU_C = 256, 4, 8.0
DEPTH = 2

OFF_FOX_QKV = 0
OFF_FOX_F = OFF_FOX_QKV + 3 * FOX_HEADS * FOX_DH
OFF_MLA_CQ = OFF_FOX_F + FOX_HEADS
OFF_MLA_CKV = OFF_MLA_CQ + MLA_Q_RANK
OFF_MLA_KR = OFF_MLA_CKV + MLA_KV_RANK
OFF_GDN_QKV = OFF_MLA_KR + MLA_ROPE
OFF_GDN_A = OFF_GDN_QKV + GDN_HEADS * (2 * GDN_DK + GDN_DV)
OFF_GDN_B = OFF_GDN_A + GDN_HEADS
OFF_GDN_G = OFF_GDN_B + GDN_HEADS
OFF_LRU = OFF_GDN_G + GDN_HEADS * GDN_DV
N_IN = OFF_LRU + LRU_WIDTH

ADAM_LR, ADAM_B1, ADAM_B2, ADAM_EPS, ADAM_WD, ADAM_STEP = 0.001, 0.9, 0.999, 1e-08, 0.01, 10

WEIGHT_SPECS = (
    ("meta", 1), ("ln_ffn1", None), ("ffn1_wi", 2), ("ffn1_wo", 1), ("ln_mix", None), ("w_in", 2),
    ("fox_bf", None), ("mla_gq", None), ("mla_wq", 2), ("mla_gkv", None), ("mla_wkv", 2),
    ("gdn_conv", 2), ("gdn_alog", None), ("gdn_dtb", None), ("gdn_gon", None), ("lru_conv", 2),
    ("lru_conv_b", None), ("lru_wa", None), ("lru_ba", None), ("lru_wx", None), ("lru_bx", None),
    ("lru_lam", None), ("w_gate", 2), ("b_gate", 2), ("w_branch", 3), ("w_out", 1),
    ("ln_ffn2", None), ("ffn2_wi", 2), ("ffn2_wo", 1), ("ln_final", None),
)
N_CHIPS = 4
LARGE = ("ffn1_wi", "ffn1_wo", "w_in", "w_gate", "w_branch", "w_out", "ffn2_wi", "ffn2_wo")

LANES = 128
VMEM_LIMIT_BYTES = 48 * 1024 * 1024
FLAT_COLS = 512
FLAT_ROW_ALIGN = 64


def _pcall(body, **kw):
    return pl.pallas_call(body, **kw)


def _pick(n, cands):
    for c in cands:
        if n % c == 0:
            return c
    return n


_DN = {"nn": (((1,), (0,)), ((), ())), "nt": (((1,), (1,)), ((), ())), "tn": (((0,), (0,)), ((), ()))}


MATMUL_OPERAND_TILE_BYTES = 8 * 1024 * 1024


def _k_tile(K, row_bytes, lane_axis):
    for tk in (K, 4224, 2816, 2112, 1408, 1056, 1024, 768, 704, 512, 384, 256, 128):
        aligned = tk == K or tk % LANES == 0 or (not lane_axis and tk % 16 == 0)
        if tk <= K and K % tk == 0 and aligned and tk * row_bytes <= MATMUL_OPERAND_TILE_BYTES:
            return tk
    return K


def _matmul(a, b, mode, name, out_dtype=F32):
    if mode == "nn":
        (M, K), (_, N) = a.shape, b.shape
    elif mode == "nt":
        (M, K), (N, _) = a.shape, b.shape
    else:
        (K, M), (_, N) = a.shape, b.shape
    tm = _pick(M, (768, 512, 1408, 384, 256, 128, 64, 32, 16, 8))
    tn = _pick(N, (512, 1408, 256, 128))
    tk = _k_tile(K, tm * a.dtype.itemsize + tn * b.dtype.itemsize, mode != "tn")
    nk = K // tk
    a_spec = {"nn": pl.BlockSpec((tm, tk), lambda i, j, k: (i, k)),
              "nt": pl.BlockSpec((tm, tk), lambda i, j, k: (i, k)),
              "tn": pl.BlockSpec((tk, tm), lambda i, j, k: (k, i))}[mode]
    b_spec = {"nn": pl.BlockSpec((tk, tn), lambda i, j, k: (k, j)),
              "nt": pl.BlockSpec((tn, tk), lambda i, j, k: (j, k)),
              "tn": pl.BlockSpec((tk, tn), lambda i, j, k: (k, j))}[mode]
    dn = _DN[mode]

    def body(a_ref, b_ref, o_ref, acc_ref):
        k = pl.program_id(2)
        part = lax.dot_general(a_ref[...].astype(BF16), b_ref[...].astype(BF16), dn,
                               preferred_element_type=F32)

        if nk == 1:
            o_ref[...] = part.astype(o_ref.dtype)
        else:
            @pl.when(k == 0)
            def _():
                acc_ref[...] = part

            @pl.when((k > 0) & (k < nk - 1))
            def _():
                acc_ref[...] += part

            @pl.when(k == nk - 1)
            def _():
                o_ref[...] = (acc_ref[...] + part).astype(o_ref.dtype)

    return _pcall(
        body, name=name, grid=(M // tm, N // tn, nk),
        in_specs=[a_spec, b_spec], out_specs=pl.BlockSpec((tm, tn), lambda i, j, k: (i, j)),
        out_shape=jax.ShapeDtypeStruct((M, N), out_dtype),
        scratch_shapes=[pltpu.VMEM((tm, tn) if nk > 1 else (8, LANES), F32)],
        compiler_params=pltpu.CompilerParams(
            dimension_semantics=("parallel", "parallel", "arbitrary"),
            vmem_limit_bytes=VMEM_LIMIT_BYTES),
    )(a, b)


@functools.partial(jax.custom_vjp, nondiff_argnums=(3,))
def dense(a, wb, w, out_dtype):
    return _matmul(a, wb, "nn", "dense_fwd", out_dtype)


def _dense_fwd(a, wb, w, out_dtype):
    return _matmul(a, wb, "nn", "dense_fwd", out_dtype), (a, wb)


def _dense_bwd(out_dtype, res, g):
    a, wb = res
    return (_matmul(g, wb, "nt", "dense_dgrad", a.dtype), jnp.zeros_like(wb),
            _matmul(a, g, "tn", "dense_wgrad", F32))


dense.defvjp(_dense_fwd, _dense_bwd)


def dense_nd(a, wb, w, out_dtype=F32):
    K, N = wb.shape
    pad = (-N) % (4 * LANES if N > 4 * LANES else LANES)
    if pad:
        wb = jnp.pad(wb, ((0, 0), (0, pad)))
        w = jnp.pad(w, ((0, 0), (0, pad)))
    out = dense(a.reshape(-1, K), wb, w, out_dtype)
    if pad:
        out = out[:, :N]
    return out.reshape(a.shape[:-1] + (N,))


_NT = (((1,), (1,)), ((), ()))


def _att_blk(T):
    return _pick(T, (384, 256, 128))


def _causal_pairs(n, by_query):
    if by_query:
        pairs = [(i, j) for i in range(n) for j in range(i + 1)]
    else:
        pairs = [(i, j) for j in range(n) for i in range(j, n)]
    return (jnp.array([p[0] for p in pairs], jnp.int32), jnp.array([p[1] for p in pairs], jnp.int32))


def _attn_fwd_call(q, k, v, cum_col, cum_row, scale):
    BH, T, dk = q.shape
    dv = v.shape[2]
    blk = _att_blk(T)
    n = T // blk
    has_cum = cum_col is not None

    def body(i_tab, j_tab, *refs):
        if has_cum:
            q_ref, k_ref, v_ref, cq_ref, ck_ref, o_ref, lse_ref, m_s, l_s, acc_s = refs
        else:
            q_ref, k_ref, v_ref, o_ref, lse_ref, m_s, l_s, acc_s = refs
        i = i_tab[pl.program_id(1)]
        j = j_tab[pl.program_id(1)]

        @pl.when(j == 0)
        def _():
            m_s[...] = jnp.full_like(m_s, NEG_INF)
            l_s[...] = jnp.zeros_like(l_s)
            acc_s[...] = jnp.zeros_like(acc_s)

        def step(masked):
            s = lax.dot_general(q_ref[0].astype(BF16), k_ref[0].astype(BF16), _NT,
                                preferred_element_type=F32) * scale
            if has_cum:
                s = s + cq_ref[0] - ck_ref[0]
            if masked:
                qpos = i * blk + lax.broadcasted_iota(jnp.int32, (blk, blk), 0)
                kpos = j * blk + lax.broadcasted_iota(jnp.int32, (blk, blk), 1)
                mask = (kpos <= qpos) & (kpos >= PAD_LEN)
                s = jnp.where(mask, s, NEG_INF)
            m_prev = m_s[...]
            m_new = jnp.maximum(m_prev, jnp.max(s, axis=1, keepdims=True))
            p = jnp.exp(s - m_new)
            alpha = jnp.exp(m_prev - m_new)
            l_s[...] = alpha * l_s[...] + jnp.sum(p, axis=1, keepdims=True)
            acc_s[...] = alpha * acc_s[...] + jnp.dot(p.astype(BF16), v_ref[0].astype(BF16),
                                                      preferred_element_type=F32)
            m_s[...] = m_new

        @pl.when((j == i) | (j == 0))
        def _():
            step(True)

        @pl.when((j < i) & (j > 0))
        def _():
            step(False)

        @pl.when(j == i)
        def _():
            o_ref[0] = acc_s[...] / l_s[...]
            lse_ref[0] = m_s[...] + jnp.log(l_s[...])

    q_idx = lambda b, t, it, jt: (b, it[t], 0)
    kv_idx = lambda b, t, it, jt: (b, jt[t], 0)
    in_specs = [pl.BlockSpec((1, blk, dk), q_idx), pl.BlockSpec((1, blk, dk), kv_idx),
                pl.BlockSpec((1, blk, dv), kv_idx)]
    args = [q, k, v]
    if has_cum:
        in_specs += [pl.BlockSpec((1, blk, 1), q_idx),
                     pl.BlockSpec((1, 1, blk), lambda b, t, it, jt: (b, 0, jt[t]))]
        args += [cum_col, cum_row]
    i_tab, j_tab = _causal_pairs(n, by_query=True)
    return _pcall(
        body, name="attn_fwd_cum" if has_cum else "attn_fwd",
        grid_spec=pltpu.PrefetchScalarGridSpec(
            num_scalar_prefetch=2, grid=(BH, len(i_tab)), in_specs=in_specs,
            out_specs=[pl.BlockSpec((1, blk, dv), q_idx), pl.BlockSpec((1, blk, 1), q_idx)],
            scratch_shapes=[pltpu.VMEM((blk, 1), F32), pltpu.VMEM((blk, 1), F32), pltpu.VMEM((blk, dv), F32)]),
        out_shape=[jax.ShapeDtypeStruct((BH, T, dv), F32), jax.ShapeDtypeStruct((BH, T, 1), F32)],
        compiler_params=pltpu.CompilerParams(
            dimension_semantics=("parallel", "arbitrary"), vmem_limit_bytes=VMEM_LIMIT_BYTES),
    )(i_tab, j_tab, *args)


def _attn_dkv_call(q, k, v, do, lse_row, delta_row, cum_row, cum_col, scale):
    BH, T, dk = q.shape
    dv = v.shape[2]
    blk = _att_blk(T)
    n = T // blk
    has_cum = cum_row is not None

    def body(i_tab, j_tab, *refs):
        if has_cum:
            (k_ref, v_ref, q_ref, do_ref, lse_ref, dl_ref, cq_ref, ck_ref,
             dk_ref, dv_ref, dc_ref, dk_s, dv_s, dc_s) = refs
        else:
            k_ref, v_ref, q_ref, do_ref, lse_ref, dl_ref, dk_ref, dv_ref, dk_s, dv_s = refs
        i = i_tab[pl.program_id(1)]
        j = j_tab[pl.program_id(1)]

        @pl.when(i == j)
        def _():
            dk_s[...] = jnp.zeros_like(dk_s)
            dv_s[...] = jnp.zeros_like(dv_s)
            if has_cum:
                dc_s[...] = jnp.zeros_like(dc_s)

        def step(masked):
            kb = k_ref[0].astype(BF16)
            qb = q_ref[0].astype(BF16)
            dob = do_ref[0].astype(BF16)
            st = lax.dot_general(kb, qb, _NT, preferred_element_type=F32) * scale
            if has_cum:
                st = st + cq_ref[0] - ck_ref[0]
            pt = jnp.exp(jnp.minimum(st - lse_ref[0], 0.0))
            if masked:
                kpos = j * blk + lax.broadcasted_iota(jnp.int32, (blk, blk), 0)
                qpos = i * blk + lax.broadcasted_iota(jnp.int32, (blk, blk), 1)
                pt = jnp.where((kpos <= qpos) & (kpos >= PAD_LEN), pt, 0.0)
            dv_s[...] += jnp.dot(pt.astype(BF16), dob, preferred_element_type=F32)
            dpt = lax.dot_general(v_ref[0].astype(BF16), dob, _NT, preferred_element_type=F32)
            dst = pt * (dpt - dl_ref[0])
            dk_s[...] += jnp.dot(dst.astype(BF16), qb, preferred_element_type=F32) * scale
            if has_cum:
                dc_s[...] -= jnp.sum(dst, axis=1, keepdims=True)

        @pl.when((i == j) | (j == 0))
        def _():
            step(True)

        @pl.when((i > j) & (j > 0))
        def _():
            step(False)

        @pl.when(i == n - 1)
        def _():
            dk_ref[0] = dk_s[...]
            dv_ref[0] = dv_s[...]
            if has_cum:
                dc_ref[0] = dc_s[...]

    k_idx = lambda b, t, it, jt: (b, jt[t], 0)
    q_idx = lambda b, t, it, jt: (b, it[t], 0)
    row_idx = lambda b, t, it, jt: (b, 0, it[t])
    in_specs = [pl.BlockSpec((1, blk, dk), k_idx), pl.BlockSpec((1, blk, dv), k_idx),
                pl.BlockSpec((1, blk, dk), q_idx), pl.BlockSpec((1, blk, dv), q_idx),
                pl.BlockSpec((1, 1, blk), row_idx), pl.BlockSpec((1, 1, blk), row_idx)]
    args = [k, v, q, do, lse_row, delta_row]
    out_specs = [pl.BlockSpec((1, blk, dk), k_idx), pl.BlockSpec((1, blk, dv), k_idx)]
    out_shape = [jax.ShapeDtypeStruct((BH, T, dk), F32), jax.ShapeDtypeStruct((BH, T, dv), F32)]
    scratch = [pltpu.VMEM((blk, dk), F32), pltpu.VMEM((blk, dv), F32)]
    if has_cum:
        in_specs += [pl.BlockSpec((1, 1, blk), row_idx), pl.BlockSpec((1, blk, 1), k_idx)]
        args += [cum_row, cum_col]
        out_specs.append(pl.BlockSpec((1, blk, 1), k_idx))
        out_shape.append(jax.ShapeDtypeStruct((BH, T, 1), F32))
        scratch.append(pltpu.VMEM((blk, 1), F32))
    i_tab, j_tab = _causal_pairs(n, by_query=False)
    return _pcall(
        body, name="attn_dkv_cum" if has_cum else "attn_dkv",
        grid_spec=pltpu.PrefetchScalarGridSpec(
            num_scalar_prefetch=2, grid=(BH, len(i_tab)), in_specs=in_specs, out_specs=out_specs,
            scratch_shapes=scratch),
        out_shape=out_shape,
        compiler_params=pltpu.CompilerParams(
            dimension_semantics=("parallel", "arbitrary"), vmem_limit_bytes=VMEM_LIMIT_BYTES),
    )(i_tab, j_tab, *args)


def _attn_dq_call(q, k, v, do, lse_col, delta_col, cum_col, cum_row, scale):
    BH, T, dk = q.shape
    dv = v.shape[2]
    blk = _att_blk(T)
    n = T // blk
    has_cum = cum_col is not None

    def body(i_tab, j_tab, *refs):
        if has_cum:
            q_ref, k_ref, v_ref, do_ref, lse_ref, dl_ref, cq_ref, ck_ref, dq_ref, dc_ref, dq_s, dc_s = refs
        else:
            q_ref, k_ref, v_ref, do_ref, lse_ref, dl_ref, dq_ref, dq_s = refs
        i = i_tab[pl.program_id(1)]
        j = j_tab[pl.program_id(1)]

        @pl.when(j == 0)
        def _():
            dq_s[...] = jnp.zeros_like(dq_s)
            if has_cum:
                dc_s[...] = jnp.zeros_like(dc_s)

        def step(masked):
            kb = k_ref[0].astype(BF16)
            s = lax.dot_general(q_ref[0].astype(BF16), kb, _NT, preferred_element_type=F32) * scale
            if has_cum:
                s = s + cq_ref[0] - ck_ref[0]
            p = jnp.exp(jnp.minimum(s - lse_ref[0], 0.0))
            if masked:
                qpos = i * blk + lax.broadcasted_iota(jnp.int32, (blk, blk), 0)
                kpos = j * blk + lax.broadcasted_iota(jnp.int32, (blk, blk), 1)
                p = jnp.where((kpos <= qpos) & (kpos >= PAD_LEN), p, 0.0)
            dp = lax.dot_general(do_ref[0].astype(BF16), v_ref[0].astype(BF16), _NT,
                                 preferred_element_type=F32)
            ds = p * (dp - dl_ref[0])
            dq_s[...] += jnp.dot(ds.astype(BF16), kb, preferred_element_type=F32) * scale
            if has_cum:
                dc_s[...] += jnp.sum(ds, axis=1, keepdims=True)

        @pl.when((j == i) | (j == 0))
        def _():
            step(True)

        @pl.when((j < i) & (j > 0))
        def _():
            step(False)

        @pl.when(j == i)
        def _():
            dq_ref[0] = dq_s[...]
            if has_cum:
                dc_ref[0] = dc_s[...]

    kv_idx = lambda b, t, it, jt: (b, jt[t], 0)
    q_idx = lambda b, t, it, jt: (b, it[t], 0)
    in_specs = [pl.BlockSpec((1, blk, dk), q_idx), pl.BlockSpec((1, blk, dk), kv_idx),
                pl.BlockSpec((1, blk, dv), kv_idx), pl.BlockSpec((1, blk, dv), q_idx),
                pl.BlockSpec((1, blk, 1), q_idx), pl.BlockSpec((1, blk, 1), q_idx)]
    args = [q, k, v, do, lse_col, delta_col]
    if has_cum:
        in_specs += [pl.BlockSpec((1, blk, 1), q_idx),
                     pl.BlockSpec((1, 1, blk), lambda b, t, it, jt: (b, 0, jt[t]))]
        args += [cum_col, cum_row]
    out_specs = [pl.BlockSpec((1, blk, dk), q_idx)]
    out_shape = [jax.ShapeDtypeStruct((BH, T, dk), F32)]
    scratch = [pltpu.VMEM((blk, dk), F32)]
    if has_cum:
        out_specs.append(pl.BlockSpec((1, blk, 1), q_idx))
        out_shape.append(jax.ShapeDtypeStruct((BH, T, 1), F32))
        scratch.append(pltpu.VMEM((blk, 1), F32))
    i_tab, j_tab = _causal_pairs(n, by_query=True)
    return _pcall(
        body, name="attn_dq_cum" if has_cum else "attn_dq",
        grid_spec=pltpu.PrefetchScalarGridSpec(
            num_scalar_prefetch=2, grid=(BH, len(i_tab)), in_specs=in_specs, out_specs=out_specs,
            scratch_shapes=scratch),
        out_shape=out_shape,
        compiler_params=pltpu.CompilerParams(
            dimension_semantics=("parallel", "arbitrary"), vmem_limit_bytes=VMEM_LIMIT_BYTES),
    )(i_tab, j_tab, *args)


def _make_attention(scale, has_cum):
    def fold(t):
        return t.reshape((-1,) + t.shape[2:])

    def run_fwd(q, k, v, cum):
        B, H, T, _ = q.shape
        col = cum.reshape(B * H, T, 1) if has_cum else None
        row = cum.reshape(B * H, 1, T) if has_cum else None
        o, lse = _attn_fwd_call(fold(q), fold(k), fold(v), col, row, scale)
        return o.reshape(B, H, T, -1), lse

    @jax.custom_vjp
    def attn(q, k, v, cum):
        return run_fwd(q, k, v, cum)[0]

    def attn_fwd(q, k, v, cum):
        o, lse = run_fwd(q, k, v, cum)
        return o, (q, k, v, cum, o, lse)

    def attn_bwd(res, do):
        q, k, v, cum, o, lse = res
        B, H, T, _ = q.shape
        delta = jnp.sum(do * o, axis=-1).reshape(B * H, T, 1)
        col = cum.reshape(B * H, T, 1) if has_cum else None
        row = cum.reshape(B * H, 1, T) if has_cum else None
        qf, kf, vf, dof = fold(q), fold(k), fold(v), fold(do)
        outs = _attn_dkv_call(qf, kf, vf, dof, lse.reshape(B * H, 1, T), delta.reshape(B * H, 1, T),
                              row, col, scale)
        dqs = _attn_dq_call(qf, kf, vf, dof, lse, delta, col, row, scale)
        dcum = (outs[2] + dqs[1]).reshape(B, H, T) if has_cum else jnp.zeros_like(cum)
        return (dqs[0].reshape(q.shape), outs[0].reshape(k.shape), outs[1].reshape(v.shape), dcum)

    attn.defvjp(attn_fwd, attn_bwd)
    return attn


def _rmsnorm(x, g):
    return x * lax.rsqrt(jnp.mean(x * x, axis=-1, keepdims=True) + EPS) * g


def _l2norm(x):
    return x * lax.rsqrt(jnp.sum(x * x, axis=-1, keepdims=True) + EPS)


def _row_tile(rows, row_bytes, budget):
    for t in (2048, 1024, 512, 256, 128, 64, 32, 16, 8):
        if rows % t == 0 and t * row_bytes <= budget:
            return t
    return rows


ROW_BLOCK_BYTES = 2 * 1024 * 1024


def _rms_fwd_call(x, g):
    M, D = x.shape
    tr = _row_tile(M, 4 * D, ROW_BLOCK_BYTES)

    def body(x_ref, g_ref, y_ref, r_ref):
        xv = x_ref[...]
        r = lax.rsqrt(jnp.mean(xv * xv, axis=-1, keepdims=True) + EPS)
        y_ref[...] = xv * r * g_ref[...]
        r_ref[...] = r

    return _pcall(
        body, name="rmsnorm_fwd", grid=(M // tr,),
        in_specs=[pl.BlockSpec((tr, D), lambda i: (i, 0)), pl.BlockSpec((1, D), lambda i: (0, 0))],
        out_specs=[pl.BlockSpec((tr, D), lambda i: (i, 0)), pl.BlockSpec((tr, 1), lambda i: (i, 0))],
        out_shape=[jax.ShapeDtypeStruct((M, D), F32), jax.ShapeDtypeStruct((M, 1), F32)],
        compiler_params=pltpu.CompilerParams(dimension_semantics=("parallel",)),
    )(x, g)


def _rms_bwd_call(x, g, r, dy):
    M, D = x.shape
    tr = _row_tile(M, 4 * D, ROW_BLOCK_BYTES)

    def body(x_ref, g_ref, r_ref, dy_ref, dx_ref, dg_ref):
        i = pl.program_id(0)
        rv = r_ref[...]
        xh = x_ref[...] * rv
        dyv = dy_ref[...]
        dyg = dyv * g_ref[...]
        dx_ref[...] = rv * (dyg - xh * jnp.mean(dyg * xh, axis=-1, keepdims=True))
        part = jnp.sum(dyv * xh, axis=0, keepdims=True)

        @pl.when(i == 0)
        def _():
            dg_ref[...] = part

        @pl.when(i > 0)
        def _():
            dg_ref[...] += part

    row = pl.BlockSpec((tr, D), lambda i: (i, 0))
    return _pcall(
        body, name="rmsnorm_bwd", grid=(M // tr,),
        in_specs=[row, pl.BlockSpec((1, D), lambda i: (0, 0)), pl.BlockSpec((tr, 1), lambda i: (i, 0)), row],
        out_specs=[row, pl.BlockSpec((1, D), lambda i: (0, 0))],
        out_shape=[jax.ShapeDtypeStruct((M, D), F32), jax.ShapeDtypeStruct((1, D), F32)],
        compiler_params=pltpu.CompilerParams(dimension_semantics=("arbitrary",)),
    )(x, g, r, dy)


@jax.custom_vjp
def _rmsnorm2d(x, g):
    return _rms_fwd_call(x, g)[0]


def _rmsnorm2d_fwd(x, g):
    y, r = _rms_fwd_call(x, g)
    return y, (x, g, r)


def _rmsnorm2d_bwd(res, dy):
    x, g, r = res
    return _rms_bwd_call(x, g, r, dy)


_rmsnorm2d.defvjp(_rmsnorm2d_fwd, _rmsnorm2d_bwd)


def _rmsnorm_rows(x, g):
    D = x.shape[-1]
    return _rmsnorm2d(x.reshape(-1, D), g.reshape(1, D)).reshape(x.shape)


def _glu_fwd_call(gu):
    M, F2 = gu.shape
    F = F2 // 2
    tr = _row_tile(M, 4 * F2, 2 * ROW_BLOCK_BYTES)

    def body(g_ref, u_ref, o_ref):
        gv = g_ref[...].astype(F32)
        o_ref[...] = (gv * jax.nn.sigmoid(gv) * u_ref[...].astype(F32)).astype(o_ref.dtype)

    return _pcall(
        body, name="swiglu_fwd", grid=(M // tr,),
        in_specs=[pl.BlockSpec((tr, F), lambda i: (i, 0)), pl.BlockSpec((tr, F), lambda i: (i, 1))],
        out_specs=pl.BlockSpec((tr, F), lambda i: (i, 0)),
        out_shape=jax.ShapeDtypeStruct((M, F), gu.dtype),
        compiler_params=pltpu.CompilerParams(dimension_semantics=("parallel",),
                                             vmem_limit_bytes=VMEM_LIMIT_BYTES),
    )(gu, gu)


def _glu_bwd_call(gu, da):
    M, F2 = gu.shape
    F = F2 // 2
    tr = _row_tile(M, 4 * F2, 2 * ROW_BLOCK_BYTES)

    def body(g_ref, u_ref, da_ref, o_ref):
        gv = g_ref[...].astype(F32)
        s = jax.nn.sigmoid(gv)
        dav = da_ref[...].astype(F32)
        o_ref[:, :F] = (dav * u_ref[...].astype(F32) * (s * (1.0 + gv * (1.0 - s)))).astype(o_ref.dtype)
        o_ref[:, F:] = (dav * (gv * s)).astype(o_ref.dtype)

    return _pcall(
        body, name="swiglu_bwd", grid=(M // tr,),
        in_specs=[pl.BlockSpec((tr, F), lambda i: (i, 0)), pl.BlockSpec((tr, F), lambda i: (i, 1)),
                  pl.BlockSpec((tr, F), lambda i: (i, 0))],
        out_specs=pl.BlockSpec((tr, F2), lambda i: (i, 0)),
        out_shape=jax.ShapeDtypeStruct((M, F2), gu.dtype),
        compiler_params=pltpu.CompilerParams(dimension_semantics=("parallel",),
                                             vmem_limit_bytes=VMEM_LIMIT_BYTES),
    )(gu, gu, da)


@jax.custom_vjp
def _glu(gu):
    return _glu_fwd_call(gu)


def _glu_fwd(gu):
    return _glu_fwd_call(gu), gu


def _glu_bwd(gu, da):
    return (_glu_bwd_call(gu, da),)


_glu.defvjp(_glu_fwd, _glu_bwd)


def _swiglu(h, wi, wo):
    lead, D = h.shape[:-1], h.shape[-1]
    gu = dense(h.reshape(-1, D), wi[0], wi[1], BF16)
    return dense(_glu(gu), wo[0], wo[1], F32).reshape(lead + (wo[0].shape[1],))


def _causal_dwconv(x, w):
    K, C = w.shape
    return lax.conv_general_dilated(
        x, w[:, None, :], window_strides=(1,), padding=[(K - 1, 0)],
        dimension_numbers=("NWC", "WIO", "NWC"), feature_group_count=C)


def _rope(x, cos, sin):
    half = x.shape[-1] // 2
    x1, x2 = x[..., :half], x[..., half:]
    return jnp.concatenate([x1 * cos - x2 * sin, x2 * cos + x1 * sin], axis=-1)


def _fox_branch(p, b_f):
    B, T, _ = p.shape
    qkv = p[..., OFF_FOX_QKV:OFF_FOX_F].reshape(B, T, 3, FOX_HEADS, FOX_DH)
    q = qkv[:, :, 0].transpose(0, 2, 1, 3)
    k = qkv[:, :, 1].transpose(0, 2, 1, 3)
    v = qkv[:, :, 2].transpose(0, 2, 1, 3)
    log_f = jax.nn.log_sigmoid(p[..., OFF_FOX_F:OFF_MLA_CQ] + b_f)
    cum = jnp.cumsum(log_f, axis=1).transpose(0, 2, 1)
    o = _make_attention(FOX_DH ** -0.5, True)(q, k, v, cum)
    return o.transpose(0, 2, 1, 3).reshape(B, T, FOX_HEADS * FOX_DH)


def _mla_branch(p, g_qn, w_q_up, g_kvn, w_kv_up, cos, sin):
    B, T, _ = p.shape
    cq = _rmsnorm(p[..., OFF_MLA_CQ:OFF_MLA_CKV], g_qn)
    q = (cq @ w_q_up).reshape(B, T, MLA_HEADS, MLA_NOPE + MLA_ROPE)
    ckv = _rmsnorm(p[..., OFF_MLA_CKV:OFF_MLA_KR], g_kvn)
    kv = (ckv @ w_kv_up).reshape(B, T, MLA_HEADS, MLA_NOPE + MLA_DV)
    k_rope = _rope(p[..., OFF_MLA_KR:OFF_GDN_QKV], cos, sin)
    q_rope = _rope(q[..., MLA_NOPE:], cos[:, None], sin[:, None])
    q = jnp.concatenate([q[..., :MLA_NOPE], q_rope], axis=-1)
    k = jnp.concatenate([kv[..., :MLA_NOPE],
                         jnp.broadcast_to(k_rope[:, :, None], (B, T, MLA_HEADS, MLA_ROPE))], axis=-1)
    v = kv[..., MLA_NOPE:]
    o = _make_attention((MLA_NOPE + MLA_ROPE) ** -0.5, False)(
        q.transpose(0, 2, 1, 3), k.transpose(0, 2, 1, 3), v.transpose(0, 2, 1, 3),
        jnp.zeros((B, MLA_HEADS, T), F32))
    return o.transpose(0, 2, 1, 3).reshape(B, T, MLA_HEADS * MLA_DV)


def _bmm(a, b):
    return lax.dot_general(a.astype(BF16), b.astype(BF16), (((2,), (1,)), ((0,), (0,))),
                           preferred_element_type=F32)


def _bmm_nt(a, b):
    return lax.dot_general(a.astype(BF16), b.astype(BF16), (((2,), (2,)), ((0,), (0,))),
                           preferred_element_type=F32)


def _chunk_spec(shape, index):
    return pl.BlockSpec((1,) + tuple(shape[1:]), lambda n: (index(n), 0, 0, 0))


def _gdn_scan_fwd_call(qd, kdt, w, u, qk, gl):
    nc, BH, C, DK = qd.shape
    DV = u.shape[3]

    def body(qd_ref, kdt_ref, w_ref, u_ref, qk_ref, gl_ref, o_ref, s_ref, vn_ref, state):
        @pl.when(pl.program_id(0) == 0)
        def _():
            state[...] = jnp.zeros_like(state)

        s = state[...]
        s_ref[0] = s
        vn = u_ref[0] - _bmm(w_ref[0], s)
        vn_ref[0] = vn
        o_ref[0] = _bmm(qd_ref[0], s) + _bmm(qk_ref[0], vn)
        state[...] = s * gl_ref[0] + _bmm(kdt_ref[0], vn)

    fwd = lambda n: n
    outs = [(nc, BH, C, DV), (nc, BH, DK, DV), (nc, BH, C, DV)]
    return _pcall(
        body, name="gdn_scan_fwd", grid=(nc,),
        in_specs=[_chunk_spec(t.shape, fwd) for t in (qd, kdt, w, u, qk, gl)],
        out_specs=[_chunk_spec(s, fwd) for s in outs],
        out_shape=[jax.ShapeDtypeStruct(s, F32) for s in outs],
        scratch_shapes=[pltpu.VMEM((BH, DK, DV), F32)],
        compiler_params=pltpu.CompilerParams(dimension_semantics=("arbitrary",)),
    )(qd, kdt, w, u, qk, gl)


def _gdn_scan_bwd_call(do, qdt, kd, wt, qkt, gl, s_all, vn):
    nc, BH, C, DV = do.shape
    DK = kd.shape[3]

    def body(do_ref, qdt_ref, kd_ref, wt_ref, qkt_ref, gl_ref, s_ref, vn_ref,
             dqd_ref, dkd_ref, dw_ref, du_ref, dqk_ref, dgl_ref, dstate):
        @pl.when(pl.program_id(0) == 0)
        def _():
            dstate[...] = jnp.zeros_like(dstate)

        ds = dstate[...]
        s, v, dov = s_ref[0], vn_ref[0], do_ref[0]
        dkd_ref[0] = _bmm_nt(v, ds)
        dgl_ref[0] = s * ds
        dqd_ref[0] = _bmm_nt(dov, s)
        dqk_ref[0] = _bmm_nt(dov, v)
        dv = _bmm(kd_ref[0], ds) + _bmm(qkt_ref[0], dov)
        du_ref[0] = dv
        dw_ref[0] = -_bmm_nt(dv, s)
        dstate[...] = ds * gl_ref[0] + _bmm(qdt_ref[0], dov) - _bmm(wt_ref[0], dv)

    rev = lambda n: nc - 1 - n
    outs = [(nc, BH, C, DK)] * 3 + [(nc, BH, C, DV), (nc, BH, C, C), (nc, BH, DK, DV)]
    return _pcall(
        body, name="gdn_scan_bwd", grid=(nc,),
        in_specs=[_chunk_spec(t.shape, rev) for t in (do, qdt, kd, wt, qkt, gl, s_all, vn)],
        out_specs=[_chunk_spec(s, rev) for s in outs],
        out_shape=[jax.ShapeDtypeStruct(s, F32) for s in outs],
        scratch_shapes=[pltpu.VMEM((BH, DK, DV), F32)],
        compiler_params=pltpu.CompilerParams(dimension_semantics=("arbitrary",)),
    )(do, qdt, kd, wt, qkt, gl, s_all, vn)


def _gl_rows(gl, dv):
    return jnp.broadcast_to(gl[:, :, None, None], gl.shape + (1, dv))


@jax.custom_vjp
def _gdn_scan(qd, kd, w, u, qk, gl):
    return _gdn_scan_fwd_call(qd, jnp.swapaxes(kd, 2, 3), w, u, qk, _gl_rows(gl, u.shape[3]))[0]


def _gdn_scan_fwd(qd, kd, w, u, qk, gl):
    o, s_all, vn = _gdn_scan_fwd_call(qd, jnp.swapaxes(kd, 2, 3), w, u, qk, _gl_rows(gl, u.shape[3]))
    return o, (qd, kd, w, qk, gl, s_all, vn)


def _gdn_scan_bwd(res, do):
    qd, kd, w, qk, gl, s_all, vn = res
    dqd, dkd, dw, du, dqk, dgl = _gdn_scan_bwd_call(
        do, jnp.swapaxes(qd, 2, 3), kd, jnp.swapaxes(w, 2, 3), jnp.swapaxes(qk, 2, 3),
        _gl_rows(gl, do.shape[3]), s_all, vn)
    return dqd, dkd, dw, du, dqk, jnp.sum(dgl, axis=(2, 3))


_gdn_scan.defvjp(_gdn_scan_fwd, _gdn_scan_bwd)


def _gdn_branch(p, conv_w, a_log, dt_bias, g_on):
    B, T, _ = p.shape
    H, DK, DV, C = GDN_HEADS, GDN_DK, GDN_DV, GDN_CHUNK
    qkv = jax.nn.silu(_causal_dwconv(p[..., OFF_GDN_QKV:OFF_GDN_A], conv_w))
    q = _l2norm(qkv[..., :H * DK].reshape(B, T, H, DK)) * DK ** -0.5
    k = _l2norm(qkv[..., H * DK:2 * H * DK].reshape(B, T, H, DK))
    v = qkv[..., 2 * H * DK:].reshape(B, T, H, DV)
    beta = jax.nn.sigmoid(p[..., OFF_GDN_B:OFF_GDN_G])
    g = -jnp.exp(a_log) * jax.nn.softplus(p[..., OFF_GDN_A:OFF_GDN_B] + dt_bias)
    nc = T // C

    def chunks(t):
        return jnp.moveaxis(t, 2, 1).reshape((B, H, nc, C) + t.shape[3:])

    q, k, v, beta, g = chunks(q), chunks(k), chunks(v), chunks(beta), chunks(g)
    G = jnp.cumsum(g, axis=-1)
    idx = jnp.arange(C)
    strict = idx[:, None] > idx[None, :]
    incl = idx[:, None] >= idx[None, :]
    decay = jnp.exp(jnp.where(incl, G[..., :, None] - G[..., None, :], NEG_INF))
    kb = k * beta[..., None]
    vb = v * beta[..., None]
    m = jnp.eye(C, dtype=F32) + jnp.where(
        strict, jnp.einsum("bhnik,bhnjk->bhnij", kb, k) * decay, 0.0)
    rhs = jnp.concatenate([kb * jnp.exp(G)[..., None], vb], axis=-1)
    sol = lax.linalg.triangular_solve(m, rhs, left_side=True, lower=True, unit_diagonal=True)
    w, u = sol[..., :DK], sol[..., DK:]
    qk = jnp.where(incl, jnp.einsum("bhnik,bhnjk->bhnij", q, k) * decay, 0.0)
    q_dec = q * jnp.exp(G)[..., None]
    k_dec = k * jnp.exp(G[..., -1:] - G)[..., None]
    g_last = jnp.exp(G[..., -1])
    def chunk_major(t):
        return jnp.moveaxis(t, 2, 0).reshape((nc, B * H) + t.shape[3:])

    o = _gdn_scan(chunk_major(q_dec), chunk_major(k_dec), chunk_major(w), chunk_major(u),
                  chunk_major(qk), chunk_major(g_last))
    o = jnp.moveaxis(o.reshape(nc, B, H, C, DV), 0, 2).reshape(B, H, T, DV).transpose(0, 2, 1, 3)
    gate = jax.nn.silu(p[..., OFF_GDN_G:OFF_LRU]).reshape(B, T, H, DV)
    o = _rmsnorm(o, g_on) * gate
    return o.reshape(B, T, H * DV)


def _rglru_branch(p, valid, conv_w, conv_b, w_a, b_a, w_x, b_x, lam):
    B, T, _ = p.shape
    xr = _causal_dwconv(p[..., OFF_LRU:N_IN], conv_w) + conv_b
    xr = jnp.where(valid[None, :, None], xr, 0)
    xb = xr.reshape(B, T, LRU_BLOCKS, LRU_WIDTH // LRU_BLOCKS)
    r = jax.nn.sigmoid(jnp.einsum("btni,nij->btnj", xb, w_a).reshape(B, T, LRU_WIDTH) + b_a)
    ig = jax.nn.sigmoid(jnp.einsum("btni,nij->btnj", xb, w_x).reshape(B, T, LRU_WIDTH) + b_x)
    log_a = -LRU_C * r * jax.nn.softplus(-lam)
    a = jnp.exp(log_a)
    b = jnp.sqrt(-jnp.expm1(2.0 * log_a)) * ig * xr

    def combine(e1, e2):
        return (e1[0] * e2[0], e2[0] * e1[1] + e2[1])

    _, h = lax.associative_scan(combine, (a, b), axis=1)
    return h


def _mixer(u, valid, cos, sin, w, wb, l):
    def pair(name, *idx):
        return wb[name][(l,) + idx], w[name][(l,) + idx]

    p = dense_nd(u, *pair("w_in"))
    ys = (_fox_branch(p, w["fox_bf"][l]),
          _mla_branch(p, w["mla_gq"][l], w["mla_wq"][l], w["mla_gkv"][l], w["mla_wkv"][l], cos, sin),
          _gdn_branch(p, w["gdn_conv"][l], w["gdn_alog"][l], w["gdn_dtb"][l], w["gdn_gon"][l]),
          _rglru_branch(p, valid, w["lru_conv"][l], w["lru_conv_b"][l], w["lru_wa"][l], w["lru_ba"][l],
                        w["lru_wx"][l], w["lru_bx"][l], w["lru_lam"][l]))
    D = u.shape[-1]
    w_gate = [jnp.moveaxis(t, 0, 1).reshape(D, N_BRANCH * D) for t in pair("w_gate")]
    gates = jax.nn.sigmoid(dense_nd(u, *w_gate) + w["b_gate"][l].reshape(-1))
    merged = gates[..., :D] * dense_nd(ys[0], *pair("w_branch", 0))
    for n in range(1, N_BRANCH):
        merged = merged + gates[..., n * D:(n + 1) * D] * dense_nd(ys[n], *pair("w_branch", n))
    return dense_nd(merged, *pair("w_out"))


def _local_loss(w, x, loss_target, wb):
    B, S, D = x.shape
    T = BLOCK + S
    h = jnp.concatenate([jnp.zeros((B, PAD_LEN, D), F32),
                         jnp.broadcast_to(w["meta"][None], (B, N_META, D)), x], axis=1)
    pos = jnp.arange(T)
    valid = pos >= PAD_LEN
    rel = (pos - PAD_LEN).astype(F32)
    inv_freq = ROPE_BASE ** (-(jnp.arange(0, MLA_ROPE, 2, dtype=F32) / MLA_ROPE))
    ang = rel[:, None] * inv_freq[None, :]
    cos, sin = jnp.cos(ang), jnp.sin(ang)
    def pair(name, l):
        return wb[name][l], w[name][l]

    for l in range(DEPTH):
        h = h + 0.5 * _swiglu(_rmsnorm_rows(h, w["ln_ffn1"][l]), pair("ffn1_wi", l), pair("ffn1_wo", l))
        u = jnp.where(valid[None, :, None], _rmsnorm_rows(h, w["ln_mix"][l]), 0)
        h = h + _mixer(u, valid, cos, sin, w, wb, l)
        h = h + 0.5 * _swiglu(_rmsnorm_rows(h, w["ln_ffn2"][l]), pair("ffn2_wi", l), pair("ffn2_wo", l))
    y = _rmsnorm_rows(h, w["ln_final"])[:, BLOCK:]
    err = jnp.square(y - loss_target)
    return 0.5 * jnp.sum(jnp.mean(err, axis=-1))


MESH = pl.DeviceIdType.MESH
HBM = pl.BlockSpec(memory_space=pl.ANY)


def _place():
    x, y, c = lax.axis_index("x"), lax.axis_index("y"), lax.axis_index("c")
    return x, y, c


def _other_chip(x, y, r):
    return (1 - x if r & 2 else x), (1 - y if r & 1 else y)


def _gather_chips(shard):
    R, C = shard.shape
    H = R // 2

    def body(x_ref, out_ref, send_sems, recv_sems, local_sem):
        x, y, c = _place()
        me = 2 * x + y

        def rows(chip, half):
            return out_ref.at[chip, pl.ds(half * H, H), :]

        def copy(sem, src, dst, to):
            return pltpu.make_async_remote_copy(src_ref=src, dst_ref=dst, send_sem=send_sems.at[sem],
                                                recv_sem=recv_sems.at[sem], device_id=to, device_id_type=MESH)

        mine = pltpu.make_async_copy(x_ref, out_ref.at[me], local_sem)
        mine.start()
        started = []
        for r in (1, 2, 3):
            ox, oy = _other_chip(x, y, r)
            cp = copy(r - 1, x_ref.at[pl.ds(c * H, H), :], rows(me, c), (ox, oy, c))
            cp.start()
            started.append(cp)
        for r in (1, 2, 3):
            ox, oy = _other_chip(x, y, r)
            src = 2 * ox + oy
            copy(r - 1, rows(src, c), rows(src, c), (x, y, c)).wait_recv()
            fw = copy(2 + r, rows(src, c), rows(src, c), (x, y, 1 - c))
            fw.start()
            started.append(fw)
        for r in (1, 2, 3):
            ox, oy = _other_chip(x, y, r)
            src = 2 * ox + oy
            copy(2 + r, rows(src, 1 - c), rows(src, 1 - c), (x, y, c)).wait_recv()
        for cp in started:
            cp.wait_send()
        mine.wait()

    return _pcall(
        body, name="gather_chips", in_specs=[HBM], out_specs=HBM,
        out_shape=jax.ShapeDtypeStruct((N_CHIPS, R, C), shard.dtype),
        scratch_shapes=[pltpu.SemaphoreType.DMA((6,)), pltpu.SemaphoreType.DMA((6,)),
                        pltpu.SemaphoreType.DMA],
    )(shard)


def _window(ref, lead, axis, chip, width):
    idx = [slice(None)] * len(ref.shape)
    if lead is not None:
        idx[0] = lead
    idx[axis] = pl.ds(chip * width, width)
    return ref.at[tuple(idx)]


def _gather_layers(shards, axes):
    n_t = len(shards)
    widths = [s.shape[a] for s, a in zip(shards, axes)]
    fulls = [s.shape[:a] + (N_CHIPS * s.shape[a],) + s.shape[a + 1:] for s, a in zip(shards, axes)]

    def body(*refs):
        x_refs, out_refs = refs[:n_t], refs[n_t:2 * n_t]
        send_sems, recv_sems = refs[2 * n_t:]
        x, y, c = _place()
        me = 2 * x + y

        def win(t, lead, chip):
            return _window(out_refs[t], lead, axes[t], chip, widths[t])

        def copy(sem, src, dst, to):
            return pltpu.make_async_remote_copy(src_ref=src, dst_ref=dst, send_sem=send_sems.at[sem],
                                                recv_sem=recv_sems.at[sem], device_id=to, device_id_type=MESH)

        started = []
        for t in range(n_t):
            cp = copy(7 * t + 6, x_refs[t], win(t, None, me), (x, y, 1 - c))
            cp.start()
            started.append(cp)
        for r in (1, 2, 3):
            ox, oy = _other_chip(x, y, r)
            for t in range(n_t):
                cp = copy(7 * t + r - 1, x_refs[t].at[c], win(t, c, me), (ox, oy, c))
                cp.start()
                started.append(cp)
        for r in (1, 2, 3):
            ox, oy = _other_chip(x, y, r)
            src = 2 * ox + oy
            for t in range(n_t):
                copy(7 * t + r - 1, win(t, c, src), win(t, c, src), (x, y, c)).wait_recv()
                fw = copy(7 * t + 2 + r, win(t, c, src), win(t, c, src), (x, y, 1 - c))
                fw.start()
                started.append(fw)
        for r in (1, 2, 3):
            ox, oy = _other_chip(x, y, r)
            src = 2 * ox + oy
            for t in range(n_t):
                copy(7 * t + 2 + r, win(t, 1 - c, src), win(t, 1 - c, src), (x, y, c)).wait_recv()
        for t in range(n_t):
            copy(7 * t + 6, win(t, None, me), win(t, None, me), (x, y, c)).wait_recv()
        for cp in started:
            cp.wait_send()

    return _pcall(
        body, name="gather_layers", in_specs=[HBM] * n_t, out_specs=[HBM] * n_t,
        out_shape=[jax.ShapeDtypeStruct(f, s.dtype) for f, s in zip(fulls, shards)],
        scratch_shapes=[pltpu.SemaphoreType.DMA((7 * n_t,)), pltpu.SemaphoreType.DMA((7 * n_t,))],
    )(*shards)


def _swap_layers(gs):
    n_t = len(gs)

    def body(*refs):
        g_refs, out_refs = refs[:n_t], refs[n_t:2 * n_t]
        send_sems, recv_sems = refs[2 * n_t:]
        x, y, c = _place()
        cps = []
        for t in range(n_t):
            cp = pltpu.make_async_remote_copy(
                src_ref=g_refs[t].at[1 - c], dst_ref=out_refs[t], send_sem=send_sems.at[t],
                recv_sem=recv_sems.at[t], device_id=(x, y, 1 - c), device_id_type=MESH)
            cp.start()
            cps.append(cp)
        for cp in cps:
            cp.wait()

    return _pcall(
        body, name="swap_layers", in_specs=[HBM] * n_t, out_specs=[HBM] * n_t,
        out_shape=[jax.ShapeDtypeStruct(g.shape[1:], g.dtype) for g in gs],
        scratch_shapes=[pltpu.SemaphoreType.DMA((n_t,)), pltpu.SemaphoreType.DMA((n_t,))],
    )(*gs)


def _add_layer(g, other, c):
    shape = other.shape
    last = shape[-1]
    rows = math.prod(shape[:-1])
    tr = _row_tile(rows, 4 * last, ROW_BLOCK_BYTES)

    def body(c_ref, a_ref, b_ref, o_ref):
        o_ref[...] = a_ref[0] + b_ref[...]

    out = _pcall(
        body, name="add_layer",
        grid_spec=pltpu.PrefetchScalarGridSpec(
            num_scalar_prefetch=1, grid=(rows // tr,),
            in_specs=[pl.BlockSpec((1, tr, last), lambda i, c_ref: (c_ref[0], i, 0)),
                      pl.BlockSpec((tr, last), lambda i, c_ref: (i, 0))],
            out_specs=pl.BlockSpec((tr, last), lambda i, c_ref: (i, 0))),
        out_shape=jax.ShapeDtypeStruct((rows, last), F32),
        compiler_params=pltpu.CompilerParams(dimension_semantics=("parallel",)),
    )(c.reshape(1).astype(jnp.int32), g.reshape(2, rows, last), other.reshape(rows, last))
    return out.reshape(shape)


def _scatter_layers(ps, axes):
    n_t = len(ps)
    widths = [p.shape[a] // N_CHIPS for p, a in zip(ps, axes)]
    wins = [p.shape[:a] + (w,) + p.shape[a + 1:] for p, a, w in zip(ps, axes, widths)]

    def body(*refs):
        p_refs, out_refs = refs[:n_t], refs[n_t:2 * n_t]
        send_sems, recv_sems = refs[2 * n_t:]
        x, y, c = _place()
        cps = []
        for r in (1, 2, 3):
            ox, oy = _other_chip(x, y, r)
            for t in range(n_t):
                cp = pltpu.make_async_remote_copy(
                    src_ref=_window(p_refs[t], None, axes[t], 2 * ox + oy, widths[t]), dst_ref=out_refs[t].at[r - 1],
                    send_sem=send_sems.at[3 * t + r - 1], recv_sem=recv_sems.at[3 * t + r - 1],
                    device_id=(ox, oy, c), device_id_type=MESH)
                cp.start()
                cps.append(cp)
        for cp in cps:
            cp.wait()

    return _pcall(
        body, name="scatter_layers", in_specs=[HBM] * n_t, out_specs=[HBM] * n_t,
        out_shape=[jax.ShapeDtypeStruct((N_CHIPS - 1,) + w, p.dtype) for w, p in zip(wins, ps)],
        scratch_shapes=[pltpu.SemaphoreType.DMA((3 * n_t,)), pltpu.SemaphoreType.DMA((3 * n_t,))],
    )(*ps)


def _sum_chips(p, q, axis, me):
    win = q.shape[1:]
    C = win[-1]
    H = math.prod(win[:-1])
    if axis == p.ndim - 1:
        tr = _row_tile(H, 4 * C, ROW_BLOCK_BYTES)
        grid = H // tr
        p = p.reshape(H, N_CHIPS * C)
        p_spec = pl.BlockSpec((tr, C), lambda i, me_ref: (i, me_ref[0]))
    else:
        pre, inner = math.prod(p.shape[:axis]), math.prod(win[axis:-1])
        tr = _row_tile(inner, 4 * C, ROW_BLOCK_BYTES)
        nb = inner // tr
        grid = pre * nb
        p = p.reshape(pre, N_CHIPS, inner, C)
        p_spec = pl.BlockSpec((1, 1, tr, C), lambda i, me_ref: (i // nb, me_ref[0], i % nb, 0))

    def body(me_ref, p_ref, q0, q1, q2, q3, o_ref):
        own = p_ref[...].reshape(tr, C)
        terms = [jnp.where(me_ref[0] == chip, own, qr[0]) for chip, qr in enumerate((q0, q1, q2, q3))]
        o_ref[...] = ((terms[0] + terms[1]) + terms[2]) + terms[3]

    def q_spec(chip):
        return pl.BlockSpec(
            (1, tr, C), lambda i, me_ref: (jnp.maximum(jnp.bitwise_xor(me_ref[0], chip), 1) - 1, i, 0))

    q = q.reshape(N_CHIPS - 1, H, C)
    return _pcall(
        body, name="sum_chips",
        grid_spec=pltpu.PrefetchScalarGridSpec(
            num_scalar_prefetch=1, grid=(grid,),
            in_specs=[p_spec, q_spec(0), q_spec(1), q_spec(2), q_spec(3)],
            out_specs=pl.BlockSpec((tr, C), lambda i, me_ref: (i, 0))),
        out_shape=jax.ShapeDtypeStruct((H, C), F32),
        compiler_params=pltpu.CompilerParams(dimension_semantics=("parallel",)),
    )(me.reshape(1).astype(jnp.int32), p, q, q, q, q).reshape(win)


def _send_layers(rs):
    n_t = len(rs)

    def body(*refs):
        r_refs, out_refs = refs[:n_t], refs[n_t:2 * n_t]
        send_sems, recv_sems = refs[2 * n_t:]
        x, y, c = _place()
        cps = []
        for t in range(n_t):
            cp = pltpu.make_async_remote_copy(
                src_ref=r_refs[t], dst_ref=out_refs[t], send_sem=send_sems.at[t],
                recv_sem=recv_sems.at[t], device_id=(x, y, 1 - c), device_id_type=MESH)
            cp.start()
            cps.append(cp)
        for cp in cps:
            cp.wait()

    return _pcall(
        body, name="send_layers", in_specs=[HBM] * n_t, out_specs=[HBM] * n_t,
        out_shape=[jax.ShapeDtypeStruct(r.shape, r.dtype) for r in rs],
        scratch_shapes=[pltpu.SemaphoreType.DMA((n_t,)), pltpu.SemaphoreType.DMA((n_t,))],
    )(*rs)


def _reduce_scatter_layers(gs, axes):
    x, y, c = _place()
    others = _swap_layers(gs)
    ps = [_add_layer(g, o, c) for g, o in zip(gs, others)]
    qs = _scatter_layers(ps, [a - 1 for a in axes])
    rs = [_sum_chips(p, q, a - 1, 2 * x + y) for p, q, a in zip(ps, qs, axes)]
    theirs = _send_layers(rs)
    return [jnp.where(c == 0, jnp.stack([r, o]), jnp.stack([o, r])) for r, o in zip(rs, theirs)]


def _allgather_devices(flat):
    R, C = flat.shape

    def body(x_ref, out_ref, send_sems, recv_sems, local_sem):
        x, y, c = _place()
        me = 4 * x + 2 * y + c
        mine = pltpu.make_async_copy(x_ref, out_ref.at[me], local_sem)
        mine.start()
        cps = []
        for m in range(1, 8):
            ox, oy = _other_chip(x, y, m >> 1)
            oc = 1 - c if m & 1 else c
            cp = pltpu.make_async_remote_copy(
                src_ref=x_ref, dst_ref=out_ref.at[me], send_sem=send_sems.at[m - 1],
                recv_sem=recv_sems.at[m - 1], device_id=(ox, oy, oc), device_id_type=MESH)
            cp.start()
            cps.append(cp)
        for cp in cps:
            cp.wait()
        mine.wait()

    return _pcall(
        body, name="allgather_devices", in_specs=[HBM], out_specs=HBM,
        out_shape=jax.ShapeDtypeStruct((8, R, C), flat.dtype),
        scratch_shapes=[pltpu.SemaphoreType.DMA((7,)), pltpu.SemaphoreType.DMA((7,)),
                        pltpu.SemaphoreType.DMA],
    )(flat)


def _sum_devices(slots):
    _, R, C = slots.shape
    tr = _row_tile(R, 4 * C, ROW_BLOCK_BYTES // 4)

    def body(*refs):
        o_ref = refs[8]
        acc = refs[0][0]
        for d in range(1, 8):
            acc = acc + refs[d][0]
        o_ref[...] = acc

    def spec(d):
        return pl.BlockSpec((1, tr, C), lambda i: (d, i, 0))

    return _pcall(
        body, name="sum_devices", grid=(R // tr,), in_specs=[spec(d) for d in range(8)],
        out_specs=pl.BlockSpec((tr, C), lambda i: (i, 0)),
        out_shape=jax.ShapeDtypeStruct((R, C), F32),
        compiler_params=pltpu.CompilerParams(dimension_semantics=("parallel",)),
    )(*([slots] * 8))


def _adamw(w, g, m, v):
    shape = w.shape
    C = shape[-1]
    R = math.prod(shape[:-1])
    w, g, m, v = (t.reshape(R, C) for t in (w, g, m, v))
    tr = _row_tile(R, 4 * C, ROW_BLOCK_BYTES // 2)
    c1 = 1.0 - ADAM_B1 ** ADAM_STEP
    c2 = 1.0 - ADAM_B2 ** ADAM_STEP

    def body(w_ref, g_ref, m_ref, v_ref, d_ref, nm_ref, nv_ref):
        gg = g_ref[...]
        nm = ADAM_B1 * m_ref[...] + (1.0 - ADAM_B1) * gg
        nv = ADAM_B2 * v_ref[...] + (1.0 - ADAM_B2) * jnp.square(gg)
        d_ref[...] = -ADAM_LR * ((nm / c1) / (jnp.sqrt(nv / c2) + ADAM_EPS) + ADAM_WD * w_ref[...])
        nm_ref[...] = nm
        nv_ref[...] = nv

    spec = pl.BlockSpec((tr, C), lambda i: (i, 0))
    outs = _pcall(
        body, name="adamw", grid=(R // tr,), in_specs=[spec] * 4, out_specs=[spec] * 3,
        out_shape=[jax.ShapeDtypeStruct((R, C), F32)] * 3,
        compiler_params=pltpu.CompilerParams(dimension_semantics=("parallel",)),
    )(w, g, m, v)
    return [o.reshape(shape) for o in outs]


def _to_flat(parts):
    flat = jnp.concatenate([p.reshape(-1) for p in parts])
    unit = FLAT_COLS * FLAT_ROW_ALIGN
    pad = (-flat.shape[0]) % unit
    if pad:
        flat = jnp.concatenate([flat, jnp.zeros((pad,), flat.dtype)])
    return flat.reshape(-1, FLAT_COLS)


def _from_flat(flat, shapes):
    flat = flat.reshape(-1)
    out, off = [], 0
    for s in shapes:
        n = math.prod(s)
        out.append(flat[off:off + n].reshape(s))
        off += n
    return out


def kernel(x, meta, ln_ffn1, ffn1_wi, ffn1_wo, ln_mix, w_in, fox_bf, mla_gq, mla_wq, mla_gkv, mla_wkv, gdn_conv, gdn_alog, gdn_dtb, gdn_gon, lru_conv, lru_conv_b, lru_wa, lru_ba, lru_wx, lru_bx, lru_lam, w_gate, b_gate, w_branch, w_out, ln_ffn2, ffn2_wi, ffn2_wo, ln_final, loss_target, m_meta, m_ln_ffn1, m_ffn1_wi, m_ffn1_wo, m_ln_mix, m_w_in, m_fox_bf, m_mla_gq, m_mla_wq, m_mla_gkv, m_mla_wkv, m_gdn_conv, m_gdn_alog, m_gdn_dtb, m_gdn_gon, m_lru_conv, m_lru_conv_b, m_lru_wa, m_lru_ba, m_lru_wx, m_lru_bx, m_lru_lam, m_w_gate, m_b_gate, m_w_branch, m_w_out, m_ln_ffn2, m_ffn2_wi, m_ffn2_wo, m_ln_final, v_meta, v_ln_ffn1, v_ffn1_wi, v_ffn1_wo, v_ln_mix, v_w_in, v_fox_bf, v_mla_gq, v_mla_wq, v_mla_gkv, v_mla_wkv, v_gdn_conv, v_gdn_alog, v_gdn_dtb, v_gdn_gon, v_lru_conv, v_lru_conv_b, v_lru_wa, v_lru_ba, v_lru_wx, v_lru_bx, v_lru_lam, v_w_gate, v_b_gate, v_w_branch, v_w_out, v_ln_ffn2, v_ffn2_wi, v_ffn2_wo, v_ln_final):
    ws = (meta, ln_ffn1, ffn1_wi, ffn1_wo, ln_mix, w_in, fox_bf, mla_gq, mla_wq, mla_gkv, mla_wkv, gdn_conv, gdn_alog, gdn_dtb, gdn_gon, lru_conv, lru_conv_b, lru_wa, lru_ba, lru_wx, lru_bx, lru_lam, w_gate, b_gate, w_branch, w_out, ln_ffn2, ffn2_wi, ffn2_wo, ln_final)
    ms = (m_meta, m_ln_ffn1, m_ffn1_wi, m_ffn1_wo, m_ln_mix, m_w_in, m_fox_bf, m_mla_gq, m_mla_wq, m_mla_gkv, m_mla_wkv, m_gdn_conv, m_gdn_alog, m_gdn_dtb, m_gdn_gon, m_lru_conv, m_lru_conv_b, m_lru_wa, m_lru_ba, m_lru_wx, m_lru_bx, m_lru_lam, m_w_gate, m_b_gate, m_w_branch, m_w_out, m_ln_ffn2, m_ffn2_wi, m_ffn2_wo, m_ln_final)
    vs = (v_meta, v_ln_ffn1, v_ffn1_wi, v_ffn1_wo, v_ln_mix, v_w_in, v_fox_bf, v_mla_gq, v_mla_wq, v_mla_gkv, v_mla_wkv, v_gdn_conv, v_gdn_alog, v_gdn_dtb, v_gdn_gon, v_lru_conv, v_lru_conv_b, v_lru_wa, v_lru_ba, v_lru_wx, v_lru_bx, v_lru_lam, v_w_gate, v_b_gate, v_w_branch, v_w_out, v_ln_ffn2, v_ffn2_wi, v_ffn2_wo, v_ln_final)
    names = [n for n, _ in WEIGHT_SPECS]
    axis = dict(WEIGHT_SPECS)
    wd, md, vd = dict(zip(names, ws)), dict(zip(names, ms)), dict(zip(names, vs))
    shapes = {n: wd[n].shape for n in names}
    big = [n for n in names if n in LARGE]
    few = [n for n in names if axis[n] is not None and n not in LARGE]
    whole = [n for n in names if axis[n] is None]
    x_, y_, _ = _place()
    chip = 2 * x_ + y_

    def to_dma(n, a):
        ax, w = axis[n], shapes[n][axis[n]]
        nd = len(shapes[n])
        if (ax == nd - 1 and w % LANES) or (ax == nd - 2 and w % 16):
            parts = a.shape[ax] // w
            a = jnp.moveaxis(a.reshape(a.shape[:ax] + (parts, w) + a.shape[ax + 1:]), ax, 1)
            return a, 1
        return a, ax

    def from_dma(n, a):
        ax = axis[n]
        if a.ndim == len(shapes[n]):
            return a
        a = jnp.moveaxis(a, 1, ax)
        return a.reshape(a.shape[:ax] + (-1,) + a.shape[ax + 2:])

    dma = [to_dma(n, wd[n].astype(BF16)) for n in big]
    fulls = _gather_layers([a for a, _ in dma], [ax for _, ax in dma])
    full_bf16 = {n: from_dma(n, f) for n, f in zip(big, fulls)}
    full = {n: jnp.zeros(f.shape, F32) for n, f in full_bf16.items()}
    gathered = _gather_chips(_to_flat([wd[n] for n in few]))
    per_chip = [_from_flat(gathered[k], [shapes[n] for n in few]) for k in range(N_CHIPS)]
    for i, n in enumerate(few):
        full[n] = jnp.concatenate([per_chip[k][i] for k in range(N_CHIPS)], axis=axis[n])
    full.update({n: wd[n] for n in whole})

    loss, (gw, gx) = jax.value_and_grad(_local_loss, argnums=(0, 1))(full, x, loss_target, full_bf16)
    loss = lax.psum(loss, ("x", "y", "c"))

    dma = [to_dma(n, gw[n]) for n in big]
    reduced = _reduce_scatter_layers([a for a, _ in dma], [ax for _, ax in dma])
    grads = {n: r.reshape(shapes[n]) for n, r in zip(big, reduced)}
    rest = few + whole
    summed = _from_flat(_sum_devices(_allgather_devices(_to_flat([gw[n] for n in rest]))),
                        [gw[n].shape for n in rest])
    for n, g in zip(rest, summed):
        if axis[n] is not None:
            g = lax.dynamic_slice_in_dim(g, chip * shapes[n][axis[n]], shapes[n][axis[n]], axis=axis[n])
        grads[n] = g

    delta, new_m, new_v = {}, {}, {}
    for n in big:
        delta[n], new_m[n], new_v[n] = _adamw(wd[n], grads[n], md[n], vd[n])
    outs = _adamw(*[_to_flat([d[n] for n in rest]) for d in (wd, grads, md, vd)])
    for res, flat in zip((delta, new_m, new_v), outs):
        res.update(zip(rest, _from_flat(flat, [shapes[n] for n in rest])))

    return (loss, gx, *[grads[n] for n in names], *[delta[n] for n in names],
            *[new_m[n] for n in names], *[new_v[n] for n in names])
```

```python
import functools
import math

import jax
import jax.numpy as jnp
from jax import lax
from jax.experimental import pallas as pl
from jax.experimental.pallas import tpu as pltpu

F32 = jnp.float32
BF16 = jnp.bfloat16

N_META = 16
BLOCK = 128
PAD_LEN = BLOCK - N_META
EPS = 1e-6
NEG_INF = -1e30
N_BRANCH = 4
FOX_HEADS, FOX_DH = 4, 64
MLA_HEADS, MLA_NOPE, MLA_ROPE, MLA_DV = 4, 64, 32, 64
MLA_Q_RANK, MLA_KV_RANK = 192, 128
ROPE_BASE = 10000.0
GDN_HEADS, GDN_DK, GDN_DV, GDN_CHUNK = 4, 64, 64, 64
LRU_WIDTH, LRU_BLOCKS, LRU_C = 256, 4, 8.0
DEPTH = 2

OFF_FOX_QKV = 0
OFF_FOX_F = OFF_FOX_QKV + 3 * FOX_HEADS * FOX_DH
OFF_MLA_CQ = OFF_FOX_F + FOX_HEADS
OFF_MLA_CKV = OFF_MLA_CQ + MLA_Q_RANK
OFF_MLA_KR = OFF_MLA_CKV + MLA_KV_RANK
OFF_GDN_QKV = OFF_MLA_KR + MLA_ROPE
OFF_GDN_A = OFF_GDN_QKV + GDN_HEADS * (2 * GDN_DK + GDN_DV)
OFF_GDN_B = OFF_GDN_A + GDN_HEADS
OFF_GDN_G = OFF_GDN_B + GDN_HEADS
OFF_LRU = OFF_GDN_G + GDN_HEADS * GDN_DV
N_IN = OFF_LRU + LRU_WIDTH

ADAM_LR, ADAM_B1, ADAM_B2, ADAM_EPS, ADAM_WD, ADAM_STEP = 0.001, 0.9, 0.999, 1e-08, 0.01, 10

WEIGHT_SPECS = (
    ("meta", 1), ("ln_ffn1", None), ("ffn1_wi", 2), ("ffn1_wo", 1), ("ln_mix", None), ("w_in", 2),
    ("fox_bf", None), ("mla_gq", None), ("mla_wq", 2), ("mla_gkv", None), ("mla_wkv", 2),
    ("gdn_conv", 2), ("gdn_alog", None), ("gdn_dtb", None), ("gdn_gon", None), ("lru_conv", 2),
    ("lru_conv_b", None), ("lru_wa", None), ("lru_ba", None), ("lru_wx", None), ("lru_bx", None),
    ("lru_lam", None), ("w_gate", 2), ("b_gate", 2), ("w_branch", 3), ("w_out", 1),
    ("ln_ffn2", None), ("ffn2_wi", 2), ("ffn2_wo", 1), ("ln_final", None),
)
N_CHIPS = 4
LARGE = ("ffn1_wi", "ffn1_wo", "w_in", "w_gate", "w_branch", "w_out", "ffn2_wi", "ffn2_wo")

LANES = 128
VMEM_LIMIT_BYTES = 48 * 1024 * 1024
FLAT_COLS = 512
FLAT_ROW_ALIGN = 64


def _pcall(body, **kw):
    return pl.pallas_call(body, **kw)


def _pick(n, cands):
    for c in cands:
        if n % c == 0:
            return c
    return n


_DN = {"nn": (((1,), (0,)), ((), ())), "nt": (((1,), (1,)), ((), ())), "tn": (((0,), (0,)), ((), ()))}


MATMUL_OPERAND_TILE_BYTES = 8 * 1024 * 1024


def _k_tile(K, row_bytes, lane_axis):
    for tk in (K, 4224, 2816, 2112, 1408, 1056, 1024, 768, 704, 512, 384, 256, 128):
        aligned = tk == K or tk % LANES == 0 or (not lane_axis and tk % 16 == 0)
        if tk <= K and K % tk == 0 and aligned and tk * row_bytes <= MATMUL_OPERAND_TILE_BYTES:
            return tk
    return K


def _matmul(a, b, mode, name, out_dtype=F32):
    if mode == "nn":
        (M, K), (_, N) = a.shape, b.shape
    elif mode == "nt":
        (M, K), (N, _) = a.shape, b.shape
    else:
        (K, M), (_, N) = a.shape, b.shape
    tm = _pick(M, (768, 512, 1408, 384, 256, 128, 64, 32, 16, 8))
    tn = _pick(N, (512, 1408, 256, 128))
    tk = _k_tile(K, tm * a.dtype.itemsize + tn * b.dtype.itemsize, mode != "tn")
    nk = K // tk
    a_spec = {"nn": pl.BlockSpec((tm, tk), lambda i, j, k: (i, k)),
              "nt": pl.BlockSpec((tm, tk), lambda i, j, k: (i, k)),
              "tn": pl.BlockSpec((tk, tm), lambda i, j, k: (k, i))}[mode]
    b_spec = {"nn": pl.BlockSpec((tk, tn), lambda i, j, k: (k, j)),
              "nt": pl.BlockSpec((tn, tk), lambda i, j, k: (j, k)),
              "tn": pl.BlockSpec((tk, tn), lambda i, j, k: (k, j))}[mode]
    dn = _DN[mode]

    def body(a_ref, b_ref, o_ref, acc_ref):
        k = pl.program_id(2)
        part = lax.dot_general(a_ref[...].astype(BF16), b_ref[...].astype(BF16), dn,
                               preferred_element_type=F32)

        if nk == 1:
            o_ref[...] = part.astype(o_ref.dtype)
        else:
            @pl.when(k == 0)
            def _():
                acc_ref[...] = part

            @pl.when((k > 0) & (k < nk - 1))
            def _():
                acc_ref[...] += part

            @pl.when(k == nk - 1)
            def _():
                o_ref[...] = (acc_ref[...] + part).astype(o_ref.dtype)

    return _pcall(
        body, name=name, grid=(M // tm, N // tn, nk),
        in_specs=[a_spec, b_spec], out_specs=pl.BlockSpec((tm, tn), lambda i, j, k: (i, j)),
        out_shape=jax.ShapeDtypeStruct((M, N), out_dtype),
        scratch_shapes=[pltpu.VMEM((tm, tn) if nk > 1 else (8, LANES), F32)],
        compiler_params=pltpu.CompilerParams(
            dimension_semantics=("parallel", "parallel", "arbitrary"),
            vmem_limit_bytes=VMEM_LIMIT_BYTES),
    )(a, b)


@functools.partial(jax.custom_vjp, nondiff_argnums=(3,))
def dense(a, wb, w, out_dtype):
    return _matmul(a, wb, "nn", "dense_fwd", out_dtype)


def _dense_fwd(a, wb, w, out_dtype):
    return _matmul(a, wb, "nn", "dense_fwd", out_dtype), (a, wb)


def _dense_bwd(out_dtype, res, g):
    a, wb = res
    return (_matmul(g, wb, "nt", "dense_dgrad", a.dtype), jnp.zeros_like(wb),
            _matmul(a, g, "tn", "dense_wgrad", F32))


dense.defvjp(_dense_fwd, _dense_bwd)


def dense_nd(a, wb, w, out_dtype=F32, keep_pad=False):
    K, N = wb.shape
    pad = (-N) % (4 * LANES if N > 4 * LANES else LANES)
    if pad:
        wb = jnp.pad(wb, ((0, 0), (0, pad)))
        w = jnp.pad(w, ((0, 0), (0, pad)))
    out = dense(a.reshape(-1, K), wb, w, out_dtype)
    if keep_pad:
        return out
    if pad:
        out = out[:, :N]
    return out.reshape(a.shape[:-1] + (N,))


@functools.partial(jax.custom_vjp, nondiff_argnums=(1,))
def _split_cols(x, bounds):
    return tuple(x[:, a:b] for a, b in bounds)


def _split_cols_fwd(x, bounds):
    return _split_cols(x, bounds), jnp.zeros((x.shape[0], x.shape[1] - bounds[-1][1]), x.dtype)


def _split_cols_bwd(bounds, rest, cts):
    return (jnp.concatenate(list(cts) + ([rest] if rest.shape[1] else []), axis=1),)


_split_cols.defvjp(_split_cols_fwd, _split_cols_bwd)


class _Cols:
    def __init__(self, ranges, shape):
        self.ranges, self.shape = ranges, shape

    def __getitem__(self, idx):
        return self.ranges[(idx[-1].start, idx[-1].stop)]


_NT = (((1,), (1,)), ((), ()))


def _att_blk(T):
    return _pick(T, (384, 256, 128))


def _causal_pairs(n, by_query):
    if by_query:
        pairs = [(i, j) for i in range(n) for j in range(i + 1)]
    else:
        pairs = [(i, j) for j in range(n) for i in range(j, n)]
    return (jnp.array([p[0] for p in pairs], jnp.int32), jnp.array([p[1] for p in pairs], jnp.int32))


def _attn_fwd_call(q, k, v, cum_col, cum_row, scale):
    BH, T, dk = q.shape
    dv = v.shape[2]
    blk = _att_blk(T)
    n = T // blk
    has_cum = cum_col is not None

    def body(i_tab, j_tab, *refs):
        if has_cum:
            q_ref, k_ref, v_ref, cq_ref, ck_ref, o_ref, lse_ref, m_s, l_s, acc_s = refs
        else:
            q_ref, k_ref, v_ref, o_ref, lse_ref, m_s, l_s, acc_s = refs
        i = i_tab[pl.program_id(1)]
        j = j_tab[pl.program_id(1)]

        @pl.when(j == 0)
        def _():
            m_s[...] = jnp.full_like(m_s, NEG_INF)
            l_s[...] = jnp.zeros_like(l_s)
            acc_s[...] = jnp.zeros_like(acc_s)

        def step(masked):
            s = lax.dot_general(q_ref[0].astype(BF16), k_ref[0].astype(BF16), _NT,
                                preferred_element_type=F32) * scale
            if has_cum:
                s = s + cq_ref[0] - ck_ref[0]
            if masked:
                qpos = i * blk + lax.broadcasted_iota(jnp.int32, (blk, blk), 0)
                kpos = j * blk + lax.broadcasted_iota(jnp.int32, (blk, blk), 1)
                mask = (kpos <= qpos) & (kpos >= PAD_LEN)
                s = jnp.where(mask, s, NEG_INF)
            m_prev = m_s[...]
            m_new = jnp.maximum(m_prev, jnp.max(s, axis=1, keepdims=True))
            p = jnp.exp(s - m_new)
            alpha = jnp.exp(m_prev - m_new)
            l_s[...] = alpha * l_s[...] + jnp.sum(p, axis=1, keepdims=True)
            acc_s[...] = alpha * acc_s[...] + jnp.dot(p.astype(BF16), v_ref[0].astype(BF16),
                                                      preferred_element_type=F32)
            m_s[...] = m_new

        @pl.when((j == i) | (j == 0))
        def _():
            step(True)

        @pl.when((j < i) & (j > 0))
        def _():
            step(False)

        @pl.when(j == i)
        def _():
            o_ref[0] = acc_s[...] / l_s[...]
            lse_ref[0] = m_s[...] + jnp.log(l_s[...])

    q_idx = lambda b, t, it, jt: (b, it[t], 0)
    kv_idx = lambda b, t, it, jt: (b, jt[t], 0)
    in_specs = [pl.BlockSpec((1, blk, dk), q_idx), pl.BlockSpec((1, blk, dk), kv_idx),
                pl.BlockSpec((1, blk, dv), kv_idx)]
    args = [q, k, v]
    if has_cum:
        in_specs += [pl.BlockSpec((1, blk, 1), q_idx),
                     pl.BlockSpec((1, 1, blk), lambda b, t, it, jt: (b, 0, jt[t]))]
        args += [cum_col, cum_row]
    i_tab, j_tab = _causal_pairs(n, by_query=True)
    return _pcall(
        body, name="attn_fwd_cum" if has_cum else "attn_fwd",
        grid_spec=pltpu.PrefetchScalarGridSpec(
            num_scalar_prefetch=2, grid=(BH, len(i_tab)), in_specs=in_specs,
            out_specs=[pl.BlockSpec((1, blk, dv), q_idx), pl.BlockSpec((1, blk, 1), q_idx)],
            scratch_shapes=[pltpu.VMEM((blk, 1), F32), pltpu.VMEM((blk, 1), F32), pltpu.VMEM((blk, dv), F32)]),
        out_shape=[jax.ShapeDtypeStruct((BH, T, dv), F32), jax.ShapeDtypeStruct((BH, T, 1), F32)],
        compiler_params=pltpu.CompilerParams(
            dimension_semantics=("parallel", "arbitrary"), vmem_limit_bytes=VMEM_LIMIT_BYTES),
    )(i_tab, j_tab, *args)


def _attn_dkv_call(q, k, v, do, lse_row, delta_row, cum_row, cum_col, scale):
    BH, T, dk = q.shape
    dv = v.shape[2]
    blk = _att_blk(T)
    n = T // blk
    has_cum = cum_row is not None

    def body(i_tab, j_tab, *refs):
        if has_cum:
            (k_ref, v_ref, q_ref, do_ref, lse_ref, dl_ref, cq_ref, ck_ref,
             dk_ref, dv_ref, dc_ref, dk_s, dv_s, dc_s) = refs
        else:
            k_ref, v_ref, q_ref, do_ref, lse_ref, dl_ref, dk_ref, dv_ref, dk_s, dv_s = refs
        i = i_tab[pl.program_id(1)]
        j = j_tab[pl.program_id(1)]

        @pl.when(i == j)
        def _():
            dk_s[...] = jnp.zeros_like(dk_s)
            dv_s[...] = jnp.zeros_like(dv_s)
            if has_cum:
                dc_s[...] = jnp.zeros_like(dc_s)

        def step(masked):
            kb = k_ref[0].astype(BF16)
            qb = q_ref[0].astype(BF16)
            dob = do_ref[0].astype(BF16)
            st = lax.dot_general(kb, qb, _NT, preferred_element_type=F32) * scale
            if has_cum:
                st = st + cq_ref[0] - ck_ref[0]
            pt = jnp.exp(jnp.minimum(st - lse_ref[0], 0.0))
            if masked:
                kpos = j * blk + lax.broadcasted_iota(jnp.int32, (blk, blk), 0)
                qpos = i * blk + lax.broadcasted_iota(jnp.int32, (blk, blk), 1)
                pt = jnp.where((kpos <= qpos) & (kpos >= PAD_LEN), pt, 0.0)
            dv_s[...] += jnp.dot(pt.astype(BF16), dob, preferred_element_type=F32)
            dpt = lax.dot_general(v_ref[0].astype(BF16), dob, _NT, preferred_element_type=F32)
            dst = pt * (dpt - dl_ref[0])
            dk_s[...] += jnp.dot(dst.astype(BF16), qb, preferred_element_type=F32) * scale
            if has_cum:
                dc_s[...] -= jnp.sum(dst, axis=1, keepdims=True)

        @pl.when((i == j) | (j == 0))
        def _():
            step(True)

        @pl.when((i > j) & (j > 0))
        def _():
            step(False)

        @pl.when(i == n - 1)
        def _():
            dk_ref[0] = dk_s[...]
            dv_ref[0] = dv_s[...]
            if has_cum:
                dc_ref[0] = dc_s[...]

    k_idx = lambda b, t, it, jt: (b, jt[t], 0)
    q_idx = lambda b, t, it, jt: (b, it[t], 0)
    row_idx = lambda b, t, it, jt: (b, 0, it[t])
    in_specs = [pl.BlockSpec((1, blk, dk), k_idx), pl.BlockSpec((1, blk, dv), k_idx),
                pl.BlockSpec((1, blk, dk), q_idx), pl.BlockSpec((1, blk, dv), q_idx),
                pl.BlockSpec((1, 1, blk), row_idx), pl.BlockSpec((1, 1, blk), row_idx)]
    args = [k, v, q, do, lse_row, delta_row]
    out_specs = [pl.BlockSpec((1, blk, dk), k_idx), pl.BlockSpec((1, blk, dv), k_idx)]
    out_shape = [jax.ShapeDtypeStruct((BH, T, dk), F32), jax.ShapeDtypeStruct((BH, T, dv), F32)]
    scratch = [pltpu.VMEM((blk, dk), F32), pltpu.VMEM((blk, dv), F32)]
    if has_cum:
        in_specs += [pl.BlockSpec((1, 1, blk), row_idx), pl.BlockSpec((1, blk, 1), k_idx)]
        args += [cum_row, cum_col]
        out_specs.append(pl.BlockSpec((1, blk, 1), k_idx))
        out_shape.append(jax.ShapeDtypeStruct((BH, T, 1), F32))
        scratch.append(pltpu.VMEM((blk, 1), F32))
    i_tab, j_tab = _causal_pairs(n, by_query=False)
    return _pcall(
        body, name="attn_dkv_cum" if has_cum else "attn_dkv",
        grid_spec=pltpu.PrefetchScalarGridSpec(
            num_scalar_prefetch=2, grid=(BH, len(i_tab)), in_specs=in_specs, out_specs=out_specs,
            scratch_shapes=scratch),
        out_shape=out_shape,
        compiler_params=pltpu.CompilerParams(
            dimension_semantics=("parallel", "arbitrary"), vmem_limit_bytes=VMEM_LIMIT_BYTES),
    )(i_tab, j_tab, *args)


def _attn_dq_call(q, k, v, do, lse_col, delta_col, cum_col, cum_row, scale):
    BH, T, dk = q.shape
    dv = v.shape[2]
    blk = _att_blk(T)
    n = T // blk
    has_cum = cum_col is not None

    def body(i_tab, j_tab, *refs):
        if has_cum:
            q_ref, k_ref, v_ref, do_ref, lse_ref, dl_ref, cq_ref, ck_ref, dq_ref, dc_ref, dq_s, dc_s = refs
        else:
            q_ref, k_ref, v_ref, do_ref, lse_ref, dl_ref, dq_ref, dq_s = refs
        i = i_tab[pl.program_id(1)]
        j = j_tab[pl.program_id(1)]

        @pl.when(j == 0)
        def _():
            dq_s[...] = jnp.zeros_like(dq_s)
            if has_cum:
                dc_s[...] = jnp.zeros_like(dc_s)

        def step(masked):
            kb = k_ref[0].astype(BF16)
            s = lax.dot_general(q_ref[0].astype(BF16), kb, _NT, preferred_element_type=F32) * scale
            if has_cum:
                s = s + cq_ref[0] - ck_ref[0]
            p = jnp.exp(jnp.minimum(s - lse_ref[0], 0.0))
            if masked:
                qpos = i * blk + lax.broadcasted_iota(jnp.int32, (blk, blk), 0)
                kpos = j * blk + lax.broadcasted_iota(jnp.int32, (blk, blk), 1)
                p = jnp.where((kpos <= qpos) & (kpos >= PAD_LEN), p, 0.0)
            dp = lax.dot_general(do_ref[0].astype(BF16), v_ref[0].astype(BF16), _NT,
                                 preferred_element_type=F32)
            ds = p * (dp - dl_ref[0])
            dq_s[...] += jnp.dot(ds.astype(BF16), kb, preferred_element_type=F32) * scale
            if has_cum:
                dc_s[...] += jnp.sum(ds, axis=1, keepdims=True)

        @pl.when((j == i) | (j == 0))
        def _():
            step(True)

        @pl.when((j < i) & (j > 0))
        def _():
            step(False)

        @pl.when(j == i)
        def _():
            dq_ref[0] = dq_s[...]
            if has_cum:
                dc_ref[0] = dc_s[...]

    kv_idx = lambda b, t, it, jt: (b, jt[t], 0)
    q_idx = lambda b, t, it, jt: (b, it[t], 0)
    in_specs = [pl.BlockSpec((1, blk, dk), q_idx), pl.BlockSpec((1, blk, dk), kv_idx),
                pl.BlockSpec((1, blk, dv), kv_idx), pl.BlockSpec((1, blk, dv), q_idx),
                pl.BlockSpec((1, blk, 1), q_idx), pl.BlockSpec((1, blk, 1), q_idx)]
    args = [q, k, v, do, lse_col, delta_col]
    if has_cum:
        in_specs += [pl.BlockSpec((1, blk, 1), q_idx),
                     pl.BlockSpec((1, 1, blk), lambda b, t, it, jt: (b, 0, jt[t]))]
        args += [cum_col, cum_row]
    out_specs = [pl.BlockSpec((1, blk, dk), q_idx)]
    out_shape = [jax.ShapeDtypeStruct((BH, T, dk), F32)]
    scratch = [pltpu.VMEM((blk, dk), F32)]
    if has_cum:
        out_specs.append(pl.BlockSpec((1, blk, 1), q_idx))
        out_shape.append(jax.ShapeDtypeStruct((BH, T, 1), F32))
        scratch.append(pltpu.VMEM((blk, 1), F32))
    i_tab, j_tab = _causal_pairs(n, by_query=True)
    return _pcall(
        body, name="attn_dq_cum" if has_cum else "attn_dq",
        grid_spec=pltpu.PrefetchScalarGridSpec(
            num_scalar_prefetch=2, grid=(BH, len(i_tab)), in_specs=in_specs, out_specs=out_specs,
            scratch_shapes=scratch),
        out_shape=out_shape,
        compiler_params=pltpu.CompilerParams(
            dimension_semantics=("parallel", "arbitrary"), vmem_limit_bytes=VMEM_LIMIT_BYTES),
    )(i_tab, j_tab, *args)


def _make_attention(scale, has_cum):
    def fold(t):
        return t.reshape((-1,) + t.shape[2:])

    def run_fwd(q, k, v, cum):
        B, H, T, _ = q.shape
        col = cum.reshape(B * H, T, 1) if has_cum else None
        row = cum.reshape(B * H, 1, T) if has_cum else None
        o, lse = _attn_fwd_call(fold(q), fold(k), fold(v), col, row, scale)
        return o.reshape(B, H, T, -1), lse

    @jax.custom_vjp
    def attn(q, k, v, cum):
        return run_fwd(q, k, v, cum)[0]

    def attn_fwd(q, k, v, cum):
        o, lse = run_fwd(q, k, v, cum)
        return o, (q, k, v, cum, o, lse)

    def attn_bwd(res, do):
        q, k, v, cum, o, lse = res
        B, H, T, _ = q.shape
        delta = jnp.sum(do * o, axis=-1).reshape(B * H, T, 1)
        col = cum.reshape(B * H, T, 1) if has_cum else None
        row = cum.reshape(B * H, 1, T) if has_cum else None
        qf, kf, vf, dof = fold(q), fold(k), fold(v), fold(do)
        outs = _attn_dkv_call(qf, kf, vf, dof, lse.reshape(B * H, 1, T), delta.reshape(B * H, 1, T),
                              row, col, scale)
        dqs = _attn_dq_call(qf, kf, vf, dof, lse, delta, col, row, scale)
        dcum = (outs[2] + dqs[1]).reshape(B, H, T) if has_cum else jnp.zeros_like(cum)
        return (dqs[0].reshape(q.shape), outs[0].reshape(k.shape), outs[1].reshape(v.shape), dcum)

    attn.defvjp(attn_fwd, attn_bwd)
    return attn


def _rmsnorm(x, g):
    return x * lax.rsqrt(jnp.mean(x * x, axis=-1, keepdims=True) + EPS) * g


def _l2norm(x):
    return x * lax.rsqrt(jnp.sum(x * x, axis=-1, keepdims=True) + EPS)


def _row_tile(rows, row_bytes, budget):
    for t in (2048, 1024, 512, 256, 128, 64, 32, 16, 8):
        if rows % t == 0 and t * row_bytes <= budget:
            return t
    return rows


ROW_BLOCK_BYTES = 2 * 1024 * 1024


def _rms_fwd_call(x, g):
    M, D = x.shape
    tr = _row_tile(M, 4 * D, ROW_BLOCK_BYTES)

    def body(x_ref, g_ref, y_ref, r_ref):
        xv = x_ref[...]
        r = lax.rsqrt(jnp.mean(xv * xv, axis=-1, keepdims=True) + EPS)
        y_ref[...] = xv * r * g_ref[...]
        r_ref[...] = r

    return _pcall(
        body, name="rmsnorm_fwd", grid=(M // tr,),
        in_specs=[pl.BlockSpec((tr, D), lambda i: (i, 0)), pl.BlockSpec((1, D), lambda i: (0, 0))],
        out_specs=[pl.BlockSpec((tr, D), lambda i: (i, 0)), pl.BlockSpec((tr, 1), lambda i: (i, 0))],
        out_shape=[jax.ShapeDtypeStruct((M, D), F32), jax.ShapeDtypeStruct((M, 1), F32)],
        compiler_params=pltpu.CompilerParams(dimension_semantics=("parallel",)),
    )(x, g)


def _rms_bwd_call(x, g, r, dy):
    M, D = x.shape
    tr = _row_tile(M, 4 * D, ROW_BLOCK_BYTES)

    def body(x_ref, g_ref, r_ref, dy_ref, dx_ref, dg_ref):
        i = pl.program_id(0)
        rv = r_ref[...]
        xh = x_ref[...] * rv
        dyv = dy_ref[...]
        dyg = dyv * g_ref[...]
        dx_ref[...] = rv * (dyg - xh * jnp.mean(dyg * xh, axis=-1, keepdims=True))
        part = jnp.sum(dyv * xh, axis=0, keepdims=True)

        @pl.when(i == 0)
        def _():
            dg_ref[...] = part

        @pl.when(i > 0)
        def _():
            dg_ref[...] += part

    row = pl.BlockSpec((tr, D), lambda i: (i, 0))
    return _pcall(
        body, name="rmsnorm_bwd", grid=(M // tr,),
        in_specs=[row, pl.BlockSpec((1, D), lambda i: (0, 0)), pl.BlockSpec((tr, 1), lambda i: (i, 0)), row],
        out_specs=[row, pl.BlockSpec((1, D), lambda i: (0, 0))],
        out_shape=[jax.ShapeDtypeStruct((M, D), F32), jax.ShapeDtypeStruct((1, D), F32)],
        compiler_params=pltpu.CompilerParams(dimension_semantics=("arbitrary",)),
    )(x, g, r, dy)


@jax.custom_vjp
def _rmsnorm2d(x, g):
    return _rms_fwd_call(x, g)[0]


def _rmsnorm2d_fwd(x, g):
    y, r = _rms_fwd_call(x, g)
    return y, (x, g, r)


def _rmsnorm2d_bwd(res, dy):
    x, g, r = res
    return _rms_bwd_call(x, g, r, dy)


_rmsnorm2d.defvjp(_rmsnorm2d_fwd, _rmsnorm2d_bwd)


def _rmsnorm_rows(x, g):
    D = x.shape[-1]
    return _rmsnorm2d(x.reshape(-1, D), g.reshape(1, D)).reshape(x.shape)


def _glu_fwd_call(gu):
    M, F2 = gu.shape
    F = F2 // 2
    tr = _row_tile(M, 4 * F2, 2 * ROW_BLOCK_BYTES)

    def body(g_ref, u_ref, o_ref):
        gv = g_ref[...].astype(F32)
        o_ref[...] = (gv * jax.nn.sigmoid(gv) * u_ref[...].astype(F32)).astype(o_ref.dtype)

    return _pcall(
        body, name="swiglu_fwd", grid=(M // tr,),
        in_specs=[pl.BlockSpec((tr, F), lambda i: (i, 0)), pl.BlockSpec((tr, F), lambda i: (i, 1))],
        out_specs=pl.BlockSpec((tr, F), lambda i: (i, 0)),
        out_shape=jax.ShapeDtypeStruct((M, F), gu.dtype),
        compiler_params=pltpu.CompilerParams(dimension_semantics=("parallel",),
                                             vmem_limit_bytes=VMEM_LIMIT_BYTES),
    )(gu, gu)


def _glu_bwd_call(gu, da):
    M, F2 = gu.shape
    F = F2 // 2
    tr = _row_tile(M, 4 * F2, 2 * ROW_BLOCK_BYTES)

    def body(g_ref, u_ref, da_ref, o_ref):
        gv = g_ref[...].astype(F32)
        s = jax.nn.sigmoid(gv)
        dav = da_ref[...].astype(F32)
        o_ref[:, :F] = (dav * u_ref[...].astype(F32) * (s * (1.0 + gv * (1.0 - s)))).astype(o_ref.dtype)
        o_ref[:, F:] = (dav * (gv * s)).astype(o_ref.dtype)

    return _pcall(
        body, name="swiglu_bwd", grid=(M // tr,),
        in_specs=[pl.BlockSpec((tr, F), lambda i: (i, 0)), pl.BlockSpec((tr, F), lambda i: (i, 1)),
                  pl.BlockSpec((tr, F), lambda i: (i, 0))],
        out_specs=pl.BlockSpec((tr, F2), lambda i: (i, 0)),
        out_shape=jax.ShapeDtypeStruct((M, F2), gu.dtype),
        compiler_params=pltpu.CompilerParams(dimension_semantics=("parallel",),
                                             vmem_limit_bytes=VMEM_LIMIT_BYTES),
    )(gu, gu, da)


@jax.custom_vjp
def _glu(gu):
    return _glu_fwd_call(gu)


def _glu_fwd(gu):
    return _glu_fwd_call(gu), gu


def _glu_bwd(gu, da):
    return (_glu_bwd_call(gu, da),)


_glu.defvjp(_glu_fwd, _glu_bwd)


def _swiglu(h, wi, wo):
    lead, D = h.shape[:-1], h.shape[-1]
    gu = dense(h.reshape(-1, D), wi[0], wi[1], BF16)
    return dense(_glu(gu), wo[0], wo[1], F32).reshape(lead + (wo[0].shape[1],))


def _causal_dwconv(x, w):
    K, C = w.shape
    return lax.conv_general_dilated(
        x, w[:, None, :], window_strides=(1,), padding=[(K - 1, 0)],
        dimension_numbers=("NWC", "WIO", "NWC"), feature_group_count=C)


def _rope(x, cos, sin):
    half = x.shape[-1] // 2
    x1, x2 = x[..., :half], x[..., half:]
    return jnp.concatenate([x1 * cos - x2 * sin, x2 * cos + x1 * sin], axis=-1)


def _fox_branch(p, b_f):
    B, T, _ = p.shape
    qkv = p[..., OFF_FOX_QKV:OFF_FOX_F].reshape(B, T, 3, FOX_HEADS, FOX_DH)
    q = qkv[:, :, 0].transpose(0, 2, 1, 3)
    k = qkv[:, :, 1].transpose(0, 2, 1, 3)
    v = qkv[:, :, 2].transpose(0, 2, 1, 3)
    log_f = jax.nn.log_sigmoid(p[..., OFF_FOX_F:OFF_MLA_CQ] + b_f)
    cum = jnp.cumsum(log_f, axis=1).transpose(0, 2, 1)
    o = _make_attention(FOX_DH ** -0.5, True)(q, k, v, cum)
    return o.transpose(0, 2, 1, 3).reshape(B, T, FOX_HEADS * FOX_DH)


def _mla_branch(p, g_qn, w_q_up, g_kvn, w_kv_up, cos, sin):
    B, T, _ = p.shape
    cq = _rmsnorm(p[..., OFF_MLA_CQ:OFF_MLA_CKV], g_qn)
    q = (cq @ w_q_up).reshape(B, T, MLA_HEADS, MLA_NOPE + MLA_ROPE)
    ckv = _rmsnorm(p[..., OFF_MLA_CKV:OFF_MLA_KR], g_kvn)
    kv = (ckv @ w_kv_up).reshape(B, T, MLA_HEADS, MLA_NOPE + MLA_DV)
    k_rope = _rope(p[..., OFF_MLA_KR:OFF_GDN_QKV], cos, sin)
    q_rope = _rope(q[..., MLA_NOPE:], cos[:, None], sin[:, None])
    q = jnp.concatenate([q[..., :MLA_NOPE], q_rope], axis=-1)
    k = jnp.concatenate([kv[..., :MLA_NOPE],
                         jnp.broadcast_to(k_rope[:, :, None], (B, T, MLA_HEADS, MLA_ROPE))], axis=-1)
    v = kv[..., MLA_NOPE:]
    o = _make_attention((MLA_NOPE + MLA_ROPE) ** -0.5, False)(
        q.transpose(0, 2, 1, 3), k.transpose(0, 2, 1, 3), v.transpose(0, 2, 1, 3),
        jnp.zeros((B, MLA_HEADS, T), F32))
    return o.transpose(0, 2, 1, 3).reshape(B, T, MLA_HEADS * MLA_DV)


def _bmm(a, b):
    return lax.dot_general(a.astype(BF16), b.astype(BF16), (((2,), (1,)), ((0,), (0,))),
                           preferred_element_type=F32)


def _bmm_nt(a, b):
    return lax.dot_general(a.astype(BF16), b.astype(BF16), (((2,), (2,)), ((0,), (0,))),
                           preferred_element_type=F32)


def _chunk_spec(shape, index):
    return pl.BlockSpec((1,) + tuple(shape[1:]), lambda n: (index(n), 0, 0, 0))


def _gdn_scan_fwd_call(qd, kdt, w, u, qk, gl):
    nc, BH, C, DK = qd.shape
    DV = u.shape[3]

    def body(qd_ref, kdt_ref, w_ref, u_ref, qk_ref, gl_ref, o_ref, s_ref, vn_ref, state):
        @pl.when(pl.program_id(0) == 0)
        def _():
            state[...] = jnp.zeros_like(state)

        s = state[...]
        s_ref[0] = s
        vn = u_ref[0] - _bmm(w_ref[0], s)
        vn_ref[0] = vn
        o_ref[0] = _bmm(qd_ref[0], s) + _bmm(qk_ref[0], vn)
        state[...] = s * gl_ref[0] + _bmm(kdt_ref[0], vn)

    fwd = lambda n: n
    outs = [(nc, BH, C, DV), (nc, BH, DK, DV), (nc, BH, C, DV)]
    return _pcall(
        body, name="gdn_scan_fwd", grid=(nc,),
        in_specs=[_chunk_spec(t.shape, fwd) for t in (qd, kdt, w, u, qk, gl)],
        out_specs=[_chunk_spec(s, fwd) for s in outs],
        out_shape=[jax.ShapeDtypeStruct(s, F32) for s in outs],
        scratch_shapes=[pltpu.VMEM((BH, DK, DV), F32)],
        compiler_params=pltpu.CompilerParams(dimension_semantics=("arbitrary",)),
    )(qd, kdt, w, u, qk, gl)


def _gdn_scan_bwd_call(do, qdt, kd, wt, qkt, gl, s_all, vn):
    nc, BH, C, DV = do.shape
    DK = kd.shape[3]

    def body(do_ref, qdt_ref, kd_ref, wt_ref, qkt_ref, gl_ref, s_ref, vn_ref,
             dqd_ref, dkd_ref, dw_ref, du_ref, dqk_ref, dgl_ref, dstate):
        @pl.when(pl.program_id(0) == 0)
        def _():
            dstate[...] = jnp.zeros_like(dstate)

        ds = dstate[...]
        s, v, dov = s_ref[0], vn_ref[0], do_ref[0]
        dkd_ref[0] = _bmm_nt(v, ds)
        dgl_ref[0] = s * ds
        dqd_ref[0] = _bmm_nt(dov, s)
        dqk_ref[0] = _bmm_nt(dov, v)
        dv = _bmm(kd_ref[0], ds) + _bmm(qkt_ref[0], dov)
        du_ref[0] = dv
        dw_ref[0] = -_bmm_nt(dv, s)
        dstate[...] = ds * gl_ref[0] + _bmm(qdt_ref[0], dov) - _bmm(wt_ref[0], dv)

    rev = lambda n: nc - 1 - n
    outs = [(nc, BH, C, DK)] * 3 + [(nc, BH, C, DV), (nc, BH, C, C), (nc, BH, DK, DV)]
    return _pcall(
        body, name="gdn_scan_bwd", grid=(nc,),
        in_specs=[_chunk_spec(t.shape, rev) for t in (do, qdt, kd, wt, qkt, gl, s_all, vn)],
        out_specs=[_chunk_spec(s, rev) for s in outs],
        out_shape=[jax.ShapeDtypeStruct(s, F32) for s in outs],
        scratch_shapes=[pltpu.VMEM((BH, DK, DV), F32)],
        compiler_params=pltpu.CompilerParams(dimension_semantics=("arbitrary",)),
    )(do, qdt, kd, wt, qkt, gl, s_all, vn)


def _gl_rows(gl, dv):
    return jnp.broadcast_to(gl[:, :, None, None], gl.shape + (1, dv))


@jax.custom_vjp
def _gdn_scan(qd, kd, w, u, qk, gl):
    return _gdn_scan_fwd_call(qd, jnp.swapaxes(kd, 2, 3), w, u, qk, _gl_rows(gl, u.shape[3]))[0]


def _gdn_scan_fwd(qd, kd, w, u, qk, gl):
    o, s_all, vn = _gdn_scan_fwd_call(qd, jnp.swapaxes(kd, 2, 3), w, u, qk, _gl_rows(gl, u.shape[3]))
    return o, (qd, kd, w, qk, gl, s_all, vn)


def _gdn_scan_bwd(res, do):
    qd, kd, w, qk, gl, s_all, vn = res
    dqd, dkd, dw, du, dqk, dgl = _gdn_scan_bwd_call(
        do, jnp.swapaxes(qd, 2, 3), kd, jnp.swapaxes(w, 2, 3), jnp.swapaxes(qk, 2, 3),
        _gl_rows(gl, do.shape[3]), s_all, vn)
    return dqd, dkd, dw, du, dqk, jnp.sum(dgl, axis=(2, 3))


_gdn_scan.defvjp(_gdn_scan_fwd, _gdn_scan_bwd)


def _gdn_branch(p, conv_w, a_log, dt_bias, g_on):
    B, T, _ = p.shape
    H, DK, DV, C = GDN_HEADS, GDN_DK, GDN_DV, GDN_CHUNK
    qkv = jax.nn.silu(_causal_dwconv(p[..., OFF_GDN_QKV:OFF_GDN_A], conv_w))
    q = _l2norm(qkv[..., :H * DK].reshape(B, T, H, DK)) * DK ** -0.5
    k = _l2norm(qkv[..., H * DK:2 * H * DK].reshape(B, T, H, DK))
    v = qkv[..., 2 * H * DK:].reshape(B, T, H, DV)
    beta = jax.nn.sigmoid(p[..., OFF_GDN_B:OFF_GDN_G])
    g = -jnp.exp(a_log) * jax.nn.softplus(p[..., OFF_GDN_A:OFF_GDN_B] + dt_bias)
    nc = T // C

    def chunks(t):
        return jnp.moveaxis(t, 2, 1).reshape((B, H, nc, C) + t.shape[3:])

    q, k, v, beta, g = chunks(q), chunks(k), chunks(v), chunks(beta), chunks(g)
    G = jnp.cumsum(g, axis=-1)
    idx = jnp.arange(C)
    strict = idx[:, None] > idx[None, :]
    incl = idx[:, None] >= idx[None, :]
    decay = jnp.exp(jnp.where(incl, G[..., :, None] - G[..., None, :], NEG_INF))
    kb = k * beta[..., None]
    vb = v * beta[..., None]
    m = jnp.eye(C, dtype=F32) + jnp.where(
        strict, jnp.einsum("bhnik,bhnjk->bhnij", kb, k) * decay, 0.0)
    rhs = jnp.concatenate([kb * jnp.exp(G)[..., None], vb], axis=-1)
    sol = lax.linalg.triangular_solve(m, rhs, left_side=True, lower=True, unit_diagonal=True)
    w, u = sol[..., :DK], sol[..., DK:]
    qk = jnp.where(incl, jnp.einsum("bhnik,bhnjk->bhnij", q, k) * decay, 0.0)
    q_dec = q * jnp.exp(G)[..., None]
    k_dec = k * jnp.exp(G[..., -1:] - G)[..., None]
    g_last = jnp.exp(G[..., -1])
    def chunk_major(t):
        return jnp.moveaxis(t, 2, 0).reshape((nc, B * H) + t.shape[3:])

    o = _gdn_scan(chunk_major(q_dec), chunk_major(k_dec), chunk_major(w), chunk_major(u),
                  chunk_major(qk), chunk_major(g_last))
    o = jnp.moveaxis(o.reshape(nc, B, H, C, DV), 0, 2).reshape(B, H, T, DV).transpose(0, 2, 1, 3)
    gate = jax.nn.silu(p[..., OFF_GDN_G:OFF_LRU]).reshape(B, T, H, DV)
    o = _rmsnorm(o, g_on) * gate
    return o.reshape(B, T, H * DV)


def _rglru_branch(p, valid, conv_w, conv_b, w_a, b_a, w_x, b_x, lam):
    B, T, _ = p.shape
    xr = _causal_dwconv(p[..., OFF_LRU:N_IN], conv_w) + conv_b
    xr = jnp.where(valid[None, :, None], xr, 0)
    xb = xr.reshape(B, T, LRU_BLOCKS, LRU_WIDTH // LRU_BLOCKS)
    r = jax.nn.sigmoid(jnp.einsum("btni,nij->btnj", xb, w_a).reshape(B, T, LRU_WIDTH) + b_a)
    ig = jax.nn.sigmoid(jnp.einsum("btni,nij->btnj", xb, w_x).reshape(B, T, LRU_WIDTH) + b_x)
    log_a = -LRU_C * r * jax.nn.softplus(-lam)
    a = jnp.exp(log_a)
    b = jnp.sqrt(-jnp.expm1(2.0 * log_a)) * ig * xr

    def combine(e1, e2):
        return (e1[0] * e2[0], e2[0] * e1[1] + e2[1])

    _, h = lax.associative_scan(combine, (a, b), axis=1)
    return h


def _mixer(u, valid, cos, sin, w, wb, l):
    def pair(name, *idx):
        return wb[name][(l,) + idx], w[name][(l,) + idx]

    B, T, D = u.shape
    cuts = (OFF_FOX_QKV, OFF_FOX_F, OFF_MLA_CQ, OFF_MLA_CKV, OFF_MLA_KR, OFF_GDN_QKV, OFF_GDN_A, OFF_GDN_B,
            OFF_GDN_G, OFF_LRU, N_IN)
    bounds = tuple(zip(cuts[:-1], cuts[1:]))
    parts = _split_cols(dense_nd(u, *pair("w_in"), keep_pad=True), bounds)
    p = _Cols({b: s.reshape(B, T, -1) for b, s in zip(bounds, parts)}, (B, T, N_IN))
    ys = (_fox_branch(p, w["fox_bf"][l]),
          _mla_branch(p, w["mla_gq"][l], w["mla_wq"][l], w["mla_gkv"][l], w["mla_wkv"][l], cos, sin),
          _gdn_branch(p, w["gdn_conv"][l], w["gdn_alog"][l], w["gdn_dtb"][l], w["gdn_gon"][l]),
          _rglru_branch(p, valid, w["lru_conv"][l], w["lru_conv_b"][l], w["lru_wa"][l], w["lru_ba"][l],
                        w["lru_wx"][l], w["lru_bx"][l], w["lru_lam"][l]))
    w_gate = [jnp.moveaxis(t, 0, 1).reshape(D, N_BRANCH * D) for t in pair("w_gate")]
    gates = jax.nn.sigmoid(dense_nd(u, *w_gate, keep_pad=True) + w["b_gate"][l].reshape(-1))
    gates = _split_cols(gates, tuple((n * D, (n + 1) * D) for n in range(N_BRANCH)))
    merged = gates[0].reshape(B, T, D) * dense_nd(ys[0], *pair("w_branch", 0))
    for n in range(1, N_BRANCH):
        merged = merged + gates[n].reshape(B, T, D) * dense_nd(ys[n], *pair("w_branch", n))
    return dense_nd(merged, *pair("w_out"))


def _local_loss(w, x, loss_target, wb):
    B, S, D = x.shape
    T = BLOCK + S
    h = jnp.concatenate([jnp.zeros((B, PAD_LEN, D), F32),
                         jnp.broadcast_to(w["meta"][None], (B, N_META, D)), x], axis=1)
    pos = jnp.arange(T)
    valid = pos >= PAD_LEN
    rel = (pos - PAD_LEN).astype(F32)
    inv_freq = ROPE_BASE ** (-(jnp.arange(0, MLA_ROPE, 2, dtype=F32) / MLA_ROPE))
    ang = rel[:, None] * inv_freq[None, :]
    cos, sin = jnp.cos(ang), jnp.sin(ang)
    def pair(name, l):
        return wb[name][l], w[name][l]

    for l in range(DEPTH):
        h = h + 0.5 * _swiglu(_rmsnorm_rows(h, w["ln_ffn1"][l]), pair("ffn1_wi", l), pair("ffn1_wo", l))
        u = jnp.where(valid[None, :, None], _rmsnorm_rows(h, w["ln_mix"][l]), 0)
        h = h + _mixer(u, valid, cos, sin, w, wb, l)
        h = h + 0.5 * _swiglu(_rmsnorm_rows(h, w["ln_ffn2"][l]), pair("ffn2_wi", l), pair("ffn2_wo", l))
    y = _rmsnorm_rows(h, w["ln_final"])[:, BLOCK:]
    err = jnp.square(y - loss_target)
    return 0.5 * jnp.sum(jnp.mean(err, axis=-1))


MESH = pl.DeviceIdType.MESH
HBM = pl.BlockSpec(memory_space=pl.ANY)


def _place():
    x, y, c = lax.axis_index("x"), lax.axis_index("y"), lax.axis_index("c")
    return x, y, c


def _other_chip(x, y, r):
    return (1 - x if r & 2 else x), (1 - y if r & 1 else y)


def _gather_chips(shard):
    R, C = shard.shape
    H = R // 2

    def body(x_ref, out_ref, send_sems, recv_sems, local_sem):
        x, y, c = _place()
        me = 2 * x + y

        def rows(chip, half):
            return out_ref.at[chip, pl.ds(half * H, H), :]

        def copy(sem, src, dst, to):
            return pltpu.make_async_remote_copy(src_ref=src, dst_ref=dst, send_sem=send_sems.at[sem],
                                                recv_sem=recv_sems.at[sem], device_id=to, device_id_type=MESH)

        mine = pltpu.make_async_copy(x_ref, out_ref.at[me], local_sem)
        mine.start()
        started = []
        for r in (1, 2, 3):
            ox, oy = _other_chip(x, y, r)
            cp = copy(r - 1, x_ref.at[pl.ds(c * H, H), :], rows(me, c), (ox, oy, c))
            cp.start()
            started.append(cp)
        for r in (1, 2, 3):
            ox, oy = _other_chip(x, y, r)
            src = 2 * ox + oy
            copy(r - 1, rows(src, c), rows(src, c), (x, y, c)).wait_recv()
            fw = copy(2 + r, rows(src, c), rows(src, c), (x, y, 1 - c))
            fw.start()
            started.append(fw)
        for r in (1, 2, 3):
            ox, oy = _other_chip(x, y, r)
            src = 2 * ox + oy
            copy(2 + r, rows(src, 1 - c), rows(src, 1 - c), (x, y, c)).wait_recv()
        for cp in started:
            cp.wait_send()
        mine.wait()

    return _pcall(
        body, name="gather_chips", in_specs=[HBM], out_specs=HBM,
        out_shape=jax.ShapeDtypeStruct((N_CHIPS, R, C), shard.dtype),
        scratch_shapes=[pltpu.SemaphoreType.DMA((6,)), pltpu.SemaphoreType.DMA((6,)),
                        pltpu.SemaphoreType.DMA],
    )(shard)


def _window(ref, lead, axis, chip, width):
    idx = [slice(None)] * len(ref.shape)
    if lead is not None:
        idx[0] = lead
    idx[axis] = pl.ds(chip * width, width)
    return ref.at[tuple(idx)]


def _gather_layers(shards, axes):
    n_t = len(shards)
    widths = [s.shape[a] for s, a in zip(shards, axes)]
    fulls = [s.shape[:a] + (N_CHIPS * s.shape[a],) + s.shape[a + 1:] for s, a in zip(shards, axes)]

    def body(*refs):
        x_refs, out_refs = refs[:n_t], refs[n_t:2 * n_t]
        send_sems, recv_sems = refs[2 * n_t:]
        x, y, c = _place()
        me = 2 * x + y

        def win(t, lead, chip):
            return _window(out_refs[t], lead, axes[t], chip, widths[t])

        def copy(sem, src, dst, to):
            return pltpu.make_async_remote_copy(src_ref=src, dst_ref=dst, send_sem=send_sems.at[sem],
                                                recv_sem=recv_sems.at[sem], device_id=to, device_id_type=MESH)

        started = []
        for t in range(n_t):
            cp = copy(7 * t + 6, x_refs[t], win(t, None, me), (x, y, 1 - c))
            cp.start()
            started.append(cp)
        for r in (1, 2, 3):
            ox, oy = _other_chip(x, y, r)
            for t in range(n_t):
                cp = copy(7 * t + r - 1, x_refs[t].at[c], win(t, c, me), (ox, oy, c))
                cp.start()
                started.append(cp)
        for r in (1, 2, 3):
            ox, oy = _other_chip(x, y, r)
            src = 2 * ox + oy
            for t in range(n_t):
                copy(7 * t + r - 1, win(t, c, src), win(t, c, src), (x, y, c)).wait_recv()
                fw = copy(7 * t + 2 + r, win(t, c, src), win(t, c, src), (x, y, 1 - c))
                fw.start()
                started.append(fw)
        for r in (1, 2, 3):
            ox, oy = _other_chip(x, y, r)
            src = 2 * ox + oy
            for t in range(n_t):
                copy(7 * t + 2 + r, win(t, 1 - c, src), win(t, 1 - c, src), (x, y, c)).wait_recv()
        for t in range(n_t):
            copy(7 * t + 6, win(t, None, me), win(t, None, me), (x, y, c)).wait_recv()
        for cp in started:
            cp.wait_send()

    return _pcall(
        body, name="gather_layers", in_specs=[HBM] * n_t, out_specs=[HBM] * n_t,
        out_shape=[jax.ShapeDtypeStruct(f, s.dtype) for f, s in zip(fulls, shards)],
        scratch_shapes=[pltpu.SemaphoreType.DMA((7 * n_t,)), pltpu.SemaphoreType.DMA((7 * n_t,))],
    )(*shards)


def _swap_layers(gs):
    n_t = len(gs)

    def body(*refs):
        g_refs, out_refs = refs[:n_t], refs[n_t:2 * n_t]
        send_sems, recv_sems = refs[2 * n_t:]
        x, y, c = _place()
        cps = []
        for t in range(n_t):
            cp = pltpu.make_async_remote_copy(
                src_ref=g_refs[t].at[1 - c], dst_ref=out_refs[t], send_sem=send_sems.at[t],
                recv_sem=recv_sems.at[t], device_id=(x, y, 1 - c), device_id_type=MESH)
            cp.start()
            cps.append(cp)
        for cp in cps:
            cp.wait()

    return _pcall(
        body, name="swap_layers", in_specs=[HBM] * n_t, out_specs=[HBM] * n_t,
        out_shape=[jax.ShapeDtypeStruct(g.shape[1:], g.dtype) for g in gs],
        scratch_shapes=[pltpu.SemaphoreType.DMA((n_t,)), pltpu.SemaphoreType.DMA((n_t,))],
    )(*gs)


def _add_layer(g, other, c):
    shape = other.shape
    last = shape[-1]
    rows = math.prod(shape[:-1])
    tr = _row_tile(rows, 4 * last, ROW_BLOCK_BYTES)

    def body(c_ref, a_ref, b_ref, o_ref):
        o_ref[...] = (a_ref[0] + b_ref[...]).astype(BF16)

    out = _pcall(
        body, name="add_layer",
        grid_spec=pltpu.PrefetchScalarGridSpec(
            num_scalar_prefetch=1, grid=(rows // tr,),
            in_specs=[pl.BlockSpec((1, tr, last), lambda i, c_ref: (c_ref[0], i, 0)),
                      pl.BlockSpec((tr, last), lambda i, c_ref: (i, 0))],
            out_specs=pl.BlockSpec((tr, last), lambda i, c_ref: (i, 0))),
        out_shape=jax.ShapeDtypeStruct((rows, last), BF16),
        compiler_params=pltpu.CompilerParams(dimension_semantics=("parallel",)),
    )(c.reshape(1).astype(jnp.int32), g.reshape(2, rows, last), other.reshape(rows, last))
    return out.reshape(shape)


def _scatter_layers(ps, axes):
    n_t = len(ps)
    widths = [p.shape[a] // N_CHIPS for p, a in zip(ps, axes)]
    wins = [p.shape[:a] + (w,) + p.shape[a + 1:] for p, a, w in zip(ps, axes, widths)]

    def body(*refs):
        p_refs, out_refs = refs[:n_t], refs[n_t:2 * n_t]
        send_sems, recv_sems = refs[2 * n_t:]
        x, y, c = _place()
        cps = []
        for r in (1, 2, 3):
            ox, oy = _other_chip(x, y, r)
            for t in range(n_t):
                cp = pltpu.make_async_remote_copy(
                    src_ref=_window(p_refs[t], None, axes[t], 2 * ox + oy, widths[t]), dst_ref=out_refs[t].at[r - 1],
                    send_sem=send_sems.at[3 * t + r - 1], recv_sem=recv_sems.at[3 * t + r - 1],
                    device_id=(ox, oy, c), device_id_type=MESH)
                cp.start()
                cps.append(cp)
        for cp in cps:
            cp.wait()

    return _pcall(
        body, name="scatter_layers", in_specs=[HBM] * n_t, out_specs=[HBM] * n_t,
        out_shape=[jax.ShapeDtypeStruct((N_CHIPS - 1,) + w, p.dtype) for w, p in zip(wins, ps)],
        scratch_shapes=[pltpu.SemaphoreType.DMA((3 * n_t,)), pltpu.SemaphoreType.DMA((3 * n_t,))],
    )(*ps)


def _sum_chips(p, q, axis, me):
    win = q.shape[1:]
    C = win[-1]
    H = math.prod(win[:-1])
    if axis == p.ndim - 1:
        tr = _row_tile(H, 4 * C, ROW_BLOCK_BYTES)
        grid = H // tr
        p = p.reshape(H, N_CHIPS * C)
        p_spec = pl.BlockSpec((tr, C), lambda i, me_ref: (i, me_ref[0]))
    else:
        pre, inner = math.prod(p.shape[:axis]), math.prod(win[axis:-1])
        tr = _row_tile(inner, 4 * C, ROW_BLOCK_BYTES)
        nb = inner // tr
        grid = pre * nb
        p = p.reshape(pre, N_CHIPS, inner, C)
        p_spec = pl.BlockSpec((1, 1, tr, C), lambda i, me_ref: (i // nb, me_ref[0], i % nb, 0))

    def body(me_ref, p_ref, q0, q1, q2, q3, o_ref):
        own = p_ref[...].reshape(tr, C).astype(F32)
        terms = [jnp.where(me_ref[0] == chip, own, qr[0].astype(F32)) for chip, qr in enumerate((q0, q1, q2, q3))]
        o_ref[...] = ((terms[0] + terms[1]) + terms[2]) + terms[3]

    def q_spec(chip):
        return pl.BlockSpec(
            (1, tr, C), lambda i, me_ref: (jnp.maximum(jnp.bitwise_xor(me_ref[0], chip), 1) - 1, i, 0))

    q = q.reshape(N_CHIPS - 1, H, C)
    return _pcall(
        body, name="sum_chips",
        grid_spec=pltpu.PrefetchScalarGridSpec(
            num_scalar_prefetch=1, grid=(grid,),
            in_specs=[p_spec, q_spec(0), q_spec(1), q_spec(2), q_spec(3)],
            out_specs=pl.BlockSpec((tr, C), lambda i, me_ref: (i, 0))),
        out_shape=jax.ShapeDtypeStruct((H, C), F32),
        compiler_params=pltpu.CompilerParams(dimension_semantics=("parallel",)),
    )(me.reshape(1).astype(jnp.int32), p, q, q, q, q).reshape(win)


def _send_layers(rs):
    n_t = len(rs)

    def body(*refs):
        r_refs, out_refs = refs[:n_t], refs[n_t:2 * n_t]
        send_sems, recv_sems = refs[2 * n_t:]
        x, y, c = _place()
        cps = []
        for t in range(n_t):
            cp = pltpu.make_async_remote_copy(
                src_ref=r_refs[t], dst_ref=out_refs[t], send_sem=send_sems.at[t],
                recv_sem=recv_sems.at[t], device_id=(x, y, 1 - c), device_id_type=MESH)
            cp.start()
            cps.append(cp)
        for cp in cps:
            cp.wait()

    return _pcall(
        body, name="send_layers", in_specs=[HBM] * n_t, out_specs=[HBM] * n_t,
        out_shape=[jax.ShapeDtypeStruct(r.shape, r.dtype) for r in rs],
        scratch_shapes=[pltpu.SemaphoreType.DMA((n_t,)), pltpu.SemaphoreType.DMA((n_t,))],
    )(*rs)


def _reduce_scatter_layers(gs, axes):
    x, y, c = _place()
    others = _swap_layers(gs)
    ps = [_add_layer(g, o, c) for g, o in zip(gs, others)]
    qs = _scatter_layers(ps, [a - 1 for a in axes])
    rs = [_sum_chips(p, q, a - 1, 2 * x + y) for p, q, a in zip(ps, qs, axes)]
    theirs = _send_layers(rs)
    return [jnp.where(c == 0, jnp.stack([r, o]), jnp.stack([o, r])) for r, o in zip(rs, theirs)]


def _allgather_devices(flat):
    R, C = flat.shape

    def body(x_ref, out_ref, send_sems, recv_sems, local_sem):
        x, y, c = _place()
        me = 4 * x + 2 * y + c
        mine = pltpu.make_async_copy(x_ref, out_ref.at[me], local_sem)
        mine.start()
        cps = []
        for m in range(1, 8):
            ox, oy = _other_chip(x, y, m >> 1)
            oc = 1 - c if m & 1 else c
            cp = pltpu.make_async_remote_copy(
                src_ref=x_ref, dst_ref=out_ref.at[me], send_sem=send_sems.at[m - 1],
                recv_sem=recv_sems.at[m - 1], device_id=(ox, oy, oc), device_id_type=MESH)
            cp.start()
            cps.append(cp)
        for cp in cps:
            cp.wait()
        mine.wait()

    return _pcall(
        body, name="allgather_devices", in_specs=[HBM], out_specs=HBM,
        out_shape=jax.ShapeDtypeStruct((8, R, C), flat.dtype),
        scratch_shapes=[pltpu.SemaphoreType.DMA((7,)), pltpu.SemaphoreType.DMA((7,)),
                        pltpu.SemaphoreType.DMA],
    )(flat)


def _sum_devices(slots):
    _, R, C = slots.shape
    tr = _row_tile(R, 4 * C, ROW_BLOCK_BYTES // 4)

    def body(*refs):
        o_ref = refs[8]
        acc = refs[0][0]
        for d in range(1, 8):
            acc = acc + refs[d][0]
        o_ref[...] = acc

    def spec(d):
        return pl.BlockSpec((1, tr, C), lambda i: (d, i, 0))

    return _pcall(
        body, name="sum_devices", grid=(R // tr,), in_specs=[spec(d) for d in range(8)],
        out_specs=pl.BlockSpec((tr, C), lambda i: (i, 0)),
        out_shape=jax.ShapeDtypeStruct((R, C), F32),
        compiler_params=pltpu.CompilerParams(dimension_semantics=("parallel",)),
    )(*([slots] * 8))


def _adamw(w, g, m, v):
    shape = w.shape
    C = shape[-1]
    R = math.prod(shape[:-1])
    w, g, m, v = (t.reshape(R, C) for t in (w, g, m, v))
    tr = _row_tile(R, 4 * C, ROW_BLOCK_BYTES // 2)
    c1 = 1.0 - ADAM_B1 ** ADAM_STEP
    c2 = 1.0 - ADAM_B2 ** ADAM_STEP

    def body(w_ref, g_ref, m_ref, v_ref, d_ref, nm_ref, nv_ref):
        gg = g_ref[...]
        nm = ADAM_B1 * m_ref[...] + (1.0 - ADAM_B1) * gg
        nv = ADAM_B2 * v_ref[...] + (1.0 - ADAM_B2) * jnp.square(gg)
        d_ref[...] = -ADAM_LR * ((nm / c1) / (jnp.sqrt(nv / c2) + ADAM_EPS) + ADAM_WD * w_ref[...])
        nm_ref[...] = nm
        nv_ref[...] = nv

    spec = pl.BlockSpec((tr, C), lambda i: (i, 0))
    outs = _pcall(
        body, name="adamw", grid=(R // tr,), in_specs=[spec] * 4, out_specs=[spec] * 3,
        out_shape=[jax.ShapeDtypeStruct((R, C), F32)] * 3,
        compiler_params=pltpu.CompilerParams(dimension_semantics=("parallel",)),
    )(w, g, m, v)
    return [o.reshape(shape) for o in outs]


def _to_flat(parts):
    flat = jnp.concatenate([p.reshape(-1) for p in parts])
    unit = FLAT_COLS * FLAT_ROW_ALIGN
    pad = (-flat.shape[0]) % unit
    if pad:
        flat = jnp.concatenate([flat, jnp.zeros((pad,), flat.dtype)])
    return flat.reshape(-1, FLAT_COLS)


def _from_flat(flat, shapes):
    flat = flat.reshape(-1)
    out, off = [], 0
    for s in shapes:
        n = math.prod(s)
        out.append(flat[off:off + n].reshape(s))
        off += n
    return out


def kernel(x, meta, ln_ffn1, ffn1_wi, ffn1_wo, ln_mix, w_in, fox_bf, mla_gq, mla_wq, mla_gkv, mla_wkv, gdn_conv, gdn_alog, gdn_dtb, gdn_gon, lru_conv, lru_conv_b, lru_wa, lru_ba, lru_wx, lru_bx, lru_lam, w_gate, b_gate, w_branch, w_out, ln_ffn2, ffn2_wi, ffn2_wo, ln_final, loss_target, m_meta, m_ln_ffn1, m_ffn1_wi, m_ffn1_wo, m_ln_mix, m_w_in, m_fox_bf, m_mla_gq, m_mla_wq, m_mla_gkv, m_mla_wkv, m_gdn_conv, m_gdn_alog, m_gdn_dtb, m_gdn_gon, m_lru_conv, m_lru_conv_b, m_lru_wa, m_lru_ba, m_lru_wx, m_lru_bx, m_lru_lam, m_w_gate, m_b_gate, m_w_branch, m_w_out, m_ln_ffn2, m_ffn2_wi, m_ffn2_wo, m_ln_final, v_meta, v_ln_ffn1, v_ffn1_wi, v_ffn1_wo, v_ln_mix, v_w_in, v_fox_bf, v_mla_gq, v_mla_wq, v_mla_gkv, v_mla_wkv, v_gdn_conv, v_gdn_alog, v_gdn_dtb, v_gdn_gon, v_lru_conv, v_lru_conv_b, v_lru_wa, v_lru_ba, v_lru_wx, v_lru_bx, v_lru_lam, v_w_gate, v_b_gate, v_w_branch, v_w_out, v_ln_ffn2, v_ffn2_wi, v_ffn2_wo, v_ln_final):
    ws = (meta, ln_ffn1, ffn1_wi, ffn1_wo, ln_mix, w_in, fox_bf, mla_gq, mla_wq, mla_gkv, mla_wkv, gdn_conv, gdn_alog, gdn_dtb, gdn_gon, lru_conv, lru_conv_b, lru_wa, lru_ba, lru_wx, lru_bx, lru_lam, w_gate, b_gate, w_branch, w_out, ln_ffn2, ffn2_wi, ffn2_wo, ln_final)
    ms = (m_meta, m_ln_ffn1, m_ffn1_wi, m_ffn1_wo, m_ln_mix, m_w_in, m_fox_bf, m_mla_gq, m_mla_wq, m_mla_gkv, m_mla_wkv, m_gdn_conv, m_gdn_alog, m_gdn_dtb, m_gdn_gon, m_lru_conv, m_lru_conv_b, m_lru_wa, m_lru_ba, m_lru_wx, m_lru_bx, m_lru_lam, m_w_gate, m_b_gate, m_w_branch, m_w_out, m_ln_ffn2, m_ffn2_wi, m_ffn2_wo, m_ln_final)
    vs = (v_meta, v_ln_ffn1, v_ffn1_wi, v_ffn1_wo, v_ln_mix, v_w_in, v_fox_bf, v_mla_gq, v_mla_wq, v_mla_gkv, v_mla_wkv, v_gdn_conv, v_gdn_alog, v_gdn_dtb, v_gdn_gon, v_lru_conv, v_lru_conv_b, v_lru_wa, v_lru_ba, v_lru_wx, v_lru_bx, v_lru_lam, v_w_gate, v_b_gate, v_w_branch, v_w_out, v_ln_ffn2, v_ffn2_wi, v_ffn2_wo, v_ln_final)
    names = [n for n, _ in WEIGHT_SPECS]
    axis = dict(WEIGHT_SPECS)
    wd, md, vd = dict(zip(names, ws)), dict(zip(names, ms)), dict(zip(names, vs))
    shapes = {n: wd[n].shape for n in names}
    big = [n for n in names if n in LARGE]
    few = [n for n in names if axis[n] is not None and n not in LARGE]
    whole = [n for n in names if axis[n] is None]
    x_, y_, _ = _place()
    chip = 2 * x_ + y_

    def to_dma(n, a):
        ax, w = axis[n], shapes[n][axis[n]]
        nd = len(shapes[n])
        if (ax == nd - 1 and w % LANES) or (ax == nd - 2 and w % 16):
            parts = a.shape[ax] // w
            a = jnp.moveaxis(a.reshape(a.shape[:ax] + (parts, w) + a.shape[ax + 1:]), ax, 1)
            return a, 1
        return a, ax

    def from_dma(n, a):
        ax = axis[n]
        if a.ndim == len(shapes[n]):
            return a
        a = jnp.moveaxis(a, 1, ax)
        return a.reshape(a.shape[:ax] + (-1,) + a.shape[ax + 2:])

    dma = [to_dma(n, wd[n].astype(BF16)) for n in big]
    fulls = _gather_layers([a for a, _ in dma], [ax for _, ax in dma])
    full_bf16 = {n: from_dma(n, f) for n, f in zip(big, fulls)}
    full = {n: jnp.zeros(f.shape, F32) for n, f in full_bf16.items()}
    gathered = _gather_chips(_to_flat([wd[n] for n in few]))
    per_chip = [_from_flat(gathered[k], [shapes[n] for n in few]) for k in range(N_CHIPS)]
    for i, n in enumerate(few):
        full[n] = jnp.concatenate([per_chip[k][i] for k in range(N_CHIPS)], axis=axis[n])
    full.update({n: wd[n] for n in whole})

    loss, (gw, gx) = jax.value_and_grad(_local_loss, argnums=(0, 1))(full, x, loss_target, full_bf16)
    loss = lax.psum(loss, ("x", "y", "c"))

    dma = [to_dma(n, gw[n]) for n in big]
    reduced = _reduce_scatter_layers([a for a, _ in dma], [ax for _, ax in dma])
    grads = {n: r.reshape(shapes[n]) for n, r in zip(big, reduced)}
    rest = few + whole
    summed = _from_flat(_sum_devices(_allgather_devices(_to_flat([gw[n] for n in rest]))),
                        [gw[n].shape for n in rest])
    for n, g in zip(rest, summed):
        if axis[n] is not None:
            g = lax.dynamic_slice_in_dim(g, chip * shapes[n][axis[n]], shapes[n][axis[n]], axis=axis[n])
        grads[n] = g

    delta, new_m, new_v = {}, {}, {}
    for n in big:
        delta[n], new_m[n], new_v[n] = _adamw(wd[n], grads[n], md[n], vd[n])
    outs = _adamw(*[_to_flat([d[n] for n in rest]) for d in (wd, grads, md, vd)])
    for res, flat in zip((delta, new_m, new_v), outs):
        res.update(zip(rest, _from_flat(flat, [shapes[n] for n in rest])))

    return (loss, gx, *[grads[n] for n in names], *[delta[n] for n in names],
            *[new_m[n] for n in names], *[new_v[n] for n in names])
```

```python
import functools
import math

import jax
import jax.numpy as jnp
from jax import lax
from jax.experimental import pallas as pl
from jax.experimental.pallas import tpu as pltpu

F32 = jnp.float32
BF16 = jnp.bfloat16

N_META = 16
BLOCK = 128
PAD_LEN = BLOCK - N_META
EPS = 1e-6
NEG_INF = -1e30
N_BRANCH = 4
FOX_HEADS, FOX_DH = 4, 64
MLA_HEADS, MLA_NOPE, MLA_ROPE, MLA_DV = 4, 64, 32, 64
MLA_Q_RANK, MLA_KV_RANK = 192, 128
ROPE_BASE = 10000.0
GDN_HEADS, GDN_DK, GDN_DV, GDN_CHUNK = 4, 64, 64, 64
LRU_WIDTH, LRU_BLOCKS, LRU_C = 256, 4, 8.0
DEPTH = 2

OFF_FOX_QKV = 0
OFF_FOX_F = OFF_FOX_QKV + 3 * FOX_HEADS * FOX_DH
OFF_MLA_CQ = OFF_FOX_F + FOX_HEADS
OFF_MLA_CKV = OFF_MLA_CQ + MLA_Q_RANK
OFF_MLA_KR = OFF_MLA_CKV + MLA_KV_RANK
OFF_GDN_QKV = OFF_MLA_KR + MLA_ROPE
OFF_GDN_A = OFF_GDN_QKV + GDN_HEADS * (2 * GDN_DK + GDN_DV)
OFF_GDN_B = OFF_GDN_A + GDN_HEADS
OFF_GDN_G = OFF_GDN_B + GDN_HEADS
OFF_LRU = OFF_GDN_G + GDN_HEADS * GDN_DV
N_IN = OFF_LRU + LRU_WIDTH

ADAM_LR, ADAM_B1, ADAM_B2, ADAM_EPS, ADAM_WD, ADAM_STEP = 0.001, 0.9, 0.999, 1e-08, 0.01, 10

WEIGHT_SPECS = (
    ("meta", 1), ("ln_ffn1", None), ("ffn1_wi", 2), ("ffn1_wo", 1), ("ln_mix", None), ("w_in", 2),
    ("fox_bf", None), ("mla_gq", None), ("mla_wq", 2), ("mla_gkv", None), ("mla_wkv", 2),
    ("gdn_conv", 2), ("gdn_alog", None), ("gdn_dtb", None), ("gdn_gon", None), ("lru_conv", 2),
    ("lru_conv_b", None), ("lru_wa", None), ("lru_ba", None), ("lru_wx", None), ("lru_bx", None),
    ("lru_lam", None), ("w_gate", 2), ("b_gate", 2), ("w_branch", 3), ("w_out", 1),
    ("ln_ffn2", None), ("ffn2_wi", 2), ("ffn2_wo", 1), ("ln_final", None),
)
N_CHIPS = 4
LARGE = ("ffn1_wi", "ffn1_wo", "w_in", "w_gate", "w_branch", "w_out", "ffn2_wi", "ffn2_wo")

LANES = 128
VMEM_LIMIT_BYTES = 48 * 1024 * 1024
FLAT_COLS = 512
FLAT_ROW_ALIGN = 64


def _pcall(body, **kw):
    return pl.pallas_call(body, **kw)


def _pick(n, cands):
    for c in cands:
        if n % c == 0:
            return c
    return n


_DN = {"nn": (((1,), (0,)), ((), ())), "nt": (((1,), (1,)), ((), ())), "tn": (((0,), (0,)), ((), ()))}


MATMUL_OPERAND_TILE_BYTES = 8 * 1024 * 1024


def _k_tile(K, row_bytes, lane_axis):
    for tk in (K, 4224, 2816, 2112, 1408, 1056, 1024, 768, 704, 512, 384, 256, 128):
        aligned = tk == K or tk % LANES == 0 or (not lane_axis and tk % 16 == 0)
        if tk <= K and K % tk == 0 and aligned and tk * row_bytes <= MATMUL_OPERAND_TILE_BYTES:
            return tk
    return K


def _matmul(a, b, mode, name, out_dtype=F32):
    if mode == "nn":
        (M, K), (_, N) = a.shape, b.shape
    elif mode == "nt":
        (M, K), (N, _) = a.shape, b.shape
    else:
        (K, M), (_, N) = a.shape, b.shape
    tm = _pick(M, (1024, 768, 512, 1408, 384, 256, 128, 64, 32, 16, 8))
    tn = _pick(N, (1408, 1024, 1280, 512, 256, 128))
    tk = _k_tile(K, tm * a.dtype.itemsize + tn * b.dtype.itemsize, mode != "tn")
    nk = K // tk
    a_spec = {"nn": pl.BlockSpec((tm, tk), lambda i, j, k: (i, k)),
              "nt": pl.BlockSpec((tm, tk), lambda i, j, k: (i, k)),
              "tn": pl.BlockSpec((tk, tm), lambda i, j, k: (k, i))}[mode]
    b_spec = {"nn": pl.BlockSpec((tk, tn), lambda i, j, k: (k, j)),
              "nt": pl.BlockSpec((tn, tk), lambda i, j, k: (j, k)),
              "tn": pl.BlockSpec((tk, tn), lambda i, j, k: (k, j))}[mode]
    dn = _DN[mode]

    def body(a_ref, b_ref, o_ref, acc_ref):
        k = pl.program_id(2)
        part = lax.dot_general(a_ref[...].astype(BF16), b_ref[...].astype(BF16), dn,
                               preferred_element_type=F32)

        if nk == 1:
            o_ref[...] = part.astype(o_ref.dtype)
        else:
            @pl.when(k == 0)
            def _():
                acc_ref[...] = part

            @pl.when((k > 0) & (k < nk - 1))
            def _():
                acc_ref[...] += part

            @pl.when(k == nk - 1)
            def _():
                o_ref[...] = (acc_ref[...] + part).astype(o_ref.dtype)

    return _pcall(
        body, name=name, grid=(M // tm, N // tn, nk),
        in_specs=[a_spec, b_spec], out_specs=pl.BlockSpec((tm, tn), lambda i, j, k: (i, j)),
        out_shape=jax.ShapeDtypeStruct((M, N), out_dtype),
        scratch_shapes=[pltpu.VMEM((tm, tn) if nk > 1 else (8, LANES), F32)],
        compiler_params=pltpu.CompilerParams(
            dimension_semantics=("parallel", "parallel", "arbitrary"),
            vmem_limit_bytes=VMEM_LIMIT_BYTES),
    )(a, b)


@functools.partial(jax.custom_vjp, nondiff_argnums=(3,))
def dense(a, wb, w, out_dtype):
    return _matmul(a, wb, "nn", "dense_fwd", out_dtype)


def _dense_fwd(a, wb, w, out_dtype):
    return _matmul(a, wb, "nn", "dense_fwd", out_dtype), (a, wb)


def _dense_bwd(out_dtype, res, g):
    a, wb = res
    return (_matmul(g, wb, "nt", "dense_dgrad", a.dtype), jnp.zeros_like(wb),
            _matmul(a, g, "tn", "dense_wgrad", F32))


dense.defvjp(_dense_fwd, _dense_bwd)


def dense_nd(a, wb, w, out_dtype=F32, keep_pad=False):
    K, N = wb.shape
    pad = (-N) % (4 * LANES if N > 4 * LANES else LANES)
    if pad:
        wb = jnp.pad(wb, ((0, 0), (0, pad)))
        w = jnp.pad(w, ((0, 0), (0, pad)))
    out = dense(a.reshape(-1, K), wb, w, out_dtype)
    if keep_pad:
        return out
    if pad:
        out = out[:, :N]
    return out.reshape(a.shape[:-1] + (N,))


@functools.partial(jax.custom_vjp, nondiff_argnums=(1,))
def _split_cols(x, bounds):
    return tuple(x[:, a:b] for a, b in bounds)


def _split_cols_fwd(x, bounds):
    return _split_cols(x, bounds), jnp.zeros((x.shape[0], x.shape[1] - bounds[-1][1]), x.dtype)


def _split_cols_bwd(bounds, rest, cts):
    return (jnp.concatenate(list(cts) + ([rest] if rest.shape[1] else []), axis=1),)


_split_cols.defvjp(_split_cols_fwd, _split_cols_bwd)


class _Cols:
    def __init__(self, ranges, shape):
        self.ranges, self.shape = ranges, shape

    def __getitem__(self, idx):
        return self.ranges[(idx[-1].start, idx[-1].stop)]


_NT = (((1,), (1,)), ((), ()))


def _att_blk(T):
    return _pick(T, (384, 256, 128))


def _causal_pairs(n, by_query):
    if by_query:
        pairs = [(i, j) for i in range(n) for j in range(i + 1)]
    else:
        pairs = [(i, j) for j in range(n) for i in range(j, n)]
    return (jnp.array([p[0] for p in pairs], jnp.int32), jnp.array([p[1] for p in pairs], jnp.int32))


def _attn_fwd_call(q, k, v, cum_col, cum_row, scale):
    BH, T, dk = q.shape
    dv = v.shape[2]
    blk = _att_blk(T)
    n = T // blk
    has_cum = cum_col is not None

    def body(i_tab, j_tab, *refs):
        if has_cum:
            q_ref, k_ref, v_ref, cq_ref, ck_ref, o_ref, lse_ref, m_s, l_s, acc_s = refs
        else:
            q_ref, k_ref, v_ref, o_ref, lse_ref, m_s, l_s, acc_s = refs
        i = i_tab[pl.program_id(1)]
        j = j_tab[pl.program_id(1)]

        @pl.when(j == 0)
        def _():
            m_s[...] = jnp.full_like(m_s, NEG_INF)
            l_s[...] = jnp.zeros_like(l_s)
            acc_s[...] = jnp.zeros_like(acc_s)

        def step(masked):
            s = lax.dot_general(q_ref[0].astype(BF16), k_ref[0].astype(BF16), _NT,
                                preferred_element_type=F32) * scale
            if has_cum:
                s = s + cq_ref[0] - ck_ref[0]
            if masked:
                qpos = i * blk + lax.broadcasted_iota(jnp.int32, (blk, blk), 0)
                kpos = j * blk + lax.broadcasted_iota(jnp.int32, (blk, blk), 1)
                mask = (kpos <= qpos) & (kpos >= PAD_LEN)
                s = jnp.where(mask, s, NEG_INF)
            m_prev = m_s[...]
            m_new = jnp.maximum(m_prev, jnp.max(s, axis=1, keepdims=True))
            p = jnp.exp(s - m_new)
            alpha = jnp.exp(m_prev - m_new)
            l_s[...] = alpha * l_s[...] + jnp.sum(p, axis=1, keepdims=True)
            acc_s[...] = alpha * acc_s[...] + jnp.dot(p.astype(BF16), v_ref[0].astype(BF16),
                                                      preferred_element_type=F32)
            m_s[...] = m_new

        @pl.when((j == i) | (j == 0))
        def _():
            step(True)

        @pl.when((j < i) & (j > 0))
        def _():
            step(False)

        @pl.when(j == i)
        def _():
            o_ref[0] = acc_s[...] / l_s[...]
            lse_ref[0] = m_s[...] + jnp.log(l_s[...])

    q_idx = lambda b, t, it, jt: (b, it[t], 0)
    kv_idx = lambda b, t, it, jt: (b, jt[t], 0)
    in_specs = [pl.BlockSpec((1, blk, dk), q_idx), pl.BlockSpec((1, blk, dk), kv_idx),
                pl.BlockSpec((1, blk, dv), kv_idx)]
    args = [q, k, v]
    if has_cum:
        in_specs += [pl.BlockSpec((1, blk, 1), q_idx),
                     pl.BlockSpec((1, 1, blk), lambda b, t, it, jt: (b, 0, jt[t]))]
        args += [cum_col, cum_row]
    i_tab, j_tab = _causal_pairs(n, by_query=True)
    return _pcall(
        body, name="attn_fwd_cum" if has_cum else "attn_fwd",
        grid_spec=pltpu.PrefetchScalarGridSpec(
            num_scalar_prefetch=2, grid=(BH, len(i_tab)), in_specs=in_specs,
            out_specs=[pl.BlockSpec((1, blk, dv), q_idx), pl.BlockSpec((1, blk, 1), q_idx)],
            scratch_shapes=[pltpu.VMEM((blk, 1), F32), pltpu.VMEM((blk, 1), F32), pltpu.VMEM((blk, dv), F32)]),
        out_shape=[jax.ShapeDtypeStruct((BH, T, dv), F32), jax.ShapeDtypeStruct((BH, T, 1), F32)],
        compiler_params=pltpu.CompilerParams(
            dimension_semantics=("parallel", "arbitrary"), vmem_limit_bytes=VMEM_LIMIT_BYTES),
    )(i_tab, j_tab, *args)


def _attn_dkv_call(q, k, v, do, lse_row, delta_row, cum_row, cum_col, scale):
    BH, T, dk = q.shape
    dv = v.shape[2]
    blk = _att_blk(T)
    n = T // blk
    has_cum = cum_row is not None

    def body(i_tab, j_tab, *refs):
        if has_cum:
            (k_ref, v_ref, q_ref, do_ref, lse_ref, dl_ref, cq_ref, ck_ref,
             dk_ref, dv_ref, dc_ref, dk_s, dv_s, dc_s) = refs
        else:
            k_ref, v_ref, q_ref, do_ref, lse_ref, dl_ref, dk_ref, dv_ref, dk_s, dv_s = refs
        i = i_tab[pl.program_id(1)]
        j = j_tab[pl.program_id(1)]

        @pl.when(i == j)
        def _():
            dk_s[...] = jnp.zeros_like(dk_s)
            dv_s[...] = jnp.zeros_like(dv_s)
            if has_cum:
                dc_s[...] = jnp.zeros_like(dc_s)

        def step(masked):
            kb = k_ref[0].astype(BF16)
            qb = q_ref[0].astype(BF16)
            dob = do_ref[0].astype(BF16)
            st = lax.dot_general(kb, qb, _NT, preferred_element_type=F32) * scale
            if has_cum:
                st = st + cq_ref[0] - ck_ref[0]
            pt = jnp.exp(jnp.minimum(st - lse_ref[0], 0.0))
            if masked:
                kpos = j * blk + lax.broadcasted_iota(jnp.int32, (blk, blk), 0)
                qpos = i * blk + lax.broadcasted_iota(jnp.int32, (blk, blk), 1)
                pt = jnp.where((kpos <= qpos) & (kpos >= PAD_LEN), pt, 0.0)
            dv_s[...] += jnp.dot(pt.astype(BF16), dob, preferred_element_type=F32)
            dpt = lax.dot_general(v_ref[0].astype(BF16), dob, _NT, preferred_element_type=F32)
            dst = pt * (dpt - dl_ref[0])
            dk_s[...] += jnp.dot(dst.astype(BF16), qb, preferred_element_type=F32) * scale
            if has_cum:
                dc_s[...] -= jnp.sum(dst, axis=1, keepdims=True)

        @pl.when((i == j) | (j == 0))
        def _():
            step(True)

        @pl.when((i > j) & (j > 0))
        def _():
            step(False)

        @pl.when(i == n - 1)
        def _():
            dk_ref[0] = dk_s[...]
            dv_ref[0] = dv_s[...]
            if has_cum:
                dc_ref[0] = dc_s[...]

    k_idx = lambda b, t, it, jt: (b, jt[t], 0)
    q_idx = lambda b, t, it, jt: (b, it[t], 0)
    row_idx = lambda b, t, it, jt: (b, 0, it[t])
    in_specs = [pl.BlockSpec((1, blk, dk), k_idx), pl.BlockSpec((1, blk, dv), k_idx),
                pl.BlockSpec((1, blk, dk), q_idx), pl.BlockSpec((1, blk, dv), q_idx),
                pl.BlockSpec((1, 1, blk), row_idx), pl.BlockSpec((1, 1, blk), row_idx)]
    args = [k, v, q, do, lse_row, delta_row]
    out_specs = [pl.BlockSpec((1, blk, dk), k_idx), pl.BlockSpec((1, blk, dv), k_idx)]
    out_shape = [jax.ShapeDtypeStruct((BH, T, dk), F32), jax.ShapeDtypeStruct((BH, T, dv), F32)]
    scratch = [pltpu.VMEM((blk, dk), F32), pltpu.VMEM((blk, dv), F32)]
    if has_cum:
        in_specs += [pl.BlockSpec((1, 1, blk), row_idx), pl.BlockSpec((1, blk, 1), k_idx)]
        args += [cum_row, cum_col]
        out_specs.append(pl.BlockSpec((1, blk, 1), k_idx))
        out_shape.append(jax.ShapeDtypeStruct((BH, T, 1), F32))
        scratch.append(pltpu.VMEM((blk, 1), F32))
    i_tab, j_tab = _causal_pairs(n, by_query=False)
    return _pcall(
        body, name="attn_dkv_cum" if has_cum else "attn_dkv",
        grid_spec=pltpu.PrefetchScalarGridSpec(
            num_scalar_prefetch=2, grid=(BH, len(i_tab)), in_specs=in_specs, out_specs=out_specs,
            scratch_shapes=scratch),
        out_shape=out_shape,
        compiler_params=pltpu.CompilerParams(
            dimension_semantics=("parallel", "arbitrary"), vmem_limit_bytes=VMEM_LIMIT_BYTES),
    )(i_tab, j_tab, *args)


def _attn_dq_call(q, k, v, do, lse_col, delta_col, cum_col, cum_row, scale):
    BH, T, dk = q.shape
    dv = v.shape[2]
    blk = _att_blk(T)
    n = T // blk
    has_cum = cum_col is not None

    def body(i_tab, j_tab, *refs):
        if has_cum:
            q_ref, k_ref, v_ref, do_ref, lse_ref, dl_ref, cq_ref, ck_ref, dq_ref, dc_ref, dq_s, dc_s = refs
        else:
            q_ref, k_ref, v_ref, do_ref, lse_ref, dl_ref, dq_ref, dq_s = refs
        i = i_tab[pl.program_id(1)]
        j = j_tab[pl.program_id(1)]

        @pl.when(j == 0)
        def _():
            dq_s[...] = jnp.zeros_like(dq_s)
            if has_cum:
                dc_s[...] = jnp.zeros_like(dc_s)

        def step(masked):
            kb = k_ref[0].astype(BF16)
            s = lax.dot_general(q_ref[0].astype(BF16), kb, _NT, preferred_element_type=F32) * scale
            if has_cum:
                s = s + cq_ref[0] - ck_ref[0]
            p = jnp.exp(jnp.minimum(s - lse_ref[0], 0.0))
            if masked:
                qpos = i * blk + lax.broadcasted_iota(jnp.int32, (blk, blk), 0)
                kpos = j * blk + lax.broadcasted_iota(jnp.int32, (blk, blk), 1)
                p = jnp.where((kpos <= qpos) & (kpos >= PAD_LEN), p, 0.0)
            dp = lax.dot_general(do_ref[0].astype(BF16), v_ref[0].astype(BF16), _NT,
                                 preferred_element_type=F32)
            ds = p * (dp - dl_ref[0])
            dq_s[...] += jnp.dot(ds.astype(BF16), kb, preferred_element_type=F32) * scale
            if has_cum:
                dc_s[...] += jnp.sum(ds, axis=1, keepdims=True)

        @pl.when((j == i) | (j == 0))
        def _():
            step(True)

        @pl.when((j < i) & (j > 0))
        def _():
            step(False)

        @pl.when(j == i)
        def _():
            dq_ref[0] = dq_s[...]
            if has_cum:
                dc_ref[0] = dc_s[...]

    kv_idx = lambda b, t, it, jt: (b, jt[t], 0)
    q_idx = lambda b, t, it, jt: (b, it[t], 0)
    in_specs = [pl.BlockSpec((1, blk, dk), q_idx), pl.BlockSpec((1, blk, dk), kv_idx),
                pl.BlockSpec((1, blk, dv), kv_idx), pl.BlockSpec((1, blk, dv), q_idx),
                pl.BlockSpec((1, blk, 1), q_idx), pl.BlockSpec((1, blk, 1), q_idx)]
    args = [q, k, v, do, lse_col, delta_col]
    if has_cum:
        in_specs += [pl.BlockSpec((1, blk, 1), q_idx),
                     pl.BlockSpec((1, 1, blk), lambda b, t, it, jt: (b, 0, jt[t]))]
        args += [cum_col, cum_row]
    out_specs = [pl.BlockSpec((1, blk, dk), q_idx)]
    out_shape = [jax.ShapeDtypeStruct((BH, T, dk), F32)]
    scratch = [pltpu.VMEM((blk, dk), F32)]
    if has_cum:
        out_specs.append(pl.BlockSpec((1, blk, 1), q_idx))
        out_shape.append(jax.ShapeDtypeStruct((BH, T, 1), F32))
        scratch.append(pltpu.VMEM((blk, 1), F32))
    i_tab, j_tab = _causal_pairs(n, by_query=True)
    return _pcall(
        body, name="attn_dq_cum" if has_cum else "attn_dq",
        grid_spec=pltpu.PrefetchScalarGridSpec(
            num_scalar_prefetch=2, grid=(BH, len(i_tab)), in_specs=in_specs, out_specs=out_specs,
            scratch_shapes=scratch),
        out_shape=out_shape,
        compiler_params=pltpu.CompilerParams(
            dimension_semantics=("parallel", "arbitrary"), vmem_limit_bytes=VMEM_LIMIT_BYTES),
    )(i_tab, j_tab, *args)


def _make_attention(scale, has_cum):
    def fold(t):
        return t.reshape((-1,) + t.shape[2:])

    def run_fwd(q, k, v, cum):
        B, H, T, _ = q.shape
        col = cum.reshape(B * H, T, 1) if has_cum else None
        row = cum.reshape(B * H, 1, T) if has_cum else None
        o, lse = _attn_fwd_call(fold(q), fold(k), fold(v), col, row, scale)
        return o.reshape(B, H, T, -1), lse

    @jax.custom_vjp
    def attn(q, k, v, cum):
        return run_fwd(q, k, v, cum)[0]

    def attn_fwd(q, k, v, cum):
        o, lse = run_fwd(q, k, v, cum)
        return o, (q, k, v, cum, o, lse)

    def attn_bwd(res, do):
        q, k, v, cum, o, lse = res
        B, H, T, _ = q.shape
        delta = jnp.sum(do * o, axis=-1).reshape(B * H, T, 1)
        col = cum.reshape(B * H, T, 1) if has_cum else None
        row = cum.reshape(B * H, 1, T) if has_cum else None
        qf, kf, vf, dof = fold(q), fold(k), fold(v), fold(do)
        outs = _attn_dkv_call(qf, kf, vf, dof, lse.reshape(B * H, 1, T), delta.reshape(B * H, 1, T),
                              row, col, scale)
        dqs = _attn_dq_call(qf, kf, vf, dof, lse, delta, col, row, scale)
        dcum = (outs[2] + dqs[1]).reshape(B, H, T) if has_cum else jnp.zeros_like(cum)
        return (dqs[0].reshape(q.shape), outs[0].reshape(k.shape), outs[1].reshape(v.shape), dcum)

    attn.defvjp(attn_fwd, attn_bwd)
    return attn


def _rmsnorm(x, g):
    return x * lax.rsqrt(jnp.mean(x * x, axis=-1, keepdims=True) + EPS) * g


def _l2norm(x):
    return x * lax.rsqrt(jnp.sum(x * x, axis=-1, keepdims=True) + EPS)


def _row_tile(rows, row_bytes, budget):
    for t in (2048, 1024, 512, 256, 128, 64, 32, 16, 8):
        if rows % t == 0 and t * row_bytes <= budget:
            return t
    return rows


ROW_BLOCK_BYTES = 2 * 1024 * 1024


def _rms_fwd_call(x, g):
    M, D = x.shape
    tr = _row_tile(M, 4 * D, ROW_BLOCK_BYTES)

    def body(x_ref, g_ref, y_ref, r_ref):
        xv = x_ref[...]
        r = lax.rsqrt(jnp.mean(xv * xv, axis=-1, keepdims=True) + EPS)
        y_ref[...] = xv * r * g_ref[...]
        r_ref[...] = r

    return _pcall(
        body, name="rmsnorm_fwd", grid=(M // tr,),
        in_specs=[pl.BlockSpec((tr, D), lambda i: (i, 0)), pl.BlockSpec((1, D), lambda i: (0, 0))],
        out_specs=[pl.BlockSpec((tr, D), lambda i: (i, 0)), pl.BlockSpec((tr, 1), lambda i: (i, 0))],
        out_shape=[jax.ShapeDtypeStruct((M, D), F32), jax.ShapeDtypeStruct((M, 1), F32)],
        compiler_params=pltpu.CompilerParams(dimension_semantics=("parallel",)),
    )(x, g)


def _rms_bwd_call(x, g, r, dy):
    M, D = x.shape
    tr = _row_tile(M, 4 * D, ROW_BLOCK_BYTES)

    def body(x_ref, g_ref, r_ref, dy_ref, dx_ref, dg_ref):
        i = pl.program_id(0)
        rv = r_ref[...]
        xh = x_ref[...] * rv
        dyv = dy_ref[...]
        dyg = dyv * g_ref[...]
        dx_ref[...] = rv * (dyg - xh * jnp.mean(dyg * xh, axis=-1, keepdims=True))
        part = jnp.sum(dyv * xh, axis=0, keepdims=True)

        @pl.when(i == 0)
        def _():
            dg_ref[...] = part

        @pl.when(i > 0)
        def _():
            dg_ref[...] += part

    row = pl.BlockSpec((tr, D), lambda i: (i, 0))
    return _pcall(
        body, name="rmsnorm_bwd", grid=(M // tr,),
        in_specs=[row, pl.BlockSpec((1, D), lambda i: (0, 0)), pl.BlockSpec((tr, 1), lambda i: (i, 0)), row],
        out_specs=[row, pl.BlockSpec((1, D), lambda i: (0, 0))],
        out_shape=[jax.ShapeDtypeStruct((M, D), F32), jax.ShapeDtypeStruct((1, D), F32)],
        compiler_params=pltpu.CompilerParams(dimension_semantics=("arbitrary",)),
    )(x, g, r, dy)


@jax.custom_vjp
def _rmsnorm2d(x, g):
    return _rms_fwd_call(x, g)[0]


def _rmsnorm2d_fwd(x, g):
    y, r = _rms_fwd_call(x, g)
    return y, (x, g, r)


def _rmsnorm2d_bwd(res, dy):
    x, g, r = res
    return _rms_bwd_call(x, g, r, dy)


_rmsnorm2d.defvjp(_rmsnorm2d_fwd, _rmsnorm2d_bwd)


def _rmsnorm_rows(x, g):
    D = x.shape[-1]
    return _rmsnorm2d(x.reshape(-1, D), g.reshape(1, D)).reshape(x.shape)


def _glu_fwd_call(gu):
    M, F2 = gu.shape
    F = F2 // 2
    tr = _row_tile(M, 4 * F2, 2 * ROW_BLOCK_BYTES)

    def body(g_ref, u_ref, o_ref):
        gv = g_ref[...].astype(F32)
        o_ref[...] = (gv * jax.nn.sigmoid(gv) * u_ref[...].astype(F32)).astype(o_ref.dtype)

    return _pcall(
        body, name="swiglu_fwd", grid=(M // tr,),
        in_specs=[pl.BlockSpec((tr, F), lambda i: (i, 0)), pl.BlockSpec((tr, F), lambda i: (i, 1))],
        out_specs=pl.BlockSpec((tr, F), lambda i: (i, 0)),
        out_shape=jax.ShapeDtypeStruct((M, F), gu.dtype),
        compiler_params=pltpu.CompilerParams(dimension_semantics=("parallel",),
                                             vmem_limit_bytes=VMEM_LIMIT_BYTES),
    )(gu, gu)


def _glu_bwd_call(gu, da):
    M, F2 = gu.shape
    F = F2 // 2
    tr = _row_tile(M, 4 * F2, 2 * ROW_BLOCK_BYTES)

    def body(g_ref, u_ref, da_ref, o_ref):
        gv = g_ref[...].astype(F32)
        s = jax.nn.sigmoid(gv)
        dav = da_ref[...].astype(F32)
        o_ref[:, :F] = (dav * u_ref[...].astype(F32) * (s * (1.0 + gv * (1.0 - s)))).astype(o_ref.dtype)
        o_ref[:, F:] = (dav * (gv * s)).astype(o_ref.dtype)

    return _pcall(
        body, name="swiglu_bwd", grid=(M // tr,),
        in_specs=[pl.BlockSpec((tr, F), lambda i: (i, 0)), pl.BlockSpec((tr, F), lambda i: (i, 1)),
                  pl.BlockSpec((tr, F), lambda i: (i, 0))],
        out_specs=pl.BlockSpec((tr, F2), lambda i: (i, 0)),
        out_shape=jax.ShapeDtypeStruct((M, F2), gu.dtype),
        compiler_params=pltpu.CompilerParams(dimension_semantics=("parallel",),
                                             vmem_limit_bytes=VMEM_LIMIT_BYTES),
    )(gu, gu, da)


@jax.custom_vjp
def _glu(gu):
    return _glu_fwd_call(gu)


def _glu_fwd(gu):
    return _glu_fwd_call(gu), gu


def _glu_bwd(gu, da):
    return (_glu_bwd_call(gu, da),)


_glu.defvjp(_glu_fwd, _glu_bwd)


def _swiglu(h, wi, wo):
    lead, D = h.shape[:-1], h.shape[-1]
    gu = dense(h.reshape(-1, D), wi[0], wi[1], BF16)
    return dense(_glu(gu), wo[0], wo[1], F32).reshape(lead + (wo[0].shape[1],))


def _causal_dwconv(x, w):
    K, C = w.shape
    return lax.conv_general_dilated(
        x, w[:, None, :], window_strides=(1,), padding=[(K - 1, 0)],
        dimension_numbers=("NWC", "WIO", "NWC"), feature_group_count=C)


def _rope(x, cos, sin):
    half = x.shape[-1] // 2
    x1, x2 = x[..., :half], x[..., half:]
    return jnp.concatenate([x1 * cos - x2 * sin, x2 * cos + x1 * sin], axis=-1)


def _fox_branch(p, b_f):
    B, T, _ = p.shape
    qkv = p[..., OFF_FOX_QKV:OFF_FOX_F].reshape(B, T, 3, FOX_HEADS, FOX_DH)
    q = qkv[:, :, 0].transpose(0, 2, 1, 3)
    k = qkv[:, :, 1].transpose(0, 2, 1, 3)
    v = qkv[:, :, 2].transpose(0, 2, 1, 3)
    log_f = jax.nn.log_sigmoid(p[..., OFF_FOX_F:OFF_MLA_CQ] + b_f)
    cum = jnp.cumsum(log_f, axis=1).transpose(0, 2, 1)
    o = _make_attention(FOX_DH ** -0.5, True)(q, k, v, cum)
    return o.transpose(0, 2, 1, 3).reshape(B, T, FOX_HEADS * FOX_DH)


def _mla_branch(p, g_qn, w_q_up, g_kvn, w_kv_up, cos, sin):
    B, T, _ = p.shape
    cq = _rmsnorm(p[..., OFF_MLA_CQ:OFF_MLA_CKV], g_qn)
    q = (cq @ w_q_up).reshape(B, T, MLA_HEADS, MLA_NOPE + MLA_ROPE)
    ckv = _rmsnorm(p[..., OFF_MLA_CKV:OFF_MLA_KR], g_kvn)
    kv = (ckv @ w_kv_up).reshape(B, T, MLA_HEADS, MLA_NOPE + MLA_DV)
    k_rope = _rope(p[..., OFF_MLA_KR:OFF_GDN_QKV], cos, sin)
    q_rope = _rope(q[..., MLA_NOPE:], cos[:, None], sin[:, None])
    q = jnp.concatenate([q[..., :MLA_NOPE], q_rope], axis=-1)
    k = jnp.concatenate([kv[..., :MLA_NOPE],
                         jnp.broadcast_to(k_rope[:, :, None], (B, T, MLA_HEADS, MLA_ROPE))], axis=-1)
    v = kv[..., MLA_NOPE:]
    o = _make_attention((MLA_NOPE + MLA_ROPE) ** -0.5, False)(
        q.transpose(0, 2, 1, 3), k.transpose(0, 2, 1, 3), v.transpose(0, 2, 1, 3),
        jnp.zeros((B, MLA_HEADS, T), F32))
    return o.transpose(0, 2, 1, 3).reshape(B, T, MLA_HEADS * MLA_DV)


def _bmm(a, b):
    return lax.dot_general(a.astype(BF16), b.astype(BF16), (((2,), (1,)), ((0,), (0,))),
                           preferred_element_type=F32)


def _bmm_nt(a, b):
    return lax.dot_general(a.astype(BF16), b.astype(BF16), (((2,), (2,)), ((0,), (0,))),
                           preferred_element_type=F32)


def _chunk_spec(shape, index):
    return pl.BlockSpec((1,) + tuple(shape[1:]), lambda n: (index(n), 0, 0, 0))


def _gdn_scan_fwd_call(qd, kdt, w, u, qk, gl):
    nc, BH, C, DK = qd.shape
    DV = u.shape[3]

    def body(qd_ref, kdt_ref, w_ref, u_ref, qk_ref, gl_ref, o_ref, s_ref, vn_ref, state):
        @pl.when(pl.program_id(0) == 0)
        def _():
            state[...] = jnp.zeros_like(state)

        s = state[...]
        s_ref[0] = s
        vn = u_ref[0] - _bmm(w_ref[0], s)
        vn_ref[0] = vn
        o_ref[0] = _bmm(qd_ref[0], s) + _bmm(qk_ref[0], vn)
        state[...] = s * gl_ref[0] + _bmm(kdt_ref[0], vn)

    fwd = lambda n: n
    outs = [(nc, BH, C, DV), (nc, BH, DK, DV), (nc, BH, C, DV)]
    return _pcall(
        body, name="gdn_scan_fwd", grid=(nc,),
        in_specs=[_chunk_spec(t.shape, fwd) for t in (qd, kdt, w, u, qk, gl)],
        out_specs=[_chunk_spec(s, fwd) for s in outs],
        out_shape=[jax.ShapeDtypeStruct(s, F32) for s in outs],
        scratch_shapes=[pltpu.VMEM((BH, DK, DV), F32)],
        compiler_params=pltpu.CompilerParams(dimension_semantics=("arbitrary",)),
    )(qd, kdt, w, u, qk, gl)


def _gdn_scan_bwd_call(do, qdt, kd, wt, qkt, gl, s_all, vn):
    nc, BH, C, DV = do.shape
    DK = kd.shape[3]

    def body(do_ref, qdt_ref, kd_ref, wt_ref, qkt_ref, gl_ref, s_ref, vn_ref,
             dqd_ref, dkd_ref, dw_ref, du_ref, dqk_ref, dgl_ref, dstate):
        @pl.when(pl.program_id(0) == 0)
        def _():
            dstate[...] = jnp.zeros_like(dstate)

        ds = dstate[...]
        s, v, dov = s_ref[0], vn_ref[0], do_ref[0]
        dkd_ref[0] = _bmm_nt(v, ds)
        dgl_ref[0] = s * ds
        dqd_ref[0] = _bmm_nt(dov, s)
        dqk_ref[0] = _bmm_nt(dov, v)
        dv = _bmm(kd_ref[0], ds) + _bmm(qkt_ref[0], dov)
        du_ref[0] = dv
        dw_ref[0] = -_bmm_nt(dv, s)
        dstate[...] = ds * gl_ref[0] + _bmm(qdt_ref[0], dov) - _bmm(wt_ref[0], dv)

    rev = lambda n: nc - 1 - n
    outs = [(nc, BH, C, DK)] * 3 + [(nc, BH, C, DV), (nc, BH, C, C), (nc, BH, DK, DV)]
    return _pcall(
        body, name="gdn_scan_bwd", grid=(nc,),
        in_specs=[_chunk_spec(t.shape, rev) for t in (do, qdt, kd, wt, qkt, gl, s_all, vn)],
        out_specs=[_chunk_spec(s, rev) for s in outs],
        out_shape=[jax.ShapeDtypeStruct(s, F32) for s in outs],
        scratch_shapes=[pltpu.VMEM((BH, DK, DV), F32)],
        compiler_params=pltpu.CompilerParams(dimension_semantics=("arbitrary",)),
    )(do, qdt, kd, wt, qkt, gl, s_all, vn)


def _gl_rows(gl, dv):
    return jnp.broadcast_to(gl[:, :, None, None], gl.shape + (1, dv))


@jax.custom_vjp
def _gdn_scan(qd, kd, w, u, qk, gl):
    return _gdn_scan_fwd_call(qd, jnp.swapaxes(kd, 2, 3), w, u, qk, _gl_rows(gl, u.shape[3]))[0]


def _gdn_scan_fwd(qd, kd, w, u, qk, gl):
    o, s_all, vn = _gdn_scan_fwd_call(qd, jnp.swapaxes(kd, 2, 3), w, u, qk, _gl_rows(gl, u.shape[3]))
    return o, (qd, kd, w, qk, gl, s_all, vn)


def _gdn_scan_bwd(res, do):
    qd, kd, w, qk, gl, s_all, vn = res
    dqd, dkd, dw, du, dqk, dgl = _gdn_scan_bwd_call(
        do, jnp.swapaxes(qd, 2, 3), kd, jnp.swapaxes(w, 2, 3), jnp.swapaxes(qk, 2, 3),
        _gl_rows(gl, do.shape[3]), s_all, vn)
    return dqd, dkd, dw, du, dqk, jnp.sum(dgl, axis=(2, 3))


_gdn_scan.defvjp(_gdn_scan_fwd, _gdn_scan_bwd)


def _gdn_branch(p, conv_w, a_log, dt_bias, g_on):
    B, T, _ = p.shape
    H, DK, DV, C = GDN_HEADS, GDN_DK, GDN_DV, GDN_CHUNK
    qkv = jax.nn.silu(_causal_dwconv(p[..., OFF_GDN_QKV:OFF_GDN_A], conv_w))
    q = _l2norm(qkv[..., :H * DK].reshape(B, T, H, DK)) * DK ** -0.5
    k = _l2norm(qkv[..., H * DK:2 * H * DK].reshape(B, T, H, DK))
    v = qkv[..., 2 * H * DK:].reshape(B, T, H, DV)
    beta = jax.nn.sigmoid(p[..., OFF_GDN_B:OFF_GDN_G])
    g = -jnp.exp(a_log) * jax.nn.softplus(p[..., OFF_GDN_A:OFF_GDN_B] + dt_bias)
    nc = T // C

    def chunks(t):
        return jnp.moveaxis(t, 2, 1).reshape((B, H, nc, C) + t.shape[3:])

    q, k, v, beta, g = chunks(q), chunks(k), chunks(v), chunks(beta), chunks(g)
    G = jnp.cumsum(g, axis=-1)
    idx = jnp.arange(C)
    strict = idx[:, None] > idx[None, :]
    incl = idx[:, None] >= idx[None, :]
    decay = jnp.exp(jnp.where(incl, G[..., :, None] - G[..., None, :], NEG_INF))
    kb = k * beta[..., None]
    vb = v * beta[..., None]
    m = jnp.eye(C, dtype=F32) + jnp.where(
        strict, jnp.einsum("bhnik,bhnjk->bhnij", kb, k) * decay, 0.0)
    rhs = jnp.concatenate([kb * jnp.exp(G)[..., None], vb], axis=-1)
    sol = lax.linalg.triangular_solve(m, rhs, left_side=True, lower=True, unit_diagonal=True)
    w, u = sol[..., :DK], sol[..., DK:]
    qk = jnp.where(incl, jnp.einsum("bhnik,bhnjk->bhnij", q, k) * decay, 0.0)
    q_dec = q * jnp.exp(G)[..., None]
    k_dec = k * jnp.exp(G[..., -1:] - G)[..., None]
    g_last = jnp.exp(G[..., -1])
    def chunk_major(t):
        return jnp.moveaxis(t, 2, 0).reshape((nc, B * H) + t.shape[3:])

    o = _gdn_scan(chunk_major(q_dec), chunk_major(k_dec), chunk_major(w), chunk_major(u),
                  chunk_major(qk), chunk_major(g_last))
    o = jnp.moveaxis(o.reshape(nc, B, H, C, DV), 0, 2).reshape(B, H, T, DV).transpose(0, 2, 1, 3)
    gate = jax.nn.silu(p[..., OFF_GDN_G:OFF_LRU]).reshape(B, T, H, DV)
    o = _rmsnorm(o, g_on) * gate
    return o.reshape(B, T, H * DV)


def _rglru_branch(p, valid, conv_w, conv_b, w_a, b_a, w_x, b_x, lam):
    B, T, _ = p.shape
    xr = _causal_dwconv(p[..., OFF_LRU:N_IN], conv_w) + conv_b
    xr = jnp.where(valid[None, :, None], xr, 0)
    xb = xr.reshape(B, T, LRU_BLOCKS, LRU_WIDTH // LRU_BLOCKS)
    r = jax.nn.sigmoid(jnp.einsum("btni,nij->btnj", xb, w_a).reshape(B, T, LRU_WIDTH) + b_a)
    ig = jax.nn.sigmoid(jnp.einsum("btni,nij->btnj", xb, w_x).reshape(B, T, LRU_WIDTH) + b_x)
    log_a = -LRU_C * r * jax.nn.softplus(-lam)
    a = jnp.exp(log_a)
    b = jnp.sqrt(-jnp.expm1(2.0 * log_a)) * ig * xr

    def combine(e1, e2):
        return (e1[0] * e2[0], e2[0] * e1[1] + e2[1])

    _, h = lax.associative_scan(combine, (a, b), axis=1)
    return h


def _mixer(u, valid, cos, sin, w, wb, l):
    def pair(name, *idx):
        return wb[name][(l,) + idx], w[name][(l,) + idx]

    B, T, D = u.shape
    cuts = (OFF_FOX_QKV, OFF_FOX_F, OFF_MLA_CQ, OFF_MLA_CKV, OFF_MLA_KR, OFF_GDN_QKV, OFF_GDN_A, OFF_GDN_B,
            OFF_GDN_G, OFF_LRU, N_IN)
    bounds = tuple(zip(cuts[:-1], cuts[1:]))
    parts = _split_cols(dense_nd(u, *pair("w_in"), keep_pad=True), bounds)
    p = _Cols({b: s.reshape(B, T, -1) for b, s in zip(bounds, parts)}, (B, T, N_IN))
    ys = (_fox_branch(p, w["fox_bf"][l]),
          _mla_branch(p, w["mla_gq"][l], w["mla_wq"][l], w["mla_gkv"][l], w["mla_wkv"][l], cos, sin),
          _gdn_branch(p, w["gdn_conv"][l], w["gdn_alog"][l], w["gdn_dtb"][l], w["gdn_gon"][l]),
          _rglru_branch(p, valid, w["lru_conv"][l], w["lru_conv_b"][l], w["lru_wa"][l], w["lru_ba"][l],
                        w["lru_wx"][l], w["lru_bx"][l], w["lru_lam"][l]))
    w_gate = [jnp.moveaxis(t, 0, 1).reshape(D, N_BRANCH * D) for t in pair("w_gate")]
    gates = jax.nn.sigmoid(dense_nd(u, *w_gate, keep_pad=True) + w["b_gate"][l].reshape(-1))
    gates = _split_cols(gates, tuple((n * D, (n + 1) * D) for n in range(N_BRANCH)))
    merged = gates[0].reshape(B, T, D) * dense_nd(ys[0], *pair("w_branch", 0))
    for n in range(1, N_BRANCH):
        merged = merged + gates[n].reshape(B, T, D) * dense_nd(ys[n], *pair("w_branch", n))
    return dense_nd(merged, *pair("w_out"))


def _local_loss(w, x, loss_target, wb):
    B, S, D = x.shape
    T = BLOCK + S
    h = jnp.concatenate([jnp.zeros((B, PAD_LEN, D), F32),
                         jnp.broadcast_to(w["meta"][None], (B, N_META, D)), x], axis=1)
    pos = jnp.arange(T)
    valid = pos >= PAD_LEN
    rel = (pos - PAD_LEN).astype(F32)
    inv_freq = ROPE_BASE ** (-(jnp.arange(0, MLA_ROPE, 2, dtype=F32) / MLA_ROPE))
    ang = rel[:, None] * inv_freq[None, :]
    cos, sin = jnp.cos(ang), jnp.sin(ang)
    def pair(name, l):
        return wb[name][l], w[name][l]

    for l in range(DEPTH):
        h = h + 0.5 * _swiglu(_rmsnorm_rows(h, w["ln_ffn1"][l]), pair("ffn1_wi", l), pair("ffn1_wo", l))
        u = jnp.where(valid[None, :, None], _rmsnorm_rows(h, w["ln_mix"][l]), 0)
        h = h + _mixer(u, valid, cos, sin, w, wb, l)
        h = h + 0.5 * _swiglu(_rmsnorm_rows(h, w["ln_ffn2"][l]), pair("ffn2_wi", l), pair("ffn2_wo", l))
    y = _rmsnorm_rows(h, w["ln_final"])[:, BLOCK:]
    err = jnp.square(y - loss_target)
    return 0.5 * jnp.sum(jnp.mean(err, axis=-1))


MESH = pl.DeviceIdType.MESH
HBM = pl.BlockSpec(memory_space=pl.ANY)


def _place():
    x, y, c = lax.axis_index("x"), lax.axis_index("y"), lax.axis_index("c")
    return x, y, c


def _other_chip(x, y, r):
    return (1 - x if r & 2 else x), (1 - y if r & 1 else y)


def _gather_chips(shard):
    R, C = shard.shape
    H = R // 2

    def body(x_ref, out_ref, send_sems, recv_sems, local_sem):
        x, y, c = _place()
        me = 2 * x + y

        def rows(chip, half):
            return out_ref.at[chip, pl.ds(half * H, H), :]

        def copy(sem, src, dst, to):
            return pltpu.make_async_remote_copy(src_ref=src, dst_ref=dst, send_sem=send_sems.at[sem],
                                                recv_sem=recv_sems.at[sem], device_id=to, device_id_type=MESH)

        mine = pltpu.make_async_copy(x_ref, out_ref.at[me], local_sem)
        mine.start()
        started = []
        for r in (1, 2, 3):
            ox, oy = _other_chip(x, y, r)
            cp = copy(r - 1, x_ref.at[pl.ds(c * H, H), :], rows(me, c), (ox, oy, c))
            cp.start()
            started.append(cp)
        for r in (1, 2, 3):
            ox, oy = _other_chip(x, y, r)
            src = 2 * ox + oy
            copy(r - 1, rows(src, c), rows(src, c), (x, y, c)).wait_recv()
            fw = copy(2 + r, rows(src, c), rows(src, c), (x, y, 1 - c))
            fw.start()
            started.append(fw)
        for r in (1, 2, 3):
            ox, oy = _other_chip(x, y, r)
            src = 2 * ox + oy
            copy(2 + r, rows(src, 1 - c), rows(src, 1 - c), (x, y, c)).wait_recv()
        for cp in started:
            cp.wait_send()
        mine.wait()

    return _pcall(
        body, name="gather_chips", in_specs=[HBM], out_specs=HBM,
        out_shape=jax.ShapeDtypeStruct((N_CHIPS, R, C), shard.dtype),
        scratch_shapes=[pltpu.SemaphoreType.DMA((6,)), pltpu.SemaphoreType.DMA((6,)),
                        pltpu.SemaphoreType.DMA],
    )(shard)


def _window(ref, lead, axis, chip, width):
    idx = [slice(None)] * len(ref.shape)
    if lead is not None:
        idx[0] = lead
    idx[axis] = pl.ds(chip * width, width)
    return ref.at[tuple(idx)]


def _gather_layers(shards, axes):
    n_t = len(shards)
    widths = [s.shape[a] for s, a in zip(shards, axes)]
    fulls = [s.shape[:a] + (N_CHIPS * s.shape[a],) + s.shape[a + 1:] for s, a in zip(shards, axes)]

    def body(*refs):
        x_refs, out_refs = refs[:n_t], refs[n_t:2 * n_t]
        send_sems, recv_sems = refs[2 * n_t:]
        x, y, c = _place()
        me = 2 * x + y

        def win(t, lead, chip):
            return _window(out_refs[t], lead, axes[t], chip, widths[t])

        def copy(sem, src, dst, to):
            return pltpu.make_async_remote_copy(src_ref=src, dst_ref=dst, send_sem=send_sems.at[sem],
                                                recv_sem=recv_sems.at[sem], device_id=to, device_id_type=MESH)

        started = []
        for t in range(n_t):
            cp = copy(7 * t + 6, x_refs[t], win(t, None, me), (x, y, 1 - c))
            cp.start()
            started.append(cp)
        for r in (1, 2, 3):
            ox, oy = _other_chip(x, y, r)
            for t in range(n_t):
                cp = copy(7 * t + r - 1, x_refs[t].at[c], win(t, c, me), (ox, oy, c))
                cp.start()
                started.append(cp)
        for r in (1, 2, 3):
            ox, oy = _other_chip(x, y, r)
            src = 2 * ox + oy
            for t in range(n_t):
                copy(7 * t + r - 1, win(t, c, src), win(t, c, src), (x, y, c)).wait_recv()
                fw = copy(7 * t + 2 + r, win(t, c, src), win(t, c, src), (x, y, 1 - c))
                fw.start()
                started.append(fw)
        for r in (1, 2, 3):
            ox, oy = _other_chip(x, y, r)
            src = 2 * ox + oy
            for t in range(n_t):
                copy(7 * t + 2 + r, win(t, 1 - c, src), win(t, 1 - c, src), (x, y, c)).wait_recv()
        for t in range(n_t):
            copy(7 * t + 6, win(t, None, me), win(t, None, me), (x, y, c)).wait_recv()
        for cp in started:
            cp.wait_send()

    return _pcall(
        body, name="gather_layers", in_specs=[HBM] * n_t, out_specs=[HBM] * n_t,
        out_shape=[jax.ShapeDtypeStruct(f, s.dtype) for f, s in zip(fulls, shards)],
        scratch_shapes=[pltpu.SemaphoreType.DMA((7 * n_t,)), pltpu.SemaphoreType.DMA((7 * n_t,))],
    )(*shards)


def _swap_layers(gs):
    n_t = len(gs)

    def body(*refs):
        g_refs, out_refs = refs[:n_t], refs[n_t:2 * n_t]
        send_sems, recv_sems = refs[2 * n_t:]
        x, y, c = _place()
        cps = []
        for t in range(n_t):
            cp = pltpu.make_async_remote_copy(
                src_ref=g_refs[t].at[1 - c], dst_ref=out_refs[t], send_sem=send_sems.at[t],
                recv_sem=recv_sems.at[t], device_id=(x, y, 1 - c), device_id_type=MESH)
            cp.start()
            cps.append(cp)
        for cp in cps:
            cp.wait()

    return _pcall(
        body, name="swap_layers", in_specs=[HBM] * n_t, out_specs=[HBM] * n_t,
        out_shape=[jax.ShapeDtypeStruct(g.shape[1:], g.dtype) for g in gs],
        scratch_shapes=[pltpu.SemaphoreType.DMA((n_t,)), pltpu.SemaphoreType.DMA((n_t,))],
    )(*gs)


def _add_layer(g, other, c):
    shape = other.shape
    last = shape[-1]
    rows = math.prod(shape[:-1])
    tr = _row_tile(rows, 4 * last, ROW_BLOCK_BYTES)

    def body(c_ref, a_ref, b_ref, o_ref):
        o_ref[...] = (a_ref[0] + b_ref[...]).astype(BF16)

    out = _pcall(
        body, name="add_layer",
        grid_spec=pltpu.PrefetchScalarGridSpec(
            num_scalar_prefetch=1, grid=(rows // tr,),
            in_specs=[pl.BlockSpec((1, tr, last), lambda i, c_ref: (c_ref[0], i, 0)),
                      pl.BlockSpec((tr, last), lambda i, c_ref: (i, 0))],
            out_specs=pl.BlockSpec((tr, last), lambda i, c_ref: (i, 0))),
        out_shape=jax.ShapeDtypeStruct((rows, last), BF16),
        compiler_params=pltpu.CompilerParams(dimension_semantics=("parallel",)),
    )(c.reshape(1).astype(jnp.int32), g.reshape(2, rows, last), other.reshape(rows, last))
    return out.reshape(shape)


def _scatter_layers(ps, axes):
    n_t = len(ps)
    widths = [p.shape[a] // N_CHIPS for p, a in zip(ps, axes)]
    wins = [p.shape[:a] + (w,) + p.shape[a + 1:] for p, a, w in zip(ps, axes, widths)]

    def body(*refs):
        p_refs, out_refs = refs[:n_t], refs[n_t:2 * n_t]
        send_sems, recv_sems = refs[2 * n_t:]
        x, y, c = _place()
        cps = []
        for r in (1, 2, 3):
            ox, oy = _other_chip(x, y, r)
            for t in range(n_t):
                cp = pltpu.make_async_remote_copy(
                    src_ref=_window(p_refs[t], None, axes[t], 2 * ox + oy, widths[t]), dst_ref=out_refs[t].at[r - 1],
                    send_sem=send_sems.at[3 * t + r - 1], recv_sem=recv_sems.at[3 * t + r - 1],
                    device_id=(ox, oy, c), device_id_type=MESH)
                cp.start()
                cps.append(cp)
        for cp in cps:
            cp.wait()

    return _pcall(
        body, name="scatter_layers", in_specs=[HBM] * n_t, out_specs=[HBM] * n_t,
        out_shape=[jax.ShapeDtypeStruct((N_CHIPS - 1,) + w, p.dtype) for w, p in zip(wins, ps)],
        scratch_shapes=[pltpu.SemaphoreType.DMA((3 * n_t,)), pltpu.SemaphoreType.DMA((3 * n_t,))],
    )(*ps)


def _sum_chips(p, q, axis, me):
    win = q.shape[1:]
    C = win[-1]
    H = math.prod(win[:-1])
    if axis == p.ndim - 1:
        tr = _row_tile(H, 4 * C, ROW_BLOCK_BYTES)
        grid = H // tr
        p = p.reshape(H, N_CHIPS * C)
        p_spec = pl.BlockSpec((tr, C), lambda i, me_ref: (i, me_ref[0]))
    else:
        pre, inner = math.prod(p.shape[:axis]), math.prod(win[axis:-1])
        tr = _row_tile(inner, 4 * C, ROW_BLOCK_BYTES)
        nb = inner // tr
        grid = pre * nb
        p = p.reshape(pre, N_CHIPS, inner, C)
        p_spec = pl.BlockSpec((1, 1, tr, C), lambda i, me_ref: (i // nb, me_ref[0], i % nb, 0))

    def body(me_ref, p_ref, q0, q1, q2, q3, o_ref):
        own = p_ref[...].reshape(tr, C).astype(F32)
        terms = [jnp.where(me_ref[0] == chip, own, qr[0].astype(F32)) for chip, qr in enumerate((q0, q1, q2, q3))]
        o_ref[...] = ((terms[0] + terms[1]) + terms[2]) + terms[3]

    def q_spec(chip):
        return pl.BlockSpec(
            (1, tr, C), lambda i, me_ref: (jnp.maximum(jnp.bitwise_xor(me_ref[0], chip), 1) - 1, i, 0))

    q = q.reshape(N_CHIPS - 1, H, C)
    return _pcall(
        body, name="sum_chips",
        grid_spec=pltpu.PrefetchScalarGridSpec(
            num_scalar_prefetch=1, grid=(grid,),
            in_specs=[p_spec, q_spec(0), q_spec(1), q_spec(2), q_spec(3)],
            out_specs=pl.BlockSpec((tr, C), lambda i, me_ref: (i, 0))),
        out_shape=jax.ShapeDtypeStruct((H, C), F32),
        compiler_params=pltpu.CompilerParams(dimension_semantics=("parallel",)),
    )(me.reshape(1).astype(jnp.int32), p, q, q, q, q).reshape(win)


def _send_layers(rs):
    n_t = len(rs)

    def body(*refs):
        r_refs, out_refs = refs[:n_t], refs[n_t:2 * n_t]
        send_sems, recv_sems = refs[2 * n_t:]
        x, y, c = _place()
        cps = []
        for t in range(n_t):
            cp = pltpu.make_async_remote_copy(
                src_ref=r_refs[t], dst_ref=out_refs[t], send_sem=send_sems.at[t],
                recv_sem=recv_sems.at[t], device_id=(x, y, 1 - c), device_id_type=MESH)
            cp.start()
            cps.append(cp)
        for cp in cps:
            cp.wait()

    return _pcall(
        body, name="send_layers", in_specs=[HBM] * n_t, out_specs=[HBM] * n_t,
        out_shape=[jax.ShapeDtypeStruct(r.shape, r.dtype) for r in rs],
        scratch_shapes=[pltpu.SemaphoreType.DMA((n_t,)), pltpu.SemaphoreType.DMA((n_t,))],
    )(*rs)


def _reduce_scatter_layers(gs, axes):
    x, y, c = _place()
    others = _swap_layers(gs)
    ps = [_add_layer(g, o, c) for g, o in zip(gs, others)]
    qs = _scatter_layers(ps, [a - 1 for a in axes])
    rs = [_sum_chips(p, q, a - 1, 2 * x + y) for p, q, a in zip(ps, qs, axes)]
    theirs = _send_layers(rs)
    return [jnp.where(c == 0, jnp.stack([r, o]), jnp.stack([o, r])) for r, o in zip(rs, theirs)]


def _allgather_devices(flat):
    R, C = flat.shape

    def body(x_ref, out_ref, send_sems, recv_sems, local_sem):
        x, y, c = _place()
        me = 4 * x + 2 * y + c
        mine = pltpu.make_async_copy(x_ref, out_ref.at[me], local_sem)
        mine.start()
        cps = []
        for m in range(1, 8):
            ox, oy = _other_chip(x, y, m >> 1)
            oc = 1 - c if m & 1 else c
            cp = pltpu.make_async_remote_copy(
                src_ref=x_ref, dst_ref=out_ref.at[me], send_sem=send_sems.at[m - 1],
                recv_sem=recv_sems.at[m - 1], device_id=(ox, oy, oc), device_id_type=MESH)
            cp.start()
            cps.append(cp)
        for cp in cps:
            cp.wait()
        mine.wait()

    return _pcall(
        body, name="allgather_devices", in_specs=[HBM], out_specs=HBM,
        out_shape=jax.ShapeDtypeStruct((8, R, C), flat.dtype),
        scratch_shapes=[pltpu.SemaphoreType.DMA((7,)), pltpu.SemaphoreType.DMA((7,)),
                        pltpu.SemaphoreType.DMA],
    )(flat)


def _sum_devices(slots):
    _, R, C = slots.shape
    tr = _row_tile(R, 4 * C, ROW_BLOCK_BYTES // 4)

    def body(*refs):
        o_ref = refs[8]
        acc = refs[0][0]
        for d in range(1, 8):
            acc = acc + refs[d][0]
        o_ref[...] = acc

    def spec(d):
        return pl.BlockSpec((1, tr, C), lambda i: (d, i, 0))

    return _pcall(
        body, name="sum_devices", grid=(R // tr,), in_specs=[spec(d) for d in range(8)],
        out_specs=pl.BlockSpec((tr, C), lambda i: (i, 0)),
        out_shape=jax.ShapeDtypeStruct((R, C), F32),
        compiler_params=pltpu.CompilerParams(dimension_semantics=("parallel",)),
    )(*([slots] * 8))


def _adamw(w, g, m, v):
    shape = w.shape
    C = shape[-1]
    R = math.prod(shape[:-1])
    w, g, m, v = (t.reshape(R, C) for t in (w, g, m, v))
    tr = _row_tile(R, 4 * C, ROW_BLOCK_BYTES // 2)
    c1 = 1.0 - ADAM_B1 ** ADAM_STEP
    c2 = 1.0 - ADAM_B2 ** ADAM_STEP

    def body(w_ref, g_ref, m_ref, v_ref, d_ref, nm_ref, nv_ref):
        gg = g_ref[...]
        nm = ADAM_B1 * m_ref[...] + (1.0 - ADAM_B1) * gg
        nv = ADAM_B2 * v_ref[...] + (1.0 - ADAM_B2) * jnp.square(gg)
        d_ref[...] = -ADAM_LR * ((nm / c1) / (jnp.sqrt(nv / c2) + ADAM_EPS) + ADAM_WD * w_ref[...])
        nm_ref[...] = nm
        nv_ref[...] = nv

    spec = pl.BlockSpec((tr, C), lambda i: (i, 0))
    outs = _pcall(
        body, name="adamw", grid=(R // tr,), in_specs=[spec] * 4, out_specs=[spec] * 3,
        out_shape=[jax.ShapeDtypeStruct((R, C), F32)] * 3,
        compiler_params=pltpu.CompilerParams(dimension_semantics=("parallel",)),
    )(w, g, m, v)
    return [o.reshape(shape) for o in outs]


def _to_flat(parts):
    flat = jnp.concatenate([p.reshape(-1) for p in parts])
    unit = FLAT_COLS * FLAT_ROW_ALIGN
    pad = (-flat.shape[0]) % unit
    if pad:
        flat = jnp.concatenate([flat, jnp.zeros((pad,), flat.dtype)])
    return flat.reshape(-1, FLAT_COLS)


def _from_flat(flat, shapes):
    flat = flat.reshape(-1)
    out, off = [], 0
    for s in shapes:
        n = math.prod(s)
        out.append(flat[off:off + n].reshape(s))
        off += n
    return out


def kernel(x, meta, ln_ffn1, ffn1_wi, ffn1_wo, ln_mix, w_in, fox_bf, mla_gq, mla_wq, mla_gkv, mla_wkv, gdn_conv, gdn_alog, gdn_dtb, gdn_gon, lru_conv, lru_conv_b, lru_wa, lru_ba, lru_wx, lru_bx, lru_lam, w_gate, b_gate, w_branch, w_out, ln_ffn2, ffn2_wi, ffn2_wo, ln_final, loss_target, m_meta, m_ln_ffn1, m_ffn1_wi, m_ffn1_wo, m_ln_mix, m_w_in, m_fox_bf, m_mla_gq, m_mla_wq, m_mla_gkv, m_mla_wkv, m_gdn_conv, m_gdn_alog, m_gdn_dtb, m_gdn_gon, m_lru_conv, m_lru_conv_b, m_lru_wa, m_lru_ba, m_lru_wx, m_lru_bx, m_lru_lam, m_w_gate, m_b_gate, m_w_branch, m_w_out, m_ln_ffn2, m_ffn2_wi, m_ffn2_wo, m_ln_final, v_meta, v_ln_ffn1, v_ffn1_wi, v_ffn1_wo, v_ln_mix, v_w_in, v_fox_bf, v_mla_gq, v_mla_wq, v_mla_gkv, v_mla_wkv, v_gdn_conv, v_gdn_alog, v_gdn_dtb, v_gdn_gon, v_lru_conv, v_lru_conv_b, v_lru_wa, v_lru_ba, v_lru_wx, v_lru_bx, v_lru_lam, v_w_gate, v_b_gate, v_w_branch, v_w_out, v_ln_ffn2, v_ffn2_wi, v_ffn2_wo, v_ln_final):
    ws = (meta, ln_ffn1, ffn1_wi, ffn1_wo, ln_mix, w_in, fox_bf, mla_gq, mla_wq, mla_gkv, mla_wkv, gdn_conv, gdn_alog, gdn_dtb, gdn_gon, lru_conv, lru_conv_b, lru_wa, lru_ba, lru_wx, lru_bx, lru_lam, w_gate, b_gate, w_branch, w_out, ln_ffn2, ffn2_wi, ffn2_wo, ln_final)
    ms = (m_meta, m_ln_ffn1, m_ffn1_wi, m_ffn1_wo, m_ln_mix, m_w_in, m_fox_bf, m_mla_gq, m_mla_wq, m_mla_gkv, m_mla_wkv, m_gdn_conv, m_gdn_alog, m_gdn_dtb, m_gdn_gon, m_lru_conv, m_lru_conv_b, m_lru_wa, m_lru_ba, m_lru_wx, m_lru_bx, m_lru_lam, m_w_gate, m_b_gate, m_w_branch, m_w_out, m_ln_ffn2, m_ffn2_wi, m_ffn2_wo, m_ln_final)
    vs = (v_meta, v_ln_ffn1, v_ffn1_wi, v_ffn1_wo, v_ln_mix, v_w_in, v_fox_bf, v_mla_gq, v_mla_wq, v_mla_gkv, v_mla_wkv, v_gdn_conv, v_gdn_alog, v_gdn_dtb, v_gdn_gon, v_lru_conv, v_lru_conv_b, v_lru_wa, v_lru_ba, v_lru_wx, v_lru_bx, v_lru_lam, v_w_gate, v_b_gate, v_w_branch, v_w_out, v_ln_ffn2, v_ffn2_wi, v_ffn2_wo, v_ln_final)
    names = [n for n, _ in WEIGHT_SPECS]
    axis = dict(WEIGHT_SPECS)
    wd, md, vd = dict(zip(names, ws)), dict(zip(names, ms)), dict(zip(names, vs))
    shapes = {n: wd[n].shape for n in names}
    big = [n for n in names if n in LARGE]
    few = [n for n in names if axis[n] is not None and n not in LARGE]
    whole = [n for n in names if axis[n] is None]
    x_, y_, _ = _place()
    chip = 2 * x_ + y_

    def to_dma(n, a):
        ax, w = axis[n], shapes[n][axis[n]]
        nd = len(shapes[n])
        if (ax == nd - 1 and w % LANES) or (ax == nd - 2 and w % 16):
            parts = a.shape[ax] // w
            a = jnp.moveaxis(a.reshape(a.shape[:ax] + (parts, w) + a.shape[ax + 1:]), ax, 1)
            return a, 1
        return a, ax

    def from_dma(n, a):
        ax = axis[n]
        if a.ndim == len(shapes[n]):
            return a
        a = jnp.moveaxis(a, 1, ax)
        return a.reshape(a.shape[:ax] + (-1,) + a.shape[ax + 2:])

    dma = [to_dma(n, wd[n].astype(BF16)) for n in big]
    fulls = _gather_layers([a for a, _ in dma], [ax for _, ax in dma])
    full_bf16 = {n: from_dma(n, f) for n, f in zip(big, fulls)}
    full = {n: jnp.zeros(f.shape, F32) for n, f in full_bf16.items()}
    gathered = _gather_chips(_to_flat([wd[n] for n in few]))
    per_chip = [_from_flat(gathered[k], [shapes[n] for n in few]) for k in range(N_CHIPS)]
    for i, n in enumerate(few):
        full[n] = jnp.concatenate([per_chip[k][i] for k in range(N_CHIPS)], axis=axis[n])
    full.update({n: wd[n] for n in whole})

    loss, (gw, gx) = jax.value_and_grad(_local_loss, argnums=(0, 1))(full, x, loss_target, full_bf16)
    loss = lax.psum(loss, ("x", "y", "c"))

    dma = [to_dma(n, gw[n]) for n in big]
    reduced = _reduce_scatter_layers([a for a, _ in dma], [ax for _, ax in dma])
    grads = {n: r.reshape(shapes[n]) for n, r in zip(big, reduced)}
    rest = few + whole
    summed = _from_flat(_sum_devices(_allgather_devices(_to_flat([gw[n] for n in rest]))),
                        [gw[n].shape for n in rest])
    for n, g in zip(rest, summed):
        if axis[n] is not None:
            g = lax.dynamic_slice_in_dim(g, chip * shapes[n][axis[n]], shapes[n][axis[n]], axis=axis[n])
        grads[n] = g

    delta, new_m, new_v = {}, {}, {}
    for n in big:
        delta[n], new_m[n], new_v[n] = _adamw(wd[n], grads[n], md[n], vd[n])
    outs = _adamw(*[_to_flat([d[n] for n in rest]) for d in (wd, grads, md, vd)])
    for res, flat in zip((delta, new_m, new_v), outs):
        res.update(zip(rest, _from_flat(flat, [shapes[n] for n in rest])))

    return (loss, gx, *[grads[n] for n in names], *[delta[n] for n in names],
            *[new_m[n] for n in names], *[new_v[n] for n in names])
```

```python
import functools
import math

import jax
import jax.numpy as jnp
from jax import lax
from jax.experimental import pallas as pl
from jax.experimental.pallas import tpu as pltpu

F32 = jnp.float32
BF16 = jnp.bfloat16

N_META = 16
BLOCK = 128
PAD_LEN = BLOCK - N_META
EPS = 1e-6
NEG_INF = -1e30
N_BRANCH = 4
FOX_HEADS, FOX_DH = 4, 64
MLA_HEADS, MLA_NOPE, MLA_ROPE, MLA_DV = 4, 64, 32, 64
MLA_Q_RANK, MLA_KV_RANK = 192, 128
ROPE_BASE = 10000.0
GDN_HEADS, GDN_DK, GDN_DV, GDN_CHUNK = 4, 64, 64, 64
LRU_WIDTH, LRU_BLOCKS, LRU_C = 256, 4, 8.0
DEPTH = 2

OFF_FOX_QKV = 0
OFF_FOX_F = OFF_FOX_QKV + 3 * FOX_HEADS * FOX_DH
OFF_MLA_CQ = OFF_FOX_F + FOX_HEADS
OFF_MLA_CKV = OFF_MLA_CQ + MLA_Q_RANK
OFF_MLA_KR = OFF_MLA_CKV + MLA_KV_RANK
OFF_GDN_QKV = OFF_MLA_KR + MLA_ROPE
OFF_GDN_A = OFF_GDN_QKV + GDN_HEADS * (2 * GDN_DK + GDN_DV)
OFF_GDN_B = OFF_GDN_A + GDN_HEADS
OFF_GDN_G = OFF_GDN_B + GDN_HEADS
OFF_LRU = OFF_GDN_G + GDN_HEADS * GDN_DV
N_IN = OFF_LRU + LRU_WIDTH

ADAM_LR, ADAM_B1, ADAM_B2, ADAM_EPS, ADAM_WD, ADAM_STEP = 0.001, 0.9, 0.999, 1e-08, 0.01, 10

WEIGHT_SPECS = (
    ("meta", 1), ("ln_ffn1", None), ("ffn1_wi", 2), ("ffn1_wo", 1), ("ln_mix", None), ("w_in", 2),
    ("fox_bf", None), ("mla_gq", None), ("mla_wq", 2), ("mla_gkv", None), ("mla_wkv", 2),
    ("gdn_conv", 2), ("gdn_alog", None), ("gdn_dtb", None), ("gdn_gon", None), ("lru_conv", 2),
    ("lru_conv_b", None), ("lru_wa", None), ("lru_ba", None), ("lru_wx", None), ("lru_bx", None),
    ("lru_lam", None), ("w_gate", 2), ("b_gate", 2), ("w_branch", 3), ("w_out", 1),
    ("ln_ffn2", None), ("ffn2_wi", 2), ("ffn2_wo", 1), ("ln_final", None),
)
N_CHIPS = 4
LARGE = ("ffn1_wi", "ffn1_wo", "w_in", "w_gate", "w_branch", "w_out", "ffn2_wi", "ffn2_wo")

LANES = 128
VMEM_LIMIT_BYTES = 48 * 1024 * 1024
FLAT_COLS = 512
FLAT_ROW_ALIGN = 64


def _pcall(body, **kw):
    return pl.pallas_call(body, **kw)


def _pick(n, cands):
    for c in cands:
        if n % c == 0:
            return c
    return n


_DN = {"nn": (((1,), (0,)), ((), ())), "nt": (((1,), (1,)), ((), ())), "tn": (((0,), (0,)), ((), ()))}


MATMUL_OPERAND_TILE_BYTES = 8 * 1024 * 1024


def _k_tile(K, row_bytes, lane_axis):
    for tk in (K, 4224, 2816, 2112, 1408, 1056, 1024, 768, 704, 512, 384, 256, 128):
        aligned = tk == K or tk % LANES == 0 or (not lane_axis and tk % 16 == 0)
        if tk <= K and K % tk == 0 and aligned and tk * row_bytes <= MATMUL_OPERAND_TILE_BYTES:
            return tk
    return K


def _matmul(a, b, mode, name, out_dtype=F32):
    if mode == "nn":
        (M, K), (_, N) = a.shape, b.shape
    elif mode == "nt":
        (M, K), (N, _) = a.shape, b.shape
    else:
        (K, M), (_, N) = a.shape, b.shape
    tm = _pick(M, (1024, 768, 512, 1408, 384, 256, 128, 64, 32, 16, 8))
    tn = _pick(N, (1408, 1024, 1280, 512, 256, 128))
    tk = _k_tile(K, tm * a.dtype.itemsize + tn * b.dtype.itemsize, mode != "tn")
    nk = K // tk
    a_spec = {"nn": pl.BlockSpec((tm, tk), lambda i, j, k: (i, k)),
              "nt": pl.BlockSpec((tm, tk), lambda i, j, k: (i, k)),
              "tn": pl.BlockSpec((tk, tm), lambda i, j, k: (k, i))}[mode]
    b_spec = {"nn": pl.BlockSpec((tk, tn), lambda i, j, k: (k, j)),
              "nt": pl.BlockSpec((tn, tk), lambda i, j, k: (j, k)),
              "tn": pl.BlockSpec((tk, tn), lambda i, j, k: (k, j))}[mode]
    dn = _DN[mode]

    def body(a_ref, b_ref, o_ref, acc_ref):
        k = pl.program_id(2)
        part = lax.dot_general(a_ref[...].astype(BF16), b_ref[...].astype(BF16), dn,
                               preferred_element_type=F32)

        if nk == 1:
            o_ref[...] = part.astype(o_ref.dtype)
        else:
            @pl.when(k == 0)
            def _():
                acc_ref[...] = part

            @pl.when((k > 0) & (k < nk - 1))
            def _():
                acc_ref[...] += part

            @pl.when(k == nk - 1)
            def _():
                o_ref[...] = (acc_ref[...] + part).astype(o_ref.dtype)

    return _pcall(
        body, name=name, grid=(M // tm, N // tn, nk),
        in_specs=[a_spec, b_spec], out_specs=pl.BlockSpec((tm, tn), lambda i, j, k: (i, j)),
        out_shape=jax.ShapeDtypeStruct((M, N), out_dtype),
        scratch_shapes=[pltpu.VMEM((tm, tn) if nk > 1 else (8, LANES), F32)],
        compiler_params=pltpu.CompilerParams(
            dimension_semantics=("parallel", "parallel", "arbitrary"),
            vmem_limit_bytes=VMEM_LIMIT_BYTES),
    )(a, b)


@functools.partial(jax.custom_vjp, nondiff_argnums=(3,))
def dense(a, wb, w, out_dtype):
    return _matmul(a, wb, "nn", "dense_fwd", out_dtype)


def _dense_fwd(a, wb, w, out_dtype):
    return _matmul(a, wb, "nn", "dense_fwd", out_dtype), (a, wb)


def _dense_bwd(out_dtype, res, g):
    a, wb = res
    return (_matmul(g, wb, "nt", "dense_dgrad", a.dtype), jnp.zeros_like(wb),
            _matmul(a, g, "tn", "dense_wgrad", F32))


dense.defvjp(_dense_fwd, _dense_bwd)


def dense_nd(a, wb, w, out_dtype=F32, keep_pad=False):
    K, N = wb.shape
    pad = (-N) % (4 * LANES if N > 4 * LANES else LANES)
    if pad:
        wb = jnp.pad(wb, ((0, 0), (0, pad)))
        w = jnp.pad(w, ((0, 0), (0, pad)))
    out = dense(a.reshape(-1, K), wb, w, out_dtype)
    if keep_pad:
        return out
    if pad:
        out = out[:, :N]
    return out.reshape(a.shape[:-1] + (N,))


@functools.partial(jax.custom_vjp, nondiff_argnums=(1,))
def _split_cols(x, bounds):
    return tuple(x[:, a:b] for a, b in bounds)


def _split_cols_fwd(x, bounds):
    return _split_cols(x, bounds), jnp.zeros((x.shape[0], x.shape[1] - bounds[-1][1]), x.dtype)


def _split_cols_bwd(bounds, rest, cts):
    return (jnp.concatenate(list(cts) + ([rest] if rest.shape[1] else []), axis=1),)


_split_cols.defvjp(_split_cols_fwd, _split_cols_bwd)


class _Cols:
    def __init__(self, ranges, shape):
        self.ranges, self.shape = ranges, shape

    def __getitem__(self, idx):
        return self.ranges[(idx[-1].start, idx[-1].stop)]


_NT = (((1,), (1,)), ((), ()))


def _att_blk(T):
    return _pick(T, (384, 256, 128))


def _causal_pairs(n, by_query):
    if by_query:
        pairs = [(i, j) for i in range(n) for j in range(i + 1)]
    else:
        pairs = [(i, j) for j in range(n) for i in range(j, n)]
    return (jnp.array([p[0] for p in pairs], jnp.int32), jnp.array([p[1] for p in pairs], jnp.int32))


def _attn_fwd_call(q, k, v, cum_col, cum_row, scale):
    BH, T, dk = q.shape
    dv = v.shape[2]
    blk = _att_blk(T)
    n = T // blk
    has_cum = cum_col is not None

    def body(i_tab, j_tab, *refs):
        if has_cum:
            q_ref, k_ref, v_ref, cq_ref, ck_ref, o_ref, lse_ref, m_s, l_s, acc_s = refs
        else:
            q_ref, k_ref, v_ref, o_ref, lse_ref, m_s, l_s, acc_s = refs
        i = i_tab[pl.program_id(1)]
        j = j_tab[pl.program_id(1)]

        @pl.when(j == 0)
        def _():
            m_s[...] = jnp.full_like(m_s, NEG_INF)
            l_s[...] = jnp.zeros_like(l_s)
            acc_s[...] = jnp.zeros_like(acc_s)

        def step(masked):
            s = lax.dot_general(q_ref[0].astype(BF16), k_ref[0].astype(BF16), _NT,
                                preferred_element_type=F32) * scale
            if has_cum:
                s = s + cq_ref[0] - ck_ref[0]
            if masked:
                qpos = i * blk + lax.broadcasted_iota(jnp.int32, (blk, blk), 0)
                kpos = j * blk + lax.broadcasted_iota(jnp.int32, (blk, blk), 1)
                mask = (kpos <= qpos) & (kpos >= PAD_LEN)
                s = jnp.where(mask, s, NEG_INF)
            m_prev = m_s[...]
            m_new = jnp.maximum(m_prev, jnp.max(s, axis=1, keepdims=True))
            p = jnp.exp(s - m_new)
            alpha = jnp.exp(m_prev - m_new)
            l_s[...] = alpha * l_s[...] + jnp.sum(p, axis=1, keepdims=True)
            acc_s[...] = alpha * acc_s[...] + jnp.dot(p.astype(BF16), v_ref[0].astype(BF16),
                                                      preferred_element_type=F32)
            m_s[...] = m_new

        @pl.when((j == i) | (j == 0))
        def _():
            step(True)

        @pl.when((j < i) & (j > 0))
        def _():
            step(False)

        @pl.when(j == i)
        def _():
            o_ref[0] = acc_s[...] / l_s[...]
            lse_ref[0] = m_s[...] + jnp.log(l_s[...])

    q_idx = lambda b, t, it, jt: (b, it[t], 0)
    kv_idx = lambda b, t, it, jt: (b, jt[t], 0)
    in_specs = [pl.BlockSpec((1, blk, dk), q_idx), pl.BlockSpec((1, blk, dk), kv_idx),
                pl.BlockSpec((1, blk, dv), kv_idx)]
    args = [q, k, v]
    if has_cum:
        in_specs += [pl.BlockSpec((1, blk, 1), q_idx),
                     pl.BlockSpec((1, 1, blk), lambda b, t, it, jt: (b, 0, jt[t]))]
        args += [cum_col, cum_row]
    i_tab, j_tab = _causal_pairs(n, by_query=True)
    return _pcall(
        body, name="attn_fwd_cum" if has_cum else "attn_fwd",
        grid_spec=pltpu.PrefetchScalarGridSpec(
            num_scalar_prefetch=2, grid=(BH, len(i_tab)), in_specs=in_specs,
            out_specs=[pl.BlockSpec((1, blk, dv), q_idx), pl.BlockSpec((1, blk, 1), q_idx)],
            scratch_shapes=[pltpu.VMEM((blk, 1), F32), pltpu.VMEM((blk, 1), F32), pltpu.VMEM((blk, dv), F32)]),
        out_shape=[jax.ShapeDtypeStruct((BH, T, dv), F32), jax.ShapeDtypeStruct((BH, T, 1), F32)],
        compiler_params=pltpu.CompilerParams(
            dimension_semantics=("parallel", "arbitrary"), vmem_limit_bytes=VMEM_LIMIT_BYTES),
    )(i_tab, j_tab, *args)


def _attn_dkv_call(q, k, v, do, lse_row, delta_row, cum_row, cum_col, scale):
    BH, T, dk = q.shape
    dv = v.shape[2]
    blk = _att_blk(T)
    n = T // blk
    has_cum = cum_row is not None

    def body(i_tab, j_tab, *refs):
        if has_cum:
            (k_ref, v_ref, q_ref, do_ref, lse_ref, dl_ref, cq_ref, ck_ref,
             dk_ref, dv_ref, dc_ref, dk_s, dv_s, dc_s) = refs
        else:
            k_ref, v_ref, q_ref, do_ref, lse_ref, dl_ref, dk_ref, dv_ref, dk_s, dv_s = refs
        i = i_tab[pl.program_id(1)]
        j = j_tab[pl.program_id(1)]

        @pl.when(i == j)
        def _():
            dk_s[...] = jnp.zeros_like(dk_s)
            dv_s[...] = jnp.zeros_like(dv_s)
            if has_cum:
                dc_s[...] = jnp.zeros_like(dc_s)

        def step(masked):
            kb = k_ref[0].astype(BF16)
            qb = q_ref[0].astype(BF16)
            dob = do_ref[0].astype(BF16)
            st = lax.dot_general(kb, qb, _NT, preferred_element_type=F32) * scale
            if has_cum:
                st = st + cq_ref[0] - ck_ref[0]
            pt = jnp.exp(jnp.minimum(st - lse_ref[0], 0.0))
            if masked:
                kpos = j * blk + lax.broadcasted_iota(jnp.int32, (blk, blk), 0)
                qpos = i * blk + lax.broadcasted_iota(jnp.int32, (blk, blk), 1)
                pt = jnp.where((kpos <= qpos) & (kpos >= PAD_LEN), pt, 0.0)
            dv_s[...] += jnp.dot(pt.astype(BF16), dob, preferred_element_type=F32)
            dpt = lax.dot_general(v_ref[0].astype(BF16), dob, _NT, preferred_element_type=F32)
            dst = pt * (dpt - dl_ref[0])
            dk_s[...] += jnp.dot(dst.astype(BF16), qb, preferred_element_type=F32) * scale
            if has_cum:
                dc_s[...] -= jnp.sum(dst, axis=1, keepdims=True)

        @pl.when((i == j) | (j == 0))
        def _():
            step(True)

        @pl.when((i > j) & (j > 0))
        def _():
            step(False)

        @pl.when(i == n - 1)
        def _():
            dk_ref[0] = dk_s[...]
            dv_ref[0] = dv_s[...]
            if has_cum:
                dc_ref[0] = dc_s[...]

    k_idx = lambda b, t, it, jt: (b, jt[t], 0)
    q_idx = lambda b, t, it, jt: (b, it[t], 0)
    row_idx = lambda b, t, it, jt: (b, 0, it[t])
    in_specs = [pl.BlockSpec((1, blk, dk), k_idx), pl.BlockSpec((1, blk, dv), k_idx),
                pl.BlockSpec((1, blk, dk), q_idx), pl.BlockSpec((1, blk, dv), q_idx),
                pl.BlockSpec((1, 1, blk), row_idx), pl.BlockSpec((1, 1, blk), row_idx)]
    args = [k, v, q, do, lse_row, delta_row]
    out_specs = [pl.BlockSpec((1, blk, dk), k_idx), pl.BlockSpec((1, blk, dv), k_idx)]
    out_shape = [jax.ShapeDtypeStruct((BH, T, dk), F32), jax.ShapeDtypeStruct((BH, T, dv), F32)]
    scratch = [pltpu.VMEM((blk, dk), F32), pltpu.VMEM((blk, dv), F32)]
    if has_cum:
        in_specs += [pl.BlockSpec((1, 1, blk), row_idx), pl.BlockSpec((1, blk, 1), k_idx)]
        args += [cum_row, cum_col]
        out_specs.append(pl.BlockSpec((1, blk, 1), k_idx))
        out_shape.append(jax.ShapeDtypeStruct((BH, T, 1), F32))
        scratch.append(pltpu.VMEM((blk, 1), F32))
    i_tab, j_tab = _causal_pairs(n, by_query=False)
    return _pcall(
        body, name="attn_dkv_cum" if has_cum else "attn_dkv",
        grid_spec=pltpu.PrefetchScalarGridSpec(
            num_scalar_prefetch=2, grid=(BH, len(i_tab)), in_specs=in_specs, out_specs=out_specs,
            scratch_shapes=scratch),
        out_shape=out_shape,
        compiler_params=pltpu.CompilerParams(
            dimension_semantics=("parallel", "arbitrary"), vmem_limit_bytes=VMEM_LIMIT_BYTES),
    )(i_tab, j_tab, *args)


def _attn_dq_call(q, k, v, do, lse_col, delta_col, cum_col, cum_row, scale):
    BH, T, dk = q.shape
    dv = v.shape[2]
    blk = _att_blk(T)
    n = T // blk
    has_cum = cum_col is not None

    def body(i_tab, j_tab, *refs):
        if has_cum:
            q_ref, k_ref, v_ref, do_ref, lse_ref, dl_ref, cq_ref, ck_ref, dq_ref, dc_ref, dq_s, dc_s = refs
        else:
            q_ref, k_ref, v_ref, do_ref, lse_ref, dl_ref, dq_ref, dq_s = refs
        i = i_tab[pl.program_id(1)]
        j = j_tab[pl.program_id(1)]

        @pl.when(j == 0)
        def _():
            dq_s[...] = jnp.zeros_like(dq_s)
            if has_cum:
                dc_s[...] = jnp.zeros_like(dc_s)

        def step(masked):
            kb = k_ref[0].astype(BF16)
            s = lax.dot_general(q_ref[0].astype(BF16), kb, _NT, preferred_element_type=F32) * scale
            if has_cum:
                s = s + cq_ref[0] - ck_ref[0]
            p = jnp.exp(jnp.minimum(s - lse_ref[0], 0.0))
            if masked:
                qpos = i * blk + lax.broadcasted_iota(jnp.int32, (blk, blk), 0)
                kpos = j * blk + lax.broadcasted_iota(jnp.int32, (blk, blk), 1)
                p = jnp.where((kpos <= qpos) & (kpos >= PAD_LEN), p, 0.0)
            dp = lax.dot_general(do_ref[0].astype(BF16), v_ref[0].astype(BF16), _NT,
                                 preferred_element_type=F32)
            ds = p * (dp - dl_ref[0])
            dq_s[...] += jnp.dot(ds.astype(BF16), kb, preferred_element_type=F32) * scale
            if has_cum:
                dc_s[...] += jnp.sum(ds, axis=1, keepdims=True)

        @pl.when((j == i) | (j == 0))
        def _():
            step(True)

        @pl.when((j < i) & (j > 0))
        def _():
            step(False)

        @pl.when(j == i)
        def _():
            dq_ref[0] = dq_s[...]
            if has_cum:
                dc_ref[0] = dc_s[...]

    kv_idx = lambda b, t, it, jt: (b, jt[t], 0)
    q_idx = lambda b, t, it, jt: (b, it[t], 0)
    in_specs = [pl.BlockSpec((1, blk, dk), q_idx), pl.BlockSpec((1, blk, dk), kv_idx),
                pl.BlockSpec((1, blk, dv), kv_idx), pl.BlockSpec((1, blk, dv), q_idx),
                pl.BlockSpec((1, blk, 1), q_idx), pl.BlockSpec((1, blk, 1), q_idx)]
    args = [q, k, v, do, lse_col, delta_col]
    if has_cum:
        in_specs += [pl.BlockSpec((1, blk, 1), q_idx),
                     pl.BlockSpec((1, 1, blk), lambda b, t, it, jt: (b, 0, jt[t]))]
        args += [cum_col, cum_row]
    out_specs = [pl.BlockSpec((1, blk, dk), q_idx)]
    out_shape = [jax.ShapeDtypeStruct((BH, T, dk), F32)]
    scratch = [pltpu.VMEM((blk, dk), F32)]
    if has_cum:
        out_specs.append(pl.BlockSpec((1, blk, 1), q_idx))
        out_shape.append(jax.ShapeDtypeStruct((BH, T, 1), F32))
        scratch.append(pltpu.VMEM((blk, 1), F32))
    i_tab, j_tab = _causal_pairs(n, by_query=True)
    return _pcall(
        body, name="attn_dq_cum" if has_cum else "attn_dq",
        grid_spec=pltpu.PrefetchScalarGridSpec(
            num_scalar_prefetch=2, grid=(BH, len(i_tab)), in_specs=in_specs, out_specs=out_specs,
            scratch_shapes=scratch),
        out_shape=out_shape,
        compiler_params=pltpu.CompilerParams(
            dimension_semantics=("parallel", "arbitrary"), vmem_limit_bytes=VMEM_LIMIT_BYTES),
    )(i_tab, j_tab, *args)


def _make_attention(scale, has_cum):
    def fold(t):
        return t.reshape((-1,) + t.shape[2:])

    def run_fwd(q, k, v, cum):
        B, H, T, _ = q.shape
        col = cum.reshape(B * H, T, 1) if has_cum else None
        row = cum.reshape(B * H, 1, T) if has_cum else None
        o, lse = _attn_fwd_call(fold(q), fold(k), fold(v), col, row, scale)
        return o.reshape(B, H, T, -1), lse

    @jax.custom_vjp
    def attn(q, k, v, cum):
        return run_fwd(q, k, v, cum)[0]

    def attn_fwd(q, k, v, cum):
        o, lse = run_fwd(q, k, v, cum)
        return o, (q, k, v, cum, o, lse)

    def attn_bwd(res, do):
        q, k, v, cum, o, lse = res
        B, H, T, _ = q.shape
        delta = jnp.sum(do * o, axis=-1).reshape(B * H, T, 1)
        col = cum.reshape(B * H, T, 1) if has_cum else None
        row = cum.reshape(B * H, 1, T) if has_cum else None
        qf, kf, vf, dof = fold(q), fold(k), fold(v), fold(do)
        outs = _attn_dkv_call(qf, kf, vf, dof, lse.reshape(B * H, 1, T), delta.reshape(B * H, 1, T),
                              row, col, scale)
        dqs = _attn_dq_call(qf, kf, vf, dof, lse, delta, col, row, scale)
        dcum = (outs[2] + dqs[1]).reshape(B, H, T) if has_cum else jnp.zeros_like(cum)
        return (dqs[0].reshape(q.shape), outs[0].reshape(k.shape), outs[1].reshape(v.shape), dcum)

    attn.defvjp(attn_fwd, attn_bwd)
    return attn


def _rmsnorm(x, g):
    return x * lax.rsqrt(jnp.mean(x * x, axis=-1, keepdims=True) + EPS) * g


def _l2norm(x):
    return x * lax.rsqrt(jnp.sum(x * x, axis=-1, keepdims=True) + EPS)


def _row_tile(rows, row_bytes, budget):
    for t in (2048, 1024, 512, 256, 128, 64, 32, 16, 8):
        if rows % t == 0 and t * row_bytes <= budget:
            return t
    return rows


ROW_BLOCK_BYTES = 2 * 1024 * 1024


def _rms_fwd_call(x, g):
    M, D = x.shape
    tr = _row_tile(M, 4 * D, ROW_BLOCK_BYTES)

    def body(x_ref, g_ref, y_ref, r_ref):
        xv = x_ref[...]
        r = lax.rsqrt(jnp.mean(xv * xv, axis=-1, keepdims=True) + EPS)
        y_ref[...] = xv * r * g_ref[...]
        r_ref[...] = r

    return _pcall(
        body, name="rmsnorm_fwd", grid=(M // tr,),
        in_specs=[pl.BlockSpec((tr, D), lambda i: (i, 0)), pl.BlockSpec((1, D), lambda i: (0, 0))],
        out_specs=[pl.BlockSpec((tr, D), lambda i: (i, 0)), pl.BlockSpec((tr, 1), lambda i: (i, 0))],
        out_shape=[jax.ShapeDtypeStruct((M, D), F32), jax.ShapeDtypeStruct((M, 1), F32)],
        compiler_params=pltpu.CompilerParams(dimension_semantics=("parallel",)),
    )(x, g)


def _rms_bwd_call(x, g, r, dy):
    M, D = x.shape
    tr = _row_tile(M, 4 * D, ROW_BLOCK_BYTES)

    def body(x_ref, g_ref, r_ref, dy_ref, dx_ref, dg_ref):
        i = pl.program_id(0)
        rv = r_ref[...]
        xh = x_ref[...] * rv
        dyv = dy_ref[...]
        dyg = dyv * g_ref[...]
        dx_ref[...] = rv * (dyg - xh * jnp.mean(dyg * xh, axis=-1, keepdims=True))
        part = jnp.sum(dyv * xh, axis=0, keepdims=True)

        @pl.when(i == 0)
        def _():
            dg_ref[...] = part

        @pl.when(i > 0)
        def _():
            dg_ref[...] += part

    row = pl.BlockSpec((tr, D), lambda i: (i, 0))
    return _pcall(
        body, name="rmsnorm_bwd", grid=(M // tr,),
        in_specs=[row, pl.BlockSpec((1, D), lambda i: (0, 0)), pl.BlockSpec((tr, 1), lambda i: (i, 0)), row],
        out_specs=[row, pl.BlockSpec((1, D), lambda i: (0, 0))],
        out_shape=[jax.ShapeDtypeStruct((M, D), F32), jax.ShapeDtypeStruct((1, D), F32)],
        compiler_params=pltpu.CompilerParams(dimension_semantics=("arbitrary",)),
    )(x, g, r, dy)


@jax.custom_vjp
def _rmsnorm2d(x, g):
    return _rms_fwd_call(x, g)[0]


def _rmsnorm2d_fwd(x, g):
    y, r = _rms_fwd_call(x, g)
    return y, (x, g, r)


def _rmsnorm2d_bwd(res, dy):
    x, g, r = res
    return _rms_bwd_call(x, g, r, dy)


_rmsnorm2d.defvjp(_rmsnorm2d_fwd, _rmsnorm2d_bwd)


def _rmsnorm_rows(x, g):
    D = x.shape[-1]
    return _rmsnorm2d(x.reshape(-1, D), g.reshape(1, D)).reshape(x.shape)


def _glu_fwd_call(gu):
    M, F2 = gu.shape
    F = F2 // 2
    tr = _row_tile(M, 4 * F2, 2 * ROW_BLOCK_BYTES)

    def body(g_ref, u_ref, o_ref):
        gv = g_ref[...].astype(F32)
        o_ref[...] = (gv * jax.nn.sigmoid(gv) * u_ref[...].astype(F32)).astype(o_ref.dtype)

    return _pcall(
        body, name="swiglu_fwd", grid=(M // tr,),
        in_specs=[pl.BlockSpec((tr, F), lambda i: (i, 0)), pl.BlockSpec((tr, F), lambda i: (i, 1))],
        out_specs=pl.BlockSpec((tr, F), lambda i: (i, 0)),
        out_shape=jax.ShapeDtypeStruct((M, F), gu.dtype),
        compiler_params=pltpu.CompilerParams(dimension_semantics=("parallel",),
                                             vmem_limit_bytes=VMEM_LIMIT_BYTES),
    )(gu, gu)


def _glu_bwd_call(gu, da):
    M, F2 = gu.shape
    F = F2 // 2
    tr = _row_tile(M, 4 * F2, 2 * ROW_BLOCK_BYTES)

    def body(g_ref, u_ref, da_ref, o_ref):
        gv = g_ref[...].astype(F32)
        s = jax.nn.sigmoid(gv)
        dav = da_ref[...].astype(F32)
        o_ref[:, :F] = (dav * u_ref[...].astype(F32) * (s * (1.0 + gv * (1.0 - s)))).astype(o_ref.dtype)
        o_ref[:, F:] = (dav * (gv * s)).astype(o_ref.dtype)

    return _pcall(
        body, name="swiglu_bwd", grid=(M // tr,),
        in_specs=[pl.BlockSpec((tr, F), lambda i: (i, 0)), pl.BlockSpec((tr, F), lambda i: (i, 1)),
                  pl.BlockSpec((tr, F), lambda i: (i, 0))],
        out_specs=pl.BlockSpec((tr, F2), lambda i: (i, 0)),
        out_shape=jax.ShapeDtypeStruct((M, F2), gu.dtype),
        compiler_params=pltpu.CompilerParams(dimension_semantics=("parallel",),
                                             vmem_limit_bytes=VMEM_LIMIT_BYTES),
    )(gu, gu, da)


@jax.custom_vjp
def _glu(gu):
    return _glu_fwd_call(gu)


def _glu_fwd(gu):
    return _glu_fwd_call(gu), gu


def _glu_bwd(gu, da):
    return (_glu_bwd_call(gu, da),)


_glu.defvjp(_glu_fwd, _glu_bwd)


def _swiglu(h, wi, wo):
    lead, D = h.shape[:-1], h.shape[-1]
    gu = dense(h.reshape(-1, D), wi[0], wi[1], BF16)
    return dense(_glu(gu), wo[0], wo[1], F32).reshape(lead + (wo[0].shape[1],))


def _causal_dwconv(x, w):
    K, C = w.shape
    return lax.conv_general_dilated(
        x, w[:, None, :], window_strides=(1,), padding=[(K - 1, 0)],
        dimension_numbers=("NWC", "WIO", "NWC"), feature_group_count=C)


def _rope(x, cos, sin):
    half = x.shape[-1] // 2
    x1, x2 = x[..., :half], x[..., half:]
    return jnp.concatenate([x1 * cos - x2 * sin, x2 * cos + x1 * sin], axis=-1)


def _fox_branch(p, b_f):
    B, T, _ = p.shape
    qkv = p[..., OFF_FOX_QKV:OFF_FOX_F].reshape(B, T, 3, FOX_HEADS, FOX_DH)
    q = qkv[:, :, 0].transpose(0, 2, 1, 3)
    k = qkv[:, :, 1].transpose(0, 2, 1, 3)
    v = qkv[:, :, 2].transpose(0, 2, 1, 3)
    log_f = jax.nn.log_sigmoid(p[..., OFF_FOX_F:OFF_MLA_CQ] + b_f)
    cum = jnp.cumsum(log_f, axis=1).transpose(0, 2, 1)
    o = _make_attention(FOX_DH ** -0.5, True)(q, k, v, cum)
    return o.transpose(0, 2, 1, 3).reshape(B, T, FOX_HEADS * FOX_DH)


def _mla_branch(p, g_qn, w_q_up, g_kvn, w_kv_up, cos, sin):
    B, T, _ = p.shape
    cq = _rmsnorm(p[..., OFF_MLA_CQ:OFF_MLA_CKV], g_qn)
    q = (cq @ w_q_up).reshape(B, T, MLA_HEADS, MLA_NOPE + MLA_ROPE)
    ckv = _rmsnorm(p[..., OFF_MLA_CKV:OFF_MLA_KR], g_kvn)
    kv = (ckv @ w_kv_up).reshape(B, T, MLA_HEADS, MLA_NOPE + MLA_DV)
    k_rope = _rope(p[..., OFF_MLA_KR:OFF_GDN_QKV], cos, sin)
    q_rope = _rope(q[..., MLA_NOPE:], cos[:, None], sin[:, None])
    q = jnp.concatenate([q[..., :MLA_NOPE], q_rope], axis=-1)
    k = jnp.concatenate([kv[..., :MLA_NOPE],
                         jnp.broadcast_to(k_rope[:, :, None], (B, T, MLA_HEADS, MLA_ROPE))], axis=-1)
    v = kv[..., MLA_NOPE:]
    o = _make_attention((MLA_NOPE + MLA_ROPE) ** -0.5, False)(
        q.transpose(0, 2, 1, 3), k.transpose(0, 2, 1, 3), v.transpose(0, 2, 1, 3),
        jnp.zeros((B, MLA_HEADS, T), F32))
    return o.transpose(0, 2, 1, 3).reshape(B, T, MLA_HEADS * MLA_DV)


def _bmm(a, b):
    return lax.dot_general(a.astype(BF16), b.astype(BF16), (((2,), (1,)), ((0,), (0,))),
                           preferred_element_type=F32)


def _bmm_nt(a, b):
    return lax.dot_general(a.astype(BF16), b.astype(BF16), (((2,), (2,)), ((0,), (0,))),
                           preferred_element_type=F32)


def _chunk_spec(shape, index):
    return pl.BlockSpec((1,) + tuple(shape[1:]), lambda n: (index(n), 0, 0, 0))


def _gdn_scan_fwd_call(qd, kdt, w, u, qk, gl):
    nc, BH, C, DK = qd.shape
    DV = u.shape[3]

    def body(qd_ref, kdt_ref, w_ref, u_ref, qk_ref, gl_ref, o_ref, s_ref, vn_ref, state):
        @pl.when(pl.program_id(0) == 0)
        def _():
            state[...] = jnp.zeros_like(state)

        s = state[...]
        s_ref[0] = s
        vn = u_ref[0] - _bmm(w_ref[0], s)
        vn_ref[0] = vn
        o_ref[0] = _bmm(qd_ref[0], s) + _bmm(qk_ref[0], vn)
        state[...] = s * gl_ref[0] + _bmm(kdt_ref[0], vn)

    fwd = lambda n: n
    outs = [(nc, BH, C, DV), (nc, BH, DK, DV), (nc, BH, C, DV)]
    return _pcall(
        body, name="gdn_scan_fwd", grid=(nc,),
        in_specs=[_chunk_spec(t.shape, fwd) for t in (qd, kdt, w, u, qk, gl)],
        out_specs=[_chunk_spec(s, fwd) for s in outs],
        out_shape=[jax.ShapeDtypeStruct(s, F32) for s in outs],
        scratch_shapes=[pltpu.VMEM((BH, DK, DV), F32)],
        compiler_params=pltpu.CompilerParams(dimension_semantics=("arbitrary",)),
    )(qd, kdt, w, u, qk, gl)


def _gdn_scan_bwd_call(do, qdt, kd, wt, qkt, gl, s_all, vn):
    nc, BH, C, DV = do.shape
    DK = kd.shape[3]

    def body(do_ref, qdt_ref, kd_ref, wt_ref, qkt_ref, gl_ref, s_ref, vn_ref,
             dqd_ref, dkd_ref, dw_ref, du_ref, dqk_ref, dgl_ref, dstate):
        @pl.when(pl.program_id(0) == 0)
        def _():
            dstate[...] = jnp.zeros_like(dstate)

        ds = dstate[...]
        s, v, dov = s_ref[0], vn_ref[0], do_ref[0]
        dkd_ref[0] = _bmm_nt(v, ds)
        dgl_ref[0] = s * ds
        dqd_ref[0] = _bmm_nt(dov, s)
        dqk_ref[0] = _bmm_nt(dov, v)
        dv = _bmm(kd_ref[0], ds) + _bmm(qkt_ref[0], dov)
        du_ref[0] = dv
        dw_ref[0] = -_bmm_nt(dv, s)
        dstate[...] = ds * gl_ref[0] + _bmm(qdt_ref[0], dov) - _bmm(wt_ref[0], dv)

    rev = lambda n: nc - 1 - n
    outs = [(nc, BH, C, DK)] * 3 + [(nc, BH, C, DV), (nc, BH, C, C), (nc, BH, DK, DV)]
    return _pcall(
        body, name="gdn_scan_bwd", grid=(nc,),
        in_specs=[_chunk_spec(t.shape, rev) for t in (do, qdt, kd, wt, qkt, gl, s_all, vn)],
        out_specs=[_chunk_spec(s, rev) for s in outs],
        out_shape=[jax.ShapeDtypeStruct(s, F32) for s in outs],
        scratch_shapes=[pltpu.VMEM((BH, DK, DV), F32)],
        compiler_params=pltpu.CompilerParams(dimension_semantics=("arbitrary",)),
    )(do, qdt, kd, wt, qkt, gl, s_all, vn)


def _unit_lower_inverse_call(mt):
    C, _, N = mt.shape

    def body(m_ref, x_ref):
        rows = lax.broadcasted_iota(jnp.int32, (C, LANES), 0)

        def outer(i, carry):
            def inner(j, acc):
                return acc - m_ref[i, pl.ds(j, 1), :] * x_ref[j]

            x_ref[i] = lax.fori_loop(0, i, inner, jnp.where(rows == i, 1.0, 0.0).astype(F32))
            return carry

        lax.fori_loop(0, C, outer, 0)

    spec = pl.BlockSpec((C, C, LANES), lambda n: (0, 0, n))
    return _pcall(
        body, name="unit_lower_inverse", grid=(N // LANES,), in_specs=[spec], out_specs=spec,
        out_shape=jax.ShapeDtypeStruct((C, C, N), F32),
        compiler_params=pltpu.CompilerParams(dimension_semantics=("parallel",)),
    )(mt)


@jax.custom_vjp
def _unit_lower_inverse(m):
    C = m.shape[-1]
    n = math.prod(m.shape[:-2])
    pad = (-n) % LANES
    mt = jnp.pad(jnp.moveaxis(m.reshape(n, C, C), 0, 2), ((0, 0), (0, 0), (0, pad)))
    x = _unit_lower_inverse_call(mt)[:, :, :n]
    return jnp.moveaxis(x, 2, 0).reshape(m.shape)


def _unit_lower_inverse_fwd(m):
    x = _unit_lower_inverse(m)
    return x, x


def _unit_lower_inverse_bwd(x, dx):
    hi = lax.Precision.HIGHEST
    xt = jnp.swapaxes(x, -1, -2)
    return (-jnp.matmul(jnp.matmul(xt, dx, precision=hi), xt, precision=hi),)


_unit_lower_inverse.defvjp(_unit_lower_inverse_fwd, _unit_lower_inverse_bwd)


def _gl_rows(gl, dv):
    return jnp.broadcast_to(gl[:, :, None, None], gl.shape + (1, dv))


@jax.custom_vjp
def _gdn_scan(qd, kd, w, u, qk, gl):
    return _gdn_scan_fwd_call(qd, jnp.swapaxes(kd, 2, 3), w, u, qk, _gl_rows(gl, u.shape[3]))[0]


def _gdn_scan_fwd(qd, kd, w, u, qk, gl):
    o, s_all, vn = _gdn_scan_fwd_call(qd, jnp.swapaxes(kd, 2, 3), w, u, qk, _gl_rows(gl, u.shape[3]))
    return o, (qd, kd, w, qk, gl, s_all, vn)


def _gdn_scan_bwd(res, do):
    qd, kd, w, qk, gl, s_all, vn = res
    dqd, dkd, dw, du, dqk, dgl = _gdn_scan_bwd_call(
        do, jnp.swapaxes(qd, 2, 3), kd, jnp.swapaxes(w, 2, 3), jnp.swapaxes(qk, 2, 3),
        _gl_rows(gl, do.shape[3]), s_all, vn)
    return dqd, dkd, dw, du, dqk, jnp.sum(dgl, axis=(2, 3))


_gdn_scan.defvjp(_gdn_scan_fwd, _gdn_scan_bwd)


def _gdn_branch(p, conv_w, a_log, dt_bias, g_on):
    B, T, _ = p.shape
    H, DK, DV, C = GDN_HEADS, GDN_DK, GDN_DV, GDN_CHUNK
    qkv = jax.nn.silu(_causal_dwconv(p[..., OFF_GDN_QKV:OFF_GDN_A], conv_w))
    q = _l2norm(qkv[..., :H * DK].reshape(B, T, H, DK)) * DK ** -0.5
    k = _l2norm(qkv[..., H * DK:2 * H * DK].reshape(B, T, H, DK))
    v = qkv[..., 2 * H * DK:].reshape(B, T, H, DV)
    beta = jax.nn.sigmoid(p[..., OFF_GDN_B:OFF_GDN_G])
    g = -jnp.exp(a_log) * jax.nn.softplus(p[..., OFF_GDN_A:OFF_GDN_B] + dt_bias)
    nc = T // C

    def chunks(t):
        return jnp.moveaxis(t, 2, 1).reshape((B, H, nc, C) + t.shape[3:])

    q, k, v, beta, g = chunks(q), chunks(k), chunks(v), chunks(beta), chunks(g)
    G = jnp.cumsum(g, axis=-1)
    idx = jnp.arange(C)
    strict = idx[:, None] > idx[None, :]
    incl = idx[:, None] >= idx[None, :]
    decay = jnp.exp(jnp.where(incl, G[..., :, None] - G[..., None, :], NEG_INF))
    kb = k * beta[..., None]
    vb = v * beta[..., None]
    m = jnp.eye(C, dtype=F32) + jnp.where(
        strict, jnp.einsum("bhnik,bhnjk->bhnij", kb, k) * decay, 0.0)
    rhs = jnp.concatenate([kb * jnp.exp(G)[..., None], vb], axis=-1)
    sol = jnp.matmul(_unit_lower_inverse(m), rhs, precision=lax.Precision.HIGHEST)
    w, u = sol[..., :DK], sol[..., DK:]
    qk = jnp.where(incl, jnp.einsum("bhnik,bhnjk->bhnij", q, k) * decay, 0.0)
    q_dec = q * jnp.exp(G)[..., None]
    k_dec = k * jnp.exp(G[..., -1:] - G)[..., None]
    g_last = jnp.exp(G[..., -1])
    def chunk_major(t):
        return jnp.moveaxis(t, 2, 0).reshape((nc, B * H) + t.shape[3:])

    o = _gdn_scan(chunk_major(q_dec), chunk_major(k_dec), chunk_major(w), chunk_major(u),
                  chunk_major(qk), chunk_major(g_last))
    o = jnp.moveaxis(o.reshape(nc, B, H, C, DV), 0, 2).reshape(B, H, T, DV).transpose(0, 2, 1, 3)
    gate = jax.nn.silu(p[..., OFF_GDN_G:OFF_LRU]).reshape(B, T, H, DV)
    o = _rmsnorm(o, g_on) * gate
    return o.reshape(B, T, H * DV)


def _rglru_branch(p, valid, conv_w, conv_b, w_a, b_a, w_x, b_x, lam):
    B, T, _ = p.shape
    xr = _causal_dwconv(p[..., OFF_LRU:N_IN], conv_w) + conv_b
    xr = jnp.where(valid[None, :, None], xr, 0)
    xb = xr.reshape(B, T, LRU_BLOCKS, LRU_WIDTH // LRU_BLOCKS)
    r = jax.nn.sigmoid(jnp.einsum("btni,nij->btnj", xb, w_a).reshape(B, T, LRU_WIDTH) + b_a)
    ig = jax.nn.sigmoid(jnp.einsum("btni,nij->btnj", xb, w_x).reshape(B, T, LRU_WIDTH) + b_x)
    log_a = -LRU_C * r * jax.nn.softplus(-lam)
    a = jnp.exp(log_a)
    b = jnp.sqrt(-jnp.expm1(2.0 * log_a)) * ig * xr

    def combine(e1, e2):
        return (e1[0] * e2[0], e2[0] * e1[1] + e2[1])

    _, h = lax.associative_scan(combine, (a, b), axis=1)
    return h


def _mixer(u, valid, cos, sin, w, wb, l):
    def pair(name, *idx):
        return wb[name][(l,) + idx], w[name][(l,) + idx]

    B, T, D = u.shape
    cuts = (OFF_FOX_QKV, OFF_FOX_F, OFF_MLA_CQ, OFF_MLA_CKV, OFF_MLA_KR, OFF_GDN_QKV, OFF_GDN_A, OFF_GDN_B,
            OFF_GDN_G, OFF_LRU, N_IN)
    bounds = tuple(zip(cuts[:-1], cuts[1:]))
    parts = _split_cols(dense_nd(u, *pair("w_in"), keep_pad=True), bounds)
    p = _Cols({b: s.reshape(B, T, -1) for b, s in zip(bounds, parts)}, (B, T, N_IN))
    ys = (_fox_branch(p, w["fox_bf"][l]),
          _mla_branch(p, w["mla_gq"][l], w["mla_wq"][l], w["mla_gkv"][l], w["mla_wkv"][l], cos, sin),
          _gdn_branch(p, w["gdn_conv"][l], w["gdn_alog"][l], w["gdn_dtb"][l], w["gdn_gon"][l]),
          _rglru_branch(p, valid, w["lru_conv"][l], w["lru_conv_b"][l], w["lru_wa"][l], w["lru_ba"][l],
                        w["lru_wx"][l], w["lru_bx"][l], w["lru_lam"][l]))
    w_gate = [jnp.moveaxis(t, 0, 1).reshape(D, N_BRANCH * D) for t in pair("w_gate")]
    gates = jax.nn.sigmoid(dense_nd(u, *w_gate, keep_pad=True) + w["b_gate"][l].reshape(-1))
    gates = _split_cols(gates, tuple((n * D, (n + 1) * D) for n in range(N_BRANCH)))
    merged = gates[0].reshape(B, T, D) * dense_nd(ys[0], *pair("w_branch", 0))
    for n in range(1, N_BRANCH):
        merged = merged + gates[n].reshape(B, T, D) * dense_nd(ys[n], *pair("w_branch", n))
    return dense_nd(merged, *pair("w_out"))


def _local_loss(w, x, loss_target, wb):
    B, S, D = x.shape
    T = BLOCK + S
    h = jnp.concatenate([jnp.zeros((B, PAD_LEN, D), F32),
                         jnp.broadcast_to(w["meta"][None], (B, N_META, D)), x], axis=1)
    pos = jnp.arange(T)
    valid = pos >= PAD_LEN
    rel = (pos - PAD_LEN).astype(F32)
    inv_freq = ROPE_BASE ** (-(jnp.arange(0, MLA_ROPE, 2, dtype=F32) / MLA_ROPE))
    ang = rel[:, None] * inv_freq[None, :]
    cos, sin = jnp.cos(ang), jnp.sin(ang)
    def pair(name, l):
        return wb[name][l], w[name][l]

    for l in range(DEPTH):
        h = h + 0.5 * _swiglu(_rmsnorm_rows(h, w["ln_ffn1"][l]), pair("ffn1_wi", l), pair("ffn1_wo", l))
        u = jnp.where(valid[None, :, None], _rmsnorm_rows(h, w["ln_mix"][l]), 0)
        h = h + _mixer(u, valid, cos, sin, w, wb, l)
        h = h + 0.5 * _swiglu(_rmsnorm_rows(h, w["ln_ffn2"][l]), pair("ffn2_wi", l), pair("ffn2_wo", l))
    y = _rmsnorm_rows(h, w["ln_final"])[:, BLOCK:]
    err = jnp.square(y - loss_target)
    return 0.5 * jnp.sum(jnp.mean(err, axis=-1))


MESH = pl.DeviceIdType.MESH
HBM = pl.BlockSpec(memory_space=pl.ANY)


def _place():
    x, y, c = lax.axis_index("x"), lax.axis_index("y"), lax.axis_index("c")
    return x, y, c


def _other_chip(x, y, r):
    return (1 - x if r & 2 else x), (1 - y if r & 1 else y)


def _gather_chips(shard):
    R, C = shard.shape
    H = R // 2

    def body(x_ref, out_ref, send_sems, recv_sems, local_sem):
        x, y, c = _place()
        me = 2 * x + y

        def rows(chip, half):
            return out_ref.at[chip, pl.ds(half * H, H), :]

        def copy(sem, src, dst, to):
            return pltpu.make_async_remote_copy(src_ref=src, dst_ref=dst, send_sem=send_sems.at[sem],
                                                recv_sem=recv_sems.at[sem], device_id=to, device_id_type=MESH)

        mine = pltpu.make_async_copy(x_ref, out_ref.at[me], local_sem)
        mine.start()
        started = []
        for r in (1, 2, 3):
            ox, oy = _other_chip(x, y, r)
            cp = copy(r - 1, x_ref.at[pl.ds(c * H, H), :], rows(me, c), (ox, oy, c))
            cp.start()
            started.append(cp)
        for r in (1, 2, 3):
            ox, oy = _other_chip(x, y, r)
            src = 2 * ox + oy
            copy(r - 1, rows(src, c), rows(src, c), (x, y, c)).wait_recv()
            fw = copy(2 + r, rows(src, c), rows(src, c), (x, y, 1 - c))
            fw.start()
            started.append(fw)
        for r in (1, 2, 3):
            ox, oy = _other_chip(x, y, r)
            src = 2 * ox + oy
            copy(2 + r, rows(src, 1 - c), rows(src, 1 - c), (x, y, c)).wait_recv()
        for cp in started:
            cp.wait_send()
        mine.wait()

    return _pcall(
        body, name="gather_chips", in_specs=[HBM], out_specs=HBM,
        out_shape=jax.ShapeDtypeStruct((N_CHIPS, R, C), shard.dtype),
        scratch_shapes=[pltpu.SemaphoreType.DMA((6,)), pltpu.SemaphoreType.DMA((6,)),
                        pltpu.SemaphoreType.DMA],
    )(shard)


def _window(ref, lead, axis, chip, width):
    idx = [slice(None)] * len(ref.shape)
    if lead is not None:
        idx[0] = lead
    idx[axis] = pl.ds(chip * width, width)
    return ref.at[tuple(idx)]


def _gather_layers(shards, axes):
    n_t = len(shards)
    widths = [s.shape[a] for s, a in zip(shards, axes)]
    fulls = [s.shape[:a] + (N_CHIPS * s.shape[a],) + s.shape[a + 1:] for s, a in zip(shards, axes)]

    def body(*refs):
        x_refs, out_refs = refs[:n_t], refs[n_t:2 * n_t]
        send_sems, recv_sems = refs[2 * n_t:]
        x, y, c = _place()
        me = 2 * x + y

        def win(t, lead, chip):
            return _window(out_refs[t], lead, axes[t], chip, widths[t])

        def copy(sem, src, dst, to):
            return pltpu.make_async_remote_copy(src_ref=src, dst_ref=dst, send_sem=send_sems.at[sem],
                                                recv_sem=recv_sems.at[sem], device_id=to, device_id_type=MESH)

        started = []
        for t in range(n_t):
            cp = copy(7 * t + 6, x_refs[t], win(t, None, me), (x, y, 1 - c))
            cp.start()
            started.append(cp)
        for r in (1, 2, 3):
            ox, oy = _other_chip(x, y, r)
            for t in range(n_t):
                cp = copy(7 * t + r - 1, x_refs[t].at[c], win(t, c, me), (ox, oy, c))
                cp.start()
                started.append(cp)
        for r in (1, 2, 3):
            ox, oy = _other_chip(x, y, r)
            src = 2 * ox + oy
            for t in range(n_t):
                copy(7 * t + r - 1, win(t, c, src), win(t, c, src), (x, y, c)).wait_recv()
                fw = copy(7 * t + 2 + r, win(t, c, src), win(t, c, src), (x, y, 1 - c))
                fw.start()
                started.append(fw)
        for r in (1, 2, 3):
            ox, oy = _other_chip(x, y, r)
            src = 2 * ox + oy
            for t in range(n_t):
                copy(7 * t + 2 + r, win(t, 1 - c, src), win(t, 1 - c, src), (x, y, c)).wait_recv()
        for t in range(n_t):
            copy(7 * t + 6, win(t, None, me), win(t, None, me), (x, y, c)).wait_recv()
        for cp in started:
            cp.wait_send()

    return _pcall(
        body, name="gather_layers", in_specs=[HBM] * n_t, out_specs=[HBM] * n_t,
        out_shape=[jax.ShapeDtypeStruct(f, s.dtype) for f, s in zip(fulls, shards)],
        scratch_shapes=[pltpu.SemaphoreType.DMA((7 * n_t,)), pltpu.SemaphoreType.DMA((7 * n_t,))],
    )(*shards)


def _swap_layers(gs):
    n_t = len(gs)

    def body(*refs):
        g_refs, out_refs = refs[:n_t], refs[n_t:2 * n_t]
        send_sems, recv_sems = refs[2 * n_t:]
        x, y, c = _place()
        cps = []
        for t in range(n_t):
            cp = pltpu.make_async_remote_copy(
                src_ref=g_refs[t].at[1 - c], dst_ref=out_refs[t], send_sem=send_sems.at[t],
                recv_sem=recv_sems.at[t], device_id=(x, y, 1 - c), device_id_type=MESH)
            cp.start()
            cps.append(cp)
        for cp in cps:
            cp.wait()

    return _pcall(
        body, name="swap_layers", in_specs=[HBM] * n_t, out_specs=[HBM] * n_t,
        out_shape=[jax.ShapeDtypeStruct(g.shape[1:], g.dtype) for g in gs],
        scratch_shapes=[pltpu.SemaphoreType.DMA((n_t,)), pltpu.SemaphoreType.DMA((n_t,))],
    )(*gs)


def _add_layer(g, other, c):
    shape = other.shape
    last = shape[-1]
    rows = math.prod(shape[:-1])
    tr = _row_tile(rows, 4 * last, ROW_BLOCK_BYTES)

    def body(c_ref, a_ref, b_ref, o_ref):
        o_ref[...] = (a_ref[0] + b_ref[...]).astype(BF16)

    out = _pcall(
        body, name="add_layer",
        grid_spec=pltpu.PrefetchScalarGridSpec(
            num_scalar_prefetch=1, grid=(rows // tr,),
            in_specs=[pl.BlockSpec((1, tr, last), lambda i, c_ref: (c_ref[0], i, 0)),
                      pl.BlockSpec((tr, last), lambda i, c_ref: (i, 0))],
            out_specs=pl.BlockSpec((tr, last), lambda i, c_ref: (i, 0))),
        out_shape=jax.ShapeDtypeStruct((rows, last), BF16),
        compiler_params=pltpu.CompilerParams(dimension_semantics=("parallel",)),
    )(c.reshape(1).astype(jnp.int32), g.reshape(2, rows, last), other.reshape(rows, last))
    return out.reshape(shape)


def _scatter_layers(ps, axes):
    n_t = len(ps)
    widths = [p.shape[a] // N_CHIPS for p, a in zip(ps, axes)]
    wins = [p.shape[:a] + (w,) + p.shape[a + 1:] for p, a, w in zip(ps, axes, widths)]

    def body(*refs):
        p_refs, out_refs = refs[:n_t], refs[n_t:2 * n_t]
        send_sems, recv_sems = refs[2 * n_t:]
        x, y, c = _place()
        cps = []
        for r in (1, 2, 3):
            ox, oy = _other_chip(x, y, r)
            for t in range(n_t):
                cp = pltpu.make_async_remote_copy(
                    src_ref=_window(p_refs[t], None, axes[t], 2 * ox + oy, widths[t]), dst_ref=out_refs[t].at[r - 1],
                    send_sem=send_sems.at[3 * t + r - 1], recv_sem=recv_sems.at[3 * t + r - 1],
                    device_id=(ox, oy, c), device_id_type=MESH)
                cp.start()
                cps.append(cp)
        for cp in cps:
            cp.wait()

    return _pcall(
        body, name="scatter_layers", in_specs=[HBM] * n_t, out_specs=[HBM] * n_t,
        out_shape=[jax.ShapeDtypeStruct((N_CHIPS - 1,) + w, p.dtype) for w, p in zip(wins, ps)],
        scratch_shapes=[pltpu.SemaphoreType.DMA((3 * n_t,)), pltpu.SemaphoreType.DMA((3 * n_t,))],
    )(*ps)


def _sum_chips(p, q, axis, me):
    win = q.shape[1:]
    C = win[-1]
    H = math.prod(win[:-1])
    if axis == p.ndim - 1:
        tr = _row_tile(H, 4 * C, ROW_BLOCK_BYTES)
        grid = H // tr
        p = p.reshape(H, N_CHIPS * C)
        p_spec = pl.BlockSpec((tr, C), lambda i, me_ref: (i, me_ref[0]))
    else:
        pre, inner = math.prod(p.shape[:axis]), math.prod(win[axis:-1])
        tr = _row_tile(inner, 4 * C, ROW_BLOCK_BYTES)
        nb = inner // tr
        grid = pre * nb
        p = p.reshape(pre, N_CHIPS, inner, C)
        p_spec = pl.BlockSpec((1, 1, tr, C), lambda i, me_ref: (i // nb, me_ref[0], i % nb, 0))

    def body(me_ref, p_ref, q0, q1, q2, q3, o_ref):
        own = p_ref[...].reshape(tr, C).astype(F32)
        terms = [jnp.where(me_ref[0] == chip, own, qr[0].astype(F32)) for chip, qr in enumerate((q0, q1, q2, q3))]
        o_ref[...] = ((terms[0] + terms[1]) + terms[2]) + terms[3]

    def q_spec(chip):
        return pl.BlockSpec(
            (1, tr, C), lambda i, me_ref: (jnp.maximum(jnp.bitwise_xor(me_ref[0], chip), 1) - 1, i, 0))

    q = q.reshape(N_CHIPS - 1, H, C)
    return _pcall(
        body, name="sum_chips",
        grid_spec=pltpu.PrefetchScalarGridSpec(
            num_scalar_prefetch=1, grid=(grid,),
            in_specs=[p_spec, q_spec(0), q_spec(1), q_spec(2), q_spec(3)],
            out_specs=pl.BlockSpec((tr, C), lambda i, me_ref: (i, 0))),
        out_shape=jax.ShapeDtypeStruct((H, C), F32),
        compiler_params=pltpu.CompilerParams(dimension_semantics=("parallel",)),
    )(me.reshape(1).astype(jnp.int32), p, q, q, q, q).reshape(win)


def _send_layers(rs):
    n_t = len(rs)

    def body(*refs):
        r_refs, out_refs = refs[:n_t], refs[n_t:2 * n_t]
        send_sems, recv_sems = refs[2 * n_t:]
        x, y, c = _place()
        cps = []
        for t in range(n_t):
            cp = pltpu.make_async_remote_copy(
                src_ref=r_refs[t], dst_ref=out_refs[t], send_sem=send_sems.at[t],
                recv_sem=recv_sems.at[t], device_id=(x, y, 1 - c), device_id_type=MESH)
            cp.start()
            cps.append(cp)
        for cp in cps:
            cp.wait()

    return _pcall(
        body, name="send_layers", in_specs=[HBM] * n_t, out_specs=[HBM] * n_t,
        out_shape=[jax.ShapeDtypeStruct(r.shape, r.dtype) for r in rs],
        scratch_shapes=[pltpu.SemaphoreType.DMA((n_t,)), pltpu.SemaphoreType.DMA((n_t,))],
    )(*rs)


def _reduce_scatter_layers(gs, axes):
    x, y, c = _place()
    others = _swap_layers(gs)
    ps = [_add_layer(g, o, c) for g, o in zip(gs, others)]
    qs = _scatter_layers(ps, [a - 1 for a in axes])
    rs = [_sum_chips(p, q, a - 1, 2 * x + y) for p, q, a in zip(ps, qs, axes)]
    theirs = _send_layers(rs)
    return [jnp.where(c == 0, jnp.stack([r, o]), jnp.stack([o, r])) for r, o in zip(rs, theirs)]


def _allgather_devices(flat):
    R, C = flat.shape

    def body(x_ref, out_ref, send_sems, recv_sems, local_sem):
        x, y, c = _place()
        me = 4 * x + 2 * y + c
        mine = pltpu.make_async_copy(x_ref, out_ref.at[me], local_sem)
        mine.start()
        cps = []
        for m in range(1, 8):
            ox, oy = _other_chip(x, y, m >> 1)
            oc = 1 - c if m & 1 else c
            cp = pltpu.make_async_remote_copy(
                src_ref=x_ref, dst_ref=out_ref.at[me], send_sem=send_sems.at[m - 1],
                recv_sem=recv_sems.at[m - 1], device_id=(ox, oy, oc), device_id_type=MESH)
            cp.start()
            cps.append(cp)
        for cp in cps:
            cp.wait()
        mine.wait()

    return _pcall(
        body, name="allgather_devices", in_specs=[HBM], out_specs=HBM,
        out_shape=jax.ShapeDtypeStruct((8, R, C), flat.dtype),
        scratch_shapes=[pltpu.SemaphoreType.DMA((7,)), pltpu.SemaphoreType.DMA((7,)),
                        pltpu.SemaphoreType.DMA],
    )(flat)


def _sum_devices(slots):
    _, R, C = slots.shape
    tr = _row_tile(R, 4 * C, ROW_BLOCK_BYTES // 4)

    def body(*refs):
        o_ref = refs[8]
        acc = refs[0][0]
        for d in range(1, 8):
            acc = acc + refs[d][0]
        o_ref[...] = acc

    def spec(d):
        return pl.BlockSpec((1, tr, C), lambda i: (d, i, 0))

    return _pcall(
        body, name="sum_devices", grid=(R // tr,), in_specs=[spec(d) for d in range(8)],
        out_specs=pl.BlockSpec((tr, C), lambda i: (i, 0)),
        out_shape=jax.ShapeDtypeStruct((R, C), F32),
        compiler_params=pltpu.CompilerParams(dimension_semantics=("parallel",)),
    )(*([slots] * 8))


def _adamw(w, g, m, v):
    shape = w.shape
    C = shape[-1]
    R = math.prod(shape[:-1])
    w, g, m, v = (t.reshape(R, C) for t in (w, g, m, v))
    tr = _row_tile(R, 4 * C, ROW_BLOCK_BYTES // 2)
    c1 = 1.0 - ADAM_B1 ** ADAM_STEP
    c2 = 1.0 - ADAM_B2 ** ADAM_STEP

    def body(w_ref, g_ref, m_ref, v_ref, d_ref, nm_ref, nv_ref):
        gg = g_ref[...]
        nm = ADAM_B1 * m_ref[...] + (1.0 - ADAM_B1) * gg
        nv = ADAM_B2 * v_ref[...] + (1.0 - ADAM_B2) * jnp.square(gg)
        d_ref[...] = -ADAM_LR * ((nm / c1) / (jnp.sqrt(nv / c2) + ADAM_EPS) + ADAM_WD * w_ref[...])
        nm_ref[...] = nm
        nv_ref[...] = nv

    spec = pl.BlockSpec((tr, C), lambda i: (i, 0))
    outs = _pcall(
        body, name="adamw", grid=(R // tr,), in_specs=[spec] * 4, out_specs=[spec] * 3,
        out_shape=[jax.ShapeDtypeStruct((R, C), F32)] * 3,
        compiler_params=pltpu.CompilerParams(dimension_semantics=("parallel",)),
    )(w, g, m, v)
    return [o.reshape(shape) for o in outs]


def _to_flat(parts):
    flat = jnp.concatenate([p.reshape(-1) for p in parts])
    unit = FLAT_COLS * FLAT_ROW_ALIGN
    pad = (-flat.shape[0]) % unit
    if pad:
        flat = jnp.concatenate([flat, jnp.zeros((pad,), flat.dtype)])
    return flat.reshape(-1, FLAT_COLS)


def _from_flat(flat, shapes):
    flat = flat.reshape(-1)
    out, off = [], 0
    for s in shapes:
        n = math.prod(s)
        out.append(flat[off:off + n].reshape(s))
        off += n
    return out


def kernel(x, meta, ln_ffn1, ffn1_wi, ffn1_wo, ln_mix, w_in, fox_bf, mla_gq, mla_wq, mla_gkv, mla_wkv, gdn_conv, gdn_alog, gdn_dtb, gdn_gon, lru_conv, lru_conv_b, lru_wa, lru_ba, lru_wx, lru_bx, lru_lam, w_gate, b_gate, w_branch, w_out, ln_ffn2, ffn2_wi, ffn2_wo, ln_final, loss_target, m_meta, m_ln_ffn1, m_ffn1_wi, m_ffn1_wo, m_ln_mix, m_w_in, m_fox_bf, m_mla_gq, m_mla_wq, m_mla_gkv, m_mla_wkv, m_gdn_conv, m_gdn_alog, m_gdn_dtb, m_gdn_gon, m_lru_conv, m_lru_conv_b, m_lru_wa, m_lru_ba, m_lru_wx, m_lru_bx, m_lru_lam, m_w_gate, m_b_gate, m_w_branch, m_w_out, m_ln_ffn2, m_ffn2_wi, m_ffn2_wo, m_ln_final, v_meta, v_ln_ffn1, v_ffn1_wi, v_ffn1_wo, v_ln_mix, v_w_in, v_fox_bf, v_mla_gq, v_mla_wq, v_mla_gkv, v_mla_wkv, v_gdn_conv, v_gdn_alog, v_gdn_dtb, v_gdn_gon, v_lru_conv, v_lru_conv_b, v_lru_wa, v_lru_ba, v_lru_wx, v_lru_bx, v_lru_lam, v_w_gate, v_b_gate, v_w_branch, v_w_out, v_ln_ffn2, v_ffn2_wi, v_ffn2_wo, v_ln_final):
    ws = (meta, ln_ffn1, ffn1_wi, ffn1_wo, ln_mix, w_in, fox_bf, mla_gq, mla_wq, mla_gkv, mla_wkv, gdn_conv, gdn_alog, gdn_dtb, gdn_gon, lru_conv, lru_conv_b, lru_wa, lru_ba, lru_wx, lru_bx, lru_lam, w_gate, b_gate, w_branch, w_out, ln_ffn2, ffn2_wi, ffn2_wo, ln_final)
    ms = (m_meta, m_ln_ffn1, m_ffn1_wi, m_ffn1_wo, m_ln_mix, m_w_in, m_fox_bf, m_mla_gq, m_mla_wq, m_mla_gkv, m_mla_wkv, m_gdn_conv, m_gdn_alog, m_gdn_dtb, m_gdn_gon, m_lru_conv, m_lru_conv_b, m_lru_wa, m_lru_ba, m_lru_wx, m_lru_bx, m_lru_lam, m_w_gate, m_b_gate, m_w_branch, m_w_out, m_ln_ffn2, m_ffn2_wi, m_ffn2_wo, m_ln_final)
    vs = (v_meta, v_ln_ffn1, v_ffn1_wi, v_ffn1_wo, v_ln_mix, v_w_in, v_fox_bf, v_mla_gq, v_mla_wq, v_mla_gkv, v_mla_wkv, v_gdn_conv, v_gdn_alog, v_gdn_dtb, v_gdn_gon, v_lru_conv, v_lru_conv_b, v_lru_wa, v_lru_ba, v_lru_wx, v_lru_bx, v_lru_lam, v_w_gate, v_b_gate, v_w_branch, v_w_out, v_ln_ffn2, v_ffn2_wi, v_ffn2_wo, v_ln_final)
    names = [n for n, _ in WEIGHT_SPECS]
    axis = dict(WEIGHT_SPECS)
    wd, md, vd = dict(zip(names, ws)), dict(zip(names, ms)), dict(zip(names, vs))
    shapes = {n: wd[n].shape for n in names}
    big = [n for n in names if n in LARGE]
    few = [n for n in names if axis[n] is not None and n not in LARGE]
    whole = [n for n in names if axis[n] is None]
    x_, y_, _ = _place()
    chip = 2 * x_ + y_

    def to_dma(n, a):
        ax, w = axis[n], shapes[n][axis[n]]
        nd = len(shapes[n])
        if (ax == nd - 1 and w % LANES) or (ax == nd - 2 and w % 16):
            parts = a.shape[ax] // w
            a = jnp.moveaxis(a.reshape(a.shape[:ax] + (parts, w) + a.shape[ax + 1:]), ax, 1)
            return a, 1
        return a, ax

    def from_dma(n, a):
        ax = axis[n]
        if a.ndim == len(shapes[n]):
            return a
        a = jnp.moveaxis(a, 1, ax)
        return a.reshape(a.shape[:ax] + (-1,) + a.shape[ax + 2:])

    dma = [to_dma(n, wd[n].astype(BF16)) for n in big]
    fulls = _gather_layers([a for a, _ in dma], [ax for _, ax in dma])
    full_bf16 = {n: from_dma(n, f) for n, f in zip(big, fulls)}
    full = {n: jnp.zeros(f.shape, F32) for n, f in full_bf16.items()}
    gathered = _gather_chips(_to_flat([wd[n] for n in few]))
    per_chip = [_from_flat(gathered[k], [shapes[n] for n in few]) for k in range(N_CHIPS)]
    for i, n in enumerate(few):
        full[n] = jnp.concatenate([per_chip[k][i] for k in range(N_CHIPS)], axis=axis[n])
    full.update({n: wd[n] for n in whole})

    loss, (gw, gx) = jax.value_and_grad(_local_loss, argnums=(0, 1))(full, x, loss_target, full_bf16)
    loss = lax.psum(loss, ("x", "y", "c"))

    dma = [to_dma(n, gw[n]) for n in big]
    reduced = _reduce_scatter_layers([a for a, _ in dma], [ax for _, ax in dma])
    grads = {n: r.reshape(shapes[n]) for n, r in zip(big, reduced)}
    rest = few + whole
    summed = _from_flat(_sum_devices(_allgather_devices(_to_flat([gw[n] for n in rest]))),
                        [gw[n].shape for n in rest])
    for n, g in zip(rest, summed):
        if axis[n] is not None:
            g = lax.dynamic_slice_in_dim(g, chip * shapes[n][axis[n]], shapes[n][axis[n]], axis=axis[n])
        grads[n] = g

    delta, new_m, new_v = {}, {}, {}
    for n in big:
        delta[n], new_m[n], new_v[n] = _adamw(wd[n], grads[n], md[n], vd[n])
    outs = _adamw(*[_to_flat([d[n] for n in rest]) for d in (wd, grads, md, vd)])
    for res, flat in zip((delta, new_m, new_v), outs):
        res.update(zip(rest, _from_flat(flat, [shapes[n] for n in rest])))

    return (loss, gx, *[grads[n] for n in names], *[delta[n] for n in names],
            *[new_m[n] for n in names], *[new_v[n] for n in names])
```

```python
import functools
import math

import jax
import jax.numpy as jnp
from jax import lax
from jax.experimental import pallas as pl
from jax.experimental.pallas import tpu as pltpu

F32 = jnp.float32
BF16 = jnp.bfloat16

N_META = 16
BLOCK = 128
PAD_LEN = BLOCK - N_META
EPS = 1e-6
NEG_INF = -1e30
N_BRANCH = 4
FOX_HEADS, FOX_DH = 4, 64
MLA_HEADS, MLA_NOPE, MLA_ROPE, MLA_DV = 4, 64, 32, 64
MLA_Q_RANK, MLA_KV_RANK = 192, 128
ROPE_BASE = 10000.0
GDN_HEADS, GDN_DK, GDN_DV, GDN_CHUNK = 4, 64, 64, 64
LRU_WIDTH, LRU_BLOCKS, LRU_C = 256, 4, 8.0
DEPTH = 2

OFF_FOX_QKV = 0
OFF_FOX_F = OFF_FOX_QKV + 3 * FOX_HEADS * FOX_DH
OFF_MLA_CQ = OFF_FOX_F + FOX_HEADS
OFF_MLA_CKV = OFF_MLA_CQ + MLA_Q_RANK
OFF_MLA_KR = OFF_MLA_CKV + MLA_KV_RANK
OFF_GDN_QKV = OFF_MLA_KR + MLA_ROPE
OFF_GDN_A = OFF_GDN_QKV + GDN_HEADS * (2 * GDN_DK + GDN_DV)
OFF_GDN_B = OFF_GDN_A + GDN_HEADS
OFF_GDN_G = OFF_GDN_B + GDN_HEADS
OFF_LRU = OFF_GDN_G + GDN_HEADS * GDN_DV
N_IN = OFF_LRU + LRU_WIDTH

ADAM_LR, ADAM_B1, ADAM_B2, ADAM_EPS, ADAM_WD, ADAM_STEP = 0.001, 0.9, 0.999, 1e-08, 0.01, 10

WEIGHT_SPECS = (
    ("meta", 1), ("ln_ffn1", None), ("ffn1_wi", 2), ("ffn1_wo", 1), ("ln_mix", None), ("w_in", 2),
    ("fox_bf", None), ("mla_gq", None), ("mla_wq", 2), ("mla_gkv", None), ("mla_wkv", 2),
    ("gdn_conv", 2), ("gdn_alog", None), ("gdn_dtb", None), ("gdn_gon", None), ("lru_conv", 2),
    ("lru_conv_b", None), ("lru_wa", None), ("lru_ba", None), ("lru_wx", None), ("lru_bx", None),
    ("lru_lam", None), ("w_gate", 2), ("b_gate", 2), ("w_branch", 3), ("w_out", 1),
    ("ln_ffn2", None), ("ffn2_wi", 2), ("ffn2_wo", 1), ("ln_final", None),
)
N_CHIPS = 4
LARGE = ("ffn1_wi", "ffn1_wo", "w_in", "w_gate", "w_branch", "w_out", "ffn2_wi", "ffn2_wo")

LANES = 128
VMEM_LIMIT_BYTES = 48 * 1024 * 1024
FLAT_COLS = 512
FLAT_ROW_ALIGN = 64


def _pcall(body, **kw):
    return pl.pallas_call(body, **kw)


def _pick(n, cands):
    for c in cands:
        if n % c == 0:
            return c
    return n


_DN = {"nn": (((1,), (0,)), ((), ())), "nt": (((1,), (1,)), ((), ())), "tn": (((0,), (0,)), ((), ()))}


MATMUL_OPERAND_TILE_BYTES = 8 * 1024 * 1024


def _k_tile(K, row_bytes, lane_axis):
    for tk in (K, 4224, 2816, 2112, 1408, 1056, 1024, 768, 704, 512, 384, 256, 128):
        aligned = tk == K or tk % LANES == 0 or (not lane_axis and tk % 16 == 0)
        if tk <= K and K % tk == 0 and aligned and tk * row_bytes <= MATMUL_OPERAND_TILE_BYTES:
            return tk
    return K


def _matmul(a, b, mode, name, out_dtype=F32):
    if mode == "nn":
        (M, K), (_, N) = a.shape, b.shape
    elif mode == "nt":
        (M, K), (N, _) = a.shape, b.shape
    else:
        (K, M), (_, N) = a.shape, b.shape
    tm = _pick(M, (1024, 768, 512, 1408, 384, 256, 128, 64, 32, 16, 8))
    tn = _pick(N, (1408, 1024, 1280, 512, 256, 128))
    tk = _k_tile(K, tm * a.dtype.itemsize + tn * b.dtype.itemsize, mode != "tn")
    nk = K // tk
    a_spec = {"nn": pl.BlockSpec((tm, tk), lambda i, j, k: (i, k)),
              "nt": pl.BlockSpec((tm, tk), lambda i, j, k: (i, k)),
              "tn": pl.BlockSpec((tk, tm), lambda i, j, k: (k, i))}[mode]
    b_spec = {"nn": pl.BlockSpec((tk, tn), lambda i, j, k: (k, j)),
              "nt": pl.BlockSpec((tn, tk), lambda i, j, k: (j, k)),
              "tn": pl.BlockSpec((tk, tn), lambda i, j, k: (k, j))}[mode]
    dn = _DN[mode]

    def body(a_ref, b_ref, o_ref, acc_ref):
        k = pl.program_id(2)
        part = lax.dot_general(a_ref[...].astype(BF16), b_ref[...].astype(BF16), dn,
                               preferred_element_type=F32)

        if nk == 1:
            o_ref[...] = part.astype(o_ref.dtype)
        else:
            @pl.when(k == 0)
            def _():
                acc_ref[...] = part

            @pl.when((k > 0) & (k < nk - 1))
            def _():
                acc_ref[...] += part

            @pl.when(k == nk - 1)
            def _():
                o_ref[...] = (acc_ref[...] + part).astype(o_ref.dtype)

    return _pcall(
        body, name=name, grid=(M // tm, N // tn, nk),
        in_specs=[a_spec, b_spec], out_specs=pl.BlockSpec((tm, tn), lambda i, j, k: (i, j)),
        out_shape=jax.ShapeDtypeStruct((M, N), out_dtype),
        scratch_shapes=[pltpu.VMEM((tm, tn) if nk > 1 else (8, LANES), F32)],
        compiler_params=pltpu.CompilerParams(
            dimension_semantics=("parallel", "parallel", "arbitrary"),
            vmem_limit_bytes=VMEM_LIMIT_BYTES),
    )(a, b)


@functools.partial(jax.custom_vjp, nondiff_argnums=(3,))
def dense(a, wb, w, out_dtype):
    return _matmul(a, wb, "nn", "dense_fwd", out_dtype)


def _dense_fwd(a, wb, w, out_dtype):
    return _matmul(a, wb, "nn", "dense_fwd", out_dtype), (a, wb)


def _dense_bwd(out_dtype, res, g):
    a, wb = res
    return (_matmul(g, wb, "nt", "dense_dgrad", a.dtype), jnp.zeros_like(wb),
            _matmul(a, g, "tn", "dense_wgrad", F32))


dense.defvjp(_dense_fwd, _dense_bwd)


def dense_nd(a, wb, w, out_dtype=F32, keep_pad=False):
    K, N = wb.shape
    pad = (-N) % (4 * LANES if N > 4 * LANES else LANES)
    if pad:
        wb = jnp.pad(wb, ((0, 0), (0, pad)))
        w = jnp.pad(w, ((0, 0), (0, pad)))
    out = dense(a.reshape(-1, K), wb, w, out_dtype)
    if keep_pad:
        return out
    if pad:
        out = out[:, :N]
    return out.reshape(a.shape[:-1] + (N,))


@functools.partial(jax.custom_vjp, nondiff_argnums=(1,))
def _split_cols(x, bounds):
    return tuple(x[:, a:b] for a, b in bounds)


def _split_cols_fwd(x, bounds):
    return _split_cols(x, bounds), jnp.zeros((x.shape[0], x.shape[1] - bounds[-1][1]), x.dtype)


def _split_cols_bwd(bounds, rest, cts):
    return (jnp.concatenate(list(cts) + ([rest] if rest.shape[1] else []), axis=1),)


_split_cols.defvjp(_split_cols_fwd, _split_cols_bwd)


class _Cols:
    def __init__(self, ranges, shape):
        self.ranges, self.shape = ranges, shape

    def __getitem__(self, idx):
        return self.ranges[(idx[-1].start, idx[-1].stop)]


_NT = (((1,), (1,)), ((), ()))


def _att_blk(T):
    return _pick(T, (384, 256, 128))


def _causal_pairs(n, by_query):
    if by_query:
        pairs = [(i, j) for i in range(n) for j in range(i + 1)]
    else:
        pairs = [(i, j) for j in range(n) for i in range(j, n)]
    return (jnp.array([p[0] for p in pairs], jnp.int32), jnp.array([p[1] for p in pairs], jnp.int32))


def _attn_fwd_call(q, k, v, cum_col, cum_row, scale):
    BH, T, dk = q.shape
    dv = v.shape[2]
    blk = _att_blk(T)
    n = T // blk
    has_cum = cum_col is not None

    def body(i_tab, j_tab, *refs):
        if has_cum:
            q_ref, k_ref, v_ref, cq_ref, ck_ref, o_ref, lse_ref, m_s, l_s, acc_s = refs
        else:
            q_ref, k_ref, v_ref, o_ref, lse_ref, m_s, l_s, acc_s = refs
        i = i_tab[pl.program_id(1)]
        j = j_tab[pl.program_id(1)]

        @pl.when(j == 0)
        def _():
            m_s[...] = jnp.full_like(m_s, NEG_INF)
            l_s[...] = jnp.zeros_like(l_s)
            acc_s[...] = jnp.zeros_like(acc_s)

        def step(masked):
            s = lax.dot_general(q_ref[0].astype(BF16), k_ref[0].astype(BF16), _NT,
                                preferred_element_type=F32) * scale
            if has_cum:
                s = s + cq_ref[0] - ck_ref[0]
            if masked:
                qpos = i * blk + lax.broadcasted_iota(jnp.int32, (blk, blk), 0)
                kpos = j * blk + lax.broadcasted_iota(jnp.int32, (blk, blk), 1)
                mask = (kpos <= qpos) & (kpos >= PAD_LEN)
                s = jnp.where(mask, s, NEG_INF)
            m_prev = m_s[...]
            m_new = jnp.maximum(m_prev, jnp.max(s, axis=1, keepdims=True))
            p = jnp.exp(s - m_new)
            alpha = jnp.exp(m_prev - m_new)
            l_s[...] = alpha * l_s[...] + jnp.sum(p, axis=1, keepdims=True)
            acc_s[...] = alpha * acc_s[...] + jnp.dot(p.astype(BF16), v_ref[0].astype(BF16),
                                                      preferred_element_type=F32)
            m_s[...] = m_new

        @pl.when((j == i) | (j == 0))
        def _():
            step(True)

        @pl.when((j < i) & (j > 0))
        def _():
            step(False)

        @pl.when(j == i)
        def _():
            o_ref[0] = acc_s[...] / l_s[...]
            lse_ref[0] = m_s[...] + jnp.log(l_s[...])

    q_idx = lambda b, t, it, jt: (b, it[t], 0)
    kv_idx = lambda b, t, it, jt: (b, jt[t], 0)
    in_specs = [pl.BlockSpec((1, blk, dk), q_idx), pl.BlockSpec((1, blk, dk), kv_idx),
                pl.BlockSpec((1, blk, dv), kv_idx)]
    args = [q, k, v]
    if has_cum:
        in_specs += [pl.BlockSpec((1, blk, 1), q_idx),
                     pl.BlockSpec((1, 1, blk), lambda b, t, it, jt: (b, 0, jt[t]))]
        args += [cum_col, cum_row]
    i_tab, j_tab = _causal_pairs(n, by_query=True)
    return _pcall(
        body, name="attn_fwd_cum" if has_cum else "attn_fwd",
        grid_spec=pltpu.PrefetchScalarGridSpec(
            num_scalar_prefetch=2, grid=(BH, len(i_tab)), in_specs=in_specs,
            out_specs=[pl.BlockSpec((1, blk, dv), q_idx), pl.BlockSpec((1, blk, 1), q_idx)],
            scratch_shapes=[pltpu.VMEM((blk, 1), F32), pltpu.VMEM((blk, 1), F32), pltpu.VMEM((blk, dv), F32)]),
        out_shape=[jax.ShapeDtypeStruct((BH, T, dv), F32), jax.ShapeDtypeStruct((BH, T, 1), F32)],
        compiler_params=pltpu.CompilerParams(
            dimension_semantics=("parallel", "arbitrary"), vmem_limit_bytes=VMEM_LIMIT_BYTES),
    )(i_tab, j_tab, *args)


def _attn_dkv_call(q, k, v, do, lse_row, delta_row, cum_row, cum_col, scale):
    BH, T, dk = q.shape
    dv = v.shape[2]
    blk = _att_blk(T)
    n = T // blk
    has_cum = cum_row is not None

    def body(i_tab, j_tab, *refs):
        if has_cum:
            (k_ref, v_ref, q_ref, do_ref, lse_ref, dl_ref, cq_ref, ck_ref,
             dk_ref, dv_ref, dc_ref, dk_s, dv_s, dc_s) = refs
        else:
            k_ref, v_ref, q_ref, do_ref, lse_ref, dl_ref, dk_ref, dv_ref, dk_s, dv_s = refs
        i = i_tab[pl.program_id(1)]
        j = j_tab[pl.program_id(1)]

        @pl.when(i == j)
        def _():
            dk_s[...] = jnp.zeros_like(dk_s)
            dv_s[...] = jnp.zeros_like(dv_s)
            if has_cum:
                dc_s[...] = jnp.zeros_like(dc_s)

        def step(masked):
            kb = k_ref[0].astype(BF16)
            qb = q_ref[0].astype(BF16)
            dob = do_ref[0].astype(BF16)
            st = lax.dot_general(kb, qb, _NT, preferred_element_type=F32) * scale
            if has_cum:
                st = st + cq_ref[0] - ck_ref[0]
            pt = jnp.exp(jnp.minimum(st - lse_ref[0], 0.0))
            if masked:
                kpos = j * blk + lax.broadcasted_iota(jnp.int32, (blk, blk), 0)
                qpos = i * blk + lax.broadcasted_iota(jnp.int32, (blk, blk), 1)
                pt = jnp.where((kpos <= qpos) & (kpos >= PAD_LEN), pt, 0.0)
            dv_s[...] += jnp.dot(pt.astype(BF16), dob, preferred_element_type=F32)
            dpt = lax.dot_general(v_ref[0].astype(BF16), dob, _NT, preferred_element_type=F32)
            dst = pt * (dpt - dl_ref[0])
            dk_s[...] += jnp.dot(dst.astype(BF16), qb, preferred_element_type=F32) * scale
            if has_cum:
                dc_s[...] -= jnp.sum(dst, axis=1, keepdims=True)

        @pl.when((i == j) | (j == 0))
        def _():
            step(True)

        @pl.when((i > j) & (j > 0))
        def _():
            step(False)

        @pl.when(i == n - 1)
        def _():
            dk_ref[0] = dk_s[...]
            dv_ref[0] = dv_s[...]
            if has_cum:
                dc_ref[0] = dc_s[...]

    k_idx = lambda b, t, it, jt: (b, jt[t], 0)
    q_idx = lambda b, t, it, jt: (b, it[t], 0)
    row_idx = lambda b, t, it, jt: (b, 0, it[t])
    in_specs = [pl.BlockSpec((1, blk, dk), k_idx), pl.BlockSpec((1, blk, dv), k_idx),
                pl.BlockSpec((1, blk, dk), q_idx), pl.BlockSpec((1, blk, dv), q_idx),
                pl.BlockSpec((1, 1, blk), row_idx), pl.BlockSpec((1, 1, blk), row_idx)]
    args = [k, v, q, do, lse_row, delta_row]
    out_specs = [pl.BlockSpec((1, blk, dk), k_idx), pl.BlockSpec((1, blk, dv), k_idx)]
    out_shape = [jax.ShapeDtypeStruct((BH, T, dk), F32), jax.ShapeDtypeStruct((BH, T, dv), F32)]
    scratch = [pltpu.VMEM((blk, dk), F32), pltpu.VMEM((blk, dv), F32)]
    if has_cum:
        in_specs += [pl.BlockSpec((1, 1, blk), row_idx), pl.BlockSpec((1, blk, 1), k_idx)]
        args += [cum_row, cum_col]
        out_specs.append(pl.BlockSpec((1, blk, 1), k_idx))
        out_shape.append(jax.ShapeDtypeStruct((BH, T, 1), F32))
        scratch.append(pltpu.VMEM((blk, 1), F32))
    i_tab, j_tab = _causal_pairs(n, by_query=False)
    return _pcall(
        body, name="attn_dkv_cum" if has_cum else "attn_dkv",
        grid_spec=pltpu.PrefetchScalarGridSpec(
            num_scalar_prefetch=2, grid=(BH, len(i_tab)), in_specs=in_specs, out_specs=out_specs,
            scratch_shapes=scratch),
        out_shape=out_shape,
        compiler_params=pltpu.CompilerParams(
            dimension_semantics=("parallel", "arbitrary"), vmem_limit_bytes=VMEM_LIMIT_BYTES),
    )(i_tab, j_tab, *args)


def _attn_dq_call(q, k, v, do, lse_col, delta_col, cum_col, cum_row, scale):
    BH, T, dk = q.shape
    dv = v.shape[2]
    blk = _att_blk(T)
    n = T // blk
    has_cum = cum_col is not None

    def body(i_tab, j_tab, *refs):
        if has_cum:
            q_ref, k_ref, v_ref, do_ref, lse_ref, dl_ref, cq_ref, ck_ref, dq_ref, dc_ref, dq_s, dc_s = refs
        else:
            q_ref, k_ref, v_ref, do_ref, lse_ref, dl_ref, dq_ref, dq_s = refs
        i = i_tab[pl.program_id(1)]
        j = j_tab[pl.program_id(1)]

        @pl.when(j == 0)
        def _():
            dq_s[...] = jnp.zeros_like(dq_s)
            if has_cum:
                dc_s[...] = jnp.zeros_like(dc_s)

        def step(masked):
            kb = k_ref[0].astype(BF16)
            s = lax.dot_general(q_ref[0].astype(BF16), kb, _NT, preferred_element_type=F32) * scale
            if has_cum:
                s = s + cq_ref[0] - ck_ref[0]
            p = jnp.exp(jnp.minimum(s - lse_ref[0], 0.0))
            if masked:
                qpos = i * blk + lax.broadcasted_iota(jnp.int32, (blk, blk), 0)
                kpos = j * blk + lax.broadcasted_iota(jnp.int32, (blk, blk), 1)
                p = jnp.where((kpos <= qpos) & (kpos >= PAD_LEN), p, 0.0)
            dp = lax.dot_general(do_ref[0].astype(BF16), v_ref[0].astype(BF16), _NT,
                                 preferred_element_type=F32)
            ds = p * (dp - dl_ref[0])
            dq_s[...] += jnp.dot(ds.astype(BF16), kb, preferred_element_type=F32) * scale
            if has_cum:
                dc_s[...] += jnp.sum(ds, axis=1, keepdims=True)

        @pl.when((j == i) | (j == 0))
        def _():
            step(True)

        @pl.when((j < i) & (j > 0))
        def _():
            step(False)

        @pl.when(j == i)
        def _():
            dq_ref[0] = dq_s[...]
            if has_cum:
                dc_ref[0] = dc_s[...]

    kv_idx = lambda b, t, it, jt: (b, jt[t], 0)
    q_idx = lambda b, t, it, jt: (b, it[t], 0)
    in_specs = [pl.BlockSpec((1, blk, dk), q_idx), pl.BlockSpec((1, blk, dk), kv_idx),
                pl.BlockSpec((1, blk, dv), kv_idx), pl.BlockSpec((1, blk, dv), q_idx),
                pl.BlockSpec((1, blk, 1), q_idx), pl.BlockSpec((1, blk, 1), q_idx)]
    args = [q, k, v, do, lse_col, delta_col]
    if has_cum:
        in_specs += [pl.BlockSpec((1, blk, 1), q_idx),
                     pl.BlockSpec((1, 1, blk), lambda b, t, it, jt: (b, 0, jt[t]))]
        args += [cum_col, cum_row]
    out_specs = [pl.BlockSpec((1, blk, dk), q_idx)]
    out_shape = [jax.ShapeDtypeStruct((BH, T, dk), F32)]
    scratch = [pltpu.VMEM((blk, dk), F32)]
    if has_cum:
        out_specs.append(pl.BlockSpec((1, blk, 1), q_idx))
        out_shape.append(jax.ShapeDtypeStruct((BH, T, 1), F32))
        scratch.append(pltpu.VMEM((blk, 1), F32))
    i_tab, j_tab = _causal_pairs(n, by_query=True)
    return _pcall(
        body, name="attn_dq_cum" if has_cum else "attn_dq",
        grid_spec=pltpu.PrefetchScalarGridSpec(
            num_scalar_prefetch=2, grid=(BH, len(i_tab)), in_specs=in_specs, out_specs=out_specs,
            scratch_shapes=scratch),
        out_shape=out_shape,
        compiler_params=pltpu.CompilerParams(
            dimension_semantics=("parallel", "arbitrary"), vmem_limit_bytes=VMEM_LIMIT_BYTES),
    )(i_tab, j_tab, *args)


def _make_attention(scale, has_cum):
    def fold(t):
        return t.reshape((-1,) + t.shape[2:])

    def run_fwd(q, k, v, cum):
        B, H, T, _ = q.shape
        col = cum.reshape(B * H, T, 1) if has_cum else None
        row = cum.reshape(B * H, 1, T) if has_cum else None
        o, lse = _attn_fwd_call(fold(q), fold(k), fold(v), col, row, scale)
        return o.reshape(B, H, T, -1), lse

    @jax.custom_vjp
    def attn(q, k, v, cum):
        return run_fwd(q, k, v, cum)[0]

    def attn_fwd(q, k, v, cum):
        o, lse = run_fwd(q, k, v, cum)
        return o, (q, k, v, cum, o, lse)

    def attn_bwd(res, do):
        q, k, v, cum, o, lse = res
        B, H, T, _ = q.shape
        delta = jnp.sum(do * o, axis=-1).reshape(B * H, T, 1)
        col = cum.reshape(B * H, T, 1) if has_cum else None
        row = cum.reshape(B * H, 1, T) if has_cum else None
        qf, kf, vf, dof = fold(q), fold(k), fold(v), fold(do)
        outs = _attn_dkv_call(qf, kf, vf, dof, lse.reshape(B * H, 1, T), delta.reshape(B * H, 1, T),
                              row, col, scale)
        dqs = _attn_dq_call(qf, kf, vf, dof, lse, delta, col, row, scale)
        dcum = (outs[2] + dqs[1]).reshape(B, H, T) if has_cum else jnp.zeros_like(cum)
        return (dqs[0].reshape(q.shape), outs[0].reshape(k.shape), outs[1].reshape(v.shape), dcum)

    attn.defvjp(attn_fwd, attn_bwd)
    return attn


def _rmsnorm(x, g):
    return x * lax.rsqrt(jnp.mean(x * x, axis=-1, keepdims=True) + EPS) * g


def _l2norm(x):
    return x * lax.rsqrt(jnp.sum(x * x, axis=-1, keepdims=True) + EPS)


def _row_tile(rows, row_bytes, budget):
    for t in (2048, 1024, 512, 256, 128, 64, 32, 16, 8):
        if rows % t == 0 and t * row_bytes <= budget:
            return t
    return rows


ROW_BLOCK_BYTES = 2 * 1024 * 1024


def _rms_fwd_call(x, g):
    M, D = x.shape
    tr = _row_tile(M, 4 * D, ROW_BLOCK_BYTES)

    def body(x_ref, g_ref, y_ref, r_ref):
        xv = x_ref[...]
        r = lax.rsqrt(jnp.mean(xv * xv, axis=-1, keepdims=True) + EPS)
        y_ref[...] = xv * r * g_ref[...]
        r_ref[...] = r

    return _pcall(
        body, name="rmsnorm_fwd", grid=(M // tr,),
        in_specs=[pl.BlockSpec((tr, D), lambda i: (i, 0)), pl.BlockSpec((1, D), lambda i: (0, 0))],
        out_specs=[pl.BlockSpec((tr, D), lambda i: (i, 0)), pl.BlockSpec((tr, 1), lambda i: (i, 0))],
        out_shape=[jax.ShapeDtypeStruct((M, D), F32), jax.ShapeDtypeStruct((M, 1), F32)],
        compiler_params=pltpu.CompilerParams(dimension_semantics=("parallel",)),
    )(x, g)


def _rms_bwd_call(x, g, r, dy):
    M, D = x.shape
    tr = _row_tile(M, 4 * D, ROW_BLOCK_BYTES)

    def body(x_ref, g_ref, r_ref, dy_ref, dx_ref, dg_ref):
        i = pl.program_id(0)
        rv = r_ref[...]
        xh = x_ref[...] * rv
        dyv = dy_ref[...]
        dyg = dyv * g_ref[...]
        dx_ref[...] = rv * (dyg - xh * jnp.mean(dyg * xh, axis=-1, keepdims=True))
        part = jnp.sum(dyv * xh, axis=0, keepdims=True)

        @pl.when(i == 0)
        def _():
            dg_ref[...] = part

        @pl.when(i > 0)
        def _():
            dg_ref[...] += part

    row = pl.BlockSpec((tr, D), lambda i: (i, 0))
    return _pcall(
        body, name="rmsnorm_bwd", grid=(M // tr,),
        in_specs=[row, pl.BlockSpec((1, D), lambda i: (0, 0)), pl.BlockSpec((tr, 1), lambda i: (i, 0)), row],
        out_specs=[row, pl.BlockSpec((1, D), lambda i: (0, 0))],
        out_shape=[jax.ShapeDtypeStruct((M, D), F32), jax.ShapeDtypeStruct((1, D), F32)],
        compiler_params=pltpu.CompilerParams(dimension_semantics=("arbitrary",)),
    )(x, g, r, dy)


@jax.custom_vjp
def _rmsnorm2d(x, g):
    return _rms_fwd_call(x, g)[0]


def _rmsnorm2d_fwd(x, g):
    y, r = _rms_fwd_call(x, g)
    return y, (x, g, r)


def _rmsnorm2d_bwd(res, dy):
    x, g, r = res
    return _rms_bwd_call(x, g, r, dy)


_rmsnorm2d.defvjp(_rmsnorm2d_fwd, _rmsnorm2d_bwd)


def _rmsnorm_rows(x, g):
    D = x.shape[-1]
    return _rmsnorm2d(x.reshape(-1, D), g.reshape(1, D)).reshape(x.shape)


def _glu_fwd_call(gu):
    M, F2 = gu.shape
    F = F2 // 2
    tr = _row_tile(M, 4 * F2, 2 * ROW_BLOCK_BYTES)

    def body(g_ref, u_ref, o_ref):
        gv = g_ref[...].astype(F32)
        o_ref[...] = (gv * jax.nn.sigmoid(gv) * u_ref[...].astype(F32)).astype(o_ref.dtype)

    return _pcall(
        body, name="swiglu_fwd", grid=(M // tr,),
        in_specs=[pl.BlockSpec((tr, F), lambda i: (i, 0)), pl.BlockSpec((tr, F), lambda i: (i, 1))],
        out_specs=pl.BlockSpec((tr, F), lambda i: (i, 0)),
        out_shape=jax.ShapeDtypeStruct((M, F), gu.dtype),
        compiler_params=pltpu.CompilerParams(dimension_semantics=("parallel",),
                                             vmem_limit_bytes=VMEM_LIMIT_BYTES),
    )(gu, gu)


def _glu_bwd_call(gu, da):
    M, F2 = gu.shape
    F = F2 // 2
    tr = _row_tile(M, 4 * F2, 2 * ROW_BLOCK_BYTES)

    def body(g_ref, u_ref, da_ref, o_ref):
        gv = g_ref[...].astype(F32)
        s = jax.nn.sigmoid(gv)
        dav = da_ref[...].astype(F32)
        o_ref[:, :F] = (dav * u_ref[...].astype(F32) * (s * (1.0 + gv * (1.0 - s)))).astype(o_ref.dtype)
        o_ref[:, F:] = (dav * (gv * s)).astype(o_ref.dtype)

    return _pcall(
        body, name="swiglu_bwd", grid=(M // tr,),
        in_specs=[pl.BlockSpec((tr, F), lambda i: (i, 0)), pl.BlockSpec((tr, F), lambda i: (i, 1)),
                  pl.BlockSpec((tr, F), lambda i: (i, 0))],
        out_specs=pl.BlockSpec((tr, F2), lambda i: (i, 0)),
        out_shape=jax.ShapeDtypeStruct((M, F2), gu.dtype),
        compiler_params=pltpu.CompilerParams(dimension_semantics=("parallel",),
                                             vmem_limit_bytes=VMEM_LIMIT_BYTES),
    )(gu, gu, da)


@jax.custom_vjp
def _glu(gu):
    return _glu_fwd_call(gu)


def _glu_fwd(gu):
    return _glu_fwd_call(gu), gu


def _glu_bwd(gu, da):
    return (_glu_bwd_call(gu, da),)


_glu.defvjp(_glu_fwd, _glu_bwd)


def _swiglu(h, wi, wo):
    lead, D = h.shape[:-1], h.shape[-1]
    gu = dense(h.reshape(-1, D), wi[0], wi[1], BF16)
    return dense(_glu(gu), wo[0], wo[1], F32).reshape(lead + (wo[0].shape[1],))


def _causal_dwconv(x, w):
    K, C = w.shape
    return lax.conv_general_dilated(
        x, w[:, None, :], window_strides=(1,), padding=[(K - 1, 0)],
        dimension_numbers=("NWC", "WIO", "NWC"), feature_group_count=C)


def _rope(x, cos, sin):
    half = x.shape[-1] // 2
    x1, x2 = x[..., :half], x[..., half:]
    return jnp.concatenate([x1 * cos - x2 * sin, x2 * cos + x1 * sin], axis=-1)


def _fox_branch(p, b_f):
    B, T, _ = p.shape
    qkv = p[..., OFF_FOX_QKV:OFF_FOX_F].reshape(B, T, 3, FOX_HEADS, FOX_DH)
    q = qkv[:, :, 0].transpose(0, 2, 1, 3)
    k = qkv[:, :, 1].transpose(0, 2, 1, 3)
    v = qkv[:, :, 2].transpose(0, 2, 1, 3)
    log_f = jax.nn.log_sigmoid(p[..., OFF_FOX_F:OFF_MLA_CQ] + b_f)
    cum = jnp.cumsum(log_f, axis=1).transpose(0, 2, 1)
    o = _make_attention(FOX_DH ** -0.5, True)(q, k, v, cum)
    return o.transpose(0, 2, 1, 3).reshape(B, T, FOX_HEADS * FOX_DH)


def _mla_branch(p, g_qn, w_q_up, g_kvn, w_kv_up, cos, sin):
    B, T, _ = p.shape
    cq = _rmsnorm(p[..., OFF_MLA_CQ:OFF_MLA_CKV], g_qn)
    q = (cq @ w_q_up).reshape(B, T, MLA_HEADS, MLA_NOPE + MLA_ROPE)
    ckv = _rmsnorm(p[..., OFF_MLA_CKV:OFF_MLA_KR], g_kvn)
    kv = (ckv @ w_kv_up).reshape(B, T, MLA_HEADS, MLA_NOPE + MLA_DV)
    k_rope = _rope(p[..., OFF_MLA_KR:OFF_GDN_QKV], cos, sin)
    q_rope = _rope(q[..., MLA_NOPE:], cos[:, None], sin[:, None])
    q = jnp.concatenate([q[..., :MLA_NOPE], q_rope], axis=-1)
    k = jnp.concatenate([kv[..., :MLA_NOPE],
                         jnp.broadcast_to(k_rope[:, :, None], (B, T, MLA_HEADS, MLA_ROPE))], axis=-1)
    v = kv[..., MLA_NOPE:]
    o = _make_attention((MLA_NOPE + MLA_ROPE) ** -0.5, False)(
        q.transpose(0, 2, 1, 3), k.transpose(0, 2, 1, 3), v.transpose(0, 2, 1, 3),
        jnp.zeros((B, MLA_HEADS, T), F32))
    return o.transpose(0, 2, 1, 3).reshape(B, T, MLA_HEADS * MLA_DV)


def _bmm(a, b):
    return lax.dot_general(a.astype(BF16), b.astype(BF16), (((2,), (1,)), ((0,), (0,))),
                           preferred_element_type=F32)


def _bmm_nt(a, b):
    return lax.dot_general(a.astype(BF16), b.astype(BF16), (((2,), (2,)), ((0,), (0,))),
                           preferred_element_type=F32)


def _chunk_spec(shape, index):
    return pl.BlockSpec((1,) + tuple(shape[1:]), lambda n: (index(n), 0, 0, 0))


def _gdn_scan_fwd_call(qd, kdt, w, u, qk, gl):
    nc, BH, C, DK = qd.shape
    DV = u.shape[3]

    def body(qd_ref, kdt_ref, w_ref, u_ref, qk_ref, gl_ref, o_ref, s_ref, vn_ref, state):
        @pl.when(pl.program_id(0) == 0)
        def _():
            state[...] = jnp.zeros_like(state)

        s = state[...]
        s_ref[0] = s
        vn = u_ref[0] - _bmm(w_ref[0], s)
        vn_ref[0] = vn
        o_ref[0] = _bmm(qd_ref[0], s) + _bmm(qk_ref[0], vn)
        state[...] = s * gl_ref[0] + _bmm(kdt_ref[0], vn)

    fwd = lambda n: n
    outs = [(nc, BH, C, DV), (nc, BH, DK, DV), (nc, BH, C, DV)]
    return _pcall(
        body, name="gdn_scan_fwd", grid=(nc,),
        in_specs=[_chunk_spec(t.shape, fwd) for t in (qd, kdt, w, u, qk, gl)],
        out_specs=[_chunk_spec(s, fwd) for s in outs],
        out_shape=[jax.ShapeDtypeStruct(s, F32) for s in outs],
        scratch_shapes=[pltpu.VMEM((BH, DK, DV), F32)],
        compiler_params=pltpu.CompilerParams(dimension_semantics=("arbitrary",)),
    )(qd, kdt, w, u, qk, gl)


def _gdn_scan_bwd_call(do, qdt, kd, wt, qkt, gl, s_all, vn):
    nc, BH, C, DV = do.shape
    DK = kd.shape[3]

    def body(do_ref, qdt_ref, kd_ref, wt_ref, qkt_ref, gl_ref, s_ref, vn_ref,
             dqd_ref, dkd_ref, dw_ref, du_ref, dqk_ref, dgl_ref, dstate):
        @pl.when(pl.program_id(0) == 0)
        def _():
            dstate[...] = jnp.zeros_like(dstate)

        ds = dstate[...]
        s, v, dov = s_ref[0], vn_ref[0], do_ref[0]
        dkd_ref[0] = _bmm_nt(v, ds)
        dgl_ref[0] = s * ds
        dqd_ref[0] = _bmm_nt(dov, s)
        dqk_ref[0] = _bmm_nt(dov, v)
        dv = _bmm(kd_ref[0], ds) + _bmm(qkt_ref[0], dov)
        du_ref[0] = dv
        dw_ref[0] = -_bmm_nt(dv, s)
        dstate[...] = ds * gl_ref[0] + _bmm(qdt_ref[0], dov) - _bmm(wt_ref[0], dv)

    rev = lambda n: nc - 1 - n
    outs = [(nc, BH, C, DK)] * 3 + [(nc, BH, C, DV), (nc, BH, C, C), (nc, BH, DK, DV)]
    return _pcall(
        body, name="gdn_scan_bwd", grid=(nc,),
        in_specs=[_chunk_spec(t.shape, rev) for t in (do, qdt, kd, wt, qkt, gl, s_all, vn)],
        out_specs=[_chunk_spec(s, rev) for s in outs],
        out_shape=[jax.ShapeDtypeStruct(s, F32) for s in outs],
        scratch_shapes=[pltpu.VMEM((BH, DK, DV), F32)],
        compiler_params=pltpu.CompilerParams(dimension_semantics=("arbitrary",)),
    )(do, qdt, kd, wt, qkt, gl, s_all, vn)


def _unit_lower_inverse_call(mt):
    C, _, N = mt.shape

    def body(m_ref, x_ref):
        rows = lax.broadcasted_iota(jnp.int32, (C, LANES), 0)

        def outer(i, carry):
            def inner(j, acc):
                return acc - m_ref[i, pl.ds(j, 1), :] * x_ref[j]

            x_ref[i] = lax.fori_loop(0, i, inner, jnp.where(rows == i, 1.0, 0.0).astype(F32))
            return carry

        lax.fori_loop(0, C, outer, 0)

    spec = pl.BlockSpec((C, C, LANES), lambda n: (0, 0, n))
    return _pcall(
        body, name="unit_lower_inverse", grid=(N // LANES,), in_specs=[spec], out_specs=spec,
        out_shape=jax.ShapeDtypeStruct((C, C, N), F32),
        compiler_params=pltpu.CompilerParams(dimension_semantics=("parallel",)),
    )(mt)


@jax.custom_vjp
def _unit_lower_inverse(m):
    C = m.shape[-1]
    n = math.prod(m.shape[:-2])
    pad = (-n) % LANES
    mt = jnp.pad(jnp.moveaxis(m.reshape(n, C, C), 0, 2), ((0, 0), (0, 0), (0, pad)))
    x = _unit_lower_inverse_call(mt)[:, :, :n]
    return jnp.moveaxis(x, 2, 0).reshape(m.shape)


def _unit_lower_inverse_fwd(m):
    x = _unit_lower_inverse(m)
    return x, x


def _unit_lower_inverse_bwd(x, dx):
    hi = lax.Precision.HIGHEST
    xt = jnp.swapaxes(x, -1, -2)
    return (-jnp.matmul(jnp.matmul(xt, dx, precision=hi), xt, precision=hi),)


_unit_lower_inverse.defvjp(_unit_lower_inverse_fwd, _unit_lower_inverse_bwd)


def _gl_rows(gl, dv):
    return jnp.broadcast_to(gl[:, :, None, None], gl.shape + (1, dv))


@jax.custom_vjp
def _gdn_scan(qd, kd, w, u, qk, gl):
    return _gdn_scan_fwd_call(qd, jnp.swapaxes(kd, 2, 3), w, u, qk, _gl_rows(gl, u.shape[3]))[0]


def _gdn_scan_fwd(qd, kd, w, u, qk, gl):
    o, s_all, vn = _gdn_scan_fwd_call(qd, jnp.swapaxes(kd, 2, 3), w, u, qk, _gl_rows(gl, u.shape[3]))
    return o, (qd, kd, w, qk, gl, s_all, vn)


def _gdn_scan_bwd(res, do):
    qd, kd, w, qk, gl, s_all, vn = res
    dqd, dkd, dw, du, dqk, dgl = _gdn_scan_bwd_call(
        do, jnp.swapaxes(qd, 2, 3), kd, jnp.swapaxes(w, 2, 3), jnp.swapaxes(qk, 2, 3),
        _gl_rows(gl, do.shape[3]), s_all, vn)
    return dqd, dkd, dw, du, dqk, jnp.sum(dgl, axis=(2, 3))


_gdn_scan.defvjp(_gdn_scan_fwd, _gdn_scan_bwd)


def _gdn_branch(p, conv_w, a_log, dt_bias, g_on):
    B, T, _ = p.shape
    H, DK, DV, C = GDN_HEADS, GDN_DK, GDN_DV, GDN_CHUNK
    qkv = jax.nn.silu(_causal_dwconv(p[..., OFF_GDN_QKV:OFF_GDN_A], conv_w))
    q = _l2norm(qkv[..., :H * DK].reshape(B, T, H, DK)) * DK ** -0.5
    k = _l2norm(qkv[..., H * DK:2 * H * DK].reshape(B, T, H, DK))
    v = qkv[..., 2 * H * DK:].reshape(B, T, H, DV)
    beta = jax.nn.sigmoid(p[..., OFF_GDN_B:OFF_GDN_G])
    g = -jnp.exp(a_log) * jax.nn.softplus(p[..., OFF_GDN_A:OFF_GDN_B] + dt_bias)
    nc = T // C

    def chunks(t):
        return jnp.moveaxis(t, 2, 1).reshape((B, H, nc, C) + t.shape[3:])

    q, k, v, beta, g = chunks(q), chunks(k), chunks(v), chunks(beta), chunks(g)
    G = jnp.cumsum(g, axis=-1)
    idx = jnp.arange(C)
    strict = idx[:, None] > idx[None, :]
    incl = idx[:, None] >= idx[None, :]
    decay = jnp.exp(jnp.where(incl, G[..., :, None] - G[..., None, :], NEG_INF))
    kb = k * beta[..., None]
    vb = v * beta[..., None]
    m = jnp.eye(C, dtype=F32) + jnp.where(
        strict, jnp.einsum("bhnik,bhnjk->bhnij", kb, k) * decay, 0.0)
    rhs = jnp.concatenate([kb * jnp.exp(G)[..., None], vb], axis=-1)
    sol = jnp.matmul(_unit_lower_inverse(m), rhs, precision=lax.Precision.HIGHEST)
    w, u = sol[..., :DK], sol[..., DK:]
    qk = jnp.where(incl, jnp.einsum("bhnik,bhnjk->bhnij", q, k) * decay, 0.0)
    q_dec = q * jnp.exp(G)[..., None]
    k_dec = k * jnp.exp(G[..., -1:] - G)[..., None]
    g_last = jnp.exp(G[..., -1])
    def chunk_major(t):
        return jnp.moveaxis(t, 2, 0).reshape((nc, B * H) + t.shape[3:])

    o = _gdn_scan(chunk_major(q_dec), chunk_major(k_dec), chunk_major(w), chunk_major(u),
                  chunk_major(qk), chunk_major(g_last))
    o = jnp.moveaxis(o.reshape(nc, B, H, C, DV), 0, 2).reshape(B, H, T, DV).transpose(0, 2, 1, 3)
    gate = jax.nn.silu(p[..., OFF_GDN_G:OFF_LRU]).reshape(B, T, H, DV)
    o = _rmsnorm(o, g_on) * gate
    return o.reshape(B, T, H * DV)


def _rglru_branch(p, valid, conv_w, conv_b, w_a, b_a, w_x, b_x, lam):
    B, T, _ = p.shape
    xr = _causal_dwconv(p[..., OFF_LRU:N_IN], conv_w) + conv_b
    xr = jnp.where(valid[None, :, None], xr, 0)
    xb = xr.reshape(B, T, LRU_BLOCKS, LRU_WIDTH // LRU_BLOCKS)
    r = jax.nn.sigmoid(jnp.einsum("btni,nij->btnj", xb, w_a).reshape(B, T, LRU_WIDTH) + b_a)
    ig = jax.nn.sigmoid(jnp.einsum("btni,nij->btnj", xb, w_x).reshape(B, T, LRU_WIDTH) + b_x)
    log_a = -LRU_C * r * jax.nn.softplus(-lam)
    a = jnp.exp(log_a)
    b = jnp.sqrt(-jnp.expm1(2.0 * log_a)) * ig * xr

    def combine(e1, e2):
        return (e1[0] * e2[0], e2[0] * e1[1] + e2[1])

    _, h = lax.associative_scan(combine, (a, b), axis=1)
    return h


def _gate_merge_fwd_call(zg, b, ys):
    M, D = ys[0].shape
    tr = _row_tile(M, 4 * zg.shape[1], ROW_BLOCK_BYTES)

    def body(z_ref, b_ref, *refs):
        y_refs, o_ref = refs[:N_BRANCH], refs[N_BRANCH]
        acc = None
        for n in range(N_BRANCH):
            cols = slice(n * D, (n + 1) * D)
            term = jax.nn.sigmoid(z_ref[:, cols] + b_ref[:, cols]) * y_refs[n][...]
            acc = term if acc is None else acc + term
        o_ref[...] = acc

    row = pl.BlockSpec((tr, D), lambda i: (i, 0))
    return _pcall(
        body, name="gate_merge_fwd", grid=(M // tr,),
        in_specs=[pl.BlockSpec((tr, zg.shape[1]), lambda i: (i, 0)),
                  pl.BlockSpec((1, zg.shape[1]), lambda i: (0, 0))] + [row] * N_BRANCH,
        out_specs=row, out_shape=jax.ShapeDtypeStruct((M, D), F32),
        compiler_params=pltpu.CompilerParams(dimension_semantics=("parallel",),
                                             vmem_limit_bytes=VMEM_LIMIT_BYTES),
    )(zg, b, *ys)


def _gate_merge_bwd_call(zg, b, ys, dm):
    M, D = ys[0].shape
    tr = _row_tile(M, 4 * zg.shape[1], ROW_BLOCK_BYTES)

    def body(z_ref, b_ref, *refs):
        y_refs, dm_ref = refs[:N_BRANCH], refs[N_BRANCH]
        dz_ref, dy_refs = refs[N_BRANCH + 1], refs[N_BRANCH + 2:]
        dmv = dm_ref[...]
        for n in range(N_BRANCH):
            cols = slice(n * D, (n + 1) * D)
            s = jax.nn.sigmoid(z_ref[:, cols] + b_ref[:, cols])
            dy_refs[n][...] = dmv * s
            dz_ref[:, cols] = dmv * y_refs[n][...] * (s * (1.0 - s))

    row = pl.BlockSpec((tr, D), lambda i: (i, 0))
    wide = pl.BlockSpec((tr, zg.shape[1]), lambda i: (i, 0))
    return _pcall(
        body, name="gate_merge_bwd", grid=(M // tr,),
        in_specs=[wide, pl.BlockSpec((1, zg.shape[1]), lambda i: (0, 0))] + [row] * (N_BRANCH + 1),
        out_specs=[wide] + [row] * N_BRANCH,
        out_shape=[jax.ShapeDtypeStruct(zg.shape, F32)] + [jax.ShapeDtypeStruct((M, D), F32)] * N_BRANCH,
        compiler_params=pltpu.CompilerParams(dimension_semantics=("parallel",),
                                             vmem_limit_bytes=VMEM_LIMIT_BYTES),
    )(zg, b, *ys, dm)


@jax.custom_vjp
def _gate_merge(zg, b, ys):
    return _gate_merge_fwd_call(zg, b, ys)


def _gate_merge_fwd(zg, b, ys):
    return _gate_merge_fwd_call(zg, b, ys), (zg, b, ys)


def _gate_merge_bwd(res, dm):
    zg, b, ys = res
    outs = _gate_merge_bwd_call(zg, b, ys, dm)
    return outs[0], jnp.sum(outs[0], axis=0, keepdims=True), tuple(outs[1:])


_gate_merge.defvjp(_gate_merge_fwd, _gate_merge_bwd)


def _mixer(u, valid, cos, sin, w, wb, l):
    def pair(name, *idx):
        return wb[name][(l,) + idx], w[name][(l,) + idx]

    B, T, D = u.shape
    cuts = (OFF_FOX_QKV, OFF_FOX_F, OFF_MLA_CQ, OFF_MLA_CKV, OFF_MLA_KR, OFF_GDN_QKV, OFF_GDN_A, OFF_GDN_B,
            OFF_GDN_G, OFF_LRU, N_IN)
    bounds = tuple(zip(cuts[:-1], cuts[1:]))
    parts = _split_cols(dense_nd(u, *pair("w_in"), keep_pad=True), bounds)
    p = _Cols({b: s.reshape(B, T, -1) for b, s in zip(bounds, parts)}, (B, T, N_IN))
    ys = (_fox_branch(p, w["fox_bf"][l]),
          _mla_branch(p, w["mla_gq"][l], w["mla_wq"][l], w["mla_gkv"][l], w["mla_wkv"][l], cos, sin),
          _gdn_branch(p, w["gdn_conv"][l], w["gdn_alog"][l], w["gdn_dtb"][l], w["gdn_gon"][l]),
          _rglru_branch(p, valid, w["lru_conv"][l], w["lru_conv_b"][l], w["lru_wa"][l], w["lru_ba"][l],
                        w["lru_wx"][l], w["lru_bx"][l], w["lru_lam"][l]))
    w_gate = [jnp.moveaxis(t, 0, 1).reshape(D, N_BRANCH * D) for t in pair("w_gate")]
    zg = dense_nd(u, *w_gate, keep_pad=True)
    yb = tuple(dense_nd(ys[n], *pair("w_branch", n)).reshape(B * T, D) for n in range(N_BRANCH))
    merged = _gate_merge(zg, w["b_gate"][l].reshape(1, -1), yb)
    return dense_nd(merged.reshape(B, T, D), *pair("w_out"))


def _local_loss(w, x, loss_target, wb):
    B, S, D = x.shape
    T = BLOCK + S
    h = jnp.concatenate([jnp.zeros((B, PAD_LEN, D), F32),
                         jnp.broadcast_to(w["meta"][None], (B, N_META, D)), x], axis=1)
    pos = jnp.arange(T)
    valid = pos >= PAD_LEN
    rel = (pos - PAD_LEN).astype(F32)
    inv_freq = ROPE_BASE ** (-(jnp.arange(0, MLA_ROPE, 2, dtype=F32) / MLA_ROPE))
    ang = rel[:, None] * inv_freq[None, :]
    cos, sin = jnp.cos(ang), jnp.sin(ang)
    def pair(name, l):
        return wb[name][l], w[name][l]

    for l in range(DEPTH):
        h = h + 0.5 * _swiglu(_rmsnorm_rows(h, w["ln_ffn1"][l]), pair("ffn1_wi", l), pair("ffn1_wo", l))
        u = jnp.where(valid[None, :, None], _rmsnorm_rows(h, w["ln_mix"][l]), 0)
        h = h + _mixer(u, valid, cos, sin, w, wb, l)
        h = h + 0.5 * _swiglu(_rmsnorm_rows(h, w["ln_ffn2"][l]), pair("ffn2_wi", l), pair("ffn2_wo", l))
    y = _rmsnorm_rows(h, w["ln_final"])[:, BLOCK:]
    err = jnp.square(y - loss_target)
    return 0.5 * jnp.sum(jnp.mean(err, axis=-1))


MESH = pl.DeviceIdType.MESH
HBM = pl.BlockSpec(memory_space=pl.ANY)


def _place():
    x, y, c = lax.axis_index("x"), lax.axis_index("y"), lax.axis_index("c")
    return x, y, c


def _other_chip(x, y, r):
    return (1 - x if r & 2 else x), (1 - y if r & 1 else y)


def _gather_chips(shard):
    R, C = shard.shape
    H = R // 2

    def body(x_ref, out_ref, send_sems, recv_sems, local_sem):
        x, y, c = _place()
        me = 2 * x + y

        def rows(chip, half):
            return out_ref.at[chip, pl.ds(half * H, H), :]

        def copy(sem, src, dst, to):
            return pltpu.make_async_remote_copy(src_ref=src, dst_ref=dst, send_sem=send_sems.at[sem],
                                                recv_sem=recv_sems.at[sem], device_id=to, device_id_type=MESH)

        mine = pltpu.make_async_copy(x_ref, out_ref.at[me], local_sem)
        mine.start()
        started = []
        for r in (1, 2, 3):
            ox, oy = _other_chip(x, y, r)
            cp = copy(r - 1, x_ref.at[pl.ds(c * H, H), :], rows(me, c), (ox, oy, c))
            cp.start()
            started.append(cp)
        for r in (1, 2, 3):
            ox, oy = _other_chip(x, y, r)
            src = 2 * ox + oy
            copy(r - 1, rows(src, c), rows(src, c), (x, y, c)).wait_recv()
            fw = copy(2 + r, rows(src, c), rows(src, c), (x, y, 1 - c))
            fw.start()
            started.append(fw)
        for r in (1, 2, 3):
            ox, oy = _other_chip(x, y, r)
            src = 2 * ox + oy
            copy(2 + r, rows(src, 1 - c), rows(src, 1 - c), (x, y, c)).wait_recv()
        for cp in started:
            cp.wait_send()
        mine.wait()

    return _pcall(
        body, name="gather_chips", in_specs=[HBM], out_specs=HBM,
        out_shape=jax.ShapeDtypeStruct((N_CHIPS, R, C), shard.dtype),
        scratch_shapes=[pltpu.SemaphoreType.DMA((6,)), pltpu.SemaphoreType.DMA((6,)),
                        pltpu.SemaphoreType.DMA],
    )(shard)


def _window(ref, lead, axis, chip, width):
    idx = [slice(None)] * len(ref.shape)
    if lead is not None:
        idx[0] = lead
    idx[axis] = pl.ds(chip * width, width)
    return ref.at[tuple(idx)]


def _gather_layers(shards, axes):
    n_t = len(shards)
    widths = [s.shape[a] for s, a in zip(shards, axes)]
    fulls = [s.shape[:a] + (N_CHIPS * s.shape[a],) + s.shape[a + 1:] for s, a in zip(shards, axes)]

    def body(*refs):
        x_refs, out_refs = refs[:n_t], refs[n_t:2 * n_t]
        send_sems, recv_sems = refs[2 * n_t:]
        x, y, c = _place()
        me = 2 * x + y

        def win(t, lead, chip):
            return _window(out_refs[t], lead, axes[t], chip, widths[t])

        def copy(sem, src, dst, to):
            return pltpu.make_async_remote_copy(src_ref=src, dst_ref=dst, send_sem=send_sems.at[sem],
                                                recv_sem=recv_sems.at[sem], device_id=to, device_id_type=MESH)

        started = []
        for t in range(n_t):
            cp = copy(7 * t + 6, x_refs[t], win(t, None, me), (x, y, 1 - c))
            cp.start()
            started.append(cp)
        for r in (1, 2, 3):
            ox, oy = _other_chip(x, y, r)
            for t in range(n_t):
                cp = copy(7 * t + r - 1, x_refs[t].at[c], win(t, c, me), (ox, oy, c))
                cp.start()
                started.append(cp)
        for r in (1, 2, 3):
            ox, oy = _other_chip(x, y, r)
            src = 2 * ox + oy
            for t in range(n_t):
                copy(7 * t + r - 1, win(t, c, src), win(t, c, src), (x, y, c)).wait_recv()
                fw = copy(7 * t + 2 + r, win(t, c, src), win(t, c, src), (x, y, 1 - c))
                fw.start()
                started.append(fw)
        for r in (1, 2, 3):
            ox, oy = _other_chip(x, y, r)
            src = 2 * ox + oy
            for t in range(n_t):
                copy(7 * t + 2 + r, win(t, 1 - c, src), win(t, 1 - c, src), (x, y, c)).wait_recv()
        for t in range(n_t):
            copy(7 * t + 6, win(t, None, me), win(t, None, me), (x, y, c)).wait_recv()
        for cp in started:
            cp.wait_send()

    return _pcall(
        body, name="gather_layers", in_specs=[HBM] * n_t, out_specs=[HBM] * n_t,
        out_shape=[jax.ShapeDtypeStruct(f, s.dtype) for f, s in zip(fulls, shards)],
        scratch_shapes=[pltpu.SemaphoreType.DMA((7 * n_t,)), pltpu.SemaphoreType.DMA((7 * n_t,))],
    )(*shards)


def _swap_layers(gs):
    n_t = len(gs)

    def body(*refs):
        g_refs, out_refs = refs[:n_t], refs[n_t:2 * n_t]
        send_sems, recv_sems = refs[2 * n_t:]
        x, y, c = _place()
        cps = []
        for t in range(n_t):
            cp = pltpu.make_async_remote_copy(
                src_ref=g_refs[t].at[1 - c], dst_ref=out_refs[t], send_sem=send_sems.at[t],
                recv_sem=recv_sems.at[t], device_id=(x, y, 1 - c), device_id_type=MESH)
            cp.start()
            cps.append(cp)
        for cp in cps:
            cp.wait()

    return _pcall(
        body, name="swap_layers", in_specs=[HBM] * n_t, out_specs=[HBM] * n_t,
        out_shape=[jax.ShapeDtypeStruct(g.shape[1:], g.dtype) for g in gs],
        scratch_shapes=[pltpu.SemaphoreType.DMA((n_t,)), pltpu.SemaphoreType.DMA((n_t,))],
    )(*gs)


def _add_layer(g, other, c):
    shape = other.shape
    last = shape[-1]
    rows = math.prod(shape[:-1])
    tr = _row_tile(rows, 4 * last, ROW_BLOCK_BYTES)

    def body(c_ref, a_ref, b_ref, o_ref):
        o_ref[...] = (a_ref[0] + b_ref[...]).astype(BF16)

    out = _pcall(
        body, name="add_layer",
        grid_spec=pltpu.PrefetchScalarGridSpec(
            num_scalar_prefetch=1, grid=(rows // tr,),
            in_specs=[pl.BlockSpec((1, tr, last), lambda i, c_ref: (c_ref[0], i, 0)),
                      pl.BlockSpec((tr, last), lambda i, c_ref: (i, 0))],
            out_specs=pl.BlockSpec((tr, last), lambda i, c_ref: (i, 0))),
        out_shape=jax.ShapeDtypeStruct((rows, last), BF16),
        compiler_params=pltpu.CompilerParams(dimension_semantics=("parallel",)),
    )(c.reshape(1).astype(jnp.int32), g.reshape(2, rows, last), other.reshape(rows, last))
    return out.reshape(shape)


def _scatter_layers(ps, axes):
    n_t = len(ps)
    widths = [p.shape[a] // N_CHIPS for p, a in zip(ps, axes)]
    wins = [p.shape[:a] + (w,) + p.shape[a + 1:] for p, a, w in zip(ps, axes, widths)]

    def body(*refs):
        p_refs, out_refs = refs[:n_t], refs[n_t:2 * n_t]
        send_sems, recv_sems = refs[2 * n_t:]
        x, y, c = _place()
        cps = []
        for r in (1, 2, 3):
            ox, oy = _other_chip(x, y, r)
            for t in range(n_t):
                cp = pltpu.make_async_remote_copy(
                    src_ref=_window(p_refs[t], None, axes[t], 2 * ox + oy, widths[t]), dst_ref=out_refs[t].at[r - 1],
                    send_sem=send_sems.at[3 * t + r - 1], recv_sem=recv_sems.at[3 * t + r - 1],
                    device_id=(ox, oy, c), device_id_type=MESH)
                cp.start()
                cps.append(cp)
        for cp in cps:
            cp.wait()

    return _pcall(
        body, name="scatter_layers", in_specs=[HBM] * n_t, out_specs=[HBM] * n_t,
        out_shape=[jax.ShapeDtypeStruct((N_CHIPS - 1,) + w, p.dtype) for w, p in zip(wins, ps)],
        scratch_shapes=[pltpu.SemaphoreType.DMA((3 * n_t,)), pltpu.SemaphoreType.DMA((3 * n_t,))],
    )(*ps)


def _sum_chips(p, q, axis, me):
    win = q.shape[1:]
    C = win[-1]
    H = math.prod(win[:-1])
    if axis == p.ndim - 1:
        tr = _row_tile(H, 4 * C, ROW_BLOCK_BYTES)
        grid = H // tr
        p = p.reshape(H, N_CHIPS * C)
        p_spec = pl.BlockSpec((tr, C), lambda i, me_ref: (i, me_ref[0]))
    else:
        pre, inner = math.prod(p.shape[:axis]), math.prod(win[axis:-1])
        tr = _row_tile(inner, 4 * C, ROW_BLOCK_BYTES)
        nb = inner // tr
        grid = pre * nb
        p = p.reshape(pre, N_CHIPS, inner, C)
        p_spec = pl.BlockSpec((1, 1, tr, C), lambda i, me_ref: (i // nb, me_ref[0], i % nb, 0))

    def body(me_ref, p_ref, q0, q1, q2, q3, o_ref):
        own = p_ref[...].reshape(tr, C).astype(F32)
        terms = [jnp.where(me_ref[0] == chip, own, qr[0].astype(F32)) for chip, qr in enumerate((q0, q1, q2, q3))]
        o_ref[...] = ((terms[0] + terms[1]) + terms[2]) + terms[3]

    def q_spec(chip):
        return pl.BlockSpec(
            (1, tr, C), lambda i, me_ref: (jnp.maximum(jnp.bitwise_xor(me_ref[0], chip), 1) - 1, i, 0))

    q = q.reshape(N_CHIPS - 1, H, C)
    return _pcall(
        body, name="sum_chips",
        grid_spec=pltpu.PrefetchScalarGridSpec(
            num_scalar_prefetch=1, grid=(grid,),
            in_specs=[p_spec, q_spec(0), q_spec(1), q_spec(2), q_spec(3)],
            out_specs=pl.BlockSpec((tr, C), lambda i, me_ref: (i, 0))),
        out_shape=jax.ShapeDtypeStruct((H, C), F32),
        compiler_params=pltpu.CompilerParams(dimension_semantics=("parallel",)),
    )(me.reshape(1).astype(jnp.int32), p, q, q, q, q).reshape(win)


def _send_layers(rs):
    n_t = len(rs)

    def body(*refs):
        r_refs, out_refs = refs[:n_t], refs[n_t:2 * n_t]
        send_sems, recv_sems = refs[2 * n_t:]
        x, y, c = _place()
        cps = []
        for t in range(n_t):
            cp = pltpu.make_async_remote_copy(
                src_ref=r_refs[t], dst_ref=out_refs[t], send_sem=send_sems.at[t],
                recv_sem=recv_sems.at[t], device_id=(x, y, 1 - c), device_id_type=MESH)
            cp.start()
            cps.append(cp)
        for cp in cps:
            cp.wait()

    return _pcall(
        body, name="send_layers", in_specs=[HBM] * n_t, out_specs=[HBM] * n_t,
        out_shape=[jax.ShapeDtypeStruct(r.shape, r.dtype) for r in rs],
        scratch_shapes=[pltpu.SemaphoreType.DMA((n_t,)), pltpu.SemaphoreType.DMA((n_t,))],
    )(*rs)


def _reduce_scatter_layers(gs, axes):
    x, y, c = _place()
    others = _swap_layers(gs)
    ps = [_add_layer(g, o, c) for g, o in zip(gs, others)]
    qs = _scatter_layers(ps, [a - 1 for a in axes])
    rs = [_sum_chips(p, q, a - 1, 2 * x + y) for p, q, a in zip(ps, qs, axes)]
    theirs = _send_layers(rs)
    return [jnp.where(c == 0, jnp.stack([r, o]), jnp.stack([o, r])) for r, o in zip(rs, theirs)]


def _allgather_devices(flat):
    R, C = flat.shape

    def body(x_ref, out_ref, send_sems, recv_sems, local_sem):
        x, y, c = _place()
        me = 4 * x + 2 * y + c
        mine = pltpu.make_async_copy(x_ref, out_ref.at[me], local_sem)
        mine.start()
        cps = []
        for m in range(1, 8):
            ox, oy = _other_chip(x, y, m >> 1)
            oc = 1 - c if m & 1 else c
            cp = pltpu.make_async_remote_copy(
                src_ref=x_ref, dst_ref=out_ref.at[me], send_sem=send_sems.at[m - 1],
                recv_sem=recv_sems.at[m - 1], device_id=(ox, oy, oc), device_id_type=MESH)
            cp.start()
            cps.append(cp)
        for cp in cps:
            cp.wait()
        mine.wait()

    return _pcall(
        body, name="allgather_devices", in_specs=[HBM], out_specs=HBM,
        out_shape=jax.ShapeDtypeStruct((8, R, C), flat.dtype),
        scratch_shapes=[pltpu.SemaphoreType.DMA((7,)), pltpu.SemaphoreType.DMA((7,)),
                        pltpu.SemaphoreType.DMA],
    )(flat)


def _sum_devices(slots):
    _, R, C = slots.shape
    tr = _row_tile(R, 4 * C, ROW_BLOCK_BYTES // 4)

    def body(*refs):
        o_ref = refs[8]
        acc = refs[0][0]
        for d in range(1, 8):
            acc = acc + refs[d][0]
        o_ref[...] = acc

    def spec(d):
        return pl.BlockSpec((1, tr, C), lambda i: (d, i, 0))

    return _pcall(
        body, name="sum_devices", grid=(R // tr,), in_specs=[spec(d) for d in range(8)],
        out_specs=pl.BlockSpec((tr, C), lambda i: (i, 0)),
        out_shape=jax.ShapeDtypeStruct((R, C), F32),
        compiler_params=pltpu.CompilerParams(dimension_semantics=("parallel",)),
    )(*([slots] * 8))


def _adamw(w, g, m, v):
    shape = w.shape
    C = shape[-1]
    R = math.prod(shape[:-1])
    w, g, m, v = (t.reshape(R, C) for t in (w, g, m, v))
    tr = _row_tile(R, 4 * C, ROW_BLOCK_BYTES // 2)
    c1 = 1.0 - ADAM_B1 ** ADAM_STEP
    c2 = 1.0 - ADAM_B2 ** ADAM_STEP

    def body(w_ref, g_ref, m_ref, v_ref, d_ref, nm_ref, nv_ref):
        gg = g_ref[...]
        nm = ADAM_B1 * m_ref[...] + (1.0 - ADAM_B1) * gg
        nv = ADAM_B2 * v_ref[...] + (1.0 - ADAM_B2) * jnp.square(gg)
        d_ref[...] = -ADAM_LR * ((nm / c1) / (jnp.sqrt(nv / c2) + ADAM_EPS) + ADAM_WD * w_ref[...])
        nm_ref[...] = nm
        nv_ref[...] = nv

    spec = pl.BlockSpec((tr, C), lambda i: (i, 0))
    outs = _pcall(
        body, name="adamw", grid=(R // tr,), in_specs=[spec] * 4, out_specs=[spec] * 3,
        out_shape=[jax.ShapeDtypeStruct((R, C), F32)] * 3,
        compiler_params=pltpu.CompilerParams(dimension_semantics=("parallel",)),
    )(w, g, m, v)
    return [o.reshape(shape) for o in outs]


def _to_flat(parts):
    flat = jnp.concatenate([p.reshape(-1) for p in parts])
    unit = FLAT_COLS * FLAT_ROW_ALIGN
    pad = (-flat.shape[0]) % unit
    if pad:
        flat = jnp.concatenate([flat, jnp.zeros((pad,), flat.dtype)])
    return flat.reshape(-1, FLAT_COLS)


def _from_flat(flat, shapes):
    flat = flat.reshape(-1)
    out, off = [], 0
    for s in shapes:
        n = math.prod(s)
        out.append(flat[off:off + n].reshape(s))
        off += n
    return out


def kernel(x, meta, ln_ffn1, ffn1_wi, ffn1_wo, ln_mix, w_in, fox_bf, mla_gq, mla_wq, mla_gkv, mla_wkv, gdn_conv, gdn_alog, gdn_dtb, gdn_gon, lru_conv, lru_conv_b, lru_wa, lru_ba, lru_wx, lru_bx, lru_lam, w_gate, b_gate, w_branch, w_out, ln_ffn2, ffn2_wi, ffn2_wo, ln_final, loss_target, m_meta, m_ln_ffn1, m_ffn1_wi, m_ffn1_wo, m_ln_mix, m_w_in, m_fox_bf, m_mla_gq, m_mla_wq, m_mla_gkv, m_mla_wkv, m_gdn_conv, m_gdn_alog, m_gdn_dtb, m_gdn_gon, m_lru_conv, m_lru_conv_b, m_lru_wa, m_lru_ba, m_lru_wx, m_lru_bx, m_lru_lam, m_w_gate, m_b_gate, m_w_branch, m_w_out, m_ln_ffn2, m_ffn2_wi, m_ffn2_wo, m_ln_final, v_meta, v_ln_ffn1, v_ffn1_wi, v_ffn1_wo, v_ln_mix, v_w_in, v_fox_bf, v_mla_gq, v_mla_wq, v_mla_gkv, v_mla_wkv, v_gdn_conv, v_gdn_alog, v_gdn_dtb, v_gdn_gon, v_lru_conv, v_lru_conv_b, v_lru_wa, v_lru_ba, v_lru_wx, v_lru_bx, v_lru_lam, v_w_gate, v_b_gate, v_w_branch, v_w_out, v_ln_ffn2, v_ffn2_wi, v_ffn2_wo, v_ln_final):
    ws = (meta, ln_ffn1, ffn1_wi, ffn1_wo, ln_mix, w_in, fox_bf, mla_gq, mla_wq, mla_gkv, mla_wkv, gdn_conv, gdn_alog, gdn_dtb, gdn_gon, lru_conv, lru_conv_b, lru_wa, lru_ba, lru_wx, lru_bx, lru_lam, w_gate, b_gate, w_branch, w_out, ln_ffn2, ffn2_wi, ffn2_wo, ln_final)
    ms = (m_meta, m_ln_ffn1, m_ffn1_wi, m_ffn1_wo, m_ln_mix, m_w_in, m_fox_bf, m_mla_gq, m_mla_wq, m_mla_gkv, m_mla_wkv, m_gdn_conv, m_gdn_alog, m_gdn_dtb, m_gdn_gon, m_lru_conv, m_lru_conv_b, m_lru_wa, m_lru_ba, m_lru_wx, m_lru_bx, m_lru_lam, m_w_gate, m_b_gate, m_w_branch, m_w_out, m_ln_ffn2, m_ffn2_wi, m_ffn2_wo, m_ln_final)
    vs = (v_meta, v_ln_ffn1, v_ffn1_wi, v_ffn1_wo, v_ln_mix, v_w_in, v_fox_bf, v_mla_gq, v_mla_wq, v_mla_gkv, v_mla_wkv, v_gdn_conv, v_gdn_alog, v_gdn_dtb, v_gdn_gon, v_lru_conv, v_lru_conv_b, v_lru_wa, v_lru_ba, v_lru_wx, v_lru_bx, v_lru_lam, v_w_gate, v_b_gate, v_w_branch, v_w_out, v_ln_ffn2, v_ffn2_wi, v_ffn2_wo, v_ln_final)
    names = [n for n, _ in WEIGHT_SPECS]
    axis = dict(WEIGHT_SPECS)
    wd, md, vd = dict(zip(names, ws)), dict(zip(names, ms)), dict(zip(names, vs))
    shapes = {n: wd[n].shape for n in names}
    big = [n for n in names if n in LARGE]
    few = [n for n in names if axis[n] is not None and n not in LARGE]
    whole = [n for n in names if axis[n] is None]
    x_, y_, _ = _place()
    chip = 2 * x_ + y_

    def to_dma(n, a):
        ax, w = axis[n], shapes[n][axis[n]]
        nd = len(shapes[n])
        if (ax == nd - 1 and w % LANES) or (ax == nd - 2 and w % 16):
            parts = a.shape[ax] // w
            a = jnp.moveaxis(a.reshape(a.shape[:ax] + (parts, w) + a.shape[ax + 1:]), ax, 1)
            return a, 1
        return a, ax

    def from_dma(n, a):
        ax = axis[n]
        if a.ndim == len(shapes[n]):
            return a
        a = jnp.moveaxis(a, 1, ax)
        return a.reshape(a.shape[:ax] + (-1,) + a.shape[ax + 2:])

    dma = [to_dma(n, wd[n].astype(BF16)) for n in big]
    fulls = _gather_layers([a for a, _ in dma], [ax for _, ax in dma])
    full_bf16 = {n: from_dma(n, f) for n, f in zip(big, fulls)}
    full = {n: jnp.zeros(f.shape, F32) for n, f in full_bf16.items()}
    gathered = _gather_chips(_to_flat([wd[n] for n in few]))
    per_chip = [_from_flat(gathered[k], [shapes[n] for n in few]) for k in range(N_CHIPS)]
    for i, n in enumerate(few):
        full[n] = jnp.concatenate([per_chip[k][i] for k in range(N_CHIPS)], axis=axis[n])
    full.update({n: wd[n] for n in whole})

    loss, (gw, gx) = jax.value_and_grad(_local_loss, argnums=(0, 1))(full, x, loss_target, full_bf16)
    loss = lax.psum(loss, ("x", "y", "c"))

    dma = [to_dma(n, gw[n]) for n in big]
    reduced = _reduce_scatter_layers([a for a, _ in dma], [ax for _, ax in dma])
    grads = {n: r.reshape(shapes[n]) for n, r in zip(big, reduced)}
    rest = few + whole
    summed = _from_flat(_sum_devices(_allgather_devices(_to_flat([gw[n] for n in rest]))),
                        [gw[n].shape for n in rest])
    for n, g in zip(rest, summed):
        if axis[n] is not None:
            g = lax.dynamic_slice_in_dim(g, chip * shapes[n][axis[n]], shapes[n][axis[n]], axis=axis[n])
        grads[n] = g

    delta, new_m, new_v = {}, {}, {}
    for n in big:
        delta[n], new_m[n], new_v[n] = _adamw(wd[n], grads[n], md[n], vd[n])
    outs = _adamw(*[_to_flat([d[n] for n in rest]) for d in (wd, grads, md, vd)])
    for res, flat in zip((delta, new_m, new_v), outs):
        res.update(zip(rest, _from_flat(flat, [shapes[n] for n in rest])))

    return (loss, gx, *[grads[n] for n in names], *[delta[n] for n in names],
            *[new_m[n] for n in names], *[new_v[n] for n in names])
```

```python
import functools
import math

import jax
import jax.numpy as jnp
from jax import lax
from jax.experimental import pallas as pl
from jax.experimental.pallas import tpu as pltpu

F32 = jnp.float32
BF16 = jnp.bfloat16

N_META = 16
BLOCK = 128
PAD_LEN = BLOCK - N_META
EPS = 1e-6
NEG_INF = -1e30
N_BRANCH = 4
FOX_HEADS, FOX_DH = 4, 64
MLA_HEADS, MLA_NOPE, MLA_ROPE, MLA_DV = 4, 64, 32, 64
MLA_Q_RANK, MLA_KV_RANK = 192, 128
ROPE_BASE = 10000.0
GDN_HEADS, GDN_DK, GDN_DV, GDN_CHUNK = 4, 64, 64, 64
LRU_WIDTH, LRU_BLOCKS, LRU_C = 256, 4, 8.0
DEPTH = 2

OFF_FOX_QKV = 0
OFF_FOX_F = OFF_FOX_QKV + 3 * FOX_HEADS * FOX_DH
OFF_MLA_CQ = OFF_FOX_F + FOX_HEADS
OFF_MLA_CKV = OFF_MLA_CQ + MLA_Q_RANK
OFF_MLA_KR = OFF_MLA_CKV + MLA_KV_RANK
OFF_GDN_QKV = OFF_MLA_KR + MLA_ROPE
OFF_GDN_A = OFF_GDN_QKV + GDN_HEADS * (2 * GDN_DK + GDN_DV)
OFF_GDN_B = OFF_GDN_A + GDN_HEADS
OFF_GDN_G = OFF_GDN_B + GDN_HEADS
OFF_LRU = OFF_GDN_G + GDN_HEADS * GDN_DV
N_IN = OFF_LRU + LRU_WIDTH

ADAM_LR, ADAM_B1, ADAM_B2, ADAM_EPS, ADAM_WD, ADAM_STEP = 0.001, 0.9, 0.999, 1e-08, 0.01, 10

WEIGHT_SPECS = (
    ("meta", 1), ("ln_ffn1", None), ("ffn1_wi", 2), ("ffn1_wo", 1), ("ln_mix", None), ("w_in", 2),
    ("fox_bf", None), ("mla_gq", None), ("mla_wq", 2), ("mla_gkv", None), ("mla_wkv", 2),
    ("gdn_conv", 2), ("gdn_alog", None), ("gdn_dtb", None), ("gdn_gon", None), ("lru_conv", 2),
    ("lru_conv_b", None), ("lru_wa", None), ("lru_ba", None), ("lru_wx", None), ("lru_bx", None),
    ("lru_lam", None), ("w_gate", 2), ("b_gate", 2), ("w_branch", 3), ("w_out", 1),
    ("ln_ffn2", None), ("ffn2_wi", 2), ("ffn2_wo", 1), ("ln_final", None),
)
N_CHIPS = 4
LARGE = ("ffn1_wi", "ffn1_wo", "w_in", "w_gate", "w_branch", "w_out", "ffn2_wi", "ffn2_wo")

LANES = 128
VMEM_LIMIT_BYTES = 48 * 1024 * 1024
FLAT_COLS = 512
FLAT_ROW_ALIGN = 64


def _pcall(body, **kw):
    return pl.pallas_call(body, **kw)


def _pick(n, cands):
    for c in cands:
        if n % c == 0:
            return c
    return n


_DN = {"nn": (((1,), (0,)), ((), ())), "nt": (((1,), (1,)), ((), ())), "tn": (((0,), (0,)), ((), ()))}


MATMUL_OPERAND_TILE_BYTES = 8 * 1024 * 1024


def _k_tile(K, row_bytes, lane_axis):
    for tk in (K, 4224, 2816, 2112, 1408, 1056, 1024, 768, 704, 512, 384, 256, 128):
        aligned = tk == K or tk % LANES == 0 or (not lane_axis and tk % 16 == 0)
        if tk <= K and K % tk == 0 and aligned and tk * row_bytes <= MATMUL_OPERAND_TILE_BYTES:
            return tk
    return K


def _matmul(a, b, mode, name, out_dtype=F32):
    if mode == "nn":
        (M, K), (_, N) = a.shape, b.shape
    elif mode == "nt":
        (M, K), (N, _) = a.shape, b.shape
    else:
        (K, M), (_, N) = a.shape, b.shape
    tm = _pick(M, (1024, 768, 512, 1408, 384, 256, 128, 64, 32, 16, 8))
    tn = _pick(N, (1408, 1024, 1280, 512, 256, 128))
    tk = _k_tile(K, tm * a.dtype.itemsize + tn * b.dtype.itemsize, mode != "tn")
    nk = K // tk
    a_spec = {"nn": pl.BlockSpec((tm, tk), lambda i, j, k: (i, k)),
              "nt": pl.BlockSpec((tm, tk), lambda i, j, k: (i, k)),
              "tn": pl.BlockSpec((tk, tm), lambda i, j, k: (k, i))}[mode]
    b_spec = {"nn": pl.BlockSpec((tk, tn), lambda i, j, k: (k, j)),
              "nt": pl.BlockSpec((tn, tk), lambda i, j, k: (j, k)),
              "tn": pl.BlockSpec((tk, tn), lambda i, j, k: (k, j))}[mode]
    dn = _DN[mode]

    def body(a_ref, b_ref, o_ref, acc_ref):
        k = pl.program_id(2)
        part = lax.dot_general(a_ref[...].astype(BF16), b_ref[...].astype(BF16), dn,
                               preferred_element_type=F32)

        if nk == 1:
            o_ref[...] = part.astype(o_ref.dtype)
        else:
            @pl.when(k == 0)
            def _():
                acc_ref[...] = part

            @pl.when((k > 0) & (k < nk - 1))
            def _():
                acc_ref[...] += part

            @pl.when(k == nk - 1)
            def _():
                o_ref[...] = (acc_ref[...] + part).astype(o_ref.dtype)

    return _pcall(
        body, name=name, grid=(M // tm, N // tn, nk),
        in_specs=[a_spec, b_spec], out_specs=pl.BlockSpec((tm, tn), lambda i, j, k: (i, j)),
        out_shape=jax.ShapeDtypeStruct((M, N), out_dtype),
        scratch_shapes=[pltpu.VMEM((tm, tn) if nk > 1 else (8, LANES), F32)],
        compiler_params=pltpu.CompilerParams(
            dimension_semantics=("parallel", "parallel", "arbitrary"),
            vmem_limit_bytes=VMEM_LIMIT_BYTES),
    )(a, b)


@functools.partial(jax.custom_vjp, nondiff_argnums=(3,))
def dense(a, wb, w, out_dtype):
    return _matmul(a, wb, "nn", "dense_fwd", out_dtype)


def _dense_fwd(a, wb, w, out_dtype):
    return _matmul(a, wb, "nn", "dense_fwd", out_dtype), (a, wb)


def _dense_bwd(out_dtype, res, g):
    a, wb = res
    return (_matmul(g, wb, "nt", "dense_dgrad", a.dtype), jnp.zeros_like(wb),
            _matmul(a, g, "tn", "dense_wgrad", F32))


dense.defvjp(_dense_fwd, _dense_bwd)


def dense_nd(a, wb, w, out_dtype=F32, keep_pad=False):
    K, N = wb.shape
    pad = (-N) % (4 * LANES if N > 4 * LANES else LANES)
    if pad:
        wb = jnp.pad(wb, ((0, 0), (0, pad)))
        w = jnp.pad(w, ((0, 0), (0, pad)))
    out = dense(a.reshape(-1, K), wb, w, out_dtype)
    if keep_pad:
        return out
    if pad:
        out = out[:, :N]
    return out.reshape(a.shape[:-1] + (N,))


@functools.partial(jax.custom_vjp, nondiff_argnums=(1,))
def _split_cols(x, bounds):
    return tuple(x[:, a:b] for a, b in bounds)


def _split_cols_fwd(x, bounds):
    return _split_cols(x, bounds), jnp.zeros((x.shape[0], x.shape[1] - bounds[-1][1]), x.dtype)


def _split_cols_bwd(bounds, rest, cts):
    return (jnp.concatenate(list(cts) + ([rest] if rest.shape[1] else []), axis=1),)


_split_cols.defvjp(_split_cols_fwd, _split_cols_bwd)


class _Cols:
    def __init__(self, ranges, shape):
        self.ranges, self.shape = ranges, shape

    def __getitem__(self, idx):
        return self.ranges[(idx[-1].start, idx[-1].stop)]


_NT = (((1,), (1,)), ((), ()))


def _att_blk(T):
    return _pick(T, (384, 256, 128))


ATT_ROW_SPLIT = 2


def _causal_pairs(n, by_query):
    if by_query:
        pairs = [(i, j) for i in range(n) for j in range(i + 1)]
    else:
        pairs = [(i, j) for j in range(n) for i in range(j, n)]
    return (jnp.array([p[0] for p in pairs], jnp.int32), jnp.array([p[1] for p in pairs], jnp.int32))


def _attn_fwd_call(q, k, v, cum_col, cum_row, scale):
    BH, T, dk = q.shape
    dv = v.shape[2]
    blk = _att_blk(T)
    n = T // blk
    hb = blk // ATT_ROW_SPLIT
    has_cum = cum_col is not None

    def body(i_tab, j_tab, *refs):
        if has_cum:
            q_ref, k_ref, v_ref, cq_ref, ck_ref, o_ref, lse_ref, m_s, l_s, acc_s = refs
        else:
            q_ref, k_ref, v_ref, o_ref, lse_ref, m_s, l_s, acc_s = refs
        i = i_tab[pl.program_id(1)]
        j = j_tab[pl.program_id(1)]

        @pl.when(j == 0)
        def _():
            m_s[...] = jnp.full_like(m_s, NEG_INF)
            l_s[...] = jnp.zeros_like(l_s)
            acc_s[...] = jnp.zeros_like(acc_s)

        def step(masked):
            kb = k_ref[0].astype(BF16)
            vb = v_ref[0].astype(BF16)
            for h in range(ATT_ROW_SPLIT):
                r = pl.ds(h * hb, hb)
                s = lax.dot_general(q_ref[0, r, :].astype(BF16), kb, _NT, preferred_element_type=F32) * scale
                if has_cum:
                    s = s + cq_ref[0, r, :] - ck_ref[0]
                if masked:
                    qpos = i * blk + h * hb + lax.broadcasted_iota(jnp.int32, (hb, blk), 0)
                    kpos = j * blk + lax.broadcasted_iota(jnp.int32, (hb, blk), 1)
                    s = jnp.where((kpos <= qpos) & (kpos >= PAD_LEN), s, NEG_INF)
                m_prev = m_s[r, :]
                m_new = jnp.maximum(m_prev, jnp.max(s, axis=1, keepdims=True))
                p = jnp.exp(s - m_new)
                alpha = jnp.exp(m_prev - m_new)
                l_s[r, :] = alpha * l_s[r, :] + jnp.sum(p, axis=1, keepdims=True)
                acc_s[r, :] = alpha * acc_s[r, :] + jnp.dot(p.astype(BF16), vb, preferred_element_type=F32)
                m_s[r, :] = m_new

        @pl.when((j == i) | (j == 0))
        def _():
            step(True)

        @pl.when((j < i) & (j > 0))
        def _():
            step(False)

        @pl.when(j == i)
        def _():
            o_ref[0] = acc_s[...] / l_s[...]
            lse_ref[0] = m_s[...] + jnp.log(l_s[...])

    q_idx = lambda b, t, it, jt: (b, it[t], 0)
    kv_idx = lambda b, t, it, jt: (b, jt[t], 0)
    in_specs = [pl.BlockSpec((1, blk, dk), q_idx), pl.BlockSpec((1, blk, dk), kv_idx),
                pl.BlockSpec((1, blk, dv), kv_idx)]
    args = [q, k, v]
    if has_cum:
        in_specs += [pl.BlockSpec((1, blk, 1), q_idx),
                     pl.BlockSpec((1, 1, blk), lambda b, t, it, jt: (b, 0, jt[t]))]
        args += [cum_col, cum_row]
    i_tab, j_tab = _causal_pairs(n, by_query=True)
    return _pcall(
        body, name="attn_fwd_cum" if has_cum else "attn_fwd",
        grid_spec=pltpu.PrefetchScalarGridSpec(
            num_scalar_prefetch=2, grid=(BH, len(i_tab)), in_specs=in_specs,
            out_specs=[pl.BlockSpec((1, blk, dv), q_idx), pl.BlockSpec((1, blk, 1), q_idx)],
            scratch_shapes=[pltpu.VMEM((blk, 1), F32), pltpu.VMEM((blk, 1), F32), pltpu.VMEM((blk, dv), F32)]),
        out_shape=[jax.ShapeDtypeStruct((BH, T, dv), F32), jax.ShapeDtypeStruct((BH, T, 1), F32)],
        compiler_params=pltpu.CompilerParams(
            dimension_semantics=("parallel", "arbitrary"), vmem_limit_bytes=VMEM_LIMIT_BYTES),
    )(i_tab, j_tab, *args)


def _attn_dkv_call(q, k, v, do, lse_row, delta_row, cum_row, cum_col, scale):
    BH, T, dk = q.shape
    dv = v.shape[2]
    blk = _att_blk(T)
    n = T // blk
    hb = blk // ATT_ROW_SPLIT
    has_cum = cum_row is not None

    def body(i_tab, j_tab, *refs):
        if has_cum:
            (k_ref, v_ref, q_ref, do_ref, lse_ref, dl_ref, cq_ref, ck_ref,
             dk_ref, dv_ref, dc_ref, dk_s, dv_s, dc_s) = refs
        else:
            k_ref, v_ref, q_ref, do_ref, lse_ref, dl_ref, dk_ref, dv_ref, dk_s, dv_s = refs
        i = i_tab[pl.program_id(1)]
        j = j_tab[pl.program_id(1)]

        @pl.when(i == j)
        def _():
            dk_s[...] = jnp.zeros_like(dk_s)
            dv_s[...] = jnp.zeros_like(dv_s)
            if has_cum:
                dc_s[...] = jnp.zeros_like(dc_s)

        def step(masked):
            qb = q_ref[0].astype(BF16)
            dob = do_ref[0].astype(BF16)
            for h in range(ATT_ROW_SPLIT):
                r = pl.ds(h * hb, hb)
                st = lax.dot_general(k_ref[0, r, :].astype(BF16), qb, _NT, preferred_element_type=F32) * scale
                if has_cum:
                    st = st + cq_ref[0] - ck_ref[0, r, :]
                pt = jnp.exp(jnp.minimum(st - lse_ref[0], 0.0))
                if masked:
                    kpos = j * blk + h * hb + lax.broadcasted_iota(jnp.int32, (hb, blk), 0)
                    qpos = i * blk + lax.broadcasted_iota(jnp.int32, (hb, blk), 1)
                    pt = jnp.where((kpos <= qpos) & (kpos >= PAD_LEN), pt, 0.0)
                dv_s[r, :] += jnp.dot(pt.astype(BF16), dob, preferred_element_type=F32)
                dpt = lax.dot_general(v_ref[0, r, :].astype(BF16), dob, _NT, preferred_element_type=F32)
                dst = pt * (dpt - dl_ref[0])
                dk_s[r, :] += jnp.dot(dst.astype(BF16), qb, preferred_element_type=F32) * scale
                if has_cum:
                    dc_s[r, :] -= jnp.sum(dst, axis=1, keepdims=True)

        @pl.when((i == j) | (j == 0))
        def _():
            step(True)

        @pl.when((i > j) & (j > 0))
        def _():
            step(False)

        @pl.when(i == n - 1)
        def _():
            dk_ref[0] = dk_s[...]
            dv_ref[0] = dv_s[...]
            if has_cum:
                dc_ref[0] = dc_s[...]

    k_idx = lambda b, t, it, jt: (b, jt[t], 0)
    q_idx = lambda b, t, it, jt: (b, it[t], 0)
    row_idx = lambda b, t, it, jt: (b, 0, it[t])
    in_specs = [pl.BlockSpec((1, blk, dk), k_idx), pl.BlockSpec((1, blk, dv), k_idx),
                pl.BlockSpec((1, blk, dk), q_idx), pl.BlockSpec((1, blk, dv), q_idx),
                pl.BlockSpec((1, 1, blk), row_idx), pl.BlockSpec((1, 1, blk), row_idx)]
    args = [k, v, q, do, lse_row, delta_row]
    out_specs = [pl.BlockSpec((1, blk, dk), k_idx), pl.BlockSpec((1, blk, dv), k_idx)]
    out_shape = [jax.ShapeDtypeStruct((BH, T, dk), F32), jax.ShapeDtypeStruct((BH, T, dv), F32)]
    scratch = [pltpu.VMEM((blk, dk), F32), pltpu.VMEM((blk, dv), F32)]
    if has_cum:
        in_specs += [pl.BlockSpec((1, 1, blk), row_idx), pl.BlockSpec((1, blk, 1), k_idx)]
        args += [cum_row, cum_col]
        out_specs.append(pl.BlockSpec((1, blk, 1), k_idx))
        out_shape.append(jax.ShapeDtypeStruct((BH, T, 1), F32))
        scratch.append(pltpu.VMEM((blk, 1), F32))
    i_tab, j_tab = _causal_pairs(n, by_query=False)
    return _pcall(
        body, name="attn_dkv_cum" if has_cum else "attn_dkv",
        grid_spec=pltpu.PrefetchScalarGridSpec(
            num_scalar_prefetch=2, grid=(BH, len(i_tab)), in_specs=in_specs, out_specs=out_specs,
            scratch_shapes=scratch),
        out_shape=out_shape,
        compiler_params=pltpu.CompilerParams(
            dimension_semantics=("parallel", "arbitrary"), vmem_limit_bytes=VMEM_LIMIT_BYTES),
    )(i_tab, j_tab, *args)


def _attn_dq_call(q, k, v, do, lse_col, delta_col, cum_col, cum_row, scale):
    BH, T, dk = q.shape
    dv = v.shape[2]
    blk = _att_blk(T)
    n = T // blk
    hb = blk // ATT_ROW_SPLIT
    has_cum = cum_col is not None

    def body(i_tab, j_tab, *refs):
        if has_cum:
            q_ref, k_ref, v_ref, do_ref, lse_ref, dl_ref, cq_ref, ck_ref, dq_ref, dc_ref, dq_s, dc_s = refs
        else:
            q_ref, k_ref, v_ref, do_ref, lse_ref, dl_ref, dq_ref, dq_s = refs
        i = i_tab[pl.program_id(1)]
        j = j_tab[pl.program_id(1)]

        @pl.when(j == 0)
        def _():
            dq_s[...] = jnp.zeros_like(dq_s)
            if has_cum:
                dc_s[...] = jnp.zeros_like(dc_s)

        def step(masked):
            kb = k_ref[0].astype(BF16)
            vb = v_ref[0].astype(BF16)
            for h in range(ATT_ROW_SPLIT):
                r = pl.ds(h * hb, hb)
                s = lax.dot_general(q_ref[0, r, :].astype(BF16), kb, _NT, preferred_element_type=F32) * scale
                if has_cum:
                    s = s + cq_ref[0, r, :] - ck_ref[0]
                p = jnp.exp(jnp.minimum(s - lse_ref[0, r, :], 0.0))
                if masked:
                    qpos = i * blk + h * hb + lax.broadcasted_iota(jnp.int32, (hb, blk), 0)
                    kpos = j * blk + lax.broadcasted_iota(jnp.int32, (hb, blk), 1)
                    p = jnp.where((kpos <= qpos) & (kpos >= PAD_LEN), p, 0.0)
                dp = lax.dot_general(do_ref[0, r, :].astype(BF16), vb, _NT, preferred_element_type=F32)
                ds = p * (dp - dl_ref[0, r, :])
                dq_s[r, :] += jnp.dot(ds.astype(BF16), kb, preferred_element_type=F32) * scale
                if has_cum:
                    dc_s[r, :] += jnp.sum(ds, axis=1, keepdims=True)

        @pl.when((j == i) | (j == 0))
        def _():
            step(True)

        @pl.when((j < i) & (j > 0))
        def _():
            step(False)

        @pl.when(j == i)
        def _():
            dq_ref[0] = dq_s[...]
            if has_cum:
                dc_ref[0] = dc_s[...]

    kv_idx = lambda b, t, it, jt: (b, jt[t], 0)
    q_idx = lambda b, t, it, jt: (b, it[t], 0)
    in_specs = [pl.BlockSpec((1, blk, dk), q_idx), pl.BlockSpec((1, blk, dk), kv_idx),
                pl.BlockSpec((1, blk, dv), kv_idx), pl.BlockSpec((1, blk, dv), q_idx),
                pl.BlockSpec((1, blk, 1), q_idx), pl.BlockSpec((1, blk, 1), q_idx)]
    args = [q, k, v, do, lse_col, delta_col]
    if has_cum:
        in_specs += [pl.BlockSpec((1, blk, 1), q_idx),
                     pl.BlockSpec((1, 1, blk), lambda b, t, it, jt: (b, 0, jt[t]))]
        args += [cum_col, cum_row]
    out_specs = [pl.BlockSpec((1, blk, dk), q_idx)]
    out_shape = [jax.ShapeDtypeStruct((BH, T, dk), F32)]
    scratch = [pltpu.VMEM((blk, dk), F32)]
    if has_cum:
        out_specs.append(pl.BlockSpec((1, blk, 1), q_idx))
        out_shape.append(jax.ShapeDtypeStruct((BH, T, 1), F32))
        scratch.append(pltpu.VMEM((blk, 1), F32))
    i_tab, j_tab = _causal_pairs(n, by_query=True)
    return _pcall(
        body, name="attn_dq_cum" if has_cum else "attn_dq",
        grid_spec=pltpu.PrefetchScalarGridSpec(
            num_scalar_prefetch=2, grid=(BH, len(i_tab)), in_specs=in_specs, out_specs=out_specs,
            scratch_shapes=scratch),
        out_shape=out_shape,
        compiler_params=pltpu.CompilerParams(
            dimension_semantics=("parallel", "arbitrary"), vmem_limit_bytes=VMEM_LIMIT_BYTES),
    )(i_tab, j_tab, *args)


def _make_attention(scale, has_cum):
    def fold(t):
        return t.reshape((-1,) + t.shape[2:])

    def run_fwd(q, k, v, cum):
        B, H, T, _ = q.shape
        col = cum.reshape(B * H, T, 1) if has_cum else None
        row = cum.reshape(B * H, 1, T) if has_cum else None
        o, lse = _attn_fwd_call(fold(q), fold(k), fold(v), col, row, scale)
        return o.reshape(B, H, T, -1), lse

    @jax.custom_vjp
    def attn(q, k, v, cum):
        return run_fwd(q, k, v, cum)[0]

    def attn_fwd(q, k, v, cum):
        o, lse = run_fwd(q, k, v, cum)
        return o, (q, k, v, cum, o, lse)

    def attn_bwd(res, do):
        q, k, v, cum, o, lse = res
        B, H, T, _ = q.shape
        delta = jnp.sum(do * o, axis=-1).reshape(B * H, T, 1)
        col = cum.reshape(B * H, T, 1) if has_cum else None
        row = cum.reshape(B * H, 1, T) if has_cum else None
        qf, kf, vf, dof = fold(q), fold(k), fold(v), fold(do)
        outs = _attn_dkv_call(qf, kf, vf, dof, lse.reshape(B * H, 1, T), delta.reshape(B * H, 1, T),
                              row, col, scale)
        dqs = _attn_dq_call(qf, kf, vf, dof, lse, delta, col, row, scale)
        dcum = (outs[2] + dqs[1]).reshape(B, H, T) if has_cum else jnp.zeros_like(cum)
        return (dqs[0].reshape(q.shape), outs[0].reshape(k.shape), outs[1].reshape(v.shape), dcum)

    attn.defvjp(attn_fwd, attn_bwd)
    return attn


def _rmsnorm(x, g):
    return x * lax.rsqrt(jnp.mean(x * x, axis=-1, keepdims=True) + EPS) * g


def _l2norm(x):
    return x * lax.rsqrt(jnp.sum(x * x, axis=-1, keepdims=True) + EPS)


def _row_tile(rows, row_bytes, budget):
    for t in (2048, 1024, 512, 256, 128, 64, 32, 16, 8):
        if rows % t == 0 and t * row_bytes <= budget:
            return t
    return rows


ROW_BLOCK_BYTES = 2 * 1024 * 1024


def _rms_fwd_call(x, g):
    M, D = x.shape
    tr = _row_tile(M, 4 * D, ROW_BLOCK_BYTES)

    def body(x_ref, g_ref, y_ref, r_ref):
        xv = x_ref[...]
        r = lax.rsqrt(jnp.mean(xv * xv, axis=-1, keepdims=True) + EPS)
        y_ref[...] = xv * r * g_ref[...]
        r_ref[...] = r

    return _pcall(
        body, name="rmsnorm_fwd", grid=(M // tr,),
        in_specs=[pl.BlockSpec((tr, D), lambda i: (i, 0)), pl.BlockSpec((1, D), lambda i: (0, 0))],
        out_specs=[pl.BlockSpec((tr, D), lambda i: (i, 0)), pl.BlockSpec((tr, 1), lambda i: (i, 0))],
        out_shape=[jax.ShapeDtypeStruct((M, D), F32), jax.ShapeDtypeStruct((M, 1), F32)],
        compiler_params=pltpu.CompilerParams(dimension_semantics=("parallel",)),
    )(x, g)


def _rms_bwd_call(x, g, r, dy):
    M, D = x.shape
    tr = _row_tile(M, 4 * D, ROW_BLOCK_BYTES)

    def body(x_ref, g_ref, r_ref, dy_ref, dx_ref, dg_ref):
        i = pl.program_id(0)
        rv = r_ref[...]
        xh = x_ref[...] * rv
        dyv = dy_ref[...]
        dyg = dyv * g_ref[...]
        dx_ref[...] = rv * (dyg - xh * jnp.mean(dyg * xh, axis=-1, keepdims=True))
        part = jnp.sum(dyv * xh, axis=0, keepdims=True)

        @pl.when(i == 0)
        def _():
            dg_ref[...] = part

        @pl.when(i > 0)
        def _():
            dg_ref[...] += part

    row = pl.BlockSpec((tr, D), lambda i: (i, 0))
    return _pcall(
        body, name="rmsnorm_bwd", grid=(M // tr,),
        in_specs=[row, pl.BlockSpec((1, D), lambda i: (0, 0)), pl.BlockSpec((tr, 1), lambda i: (i, 0)), row],
        out_specs=[row, pl.BlockSpec((1, D), lambda i: (0, 0))],
        out_shape=[jax.ShapeDtypeStruct((M, D), F32), jax.ShapeDtypeStruct((1, D), F32)],
        compiler_params=pltpu.CompilerParams(dimension_semantics=("arbitrary",)),
    )(x, g, r, dy)


@jax.custom_vjp
def _rmsnorm2d(x, g):
    return _rms_fwd_call(x, g)[0]


def _rmsnorm2d_fwd(x, g):
    y, r = _rms_fwd_call(x, g)
    return y, (x, g, r)


def _rmsnorm2d_bwd(res, dy):
    x, g, r = res
    return _rms_bwd_call(x, g, r, dy)


_rmsnorm2d.defvjp(_rmsnorm2d_fwd, _rmsnorm2d_bwd)


def _rmsnorm_rows(x, g):
    D = x.shape[-1]
    return _rmsnorm2d(x.reshape(-1, D), g.reshape(1, D)).reshape(x.shape)


def _glu_fwd_call(gu):
    M, F2 = gu.shape
    F = F2 // 2
    tr = _row_tile(M, 4 * F2, 2 * ROW_BLOCK_BYTES)

    def body(g_ref, u_ref, o_ref):
        gv = g_ref[...].astype(F32)
        o_ref[...] = (gv * jax.nn.sigmoid(gv) * u_ref[...].astype(F32)).astype(o_ref.dtype)

    return _pcall(
        body, name="swiglu_fwd", grid=(M // tr,),
        in_specs=[pl.BlockSpec((tr, F), lambda i: (i, 0)), pl.BlockSpec((tr, F), lambda i: (i, 1))],
        out_specs=pl.BlockSpec((tr, F), lambda i: (i, 0)),
        out_shape=jax.ShapeDtypeStruct((M, F), gu.dtype),
        compiler_params=pltpu.CompilerParams(dimension_semantics=("parallel",),
                                             vmem_limit_bytes=VMEM_LIMIT_BYTES),
    )(gu, gu)


def _glu_bwd_call(gu, da):
    M, F2 = gu.shape
    F = F2 // 2
    tr = _row_tile(M, 4 * F2, 2 * ROW_BLOCK_BYTES)

    def body(g_ref, u_ref, da_ref, o_ref):
        gv = g_ref[...].astype(F32)
        s = jax.nn.sigmoid(gv)
        dav = da_ref[...].astype(F32)
        o_ref[:, :F] = (dav * u_ref[...].astype(F32) * (s * (1.0 + gv * (1.0 - s)))).astype(o_ref.dtype)
        o_ref[:, F:] = (dav * (gv * s)).astype(o_ref.dtype)

    return _pcall(
        body, name="swiglu_bwd", grid=(M // tr,),
        in_specs=[pl.BlockSpec((tr, F), lambda i: (i, 0)), pl.BlockSpec((tr, F), lambda i: (i, 1)),
                  pl.BlockSpec((tr, F), lambda i: (i, 0))],
        out_specs=pl.BlockSpec((tr, F2), lambda i: (i, 0)),
        out_shape=jax.ShapeDtypeStruct((M, F2), gu.dtype),
        compiler_params=pltpu.CompilerParams(dimension_semantics=("parallel",),
                                             vmem_limit_bytes=VMEM_LIMIT_BYTES),
    )(gu, gu, da)


@jax.custom_vjp
def _glu(gu):
    return _glu_fwd_call(gu)


def _glu_fwd(gu):
    return _glu_fwd_call(gu), gu


def _glu_bwd(gu, da):
    return (_glu_bwd_call(gu, da),)


_glu.defvjp(_glu_fwd, _glu_bwd)


def _swiglu(h, wi, wo):
    lead, D = h.shape[:-1], h.shape[-1]
    gu = dense(h.reshape(-1, D), wi[0], wi[1], BF16)
    return dense(_glu(gu), wo[0], wo[1], F32).reshape(lead + (wo[0].shape[1],))


def _causal_dwconv(x, w):
    K, C = w.shape
    return lax.conv_general_dilated(
        x, w[:, None, :], window_strides=(1,), padding=[(K - 1, 0)],
        dimension_numbers=("NWC", "WIO", "NWC"), feature_group_count=C)


def _rope(x, cos, sin):
    half = x.shape[-1] // 2
    x1, x2 = x[..., :half], x[..., half:]
    return jnp.concatenate([x1 * cos - x2 * sin, x2 * cos + x1 * sin], axis=-1)


def _fox_branch(p, b_f):
    B, T, _ = p.shape
    qkv = p[..., OFF_FOX_QKV:OFF_FOX_F].reshape(B, T, 3, FOX_HEADS, FOX_DH)
    q = qkv[:, :, 0].transpose(0, 2, 1, 3)
    k = qkv[:, :, 1].transpose(0, 2, 1, 3)
    v = qkv[:, :, 2].transpose(0, 2, 1, 3)
    log_f = jax.nn.log_sigmoid(p[..., OFF_FOX_F:OFF_MLA_CQ] + b_f)
    cum = jnp.cumsum(log_f, axis=1).transpose(0, 2, 1)
    o = _make_attention(FOX_DH ** -0.5, True)(q, k, v, cum)
    return o.transpose(0, 2, 1, 3).reshape(B, T, FOX_HEADS * FOX_DH)


def _mla_branch(p, g_qn, w_q_up, g_kvn, w_kv_up, cos, sin):
    B, T, _ = p.shape
    cq = _rmsnorm(p[..., OFF_MLA_CQ:OFF_MLA_CKV], g_qn)
    q = (cq @ w_q_up).reshape(B, T, MLA_HEADS, MLA_NOPE + MLA_ROPE)
    ckv = _rmsnorm(p[..., OFF_MLA_CKV:OFF_MLA_KR], g_kvn)
    kv = (ckv @ w_kv_up).reshape(B, T, MLA_HEADS, MLA_NOPE + MLA_DV)
    k_rope = _rope(p[..., OFF_MLA_KR:OFF_GDN_QKV], cos, sin)
    q_rope = _rope(q[..., MLA_NOPE:], cos[:, None], sin[:, None])
    q = jnp.concatenate([q[..., :MLA_NOPE], q_rope], axis=-1)
    k = jnp.concatenate([kv[..., :MLA_NOPE],
                         jnp.broadcast_to(k_rope[:, :, None], (B, T, MLA_HEADS, MLA_ROPE))], axis=-1)
    v = kv[..., MLA_NOPE:]
    o = _make_attention((MLA_NOPE + MLA_ROPE) ** -0.5, False)(
        q.transpose(0, 2, 1, 3), k.transpose(0, 2, 1, 3), v.transpose(0, 2, 1, 3),
        jnp.zeros((B, MLA_HEADS, T), F32))
    return o.transpose(0, 2, 1, 3).reshape(B, T, MLA_HEADS * MLA_DV)


def _bmm(a, b):
    return lax.dot_general(a.astype(BF16), b.astype(BF16), (((2,), (1,)), ((0,), (0,))),
                           preferred_element_type=F32)


def _bmm_nt(a, b):
    return lax.dot_general(a.astype(BF16), b.astype(BF16), (((2,), (2,)), ((0,), (0,))),
                           preferred_element_type=F32)


def _chunk_spec(shape, index):
    return pl.BlockSpec((1,) + tuple(shape[1:]), lambda n: (index(n), 0, 0, 0))


def _gdn_scan_fwd_call(qd, kdt, w, u, qk, gl):
    nc, BH, C, DK = qd.shape
    DV = u.shape[3]

    def body(qd_ref, kdt_ref, w_ref, u_ref, qk_ref, gl_ref, o_ref, s_ref, vn_ref, state):
        @pl.when(pl.program_id(0) == 0)
        def _():
            state[...] = jnp.zeros_like(state)

        s = state[...]
        s_ref[0] = s
        vn = u_ref[0] - _bmm(w_ref[0], s)
        vn_ref[0] = vn
        o_ref[0] = _bmm(qd_ref[0], s) + _bmm(qk_ref[0], vn)
        state[...] = s * gl_ref[0] + _bmm(kdt_ref[0], vn)

    fwd = lambda n: n
    outs = [(nc, BH, C, DV), (nc, BH, DK, DV), (nc, BH, C, DV)]
    return _pcall(
        body, name="gdn_scan_fwd", grid=(nc,),
        in_specs=[_chunk_spec(t.shape, fwd) for t in (qd, kdt, w, u, qk, gl)],
        out_specs=[_chunk_spec(s, fwd) for s in outs],
        out_shape=[jax.ShapeDtypeStruct(s, F32) for s in outs],
        scratch_shapes=[pltpu.VMEM((BH, DK, DV), F32)],
        compiler_params=pltpu.CompilerParams(dimension_semantics=("arbitrary",)),
    )(qd, kdt, w, u, qk, gl)


def _gdn_scan_bwd_call(do, qdt, kd, wt, qkt, gl, s_all, vn):
    nc, BH, C, DV = do.shape
    DK = kd.shape[3]

    def body(do_ref, qdt_ref, kd_ref, wt_ref, qkt_ref, gl_ref, s_ref, vn_ref,
             dqd_ref, dkd_ref, dw_ref, du_ref, dqk_ref, dgl_ref, dstate):
        @pl.when(pl.program_id(0) == 0)
        def _():
            dstate[...] = jnp.zeros_like(dstate)

        ds = dstate[...]
        s, v, dov = s_ref[0], vn_ref[0], do_ref[0]
        dkd_ref[0] = _bmm_nt(v, ds)
        dgl_ref[0] = s * ds
        dqd_ref[0] = _bmm_nt(dov, s)
        dqk_ref[0] = _bmm_nt(dov, v)
        dv = _bmm(kd_ref[0], ds) + _bmm(qkt_ref[0], dov)
        du_ref[0] = dv
        dw_ref[0] = -_bmm_nt(dv, s)
        dstate[...] = ds * gl_ref[0] + _bmm(qdt_ref[0], dov) - _bmm(wt_ref[0], dv)

    rev = lambda n: nc - 1 - n
    outs = [(nc, BH, C, DK)] * 3 + [(nc, BH, C, DV), (nc, BH, C, C), (nc, BH, DK, DV)]
    return _pcall(
        body, name="gdn_scan_bwd", grid=(nc,),
        in_specs=[_chunk_spec(t.shape, rev) for t in (do, qdt, kd, wt, qkt, gl, s_all, vn)],
        out_specs=[_chunk_spec(s, rev) for s in outs],
        out_shape=[jax.ShapeDtypeStruct(s, F32) for s in outs],
        scratch_shapes=[pltpu.VMEM((BH, DK, DV), F32)],
        compiler_params=pltpu.CompilerParams(dimension_semantics=("arbitrary",)),
    )(do, qdt, kd, wt, qkt, gl, s_all, vn)


def _unit_lower_inverse_call(mt):
    C, _, N = mt.shape

    def body(m_ref, x_ref):
        rows = lax.broadcasted_iota(jnp.int32, (C, LANES), 0)

        def outer(i, carry):
            def inner(j, acc):
                return acc - m_ref[i, pl.ds(j, 1), :] * x_ref[j]

            x_ref[i] = lax.fori_loop(0, i, inner, jnp.where(rows == i, 1.0, 0.0).astype(F32))
            return carry

        lax.fori_loop(0, C, outer, 0)

    spec = pl.BlockSpec((C, C, LANES), lambda n: (0, 0, n))
    return _pcall(
        body, name="unit_lower_inverse", grid=(N // LANES,), in_specs=[spec], out_specs=spec,
        out_shape=jax.ShapeDtypeStruct((C, C, N), F32),
        compiler_params=pltpu.CompilerParams(dimension_semantics=("parallel",)),
    )(mt)


@jax.custom_vjp
def _unit_lower_inverse(m):
    C = m.shape[-1]
    n = math.prod(m.shape[:-2])
    pad = (-n) % LANES
    mt = jnp.pad(jnp.moveaxis(m.reshape(n, C, C), 0, 2), ((0, 0), (0, 0), (0, pad)))
    x = _unit_lower_inverse_call(mt)[:, :, :n]
    return jnp.moveaxis(x, 2, 0).reshape(m.shape)


def _unit_lower_inverse_fwd(m):
    x = _unit_lower_inverse(m)
    return x, x


def _unit_lower_inverse_bwd(x, dx):
    hi = lax.Precision.HIGHEST
    xt = jnp.swapaxes(x, -1, -2)
    return (-jnp.matmul(jnp.matmul(xt, dx, precision=hi), xt, precision=hi),)


_unit_lower_inverse.defvjp(_unit_lower_inverse_fwd, _unit_lower_inverse_bwd)


def _gl_rows(gl, dv):
    return jnp.broadcast_to(gl[:, :, None, None], gl.shape + (1, dv))


@jax.custom_vjp
def _gdn_scan(qd, kd, w, u, qk, gl):
    return _gdn_scan_fwd_call(qd, jnp.swapaxes(kd, 2, 3), w, u, qk, _gl_rows(gl, u.shape[3]))[0]


def _gdn_scan_fwd(qd, kd, w, u, qk, gl):
    o, s_all, vn = _gdn_scan_fwd_call(qd, jnp.swapaxes(kd, 2, 3), w, u, qk, _gl_rows(gl, u.shape[3]))
    return o, (qd, kd, w, qk, gl, s_all, vn)


def _gdn_scan_bwd(res, do):
    qd, kd, w, qk, gl, s_all, vn = res
    dqd, dkd, dw, du, dqk, dgl = _gdn_scan_bwd_call(
        do, jnp.swapaxes(qd, 2, 3), kd, jnp.swapaxes(w, 2, 3), jnp.swapaxes(qk, 2, 3),
        _gl_rows(gl, do.shape[3]), s_all, vn)
    return dqd, dkd, dw, du, dqk, jnp.sum(dgl, axis=(2, 3))


_gdn_scan.defvjp(_gdn_scan_fwd, _gdn_scan_bwd)


def _gdn_branch(p, conv_w, a_log, dt_bias, g_on):
    B, T, _ = p.shape
    H, DK, DV, C = GDN_HEADS, GDN_DK, GDN_DV, GDN_CHUNK
    qkv = jax.nn.silu(_causal_dwconv(p[..., OFF_GDN_QKV:OFF_GDN_A], conv_w))
    q = _l2norm(qkv[..., :H * DK].reshape(B, T, H, DK)) * DK ** -0.5
    k = _l2norm(qkv[..., H * DK:2 * H * DK].reshape(B, T, H, DK))
    v = qkv[..., 2 * H * DK:].reshape(B, T, H, DV)
    beta = jax.nn.sigmoid(p[..., OFF_GDN_B:OFF_GDN_G])
    g = -jnp.exp(a_log) * jax.nn.softplus(p[..., OFF_GDN_A:OFF_GDN_B] + dt_bias)
    nc = T // C

    def chunks(t):
        return jnp.moveaxis(t, 2, 1).reshape((B, H, nc, C) + t.shape[3:])

    q, k, v, beta, g = chunks(q), chunks(k), chunks(v), chunks(beta), chunks(g)
    G = jnp.cumsum(g, axis=-1)
    idx = jnp.arange(C)
    strict = idx[:, None] > idx[None, :]
    incl = idx[:, None] >= idx[None, :]
    decay = jnp.exp(jnp.where(incl, G[..., :, None] - G[..., None, :], NEG_INF))
    kb = k * beta[..., None]
    vb = v * beta[..., None]
    m = jnp.eye(C, dtype=F32) + jnp.where(
        strict, jnp.einsum("bhnik,bhnjk->bhnij", kb, k) * decay, 0.0)
    rhs = jnp.concatenate([kb * jnp.exp(G)[..., None], vb], axis=-1)
    sol = jnp.matmul(_unit_lower_inverse(m), rhs, precision=lax.Precision.HIGHEST)
    w, u = sol[..., :DK], sol[..., DK:]
    qk = jnp.where(incl, jnp.einsum("bhnik,bhnjk->bhnij", q, k) * decay, 0.0)
    q_dec = q * jnp.exp(G)[..., None]
    k_dec = k * jnp.exp(G[..., -1:] - G)[..., None]
    g_last = jnp.exp(G[..., -1])
    def chunk_major(t):
        return jnp.moveaxis(t, 2, 0).reshape((nc, B * H) + t.shape[3:])

    o = _gdn_scan(chunk_major(q_dec), chunk_major(k_dec), chunk_major(w), chunk_major(u),
                  chunk_major(qk), chunk_major(g_last))
    o = jnp.moveaxis(o.reshape(nc, B, H, C, DV), 0, 2).reshape(B, H, T, DV).transpose(0, 2, 1, 3)
    gate = jax.nn.silu(p[..., OFF_GDN_G:OFF_LRU]).reshape(B, T, H, DV)
    o = _rmsnorm(o, g_on) * gate
    return o.reshape(B, T, H * DV)


def _rglru_branch(p, valid, conv_w, conv_b, w_a, b_a, w_x, b_x, lam):
    B, T, _ = p.shape
    xr = _causal_dwconv(p[..., OFF_LRU:N_IN], conv_w) + conv_b
    xr = jnp.where(valid[None, :, None], xr, 0)
    xb = xr.reshape(B, T, LRU_BLOCKS, LRU_WIDTH // LRU_BLOCKS)
    r = jax.nn.sigmoid(jnp.einsum("btni,nij->btnj", xb, w_a).reshape(B, T, LRU_WIDTH) + b_a)
    ig = jax.nn.sigmoid(jnp.einsum("btni,nij->btnj", xb, w_x).reshape(B, T, LRU_WIDTH) + b_x)
    log_a = -LRU_C * r * jax.nn.softplus(-lam)
    a = jnp.exp(log_a)
    b = jnp.sqrt(-jnp.expm1(2.0 * log_a)) * ig * xr

    def combine(e1, e2):
        return (e1[0] * e2[0], e2[0] * e1[1] + e2[1])

    _, h = lax.associative_scan(combine, (a, b), axis=1)
    return h


def _gate_merge_fwd_call(zg, b, ys):
    M, D = ys[0].shape
    tr = _row_tile(M, 4 * zg.shape[1], ROW_BLOCK_BYTES)

    def body(z_ref, b_ref, *refs):
        y_refs, o_ref = refs[:N_BRANCH], refs[N_BRANCH]
        acc = None
        for n in range(N_BRANCH):
            cols = slice(n * D, (n + 1) * D)
            term = jax.nn.sigmoid(z_ref[:, cols] + b_ref[:, cols]) * y_refs[n][...]
            acc = term if acc is None else acc + term
        o_ref[...] = acc

    row = pl.BlockSpec((tr, D), lambda i: (i, 0))
    return _pcall(
        body, name="gate_merge_fwd", grid=(M // tr,),
        in_specs=[pl.BlockSpec((tr, zg.shape[1]), lambda i: (i, 0)),
                  pl.BlockSpec((1, zg.shape[1]), lambda i: (0, 0))] + [row] * N_BRANCH,
        out_specs=row, out_shape=jax.ShapeDtypeStruct((M, D), F32),
        compiler_params=pltpu.CompilerParams(dimension_semantics=("parallel",),
                                             vmem_limit_bytes=VMEM_LIMIT_BYTES),
    )(zg, b, *ys)


def _gate_merge_bwd_call(zg, b, ys, dm):
    M, D = ys[0].shape
    tr = _row_tile(M, 4 * zg.shape[1], ROW_BLOCK_BYTES)

    def body(z_ref, b_ref, *refs):
        y_refs, dm_ref = refs[:N_BRANCH], refs[N_BRANCH]
        dz_ref, dy_refs = refs[N_BRANCH + 1], refs[N_BRANCH + 2:]
        dmv = dm_ref[...]
        for n in range(N_BRANCH):
            cols = slice(n * D, (n + 1) * D)
            s = jax.nn.sigmoid(z_ref[:, cols] + b_ref[:, cols])
            dy_refs[n][...] = dmv * s
            dz_ref[:, cols] = dmv * y_refs[n][...] * (s * (1.0 - s))

    row = pl.BlockSpec((tr, D), lambda i: (i, 0))
    wide = pl.BlockSpec((tr, zg.shape[1]), lambda i: (i, 0))
    return _pcall(
        body, name="gate_merge_bwd", grid=(M // tr,),
        in_specs=[wide, pl.BlockSpec((1, zg.shape[1]), lambda i: (0, 0))] + [row] * (N_BRANCH + 1),
        out_specs=[wide] + [row] * N_BRANCH,
        out_shape=[jax.ShapeDtypeStruct(zg.shape, F32)] + [jax.ShapeDtypeStruct((M, D), F32)] * N_BRANCH,
        compiler_params=pltpu.CompilerParams(dimension_semantics=("parallel",),
                                             vmem_limit_bytes=VMEM_LIMIT_BYTES),
    )(zg, b, *ys, dm)


@jax.custom_vjp
def _gate_merge(zg, b, ys):
    return _gate_merge_fwd_call(zg, b, ys)


def _gate_merge_fwd(zg, b, ys):
    return _gate_merge_fwd_call(zg, b, ys), (zg, b, ys)


def _gate_merge_bwd(res, dm):
    zg, b, ys = res
    outs = _gate_merge_bwd_call(zg, b, ys, dm)
    return outs[0], jnp.sum(outs[0], axis=0, keepdims=True), tuple(outs[1:])


_gate_merge.defvjp(_gate_merge_fwd, _gate_merge_bwd)


def _mixer(u, valid, cos, sin, w, wb, l):
    def pair(name, *idx):
        return wb[name][(l,) + idx], w[name][(l,) + idx]

    B, T, D = u.shape
    cuts = (OFF_FOX_QKV, OFF_FOX_F, OFF_MLA_CQ, OFF_MLA_CKV, OFF_MLA_KR, OFF_GDN_QKV, OFF_GDN_A, OFF_GDN_B,
            OFF_GDN_G, OFF_LRU, N_IN)
    bounds = tuple(zip(cuts[:-1], cuts[1:]))
    parts = _split_cols(dense_nd(u, *pair("w_in"), keep_pad=True), bounds)
    p = _Cols({b: s.reshape(B, T, -1) for b, s in zip(bounds, parts)}, (B, T, N_IN))
    ys = (_fox_branch(p, w["fox_bf"][l]),
          _mla_branch(p, w["mla_gq"][l], w["mla_wq"][l], w["mla_gkv"][l], w["mla_wkv"][l], cos, sin),
          _gdn_branch(p, w["gdn_conv"][l], w["gdn_alog"][l], w["gdn_dtb"][l], w["gdn_gon"][l]),
          _rglru_branch(p, valid, w["lru_conv"][l], w["lru_conv_b"][l], w["lru_wa"][l], w["lru_ba"][l],
                        w["lru_wx"][l], w["lru_bx"][l], w["lru_lam"][l]))
    w_gate = [jnp.moveaxis(t, 0, 1).reshape(D, N_BRANCH * D) for t in pair("w_gate")]
    zg = dense_nd(u, *w_gate, keep_pad=True)
    yb = tuple(dense_nd(ys[n], *pair("w_branch", n)).reshape(B * T, D) for n in range(N_BRANCH))
    merged = _gate_merge(zg, w["b_gate"][l].reshape(1, -1), yb)
    return dense_nd(merged.reshape(B, T, D), *pair("w_out"))


def _local_loss(w, x, loss_target, wb):
    B, S, D = x.shape
    T = BLOCK + S
    h = jnp.concatenate([jnp.zeros((B, PAD_LEN, D), F32),
                         jnp.broadcast_to(w["meta"][None], (B, N_META, D)), x], axis=1)
    pos = jnp.arange(T)
    valid = pos >= PAD_LEN
    rel = (pos - PAD_LEN).astype(F32)
    inv_freq = ROPE_BASE ** (-(jnp.arange(0, MLA_ROPE, 2, dtype=F32) / MLA_ROPE))
    ang = rel[:, None] * inv_freq[None, :]
    cos, sin = jnp.cos(ang), jnp.sin(ang)
    def pair(name, l):
        return wb[name][l], w[name][l]

    for l in range(DEPTH):
        h = h + 0.5 * _swiglu(_rmsnorm_rows(h, w["ln_ffn1"][l]), pair("ffn1_wi", l), pair("ffn1_wo", l))
        u = jnp.where(valid[None, :, None], _rmsnorm_rows(h, w["ln_mix"][l]), 0)
        h = h + _mixer(u, valid, cos, sin, w, wb, l)
        h = h + 0.5 * _swiglu(_rmsnorm_rows(h, w["ln_ffn2"][l]), pair("ffn2_wi", l), pair("ffn2_wo", l))
    y = _rmsnorm_rows(h, w["ln_final"])[:, BLOCK:]
    err = jnp.square(y - loss_target)
    return 0.5 * jnp.sum(jnp.mean(err, axis=-1))


MESH = pl.DeviceIdType.MESH
HBM = pl.BlockSpec(memory_space=pl.ANY)


def _place():
    x, y, c = lax.axis_index("x"), lax.axis_index("y"), lax.axis_index("c")
    return x, y, c


def _other_chip(x, y, r):
    return (1 - x if r & 2 else x), (1 - y if r & 1 else y)


def _gather_chips(shard):
    R, C = shard.shape
    H = R // 2

    def body(x_ref, out_ref, send_sems, recv_sems, local_sem):
        x, y, c = _place()
        me = 2 * x + y

        def rows(chip, half):
            return out_ref.at[chip, pl.ds(half * H, H), :]

        def copy(sem, src, dst, to):
            return pltpu.make_async_remote_copy(src_ref=src, dst_ref=dst, send_sem=send_sems.at[sem],
                                                recv_sem=recv_sems.at[sem], device_id=to, device_id_type=MESH)

        mine = pltpu.make_async_copy(x_ref, out_ref.at[me], local_sem)
        mine.start()
        started = []
        for r in (1, 2, 3):
            ox, oy = _other_chip(x, y, r)
            cp = copy(r - 1, x_ref.at[pl.ds(c * H, H), :], rows(me, c), (ox, oy, c))
            cp.start()
            started.append(cp)
        for r in (1, 2, 3):
            ox, oy = _other_chip(x, y, r)
            src = 2 * ox + oy
            copy(r - 1, rows(src, c), rows(src, c), (x, y, c)).wait_recv()
            fw = copy(2 + r, rows(src, c), rows(src, c), (x, y, 1 - c))
            fw.start()
            started.append(fw)
        for r in (1, 2, 3):
            ox, oy = _other_chip(x, y, r)
            src = 2 * ox + oy
            copy(2 + r, rows(src, 1 - c), rows(src, 1 - c), (x, y, c)).wait_recv()
        for cp in started:
            cp.wait_send()
        mine.wait()

    return _pcall(
        body, name="gather_chips", in_specs=[HBM], out_specs=HBM,
        out_shape=jax.ShapeDtypeStruct((N_CHIPS, R, C), shard.dtype),
        scratch_shapes=[pltpu.SemaphoreType.DMA((6,)), pltpu.SemaphoreType.DMA((6,)),
                        pltpu.SemaphoreType.DMA],
    )(shard)


def _window(ref, lead, axis, chip, width):
    idx = [slice(None)] * len(ref.shape)
    if lead is not None:
        idx[0] = lead
    idx[axis] = pl.ds(chip * width, width)
    return ref.at[tuple(idx)]


def _gather_layers(shards, axes):
    n_t = len(shards)
    widths = [s.shape[a] for s, a in zip(shards, axes)]
    fulls = [s.shape[:a] + (N_CHIPS * s.shape[a],) + s.shape[a + 1:] for s, a in zip(shards, axes)]

    def body(*refs):
        x_refs, out_refs = refs[:n_t], refs[n_t:2 * n_t]
        send_sems, recv_sems = refs[2 * n_t:]
        x, y, c = _place()
        me = 2 * x + y

        def win(t, lead, chip):
            return _window(out_refs[t], lead, axes[t], chip, widths[t])

        def copy(sem, src, dst, to):
            return pltpu.make_async_remote_copy(src_ref=src, dst_ref=dst, send_sem=send_sems.at[sem],
                                                recv_sem=recv_sems.at[sem], device_id=to, device_id_type=MESH)

        started = []
        for t in range(n_t):
            cp = copy(7 * t + 6, x_refs[t], win(t, None, me), (x, y, 1 - c))
            cp.start()
            started.append(cp)
        for r in (1, 2, 3):
            ox, oy = _other_chip(x, y, r)
            for t in range(n_t):
                cp = copy(7 * t + r - 1, x_refs[t].at[c], win(t, c, me), (ox, oy, c))
                cp.start()
                started.append(cp)
        for r in (1, 2, 3):
            ox, oy = _other_chip(x, y, r)
            src = 2 * ox + oy
            for t in range(n_t):
                copy(7 * t + r - 1, win(t, c, src), win(t, c, src), (x, y, c)).wait_recv()
                fw = copy(7 * t + 2 + r, win(t, c, src), win(t, c, src), (x, y, 1 - c))
                fw.start()
                started.append(fw)
        for r in (1, 2, 3):
            ox, oy = _other_chip(x, y, r)
            src = 2 * ox + oy
            for t in range(n_t):
                copy(7 * t + 2 + r, win(t, 1 - c, src), win(t, 1 - c, src), (x, y, c)).wait_recv()
        for t in range(n_t):
            copy(7 * t + 6, win(t, None, me), win(t, None, me), (x, y, c)).wait_recv()
        for cp in started:
            cp.wait_send()

    return _pcall(
        body, name="gather_layers", in_specs=[HBM] * n_t, out_specs=[HBM] * n_t,
        out_shape=[jax.ShapeDtypeStruct(f, s.dtype) for f, s in zip(fulls, shards)],
        scratch_shapes=[pltpu.SemaphoreType.DMA((7 * n_t,)), pltpu.SemaphoreType.DMA((7 * n_t,))],
    )(*shards)


def _swap_layers(gs):
    n_t = len(gs)

    def body(*refs):
        g_refs, out_refs = refs[:n_t], refs[n_t:2 * n_t]
        send_sems, recv_sems = refs[2 * n_t:]
        x, y, c = _place()
        cps = []
        for t in range(n_t):
            cp = pltpu.make_async_remote_copy(
                src_ref=g_refs[t].at[1 - c], dst_ref=out_refs[t], send_sem=send_sems.at[t],
                recv_sem=recv_sems.at[t], device_id=(x, y, 1 - c), device_id_type=MESH)
            cp.start()
            cps.append(cp)
        for cp in cps:
            cp.wait()

    return _pcall(
        body, name="swap_layers", in_specs=[HBM] * n_t, out_specs=[HBM] * n_t,
        out_shape=[jax.ShapeDtypeStruct(g.shape[1:], g.dtype) for g in gs],
        scratch_shapes=[pltpu.SemaphoreType.DMA((n_t,)), pltpu.SemaphoreType.DMA((n_t,))],
    )(*gs)


def _add_layer(g, other, c):
    shape = other.shape
    last = shape[-1]
    rows = math.prod(shape[:-1])
    tr = _row_tile(rows, 4 * last, ROW_BLOCK_BYTES)

    def body(c_ref, a_ref, b_ref, o_ref):
        o_ref[...] = (a_ref[0] + b_ref[...]).astype(BF16)

    out = _pcall(
        body, name="add_layer",
        grid_spec=pltpu.PrefetchScalarGridSpec(
            num_scalar_prefetch=1, grid=(rows // tr,),
            in_specs=[pl.BlockSpec((1, tr, last), lambda i, c_ref: (c_ref[0], i, 0)),
                      pl.BlockSpec((tr, last), lambda i, c_ref: (i, 0))],
            out_specs=pl.BlockSpec((tr, last), lambda i, c_ref: (i, 0))),
        out_shape=jax.ShapeDtypeStruct((rows, last), BF16),
        compiler_params=pltpu.CompilerParams(dimension_semantics=("parallel",)),
    )(c.reshape(1).astype(jnp.int32), g.reshape(2, rows, last), other.reshape(rows, last))
    return out.reshape(shape)


def _scatter_layers(ps, axes):
    n_t = len(ps)
    widths = [p.shape[a] // N_CHIPS for p, a in zip(ps, axes)]
    wins = [p.shape[:a] + (w,) + p.shape[a + 1:] for p, a, w in zip(ps, axes, widths)]

    def body(*refs):
        p_refs, out_refs = refs[:n_t], refs[n_t:2 * n_t]
        send_sems, recv_sems = refs[2 * n_t:]
        x, y, c = _place()
        cps = []
        for r in (1, 2, 3):
            ox, oy = _other_chip(x, y, r)
            for t in range(n_t):
                cp = pltpu.make_async_remote_copy(
                    src_ref=_window(p_refs[t], None, axes[t], 2 * ox + oy, widths[t]), dst_ref=out_refs[t].at[r - 1],
                    send_sem=send_sems.at[3 * t + r - 1], recv_sem=recv_sems.at[3 * t + r - 1],
                    device_id=(ox, oy, c), device_id_type=MESH)
                cp.start()
                cps.append(cp)
        for cp in cps:
            cp.wait()

    return _pcall(
        body, name="scatter_layers", in_specs=[HBM] * n_t, out_specs=[HBM] * n_t,
        out_shape=[jax.ShapeDtypeStruct((N_CHIPS - 1,) + w, p.dtype) for w, p in zip(wins, ps)],
        scratch_shapes=[pltpu.SemaphoreType.DMA((3 * n_t,)), pltpu.SemaphoreType.DMA((3 * n_t,))],
    )(*ps)


def _sum_chips(p, q, axis, me):
    win = q.shape[1:]
    C = win[-1]
    H = math.prod(win[:-1])
    if axis == p.ndim - 1:
        tr = _row_tile(H, 4 * C, ROW_BLOCK_BYTES)
        grid = H // tr
        p = p.reshape(H, N_CHIPS * C)
        p_spec = pl.BlockSpec((tr, C), lambda i, me_ref: (i, me_ref[0]))
    else:
        pre, inner = math.prod(p.shape[:axis]), math.prod(win[axis:-1])
        tr = _row_tile(inner, 4 * C, ROW_BLOCK_BYTES)
        nb = inner // tr
        grid = pre * nb
        p = p.reshape(pre, N_CHIPS, inner, C)
        p_spec = pl.BlockSpec((1, 1, tr, C), lambda i, me_ref: (i // nb, me_ref[0], i % nb, 0))

    def body(me_ref, p_ref, q0, q1, q2, q3, o_ref):
        own = p_ref[...].reshape(tr, C).astype(F32)
        terms = [jnp.where(me_ref[0] == chip, own, qr[0].astype(F32)) for chip, qr in enumerate((q0, q1, q2, q3))]
        o_ref[...] = ((terms[0] + terms[1]) + terms[2]) + terms[3]

    def q_spec(chip):
        return pl.BlockSpec(
            (1, tr, C), lambda i, me_ref: (jnp.maximum(jnp.bitwise_xor(me_ref[0], chip), 1) - 1, i, 0))

    q = q.reshape(N_CHIPS - 1, H, C)
    return _pcall(
        body, name="sum_chips",
        grid_spec=pltpu.PrefetchScalarGridSpec(
            num_scalar_prefetch=1, grid=(grid,),
            in_specs=[p_spec, q_spec(0), q_spec(1), q_spec(2), q_spec(3)],
            out_specs=pl.BlockSpec((tr, C), lambda i, me_ref: (i, 0))),
        out_shape=jax.ShapeDtypeStruct((H, C), F32),
        compiler_params=pltpu.CompilerParams(dimension_semantics=("parallel",)),
    )(me.reshape(1).astype(jnp.int32), p, q, q, q, q).reshape(win)


def _send_layers(rs):
    n_t = len(rs)

    def body(*refs):
        r_refs, out_refs = refs[:n_t], refs[n_t:2 * n_t]
        send_sems, recv_sems = refs[2 * n_t:]
        x, y, c = _place()
        cps = []
        for t in range(n_t):
            cp = pltpu.make_async_remote_copy(
                src_ref=r_refs[t], dst_ref=out_refs[t], send_sem=send_sems.at[t],
                recv_sem=recv_sems.at[t], device_id=(x, y, 1 - c), device_id_type=MESH)
            cp.start()
            cps.append(cp)
        for cp in cps:
            cp.wait()

    return _pcall(
        body, name="send_layers", in_specs=[HBM] * n_t, out_specs=[HBM] * n_t,
        out_shape=[jax.ShapeDtypeStruct(r.shape, r.dtype) for r in rs],
        scratch_shapes=[pltpu.SemaphoreType.DMA((n_t,)), pltpu.SemaphoreType.DMA((n_t,))],
    )(*rs)


def _reduce_scatter_layers(gs, axes):
    x, y, c = _place()
    others = _swap_layers(gs)
    ps = [_add_layer(g, o, c) for g, o in zip(gs, others)]
    qs = _scatter_layers(ps, [a - 1 for a in axes])
    rs = [_sum_chips(p, q, a - 1, 2 * x + y) for p, q, a in zip(ps, qs, axes)]
    theirs = _send_layers(rs)
    return [jnp.where(c == 0, jnp.stack([r, o]), jnp.stack([o, r])) for r, o in zip(rs, theirs)]


def _allgather_devices(flat):
    R, C = flat.shape

    def body(x_ref, out_ref, send_sems, recv_sems, local_sem):
        x, y, c = _place()
        me = 4 * x + 2 * y + c
        mine = pltpu.make_async_copy(x_ref, out_ref.at[me], local_sem)
        mine.start()
        cps = []
        for m in range(1, 8):
            ox, oy = _other_chip(x, y, m >> 1)
            oc = 1 - c if m & 1 else c
            cp = pltpu.make_async_remote_copy(
                src_ref=x_ref, dst_ref=out_ref.at[me], send_sem=send_sems.at[m - 1],
                recv_sem=recv_sems.at[m - 1], device_id=(ox, oy, oc), device_id_type=MESH)
            cp.start()
            cps.append(cp)
        for cp in cps:
            cp.wait()
        mine.wait()

    return _pcall(
        body, name="allgather_devices", in_specs=[HBM], out_specs=HBM,
        out_shape=jax.ShapeDtypeStruct((8, R, C), flat.dtype),
        scratch_shapes=[pltpu.SemaphoreType.DMA((7,)), pltpu.SemaphoreType.DMA((7,)),
                        pltpu.SemaphoreType.DMA],
    )(flat)


def _sum_devices(slots):
    _, R, C = slots.shape
    tr = _row_tile(R, 4 * C, ROW_BLOCK_BYTES // 4)

    def body(*refs):
        o_ref = refs[8]
        acc = refs[0][0]
        for d in range(1, 8):
            acc = acc + refs[d][0]
        o_ref[...] = acc

    def spec(d):
        return pl.BlockSpec((1, tr, C), lambda i: (d, i, 0))

    return _pcall(
        body, name="sum_devices", grid=(R // tr,), in_specs=[spec(d) for d in range(8)],
        out_specs=pl.BlockSpec((tr, C), lambda i: (i, 0)),
        out_shape=jax.ShapeDtypeStruct((R, C), F32),
        compiler_params=pltpu.CompilerParams(dimension_semantics=("parallel",)),
    )(*([slots] * 8))


def _adamw(w, g, m, v):
    shape = w.shape
    C = shape[-1]
    R = math.prod(shape[:-1])
    w, g, m, v = (t.reshape(R, C) for t in (w, g, m, v))
    tr = _row_tile(R, 4 * C, ROW_BLOCK_BYTES // 2)
    c1 = 1.0 - ADAM_B1 ** ADAM_STEP
    c2 = 1.0 - ADAM_B2 ** ADAM_STEP

    def body(w_ref, g_ref, m_ref, v_ref, d_ref, nm_ref, nv_ref):
        gg = g_ref[...]
        nm = ADAM_B1 * m_ref[...] + (1.0 - ADAM_B1) * gg
        nv = ADAM_B2 * v_ref[...] + (1.0 - ADAM_B2) * jnp.square(gg)
        d_ref[...] = -ADAM_LR * ((nm / c1) / (jnp.sqrt(nv / c2) + ADAM_EPS) + ADAM_WD * w_ref[...])
        nm_ref[...] = nm
        nv_ref[...] = nv

    spec = pl.BlockSpec((tr, C), lambda i: (i, 0))
    outs = _pcall(
        body, name="adamw", grid=(R // tr,), in_specs=[spec] * 4, out_specs=[spec] * 3,
        out_shape=[jax.ShapeDtypeStruct((R, C), F32)] * 3,
        compiler_params=pltpu.CompilerParams(dimension_semantics=("parallel",)),
    )(w, g, m, v)
    return [o.reshape(shape) for o in outs]


def _to_flat(parts):
    flat = jnp.concatenate([p.reshape(-1) for p in parts])
    unit = FLAT_COLS * FLAT_ROW_ALIGN
    pad = (-flat.shape[0]) % unit
    if pad:
        flat = jnp.concatenate([flat, jnp.zeros((pad,), flat.dtype)])
    return flat.reshape(-1, FLAT_COLS)


def _from_flat(flat, shapes):
    flat = flat.reshape(-1)
    out, off = [], 0
    for s in shapes:
        n = math.prod(s)
        out.append(flat[off:off + n].reshape(s))
        off += n
    return out


def kernel(x, meta, ln_ffn1, ffn1_wi, ffn1_wo, ln_mix, w_in, fox_bf, mla_gq, mla_wq, mla_gkv, mla_wkv, gdn_conv, gdn_alog, gdn_dtb, gdn_gon, lru_conv, lru_conv_b, lru_wa, lru_ba, lru_wx, lru_bx, lru_lam, w_gate, b_gate, w_branch, w_out, ln_ffn2, ffn2_wi, ffn2_wo, ln_final, loss_target, m_meta, m_ln_ffn1, m_ffn1_wi, m_ffn1_wo, m_ln_mix, m_w_in, m_fox_bf, m_mla_gq, m_mla_wq, m_mla_gkv, m_mla_wkv, m_gdn_conv, m_gdn_alog, m_gdn_dtb, m_gdn_gon, m_lru_conv, m_lru_conv_b, m_lru_wa, m_lru_ba, m_lru_wx, m_lru_bx, m_lru_lam, m_w_gate, m_b_gate, m_w_branch, m_w_out, m_ln_ffn2, m_ffn2_wi, m_ffn2_wo, m_ln_final, v_meta, v_ln_ffn1, v_ffn1_wi, v_ffn1_wo, v_ln_mix, v_w_in, v_fox_bf, v_mla_gq, v_mla_wq, v_mla_gkv, v_mla_wkv, v_gdn_conv, v_gdn_alog, v_gdn_dtb, v_gdn_gon, v_lru_conv, v_lru_conv_b, v_lru_wa, v_lru_ba, v_lru_wx, v_lru_bx, v_lru_lam, v_w_gate, v_b_gate, v_w_branch, v_w_out, v_ln_ffn2, v_ffn2_wi, v_ffn2_wo, v_ln_final):
    ws = (meta, ln_ffn1, ffn1_wi, ffn1_wo, ln_mix, w_in, fox_bf, mla_gq, mla_wq, mla_gkv, mla_wkv, gdn_conv, gdn_alog, gdn_dtb, gdn_gon, lru_conv, lru_conv_b, lru_wa, lru_ba, lru_wx, lru_bx, lru_lam, w_gate, b_gate, w_branch, w_out, ln_ffn2, ffn2_wi, ffn2_wo, ln_final)
    ms = (m_meta, m_ln_ffn1, m_ffn1_wi, m_ffn1_wo, m_ln_mix, m_w_in, m_fox_bf, m_mla_gq, m_mla_wq, m_mla_gkv, m_mla_wkv, m_gdn_conv, m_gdn_alog, m_gdn_dtb, m_gdn_gon, m_lru_conv, m_lru_conv_b, m_lru_wa, m_lru_ba, m_lru_wx, m_lru_bx, m_lru_lam, m_w_gate, m_b_gate, m_w_branch, m_w_out, m_ln_ffn2, m_ffn2_wi, m_ffn2_wo, m_ln_final)
    vs = (v_meta, v_ln_ffn1, v_ffn1_wi, v_ffn1_wo, v_ln_mix, v_w_in, v_fox_bf, v_mla_gq, v_mla_wq, v_mla_gkv, v_mla_wkv, v_gdn_conv, v_gdn_alog, v_gdn_dtb, v_gdn_gon, v_lru_conv, v_lru_conv_b, v_lru_wa, v_lru_ba, v_lru_wx, v_lru_bx, v_lru_lam, v_w_gate, v_b_gate, v_w_branch, v_w_out, v_ln_ffn2, v_ffn2_wi, v_ffn2_wo, v_ln_final)
    names = [n for n, _ in WEIGHT_SPECS]
    axis = dict(WEIGHT_SPECS)
    wd, md, vd = dict(zip(names, ws)), dict(zip(names, ms)), dict(zip(names, vs))
    shapes = {n: wd[n].shape for n in names}
    big = [n for n in names if n in LARGE]
    few = [n for n in names if axis[n] is not None and n not in LARGE]
    whole = [n for n in names if axis[n] is None]
    x_, y_, _ = _place()
    chip = 2 * x_ + y_

    def to_dma(n, a):
        ax, w = axis[n], shapes[n][axis[n]]
        nd = len(shapes[n])
        if (ax == nd - 1 and w % LANES) or (ax == nd - 2 and w % 16):
            parts = a.shape[ax] // w
            a = jnp.moveaxis(a.reshape(a.shape[:ax] + (parts, w) + a.shape[ax + 1:]), ax, 1)
            return a, 1
        return a, ax

    def from_dma(n, a):
        ax = axis[n]
        if a.ndim == len(shapes[n]):
            return a
        a = jnp.moveaxis(a, 1, ax)
        return a.reshape(a.shape[:ax] + (-1,) + a.shape[ax + 2:])

    dma = [to_dma(n, wd[n].astype(BF16)) for n in big]
    fulls = _gather_layers([a for a, _ in dma], [ax for _, ax in dma])
    full_bf16 = {n: from_dma(n, f) for n, f in zip(big, fulls)}
    full = {n: jnp.zeros(f.shape, F32) for n, f in full_bf16.items()}
    gathered = _gather_chips(_to_flat([wd[n] for n in few]))
    per_chip = [_from_flat(gathered[k], [shapes[n] for n in few]) for k in range(N_CHIPS)]
    for i, n in enumerate(few):
        full[n] = jnp.concatenate([per_chip[k][i] for k in range(N_CHIPS)], axis=axis[n])
    full.update({n: wd[n] for n in whole})

    loss, (gw, gx) = jax.value_and_grad(_local_loss, argnums=(0, 1))(full, x, loss_target, full_bf16)
    loss = lax.psum(loss, ("x", "y", "c"))

    dma = [to_dma(n, gw[n]) for n in big]
    reduced = _reduce_scatter_layers([a for a, _ in dma], [ax for _, ax in dma])
    grads = {n: r.reshape(shapes[n]) for n, r in zip(big, reduced)}
    rest = few + whole
    summed = _from_flat(_sum_devices(_allgather_devices(_to_flat([gw[n] for n in rest]))),
                        [gw[n].shape for n in rest])
    for n, g in zip(rest, summed):
        if axis[n] is not None:
            g = lax.dynamic_slice_in_dim(g, chip * shapes[n][axis[n]], shapes[n][axis[n]], axis=axis[n])
        grads[n] = g

    delta, new_m, new_v = {}, {}, {}
    for n in big:
        delta[n], new_m[n], new_v[n] = _adamw(wd[n], grads[n], md[n], vd[n])
    outs = _adamw(*[_to_flat([d[n] for n in rest]) for d in (wd, grads, md, vd)])
    for res, flat in zip((delta, new_m, new_v), outs):
        res.update(zip(rest, _from_flat(flat, [shapes[n] for n in rest])))

    return (loss, gx, *[grads[n] for n in names], *[delta[n] for n in names],
            *[new_m[n] for n in names], *[new_v[n] for n in names])
```
